```python
import math
import jax, jax.numpy as jnp
from jax import lax
import numpy as np

D_MODEL = 1024
BATCH = 8
SEQ = 2048
DEPTH = 1
DEC_BATCH = 128
DEC_SEQ = 4
PAST_LEN = 2048
PAGE_SIZE = 128

N_ATT_HEADS = 8
ATT_HEAD_DIM = 64
D_ATT = N_ATT_HEADS * ATT_HEAD_DIM
DILATED_CFGS = ((128, 1), (512, 4), (2048, 16))
WIN_MAX = 2048
SSM_CH = 16
D_SSM = D_MODEL - D_ATT
N_SSM_GROUPS = D_SSM // SSM_CH
SSM_STATE = 64
D_IN = 3 * D_ATT + D_SSM
N_MEM = 256
MEM_HEADS = 4
MEM_HEAD_DIM = D_MODEL // MEM_HEADS
D_FF = -(-8 * D_MODEL // (3 * 256)) * 256
DEEPNORM_ALPHA = (2 * DEPTH) ** 0.25
DEEPNORM_BETA = (8 * DEPTH) ** -0.25
EPS = 1e-5
NEG = -1e30

kernel_name = "hymba_s5_longnet_memxattn_deepnorm_step"


def layer_norm(x, g, b):
    xf = x.astype(jnp.float32)
    mu = jnp.mean(xf, -1, keepdims=True)
    var = jnp.mean(jnp.square(xf - mu), -1, keepdims=True)
    return ((xf - mu) * lax.rsqrt(var + EPS) * g + b).astype(x.dtype)


def rms_norm(x, g):
    xf = x.astype(jnp.float32)
    return (xf * lax.rsqrt(jnp.mean(jnp.square(xf), -1, keepdims=True) + EPS) * g).astype(x.dtype)


def combine_by_denominators(outs, lses):
    lse = jnp.stack(lses, 0)
    w = jnp.exp(lse - jax.nn.logsumexp(lse, axis=0, keepdims=True))
    return jnp.sum(w[..., None] * jnp.stack(outs, 0), axis=0)


def dilated_attention_prompt(q, k, v):
    B, S, H, dh = q.shape
    scale = dh ** -0.5
    outs, lses = [], []
    for window, d in DILATED_CFGS:
        reach = window // d
        blk = reach
        span = d * blk
        Sp = -(-S // span) * span
        L = Sp // d
        nb = L // blk
        pad = ((0, 0), (0, Sp - S), (0, 0), (0, 0))

        def to_blocks(t):
            t = jnp.pad(t, pad).reshape(B, L, d, H, dh).transpose(0, 2, 1, 3, 4)
            return t.reshape(B, d, nb, blk, H, dh)

        qb, kb, vb = to_blocks(q), to_blocks(k), to_blocks(v)
        shift = ((0, 0), (0, 0), (1, 0), (0, 0), (0, 0), (0, 0))
        kk = jnp.concatenate([jnp.pad(kb, shift)[:, :, :-1], kb], axis=3)
        vv = jnp.concatenate([jnp.pad(vb, shift)[:, :, :-1], vb], axis=3)
        s = jnp.einsum("brnqhd,brnkhd->brnhqk", qb, kk).astype(jnp.float32) * scale
        qi = jnp.arange(blk)[:, None]
        kj = jnp.arange(2 * blk)[None, :]
        dist = qi + blk - kj
        band = (dist >= 0) & (dist <= reach)
        real = (jnp.arange(nb)[:, None, None] > 0) | (kj[None] >= blk)
        valid = (band[None] & real)[:, None]
        s = jnp.where(valid, s, NEG)
        m = jnp.max(s, -1, keepdims=True)
        p = jnp.exp(s - m)
        den = jnp.sum(p, -1, keepdims=True)
        o = jnp.einsum("brnhqk,brnkhd->brnqhd", p, vv.astype(jnp.float32))
        o = o / jnp.transpose(den, (0, 1, 2, 4, 3, 5))
        lse = (m + jnp.log(den))[..., 0]
        o = o.reshape(B, d, L, H, dh).transpose(0, 2, 1, 3, 4).reshape(B, Sp, H, dh)[:, :S]
        lse = lse.transpose(0, 1, 2, 4, 3).reshape(B, d, L, H).transpose(0, 2, 1, 3).reshape(B, Sp, H)[:, :S]
        outs.append(o)
        lses.append(lse)
    return combine_by_denominators(outs, lses).astype(q.dtype)


def dilated_attention_sample(q, k, v, cache_k, cache_v):
    W = cache_k.shape[1]
    T = q.shape[1]
    scale = q.shape[-1] ** -0.5
    kk = jnp.concatenate([cache_k.astype(k.dtype), k], axis=1)
    vv = jnp.concatenate([cache_v.astype(v.dtype), v], axis=1)
    outs, lses = [], []
    for window, d in DILATED_CFGS:
        steps = jnp.arange(window // d + 1)
        idx = W + jnp.arange(T)[:, None] - steps[None, :] * d
        valid = idx >= 0
        idxc = jnp.maximum(idx, 0)
        kg, vg = kk[:, idxc], vv[:, idxc]
        s = jnp.einsum("bthd,btmhd->bhtm", q, kg).astype(jnp.float32) * scale
        s = jnp.where(valid[None, None], s, NEG)
        m = jnp.max(s, -1, keepdims=True)
        p = jnp.exp(s - m)
        den = jnp.sum(p, -1, keepdims=True)
        o = jnp.einsum("bhtm,btmhd->bthd", p, vg.astype(jnp.float32))
        o = o / jnp.transpose(den, (0, 2, 1, 3))
        lse = jnp.transpose((m + jnp.log(den))[..., 0], (0, 2, 1))
        outs.append(o)
        lses.append(lse)
    return combine_by_denominators(outs, lses).astype(q.dtype)


def s5_mixer(u, h0_re, h0_im, a_re, a_im, log_dt, b_re, b_im, c_re, c_im, d_skip, w_glu, b_glu):
    Bn, L, _ = u.shape
    uf = u.astype(jnp.float32).reshape(Bn, L, N_SSM_GROUPS, SSM_CH)
    dt = jnp.exp(log_dt.astype(jnp.float32))[:, None]
    ar, ai = a_re.astype(jnp.float32), a_im.astype(jnp.float32)
    mag = jnp.exp(dt * ar)
    abar_re, abar_im = mag * jnp.cos(dt * ai), mag * jnp.sin(dt * ai)
    den = ar * ar + ai * ai
    nr, ni = abar_re - 1.0, abar_im
    coef_re = (nr * ar + ni * ai) / den
    coef_im = (ni * ar - nr * ai) / den
    br, bi = b_re.astype(jnp.float32), b_im.astype(jnp.float32)
    bbar_re = coef_re[..., None] * br - coef_im[..., None] * bi
    bbar_im = coef_re[..., None] * bi + coef_im[..., None] * br
    bu_re = jnp.einsum("blgh,gph->blgp", uf, bbar_re)
    bu_im = jnp.einsum("blgh,gph->blgp", uf, bbar_im)
    h0r, h0i = h0_re.astype(jnp.float32), h0_im.astype(jnp.float32)
    bu_re = bu_re.at[:, 0].add(abar_re * h0r - abar_im * h0i)
    bu_im = bu_im.at[:, 0].add(abar_re * h0i + abar_im * h0r)
    a_full_re = jnp.broadcast_to(abar_re, bu_re.shape)
    a_full_im = jnp.broadcast_to(abar_im, bu_im.shape)

    def combine(e1, e2):
        a1r, a1i, b1r, b1i = e1
        a2r, a2i, b2r, b2i = e2
        return (a2r * a1r - a2i * a1i, a2r * a1i + a2i * a1r,
                a2r * b1r - a2i * b1i + b2r, a2r * b1i + a2i * b1r + b2i)

    _, _, hr, hi = lax.associative_scan(combine, (a_full_re, a_full_im, bu_re, bu_im), axis=1)
    y = (jnp.einsum("ghp,blgp->blgh", c_re.astype(jnp.float32), hr)
         - jnp.einsum("ghp,blgp->blgh", c_im.astype(jnp.float32), hi)
         + d_skip.astype(jnp.float32) * uf).reshape(Bn, L, D_SSM)
    g = jax.nn.gelu(y)
    out = g * jax.nn.sigmoid(g @ w_glu.astype(jnp.float32) + b_glu.astype(jnp.float32))
    return out.astype(u.dtype), hr[:, -1], hi[:, -1]


def memory_attention(x, mem_k, mem_v, w_mem_q, w_mem_o):
    Bn, L, _ = x.shape
    q = (x @ w_mem_q).reshape(Bn, L, MEM_HEADS, MEM_HEAD_DIM)
    s = jnp.einsum("blhd,bmhd->bhlm", q, mem_k.astype(q.dtype)).astype(jnp.float32) * MEM_HEAD_DIM ** -0.5
    p = jax.nn.softmax(s, axis=-1)
    o = jnp.einsum("bhlm,bmhd->blhd", p, mem_v.astype(jnp.float32)).astype(x.dtype)
    return o.reshape(Bn, L, D_MODEL) @ w_mem_o


def trunk_layer(x, h0_re, h0_im, mem_k, mem_v, attend,
                w_in, g_att, g_ssm, a_re, a_im, log_dt, b_re, b_im, c_re, c_im, d_skip, w_glu, b_glu, w_out,
                ln1_g, ln1_b, w_mem_q, w_mem_o, ln2_g, ln2_b, w_gate, w_up, w_down, ln3_g, ln3_b):
    Bn, L, _ = x.shape
    proj = x @ w_in
    q, k, v, u = jnp.split(proj, [D_ATT, 2 * D_ATT, 3 * D_ATT], axis=-1)
    q = q.reshape(Bn, L, N_ATT_HEADS, ATT_HEAD_DIM)
    k = k.reshape(Bn, L, N_ATT_HEADS, ATT_HEAD_DIM)
    v = v.reshape(Bn, L, N_ATT_HEADS, ATT_HEAD_DIM)
    o_att = attend(q, k, v).reshape(Bn, L, D_ATT)
    y_ssm, h_re, h_im = s5_mixer(u, h0_re, h0_im, a_re, a_im, log_dt, b_re, b_im, c_re, c_im, d_skip, w_glu, b_glu)
    mixed = jnp.concatenate([rms_norm(o_att, g_att), rms_norm(y_ssm, g_ssm)], axis=-1) @ w_out
    x = layer_norm(DEEPNORM_ALPHA * x + mixed, ln1_g, ln1_b)
    x = layer_norm(DEEPNORM_ALPHA * x + memory_attention(x, mem_k, mem_v, w_mem_q, w_mem_o), ln2_g, ln2_b)
    ffn = (jax.nn.silu(x @ w_gate) * (x @ w_up)) @ w_down
    x = layer_norm(DEEPNORM_ALPHA * x + ffn, ln3_g, ln3_b)
    return x, k, v, h_re, h_im


def setup_inputs(seed: int = 0) -> dict:
    key = jax.random.key(seed)
    ks = iter(jax.random.split(key, 64))
    f32 = jnp.float32

    def nrm(shape, scale=1.0):
        return scale * jax.random.normal(next(ks), shape, f32)

    w_buf = min(WIN_MAX, PAST_LEN)
    G, P = N_SSM_GROUPS, SSM_STATE
    n = jnp.arange(P, dtype=f32)
    return {
        "x_prompt": nrm((BATCH, SEQ, D_MODEL)),
        "x_sample": nrm((DEC_BATCH, DEC_SEQ, D_MODEL)),
        "cache_win_k": nrm((DEPTH, DEC_BATCH, w_buf, N_ATT_HEADS, ATT_HEAD_DIM)),
        "cache_win_v": nrm((DEPTH, DEC_BATCH, w_buf, N_ATT_HEADS, ATT_HEAD_DIM)),
        "state_ssm_re": nrm((DEPTH, DEC_BATCH, G, P), 0.3),
        "state_ssm_im": nrm((DEPTH, DEC_BATCH, G, P), 0.3),
        "cache_mem_k": nrm((DEPTH, DEC_BATCH, N_MEM, MEM_HEADS, MEM_HEAD_DIM)),
        "cache_mem_v": nrm((DEPTH, DEC_BATCH, N_MEM, MEM_HEADS, MEM_HEAD_DIM)),
        "mem_prompt": nrm((BATCH, N_MEM, D_MODEL)),
        "w_in": nrm((DEPTH, D_MODEL, D_IN), D_MODEL ** -0.5),
        "g_att": 1.0 + nrm((DEPTH, D_ATT), 0.01),
        "g_ssm": 1.0 + nrm((DEPTH, D_SSM), 0.01),
        "ssm_a_re": -0.5 * jnp.exp(nrm((DEPTH, G, P), 0.05)),
        "ssm_a_im": math.pi * n + nrm((DEPTH, G, P), 0.01),
        "ssm_log_dt": jax.random.uniform(next(ks), (DEPTH, G), f32, math.log(1e-3), math.log(1e-1)),
        "ssm_b_re": nrm((DEPTH, G, P, SSM_CH), (2 * SSM_CH) ** -0.5),
        "ssm_b_im": nrm((DEPTH, G, P, SSM_CH), (2 * SSM_CH) ** -0.5),
        "ssm_c_re": nrm((DEPTH, G, SSM_CH, P), (2 * P) ** -0.5),
        "ssm_c_im": nrm((DEPTH, G, SSM_CH, P), (2 * P) ** -0.5),
        "ssm_d": nrm((DEPTH, G, SSM_CH)),
        "w_glu": nrm((DEPTH, D_SSM, D_SSM), D_SSM ** -0.5),
        "b_glu": nrm((DEPTH, D_SSM), 0.02),
        "w_out": nrm((DEPTH, D_ATT + D_SSM, D_MODEL), (D_ATT + D_SSM) ** -0.5 * DEEPNORM_BETA),
        "ln1_g": 1.0 + nrm((DEPTH, D_MODEL), 0.01),
        "ln1_b": nrm((DEPTH, D_MODEL), 0.01),
        "w_mem_q": nrm((DEPTH, D_MODEL, D_MODEL), D_MODEL ** -0.5),
        "w_mem_k": nrm((DEPTH, D_MODEL, D_MODEL), D_MODEL ** -0.5),
        "w_mem_v": nrm((DEPTH, D_MODEL, D_MODEL), D_MODEL ** -0.5),
        "w_mem_o": nrm((DEPTH, D_MODEL, D_MODEL), D_MODEL ** -0.5 * DEEPNORM_BETA),
        "ln2_g": 1.0 + nrm((DEPTH, D_MODEL), 0.01),
        "ln2_b": nrm((DEPTH, D_MODEL), 0.01),
        "w_gate": nrm((DEPTH, D_MODEL, D_FF), D_MODEL ** -0.5),
        "w_up": nrm((DEPTH, D_MODEL, D_FF), D_MODEL ** -0.5),
        "w_down": nrm((DEPTH, D_FF, D_MODEL), D_FF ** -0.5 * DEEPNORM_BETA),
        "ln3_g": 1.0 + nrm((DEPTH, D_MODEL), 0.01),
        "ln3_b": nrm((DEPTH, D_MODEL), 0.01),
    }


def reference(x_prompt, x_sample, cache_win_k, cache_win_v, state_ssm_re, state_ssm_im, cache_mem_k, cache_mem_v,
              mem_prompt, w_in, g_att, g_ssm, ssm_a_re, ssm_a_im, ssm_log_dt, ssm_b_re, ssm_b_im, ssm_c_re, ssm_c_im,
              ssm_d, w_glu, b_glu, w_out, ln1_g, ln1_b, w_mem_q, w_mem_k, w_mem_v, w_mem_o, ln2_g, ln2_b,
              w_gate, w_up, w_down, ln3_g, ln3_b):
    Bp, S, _ = x_prompt.shape
    keep = min(WIN_MAX, S)
    y_p, y_s = x_prompt, x_sample
    wk_p, wv_p, wk_s, wv_s = [], [], [], []
    hr_p, hi_p, hr_s, hi_s = [], [], [], []
    mk_p, mv_p = [], []
    for l in range(DEPTH):
        lw = (w_in[l], g_att[l], g_ssm[l], ssm_a_re[l], ssm_a_im[l], ssm_log_dt[l], ssm_b_re[l], ssm_b_im[l],
              ssm_c_re[l], ssm_c_im[l], ssm_d[l], w_glu[l], b_glu[l], w_out[l], ln1_g[l], ln1_b[l],
              w_mem_q[l], w_mem_o[l], ln2_g[l], ln2_b[l], w_gate[l], w_up[l], w_down[l], ln3_g[l], ln3_b[l])
        mem_k = (mem_prompt @ w_mem_k[l]).reshape(Bp, N_MEM, MEM_HEADS, MEM_HEAD_DIM)
        mem_v = (mem_prompt @ w_mem_v[l]).reshape(Bp, N_MEM, MEM_HEADS, MEM_HEAD_DIM)
        h0 = jnp.zeros((Bp, N_SSM_GROUPS, SSM_STATE), jnp.float32)
        y_p, k_p, v_p, h_re_p, h_im_p = trunk_layer(y_p, h0, h0, mem_k, mem_v, dilated_attention_prompt, *lw)
        ck, cv = cache_win_k[l], cache_win_v[l]
        attend_s = lambda q, k, v, ck=ck, cv=cv: dilated_attention_sample(q, k, v, ck, cv)
        y_s, k_s, v_s, h_re_s, h_im_s = trunk_layer(y_s, state_ssm_re[l], state_ssm_im[l],
                                                    cache_mem_k[l], cache_mem_v[l], attend_s, *lw)
        wk_p.append(k_p[:, S - keep:])
        wv_p.append(v_p[:, S - keep:])
        wk_s.append(k_s)
        wv_s.append(v_s)
        hr_p.append(h_re_p)
        hi_p.append(h_im_p)
        hr_s.append(h_re_s)
        hi_s.append(h_im_s)
        mk_p.append(mem_k)
        mv_p.append(mem_v)
    return (y_p, y_s,
            jnp.stack(wk_p), jnp.stack(wv_p), jnp.stack(wk_s), jnp.stack(wv_s),
            jnp.stack(hr_p), jnp.stack(hi_p), jnp.stack(hr_s), jnp.stack(hi_s),
            jnp.stack(mk_p), jnp.stack(mv_p))
```

```python
import functools
import math

import jax
import jax.numpy as jnp
from jax import lax
from jax.experimental import pallas as pl
from jax.experimental.pallas import tpu as pltpu

F32 = jnp.float32
BF16 = jnp.bfloat16

N_ATT_HEADS = 8
ATT_HEAD_DIM = 64
D_ATT = N_ATT_HEADS * ATT_HEAD_DIM
DILATED_CFGS = ((128, 1), (512, 4), (2048, 16))
ATT_BLK = 128
SSM_CH = 16
SSM_STATE = 64
N_MEM_HEADS = 4
EPS = 1e-5
NEG = -1e30

LANES = 128
V7X_VMEM_CAP_BYTES = 56 * 1024 * 1024

SSM_GB = LANES // SSM_CH
SSM_ROW_TILE = 32


def _params(sem, vmem_bytes):
    return pltpu.CompilerParams(
        dimension_semantics=sem,
        vmem_limit_bytes=int(min(max(vmem_bytes, 16 * 1024 * 1024), V7X_VMEM_CAP_BYTES)),
    )


def _dot(a, b):
    return jnp.dot(a, b, preferred_element_type=F32)


def _dot_nt(a, b):
    return lax.dot_general(a, b, (((1,), (1,)), ((), ())), preferred_element_type=F32)


def _layer_norm(x, g, b):
    mu = jnp.mean(x, axis=-1, keepdims=True)
    xc = x - mu
    var = jnp.mean(xc * xc, axis=-1, keepdims=True)
    return xc * lax.rsqrt(var + EPS) * g + b


def _rms_norm(x, g):
    return x * lax.rsqrt(jnp.mean(x * x, axis=-1, keepdims=True) + EPS) * g


def _const_spec(shape):
    n = len(shape)
    return pl.BlockSpec(shape, lambda *_: (0,) * n)


def _inproj_body(x_ref, w_ref, q_ref, k_ref, v_ref, u_ref):
    xb = x_ref[...].astype(BF16)
    d = q_ref.shape[1]
    q_ref[...] = _dot(xb, w_ref[:, 0 * d:1 * d])
    k_ref[...] = _dot(xb, w_ref[:, 1 * d:2 * d])
    v_ref[...] = _dot(xb, w_ref[:, 2 * d:3 * d])
    u_ref[...] = _dot(xb, w_ref[:, 3 * d:4 * d])


def _inproj(x2d, w_in_bf16, tm):
    rows, dm = x2d.shape
    d = w_in_bf16.shape[1] // 4
    out = jax.ShapeDtypeStruct((rows, d), F32)
    row_spec = pl.BlockSpec((tm, d), lambda i: (i, 0))
    vmem = 2 * (tm * dm * 4 + 4 * tm * d * 4) + 2 * w_in_bf16.size * 2 + 8 * tm * d * 4
    return pl.pallas_call(
        _inproj_body,
        grid=(rows // tm,),
        in_specs=[pl.BlockSpec((tm, dm), lambda i: (i, 0)), _const_spec(w_in_bf16.shape)],
        out_specs=[row_spec] * 4,
        out_shape=[out] * 4,
        compiler_params=_params(("parallel",), vmem),
        name="inproj",
    )(x2d, w_in_bf16)


def _attn_prompt_body(q_ref, k_ref, v_ref, o_ref, acc_ref, m_ref, l_ref):
    seq = q_ref.shape[0]
    blk = ATT_BLK
    scale = ATT_HEAD_DIM ** -0.5
    lane = lax.broadcasted_iota(jnp.int32, (blk, LANES), 1)
    head0 = lane < ATT_HEAD_DIM
    qi = lax.broadcasted_iota(jnp.int32, (blk, blk), 0)
    kj = lax.broadcasted_iota(jnp.int32, (blk, blk), 1)
    tri = kj <= qi
    mask_rest = jnp.concatenate([kj >= qi, tri], axis=1)

    def rows_of(start, d):
        return pl.ds(start, blk) if d == 1 else pl.ds(start, blk, stride=d)

    def attend(cfg, d, q_start, prev_start):
        rows = rows_of(q_start, d)
        q = q_ref[rows, :] * scale
        k2 = k_ref[rows, :].astype(BF16)
        v2 = v_ref[rows, :].astype(BF16)
        mask = tri
        if prev_start is not None:
            prow = rows_of(prev_start, d)
            k2 = jnp.concatenate([k_ref[prow, :].astype(BF16), k2], axis=0)
            v2 = jnp.concatenate([v_ref[prow, :].astype(BF16), v2], axis=0)
            mask = mask_rest
        parts = []
        for h in range(2):
            sel = head0 if h == 0 else jnp.logical_not(head0)
            qh = jnp.where(sel, q, 0.0).astype(BF16)
            s = jnp.where(mask, _dot_nt(qh, k2), NEG)
            m = jnp.max(s, axis=1, keepdims=True)
            p = jnp.exp(s - m)
            l = jnp.sum(p, axis=1, keepdims=True)
            parts.append((m, l, _dot(p.astype(BF16), v2)))
        m = jnp.where(head0, parts[0][0], parts[1][0])
        l = jnp.where(head0, parts[0][1], parts[1][1])
        num = jnp.where(head0, parts[0][2], parts[1][2])
        if cfg == 0:
            acc_ref[rows, :] = num
            m_ref[rows, :] = m
            l_ref[rows, :] = l
            return
        m_old = m_ref[rows, :]
        m_new = jnp.maximum(m_old, m)
        a = jnp.exp(m_old - m_new)
        b = jnp.exp(m - m_new)
        num = a * acc_ref[rows, :] + b * num
        l = a * l_ref[rows, :] + b * l
        if cfg == len(DILATED_CFGS) - 1:
            o_ref[rows, :] = num / l
        else:
            acc_ref[rows, :] = num
            m_ref[rows, :] = m_new
            l_ref[rows, :] = l

    for cfg, (window, d) in enumerate(DILATED_CFGS):
        span = d * blk
        nb = seq // span

        def residue(r, carry, cfg=cfg, d=d, span=span, nb=nb):
            attend(cfg, d, r, None)
            if nb > 1:
                def later(n, c):
                    attend(cfg, d, r + n * span, r + (n - 1) * span)
                    return c
                lax.fori_loop(1, nb, later, 0)
            return carry

        if d == 1:
            residue(0, 0)
        else:
            lax.fori_loop(0, d, residue, 0)


def _attn_prompt(q, k, v):
    bsz, seq, d_att = q.shape
    assert d_att % LANES == 0 and LANES == 2 * ATT_HEAD_DIM
    for window, d in DILATED_CFGS:
        assert window // d == ATT_BLK and seq % (d * ATT_BLK) == 0
    spec = pl.BlockSpec((None, seq, LANES), lambda b, h: (b, 0, h))
    blk_bytes = seq * LANES * 4
    return pl.pallas_call(
        _attn_prompt_body,
        grid=(bsz, d_att // LANES),
        in_specs=[spec, spec, spec],
        out_specs=spec,
        out_shape=jax.ShapeDtypeStruct(q.shape, F32),
        scratch_shapes=[pltpu.VMEM((seq, LANES), F32)] * 3,
        compiler_params=_params(("parallel", "parallel"), 11 * blk_bytes + 8 * 1024 * 1024),
        name="attn_prompt",
    )(q, k, v)


def _attn_sample_body(q_ref, kn_ref, vn_ref, kt_ref, ks_ref, vt_ref, vs_ref, o_ref):
    nb, t_new, d_att = q_ref.shape
    tail = kt_ref.shape[1]
    n_str = ks_ref.shape[1]
    nh = N_ATT_HEADS
    rows = t_new * nh
    scale = ATT_HEAD_DIM ** -0.5
    reach = ATT_BLK
    head_shift = ATT_HEAD_DIM.bit_length() - 1
    row_shift = nh.bit_length() - 1
    head_mask = (jnp.right_shift(lax.broadcasted_iota(jnp.int32, (nh, d_att), 1), head_shift)
                 == lax.broadcasted_iota(jnp.int32, (nh, d_att), 0))
    trow = jnp.right_shift(lax.broadcasted_iota(jnp.int32, (rows, 1), 0), row_shift)
    lane_tail = lax.broadcasted_iota(jnp.int32, (rows, tail), 1)
    mask_d4 = (lane_tail & 3) == trow
    lane_near = lax.broadcasted_iota(jnp.int32, (rows, reach), 1)
    mask_d1 = lane_near >= trow

    def one_sequence(b, carry):
        q = q_ref[b] * scale
        kn = kn_ref[b]
        vn = vn_ref[b]
        qexp = jnp.concatenate(
            [jnp.where(head_mask, jnp.broadcast_to(q[t:t + 1], (nh, d_att)), 0.0) for t in range(t_new)], axis=0)
        s_tail = _dot_nt(qexp.astype(BF16), kt_ref[b].astype(BF16))
        s_str = jnp.concatenate(
            [_dot_nt(qexp[t * nh:(t + 1) * nh].astype(BF16),
                     ks_ref[b, :, t * d_att:(t + 1) * d_att].astype(BF16))
             for t in range(t_new)], axis=0)
        s_new = [jnp.sum(qexp * kn[t:t + 1], axis=1, keepdims=True) for t in range(t_new)]

        def softmax_parts(main, main_mask, new_valid):
            sm = main if main_mask is None else jnp.where(main_mask, main, NEG)
            m = jnp.max(sm, axis=1, keepdims=True)
            for t in range(t_new):
                m = jnp.maximum(m, jnp.where(new_valid[t], s_new[t], NEG))
            p = jnp.exp(sm - m)
            den = jnp.sum(p, axis=1, keepdims=True)
            extra = jnp.zeros((rows, d_att), F32)
            for t in range(t_new):
                pe = jnp.where(new_valid[t], jnp.exp(s_new[t] - m), 0.0)
                den = den + pe
                extra = extra + pe * vn[t:t + 1]
            return m, p, den, extra

        only_self = [trow == t for t in range(t_new)]
        causal_new = [trow >= t for t in range(t_new)]
        vt = vt_ref[b].astype(BF16)
        m1, p1, den1, num1 = softmax_parts(s_tail[:, tail - reach:], mask_d1, causal_new)
        num1 = num1 + _dot(p1.astype(BF16), vt[tail - reach:])
        m4, p4, den4, num4 = softmax_parts(s_tail, mask_d4, only_self)
        num4 = num4 + _dot(p4.astype(BF16), vt)
        m16, p16, den16, num16 = softmax_parts(s_str, None, only_self)
        num16 = num16 + jnp.concatenate(
            [_dot(p16[t * nh:(t + 1) * nh].astype(BF16), vs_ref[b, :, t * d_att:(t + 1) * d_att].astype(BF16))
             for t in range(t_new)], axis=0)
        m = jnp.maximum(jnp.maximum(m1, m4), m16)
        w1, w4, w16 = jnp.exp(m1 - m), jnp.exp(m4 - m), jnp.exp(m16 - m)
        out = (w1 * num1 + w4 * num4 + w16 * num16) / (w1 * den1 + w4 * den4 + w16 * den16)
        o_ref[b] = jnp.concatenate(
            [jnp.sum(jnp.where(head_mask, out[t * nh:(t + 1) * nh], 0.0), axis=0, keepdims=True)
             for t in range(t_new)], axis=0)
        return carry

    lax.fori_loop(0, nb, one_sequence, 0)


def _attn_sample(q, k_new, v_new, cache_k, cache_v, nb):
    bsz, t_new, d_att = q.shape
    w_buf = cache_k.shape[1]
    d_max = DILATED_CFGS[-1][1]
    tail = DILATED_CFGS[1][0]
    assert w_buf == DILATED_CFGS[-1][0] and t_new <= DILATED_CFGS[1][1] and w_buf % tail == 0
    ck = cache_k.reshape(bsz, w_buf, d_att)
    cv = cache_v.reshape(bsz, w_buf, d_att)
    ck_str = cache_k.reshape(bsz, w_buf // d_max, d_max * d_att)
    cv_str = cache_v.reshape(bsz, w_buf // d_max, d_max * d_att)
    new_spec = pl.BlockSpec((nb, t_new, d_att), lambda i: (i, 0, 0))
    tail_spec = pl.BlockSpec((nb, tail, d_att), lambda i: (i, w_buf // tail - 1, 0))
    str_spec = pl.BlockSpec((nb, w_buf // d_max, t_new * d_att), lambda i: (i, 0, 0))
    vmem = 2 * 4 * nb * tail * d_att * 4 + 16 * 1024 * 1024
    return pl.pallas_call(
        _attn_sample_body,
        grid=(bsz // nb,),
        in_specs=[new_spec, new_spec, new_spec, tail_spec, str_spec, tail_spec, str_spec],
        out_specs=new_spec,
        out_shape=jax.ShapeDtypeStruct(q.shape, F32),
        compiler_params=_params(("parallel",), vmem),
        name="attn_sample",
    )(q, k_new, v_new, ck, ck_str, cv, cv_str)


def _ssm_prep_body(a_re_ref, a_im_ref, log_dt_ref, b_re_ref, b_im_ref, lam_re_ref, lam_im_ref,
                   bbar_re_ref, bbar_im_ref):
    ar = a_re_ref[...]
    ai = a_im_ref[...]
    dt = jnp.exp(log_dt_ref[...])
    mag = jnp.exp(dt * ar)
    lr = mag * jnp.cos(dt * ai)
    li = mag * jnp.sin(dt * ai)
    den = ar * ar + ai * ai
    nr, ni = lr - 1.0, li
    cr = (nr * ar + ni * ai) / den
    ci = (ni * ar - nr * ai) / den
    for h in range(b_re_ref.shape[0]):
        br, bi = b_re_ref[h], b_im_ref[h]
        bbar_re_ref[h] = cr * br - ci * bi
        bbar_im_ref[h] = cr * bi + ci * br
    pr, pi = lr, li
    for j in range(lam_re_ref.shape[0]):
        lam_re_ref[j] = pr
        lam_im_ref[j] = pi
        pr, pi = pr * lr - pi * li, pr * li + pi * lr


def _ssm_prep(a_re, a_im, log_dt, b_re, b_im, n_pow):
    g, p = a_re.shape
    h = b_re.shape[-1]
    b_re_t = jnp.transpose(b_re, (2, 0, 1))
    b_im_t = jnp.transpose(b_im, (2, 0, 1))
    gp = jax.ShapeDtypeStruct((n_pow, g, p), F32)
    hgp = jax.ShapeDtypeStruct((h, g, p), F32)
    return pl.pallas_call(
        _ssm_prep_body,
        out_shape=[gp, gp, hgp, hgp],
        name="ssm_prep",
    )(a_re, a_im, log_dt.reshape(g, 1), b_re_t, b_im_t)


def _ssm_layouts(lam_re, lam_im, bbar_re_t, bbar_im_t, c_re, c_im, d_skip):
    n_pow, g, p = lam_re.shape
    h = bbar_re_t.shape[0]
    nq = g // SSM_GB
    eye = jnp.eye(SSM_GB, dtype=bool)
    bb = jnp.stack([bbar_re_t, bbar_im_t], 0).reshape(2, h, nq, SSM_GB, p)
    bb = jnp.transpose(bb, (2, 3, 1, 0, 4))
    bmat = jnp.where(eye[None, :, None, None, :, None], bb[:, :, :, :, None, :], 0.0)
    bmat = bmat.reshape(nq, SSM_GB * h, 2 * SSM_GB * p).astype(BF16)
    def c_layout(c):
        cc = jnp.transpose(c.reshape(nq, SSM_GB, h, p), (0, 1, 3, 2))
        m = jnp.where(eye[None, :, None, :, None], cc[:, :, :, None, :], 0.0)
        return m.reshape(nq, SSM_GB * p, SSM_GB * h).astype(BF16)
    lam_re_q = lam_re.reshape(n_pow, nq, 1, SSM_GB * p)
    lam_im_q = lam_im.reshape(n_pow, nq, 1, SSM_GB * p)
    d_q = d_skip.reshape(nq, 1, SSM_GB * h)
    return bmat, c_layout(c_re), c_layout(c_im), lam_re_q, lam_im_q, d_q


def _ssm_body(u_ref, h0r_ref, h0i_ref, bmat_ref, cre_ref, cim_ref, lamr_ref, lami_ref, d_ref,
              y_ref, hfr_ref, hfi_ref, hloc_ref, hs_ref, *, t1, chained):
    rows = u_ref.shape[0]
    d_ssm = y_ref.shape[1] // t1
    nq = bmat_ref.shape[0]
    ns = lamr_ref.shape[-1]
    rt = SSM_ROW_TILE
    for q in range(nq):
        def u_lanes(i, q=q):
            return slice(i * d_ssm + q * LANES, i * d_ssm + (q + 1) * LANES)
        s_lanes = slice(q * ns, (q + 1) * ns)
        for i in range(t1):
            hloc_ref[i] = _dot(u_ref[:, u_lanes(i)].astype(BF16), bmat_ref[q])
        lr, li = lamr_ref[0, q], lami_ref[0, q]

        def local_scan(tile, carry, lr=lr, li=li):
            r = pl.ds(pl.multiple_of(tile * rt, rt), rt)
            hr = hloc_ref[0, r, 0:ns]
            hi = hloc_ref[0, r, ns:2 * ns]
            for i in range(1, t1):
                hr, hi = (lr * hr - li * hi + hloc_ref[i, r, 0:ns],
                          lr * hi + li * hr + hloc_ref[i, r, ns:2 * ns])
                hloc_ref[i, r, 0:ns] = hr
                hloc_ref[i, r, ns:2 * ns] = hi
            return carry

        lax.fori_loop(0, rows // rt, local_scan, 0)
        if chained:
            cr, ci = lamr_ref[t1 - 1, q], lami_ref[t1 - 1, q]

            def chain(c, g, cr=cr, ci=ci):
                gr, gi = g
                row = pl.ds(c, 1)
                hs_ref[row, 0:ns] = gr
                hs_ref[row, ns:2 * ns] = gi
                er = hloc_ref[t1 - 1, row, 0:ns]
                ei = hloc_ref[t1 - 1, row, ns:2 * ns]
                return cr * gr - ci * gi + er, cr * gi + ci * gr + ei

            gr, gi = lax.fori_loop(0, rows, chain, (h0r_ref[:, s_lanes], h0i_ref[:, s_lanes]))
            hfr_ref[:, s_lanes] = gr
            hfi_ref[:, s_lanes] = gi
        else:
            hs_ref[:, 0:ns] = h0r_ref[:, s_lanes]
            hs_ref[:, ns:2 * ns] = h0i_ref[:, s_lanes]

        def add_carry(tile, carry, q=q):
            r = pl.ds(pl.multiple_of(tile * rt, rt), rt)
            hr0 = hs_ref[r, 0:ns]
            hi0 = hs_ref[r, ns:2 * ns]
            for i in range(t1):
                pr, pi = lamr_ref[i, q], lami_ref[i, q]
                hloc_ref[i, r, 0:ns] = hloc_ref[i, r, 0:ns] + (pr * hr0 - pi * hi0)
                hloc_ref[i, r, ns:2 * ns] = hloc_ref[i, r, ns:2 * ns] + (pr * hi0 + pi * hr0)
            return carry

        lax.fori_loop(0, rows // rt, add_carry, 0)
        if not chained:
            hfr_ref[:, s_lanes] = hloc_ref[t1 - 1, :, 0:ns]
            hfi_ref[:, s_lanes] = hloc_ref[t1 - 1, :, ns:2 * ns]
        for i in range(t1):
            y_ref[:, u_lanes(i)] = (_dot(hloc_ref[i, :, 0:ns].astype(BF16), cre_ref[q])
                                    - _dot(hloc_ref[i, :, ns:2 * ns].astype(BF16), cim_ref[q])
                                    + d_ref[q] * u_ref[:, u_lanes(i)])


def _ssm(u_chunks, h0_re, h0_im, layouts, t1, rows, chained):
    bmat, cre, cim, lam_re_q, lam_im_q, d_q = layouts
    lam_re_q, lam_im_q = lam_re_q[:t1], lam_im_q[:t1]
    n_blocks, rows_h, n_state = h0_re.shape
    width = u_chunks.shape[1]
    assert rows % SSM_ROW_TILE == 0 and u_chunks.shape[0] == n_blocks * rows
    ns2 = bmat.shape[2]
    u_spec = pl.BlockSpec((rows, width), lambda b: (b, 0))
    h_spec = pl.BlockSpec((None, rows_h, n_state), lambda b: (b, 0, 0))
    consts = [bmat, cre, cim, lam_re_q, lam_im_q, d_q]
    vmem = (4 * rows * width * 4 + (t1 + 1) * rows * ns2 * 4 + 2 * sum(c.size * c.dtype.itemsize for c in consts)
            + 8 * rows_h * n_state * 4 + 8 * 1024 * 1024)
    h_out = jax.ShapeDtypeStruct(h0_re.shape, F32)
    return pl.pallas_call(
        functools.partial(_ssm_body, t1=t1, chained=chained),
        grid=(n_blocks,),
        in_specs=[u_spec, h_spec, h_spec] + [_const_spec(c.shape) for c in consts],
        out_specs=[u_spec, h_spec, h_spec],
        out_shape=[jax.ShapeDtypeStruct(u_chunks.shape, F32), h_out, h_out],
        scratch_shapes=[pltpu.VMEM((t1, rows, ns2), F32), pltpu.VMEM((rows, ns2), F32)],
        compiler_params=_params(("parallel",), vmem),
        name="ssm_chained" if chained else "ssm_rows",
    )(u_chunks, h0_re, h0_im, *consts)


def _mix_body(x_ref, oatt_ref, y_ref, wglu_ref, bglu_ref, gatt_ref, gssm_ref, wout_ref, lng_ref, lnb_ref,
              o_ref, *, alpha):
    d_att = oatt_ref.shape[1]
    g = jax.nn.gelu(y_ref[...])
    z = g * jax.nn.sigmoid(_dot(g.astype(BF16), wglu_ref[...]) + bglu_ref[...])
    ra = _rms_norm(oatt_ref[...], gatt_ref[...]).astype(BF16)
    rz = _rms_norm(z, gssm_ref[...]).astype(BF16)
    mixed = _dot(ra, wout_ref[0:d_att, :]) + _dot(rz, wout_ref[d_att:, :])
    o_ref[...] = _layer_norm(alpha * x_ref[...] + mixed, lng_ref[...], lnb_ref[...])


def _mix(x2d, o_att, y_ssm, w_glu, b_glu, g_att, g_ssm, w_out, ln_g, ln_b, alpha, tm):
    rows, dm = x2d.shape
    d_att, d_ssm = o_att.shape[1], y_ssm.shape[1]
    consts = [w_glu, b_glu.reshape(1, -1), g_att.reshape(1, -1), g_ssm.reshape(1, -1), w_out,
              ln_g.reshape(1, -1), ln_b.reshape(1, -1)]
    row = lambda width: pl.BlockSpec((tm, width), lambda i: (i, 0))
    vmem = 2 * tm * (2 * dm + d_att + d_ssm) * 4 + 4 * (w_glu.size + w_out.size) + 12 * tm * dm * 4
    return pl.pallas_call(
        functools.partial(_mix_body, alpha=alpha),
        grid=(rows // tm,),
        in_specs=[row(dm), row(d_att), row(d_ssm)] + [_const_spec(c.shape) for c in consts],
        out_specs=row(dm),
        out_shape=jax.ShapeDtypeStruct(x2d.shape, F32),
        compiler_params=_params(("parallel",), vmem),
        name="mix_out_ln1",
    )(x2d, o_att, y_ssm, *consts)


def _memkv_body(m_ref, wk_ref, wv_ref, k_ref, v_ref):
    mb = m_ref[...].astype(BF16)
    k_ref[...] = _dot(mb, wk_ref[...])
    v_ref[...] = _dot(mb, wv_ref[...])


def _memkv(mem2d, wk, wv, tm):
    rows, dm = mem2d.shape
    row = pl.BlockSpec((tm, dm), lambda i: (i, 0))
    out = jax.ShapeDtypeStruct((rows, wk.shape[1]), F32)
    vmem = 2 * 3 * tm * dm * 4 + 4 * (wk.size + wv.size) + 4 * tm * dm * 4
    return pl.pallas_call(
        _memkv_body,
        grid=(rows // tm,),
        in_specs=[row, _const_spec(wk.shape), _const_spec(wv.shape)],
        out_specs=[row, row],
        out_shape=[out, out],
        compiler_params=_params(("parallel",), vmem),
        name="mem_kv",
    )(mem2d, wk, wv)


def _memattn_body(x_ref, mk_ref, mv_ref, wq_ref, wo_ref, lng_ref, lnb_ref, o_ref, q_scr, a_scr, *, alpha, tl):
    nseq = mk_ref.shape[0]
    dm = x_ref.shape[1]
    hd = dm // N_MEM_HEADS
    x = x_ref[...]
    q_scr[...] = _dot(x.astype(BF16), wq_ref[...]) * (hd ** -0.5)
    for j in range(nseq):
        r = slice(j * tl, (j + 1) * tl)
        for h in range(N_MEM_HEADS):
            c = slice(h * hd, (h + 1) * hd)
            s = _dot_nt(q_scr[r, c].astype(BF16), mk_ref[j, :, c].astype(BF16))
            p = jnp.exp(s - jnp.max(s, axis=1, keepdims=True))
            l = jnp.sum(p, axis=1, keepdims=True)
            a_scr[r, c] = _dot(p.astype(BF16), mv_ref[j, :, c].astype(BF16)) / l
    att = _dot(a_scr[...].astype(BF16), wo_ref[...])
    o_ref[...] = _layer_norm(alpha * x + att, lng_ref[...], lnb_ref[...])


def _memattn(x2d, mem_k, mem_v, wq, wo, ln_g, ln_b, alpha, nseq, tl):
    rows, dm = x2d.shape
    n_mem = mem_k.shape[1]
    per_mem = rows // mem_k.shape[0]
    steps_per_mem = per_mem // tl if nseq == 1 else 1
    tm = nseq * tl
    row = pl.BlockSpec((tm, dm), lambda i: (i, 0))
    mem_spec = pl.BlockSpec((nseq, n_mem, dm), lambda i: (i // steps_per_mem, 0, 0))
    consts = [wq, wo, ln_g.reshape(1, -1), ln_b.reshape(1, -1)]
    vmem = 2 * 2 * tm * dm * 4 + 2 * 2 * nseq * n_mem * dm * 4 + 4 * (wq.size + wo.size) + 10 * tm * dm * 4
    return pl.pallas_call(
        functools.partial(_memattn_body, alpha=alpha, tl=tl),
        grid=(rows // tm,),
        in_specs=[row, mem_spec, mem_spec] + [_const_spec(c.shape) for c in consts],
        out_specs=row,
        out_shape=jax.ShapeDtypeStruct(x2d.shape, F32),
        scratch_shapes=[pltpu.VMEM((tm, dm), F32), pltpu.VMEM((tm, dm), F32)],
        compiler_params=_params(("parallel",), vmem),
        name="mem_attn_ln2",
    )(x2d, mem_k, mem_v, *consts)


def _ffn_body(x_ref, wg_ref, wu_ref, wd_ref, lng_ref, lnb_ref, o_ref, acc_ref, *, alpha, tf):
    x = x_ref[...]
    xb = x.astype(BF16)
    d_ff = wg_ref.shape[1]
    for c in range(d_ff // tf):
        cols = slice(c * tf, (c + 1) * tf)
        hid = (jax.nn.silu(_dot(xb, wg_ref[:, cols])) * _dot(xb, wu_ref[:, cols])).astype(BF16)
        part = _dot(hid, wd_ref[cols, :])
        if c == 0:
            acc_ref[...] = part
        else:
            acc_ref[...] += part
    o_ref[...] = _layer_norm(alpha * x + acc_ref[...], lng_ref[...], lnb_ref[...])


def _ffn(x2d, wg, wu, wd, ln_g, ln_b, alpha, tm, tf):
    rows, dm = x2d.shape
    d_ff = wg.shape[1]
    assert d_ff % tf == 0 and tf % LANES == 0
    row = pl.BlockSpec((tm, dm), lambda i: (i, 0))
    consts = [wg, wu, wd, ln_g.reshape(1, -1), ln_b.reshape(1, -1)]
    vmem = 2 * 2 * tm * dm * 4 + 4 * 3 * wg.size + tm * dm * 4 + 6 * tm * max(tf, dm) * 4
    return pl.pallas_call(
        functools.partial(_ffn_body, alpha=alpha, tf=tf),
        grid=(rows // tm,),
        in_specs=[row] + [_const_spec(c.shape) for c in consts],
        out_specs=row,
        out_shape=jax.ShapeDtypeStruct(x2d.shape, F32),
        scratch_shapes=[pltpu.VMEM((tm, dm), F32)],
        compiler_params=_params(("parallel",), vmem),
        name="swiglu_ln3",
    )(x2d, *consts)


PROMPT_CHUNK = 8
ROW_TILE = 512
FFN_COL_TILE = 256
SAMPLE_ATTN_SEQS = 4
SAMPLE_MEM_SEQS = 4


def kernel(x_prompt, x_sample, cache_win_k, cache_win_v, state_ssm_re, state_ssm_im, cache_mem_k, cache_mem_v, mem_prompt, w_in, g_att, g_ssm, ssm_a_re, ssm_a_im, ssm_log_dt, ssm_b_re, ssm_b_im, ssm_c_re, ssm_c_im, ssm_d, w_glu, b_glu, w_out, ln1_g, ln1_b, w_mem_q, w_mem_k, w_mem_v, w_mem_o, ln2_g, ln2_b, w_gate, w_up, w_down, ln3_g, ln3_b):
    depth = w_in.shape[0]
    bp, seq, dm = x_prompt.shape
    bs, t_new, _ = x_sample.shape
    n_groups, n_state = ssm_a_re.shape[1:]
    n_mem = mem_prompt.shape[1]
    alpha = (2 * depth) ** 0.25
    keep = min(DILATED_CFGS[-1][0], seq)
    assert keep == seq
    n_chunks = seq // PROMPT_CHUNK

    y_p = x_prompt.reshape(bp * seq, dm)
    y_s = x_sample.reshape(bs * t_new, dm)
    mem2d = mem_prompt.reshape(bp * n_mem, dm)
    outs = [[] for _ in range(10)]
    for l in range(depth):
        bf = lambda w: w[l].astype(BF16)
        w_in_l, w_glu_l, w_out_l = bf(w_in), bf(w_glu), bf(w_out)
        wq_l, wk_l, wv_l, wo_l = bf(w_mem_q), bf(w_mem_k), bf(w_mem_v), bf(w_mem_o)
        wg_l, wu_l, wd_l = bf(w_gate), bf(w_up), bf(w_down)
        lam_re, lam_im, bbar_re_t, bbar_im_t = _ssm_prep(
            ssm_a_re[l], ssm_a_im[l], ssm_log_dt[l], ssm_b_re[l], ssm_b_im[l], max(PROMPT_CHUNK, t_new))
        layouts = _ssm_layouts(lam_re, lam_im, bbar_re_t, bbar_im_t, ssm_c_re[l], ssm_c_im[l], ssm_d[l])
        mix_w = (w_glu_l, b_glu[l], g_att[l], g_ssm[l], w_out_l, ln1_g[l], ln1_b[l])

        q, k, v, u = _inproj(y_p, w_in_l, ROW_TILE)
        d_att = q.shape[1]
        o_att = _attn_prompt(q.reshape(bp, seq, d_att), k.reshape(bp, seq, d_att), v.reshape(bp, seq, d_att))
        zeros = jnp.zeros((bp, 1, n_groups * n_state), F32)
        y_ssm, hr_p, hi_p = _ssm(u.reshape(bp * n_chunks, PROMPT_CHUNK * u.shape[1]), zeros, zeros, layouts,
                                 PROMPT_CHUNK, n_chunks, True)
        x1 = _mix(y_p, o_att.reshape(bp * seq, d_att), y_ssm.reshape(bp * seq, -1), *mix_w, alpha, ROW_TILE)
        mk_p, mv_p = _memkv(mem2d, wk_l, wv_l, ROW_TILE)
        x2 = _memattn(x1, mk_p.reshape(bp, n_mem, dm), mv_p.reshape(bp, n_mem, dm), wq_l, wo_l,
                      ln2_g[l], ln2_b[l], alpha, 1, ROW_TILE)
        y_p = _ffn(x2, wg_l, wu_l, wd_l, ln3_g[l], ln3_b[l], alpha, ROW_TILE, FFN_COL_TILE)

        qs, ks, vs, us = _inproj(y_s, w_in_l, bs * t_new)
        shp = (bs, t_new, d_att)
        o_att_s = _attn_sample(qs.reshape(shp), ks.reshape(shp), vs.reshape(shp), cache_win_k[l], cache_win_v[l],
                               SAMPLE_ATTN_SEQS)
        y_ssm_s, hr_s, hi_s = _ssm(us.reshape(bs, t_new * us.shape[1]),
                                   state_ssm_re[l].reshape(1, bs, -1), state_ssm_im[l].reshape(1, bs, -1),
                                   layouts, t_new, bs, False)
        x1s = _mix(y_s, o_att_s.reshape(bs * t_new, d_att), y_ssm_s.reshape(bs * t_new, -1), *mix_w, alpha,
                   bs * t_new)
        x2s = _memattn(x1s, cache_mem_k[l].reshape(bs, n_mem, dm), cache_mem_v[l].reshape(bs, n_mem, dm),
                       wq_l, wo_l, ln2_g[l], ln2_b[l], alpha, SAMPLE_MEM_SEQS, t_new)
        y_s = _ffn(x2s, wg_l, wu_l, wd_l, ln3_g[l], ln3_b[l], alpha, bs * t_new, FFN_COL_TILE)

        head_shape = (N_ATT_HEADS, ATT_HEAD_DIM)
        state_shape = (n_groups, n_state)
        mem_shape = (bp, n_mem, N_MEM_HEADS, dm // N_MEM_HEADS)
        for lst, val in zip(outs, (
                k.reshape(bp, seq, *head_shape)[:, seq - keep:], v.reshape(bp, seq, *head_shape)[:, seq - keep:],
                ks.reshape(bs, t_new, *head_shape), vs.reshape(bs, t_new, *head_shape),
                hr_p.reshape(bp, *state_shape), hi_p.reshape(bp, *state_shape),
                hr_s.reshape(bs, *state_shape), hi_s.reshape(bs, *state_shape),
                mk_p.reshape(mem_shape), mv_p.reshape(mem_shape))):
            lst.append(val)
    return (y_p.reshape(bp, seq, dm), y_s.reshape(bs, t_new, dm)) + tuple(jnp.stack(o) for o in outs)
```

```python
import functools
import math

import jax
import jax.numpy as jnp
from jax import lax
from jax.experimental import pallas as pl
from jax.experimental.pallas import tpu as pltpu

F32 = jnp.float32
BF16 = jnp.bfloat16

N_ATT_HEADS = 8
ATT_HEAD_DIM = 64
D_ATT = N_ATT_HEADS * ATT_HEAD_DIM
DILATED_CFGS = ((128, 1), (512, 4), (2048, 16))
ATT_BLK = 128
ATT_GROUP = 4
SSM_CH = 16
SSM_STATE = 64
N_MEM_HEADS = 4
EPS = 1e-5
NEG = -1e30

LANES = 128
V7X_VMEM_CAP_BYTES = 56 * 1024 * 1024

SSM_GB = LANES // SSM_CH
SSM_ROW_TILE = 32


def _params(sem, vmem_bytes):
    return pltpu.CompilerParams(
        dimension_semantics=sem,
        vmem_limit_bytes=int(min(max(vmem_bytes, 16 * 1024 * 1024), V7X_VMEM_CAP_BYTES)),
    )


def _dot(a, b):
    return jnp.dot(a, b, preferred_element_type=F32)


def _dot_nt(a, b):
    return lax.dot_general(a, b, (((1,), (1,)), ((), ())), preferred_element_type=F32)


def _layer_norm(x, g, b):
    mu = jnp.mean(x, axis=-1, keepdims=True)
    xc = x - mu
    var = jnp.mean(xc * xc, axis=-1, keepdims=True)
    return xc * lax.rsqrt(var + EPS) * g + b


def _rms_norm(x, g):
    return x * lax.rsqrt(jnp.mean(x * x, axis=-1, keepdims=True) + EPS) * g


def _const_spec(shape):
    n = len(shape)
    return pl.BlockSpec(shape, lambda *_: (0,) * n)


def _inproj_body(x_ref, w_ref, q_ref, k_ref, v_ref, u_ref):
    xb = x_ref[...].astype(BF16)
    d = q_ref.shape[1]
    q_ref[...] = _dot(xb, w_ref[:, 0 * d:1 * d])
    k_ref[...] = _dot(xb, w_ref[:, 1 * d:2 * d])
    v_ref[...] = _dot(xb, w_ref[:, 2 * d:3 * d])
    u_ref[...] = _dot(xb, w_ref[:, 3 * d:4 * d])


def _inproj(x2d, w_in_bf16, tm):
    rows, dm = x2d.shape
    d = w_in_bf16.shape[1] // 4
    out = jax.ShapeDtypeStruct((rows, d), F32)
    row_spec = pl.BlockSpec((tm, d), lambda i: (i, 0))
    vmem = 2 * (tm * dm * 4 + 4 * tm * d * 4) + 2 * w_in_bf16.size * 2 + 8 * tm * d * 4
    return pl.pallas_call(
        _inproj_body,
        grid=(rows // tm,),
        in_specs=[pl.BlockSpec((tm, dm), lambda i: (i, 0)), _const_spec(w_in_bf16.shape)],
        out_specs=[row_spec] * 4,
        out_shape=[out] * 4,
        compiler_params=_params(("parallel",), vmem),
        name="inproj",
    )(x2d, w_in_bf16)


def _attn_prompt_body(q_ref, k_ref, v_ref, o_ref, acc_ref, m_ref, l_ref):
    seq = q_ref.shape[0]
    blk = ATT_BLK
    scale = ATT_HEAD_DIM ** -0.5
    lane = lax.broadcasted_iota(jnp.int32, (blk, LANES), 1)
    head0 = lane < ATT_HEAD_DIM
    qi = lax.broadcasted_iota(jnp.int32, (blk, blk), 0)
    kj = lax.broadcasted_iota(jnp.int32, (blk, blk), 1)
    tri = kj <= qi
    mask_rest = jnp.concatenate([kj >= qi, tri], axis=1)

    def rows_of(start, d):
        return pl.ds(start, blk) if d == 1 else pl.ds(start, blk, stride=d)

    def attend(cfg, d, q_start, prev_start):
        rows = rows_of(q_start, d)
        q = q_ref[rows, :] * scale
        k2 = k_ref[rows, :].astype(BF16)
        v2 = v_ref[rows, :].astype(BF16)
        mask = tri
        if prev_start is not None:
            prow = rows_of(prev_start, d)
            k2 = jnp.concatenate([k_ref[prow, :].astype(BF16), k2], axis=0)
            v2 = jnp.concatenate([v_ref[prow, :].astype(BF16), v2], axis=0)
            mask = mask_rest
        parts = []
        for h in range(2):
            sel = head0 if h == 0 else jnp.logical_not(head0)
            qh = jnp.where(sel, q, 0.0).astype(BF16)
            s = jnp.where(mask, _dot_nt(qh, k2), NEG)
            m = jnp.max(s, axis=1, keepdims=True)
            p = jnp.exp(s - m)
            l = jnp.sum(p, axis=1, keepdims=True)
            parts.append((m, l, _dot(p.astype(BF16), v2)))
        m = jnp.where(head0, parts[0][0], parts[1][0])
        l = jnp.where(head0, parts[0][1], parts[1][1])
        num = jnp.where(head0, parts[0][2], parts[1][2])
        if cfg == 0:
            acc_ref[rows, :] = num
            m_ref[rows, :] = m
            l_ref[rows, :] = l
            return
        m_old = m_ref[rows, :]
        m_new = jnp.maximum(m_old, m)
        a = jnp.exp(m_old - m_new)
        b = jnp.exp(m - m_new)
        num = a * acc_ref[rows, :] + b * num
        l = a * l_ref[rows, :] + b * l
        if cfg == len(DILATED_CFGS) - 1:
            o_ref[rows, :] = num / l
        else:
            acc_ref[rows, :] = num
            m_ref[rows, :] = m_new
            l_ref[rows, :] = l

    grp = ATT_GROUP
    for cfg, (window, d) in enumerate(DILATED_CFGS):
        span = d * blk
        nb = seq // span
        if d >= grp:
            def residue_group(g, carry, cfg=cfg, d=d, span=span, nb=nb):
                for j in range(grp):
                    attend(cfg, d, g * grp + j, None)
                if nb > 1:
                    def later(n, c):
                        for j in range(grp):
                            r = g * grp + j
                            attend(cfg, d, r + n * span, r + (n - 1) * span)
                        return c
                    lax.fori_loop(1, nb, later, 0)
                return carry

            lax.fori_loop(0, d // grp, residue_group, 0)
        else:
            assert d == 1 and nb % grp == 0
            attend(cfg, d, 0, None)
            for n in range(1, grp):
                attend(cfg, d, n * span, (n - 1) * span)

            def block_group(g, carry, cfg=cfg, d=d, span=span):
                for j in range(grp):
                    n = g * grp + j
                    attend(cfg, d, n * span, (n - 1) * span)
                return carry

            lax.fori_loop(1, nb // grp, block_group, 0)


def _attn_prompt(q, k, v):
    bsz, seq, d_att = q.shape
    assert d_att % LANES == 0 and LANES == 2 * ATT_HEAD_DIM
    for window, d in DILATED_CFGS:
        assert window // d == ATT_BLK and seq % (d * ATT_BLK) == 0
    spec = pl.BlockSpec((None, seq, LANES), lambda b, h: (b, 0, h))
    blk_bytes = seq * LANES * 4
    return pl.pallas_call(
        _attn_prompt_body,
        grid=(bsz, d_att // LANES),
        in_specs=[spec, spec, spec],
        out_specs=spec,
        out_shape=jax.ShapeDtypeStruct(q.shape, F32),
        scratch_shapes=[pltpu.VMEM((seq, LANES), F32)] * 3,
        compiler_params=_params(("parallel", "parallel"), 11 * blk_bytes + 8 * 1024 * 1024),
        name="attn_prompt",
    )(q, k, v)


def _attn_window_body(q_ref, kn_ref, vn_ref, kt_ref, ks_ref, vt_ref, vs_ref, o_ref):
    nb, t_new, nh, hd = q_ref.shape
    tail = kt_ref.shape[1]
    n_str = ks_ref.shape[1]
    rows = t_new * nh
    reach = ATT_BLK
    scale = hd ** -0.5
    hbits = nh.bit_length() - 1
    row_id = lax.broadcasted_iota(jnp.int32, (rows, 1), 0)
    hrow = row_id & (nh - 1)
    trow = jnp.right_shift(row_id, hbits)

    def cols(n):
        c = lax.broadcasted_iota(jnp.int32, (rows, n * nh), 1)
        return c, jnp.right_shift(c, hbits), (c & (nh - 1)) == hrow

    c_tail, _, _ = cols(tail)
    mask_d4 = (c_tail & (DILATED_CFGS[1][1] * nh - 1)) == row_id
    _, r_near, head_near = cols(reach)
    mask_d1 = jnp.where(head_near, r_near, -1) >= trow
    _, _, mask_d16 = cols(n_str)
    c_new, t_col, head_new = cols(t_new)
    new_self = c_new == row_id
    new_causal = jnp.where(head_new, t_col, t_new) <= trow

    def one_sequence(b, carry):
        q = (q_ref[b] * scale).reshape(rows, hd)
        qb = q.astype(BF16)
        kn = kn_ref[b].reshape(rows, hd).astype(BF16)
        vn = vn_ref[b].reshape(rows, hd).astype(BF16)
        kt = kt_ref[b].reshape(tail * nh, hd).astype(BF16)
        vt = vt_ref[b].reshape(tail * nh, hd).astype(BF16)
        s_tail = _dot_nt(qb, kt)
        s_new = _dot_nt(qb, kn)
        s_str = jnp.concatenate(
            [_dot_nt(q[t * nh:(t + 1) * nh].astype(BF16), ks_ref[b, :, t].reshape(n_str * nh, hd).astype(BF16))
             for t in range(t_new)], axis=0)

        def softmax_parts(main, main_mask, new_mask):
            sm = jnp.where(main_mask, main, NEG)
            sn = jnp.where(new_mask, s_new, NEG)
            m = jnp.maximum(jnp.max(sm, axis=1, keepdims=True), jnp.max(sn, axis=1, keepdims=True))
            p = jnp.exp(sm - m)
            pn = jnp.exp(sn - m)
            den = jnp.sum(p, axis=1, keepdims=True) + jnp.sum(pn, axis=1, keepdims=True)
            return m, p, den, _dot(pn.astype(BF16), vn)

        near = (tail - reach) * nh
        m1, p1, den1, num1 = softmax_parts(s_tail[:, near:], mask_d1, new_causal)
        num1 = num1 + _dot(p1.astype(BF16), vt[near:])
        m4, p4, den4, num4 = softmax_parts(s_tail, mask_d4, new_self)
        num4 = num4 + _dot(p4.astype(BF16), vt)
        m16, p16, den16, num16 = softmax_parts(s_str, mask_d16, new_self)
        num16 = num16 + jnp.concatenate(
            [_dot(p16[t * nh:(t + 1) * nh].astype(BF16), vs_ref[b, :, t].reshape(n_str * nh, hd).astype(BF16))
             for t in range(t_new)], axis=0)
        m = jnp.maximum(jnp.maximum(m1, m4), m16)
        w1, w4, w16 = jnp.exp(m1 - m), jnp.exp(m4 - m), jnp.exp(m16 - m)
        out = (w1 * num1 + w4 * num4 + w16 * num16) / (w1 * den1 + w4 * den4 + w16 * den16)
        o_ref[b] = out.reshape(t_new, nh, hd)
        return carry

    lax.fori_loop(0, nb, one_sequence, 0)


def _attn_window(q, k_new, v_new, cache_k, cache_v, layer, nb):
    bsz, t_new, nh, hd = q.shape
    depth, _, w_buf = cache_k.shape[:3]
    d_max = DILATED_CFGS[-1][1]
    tail = DILATED_CFGS[1][0]
    assert w_buf == DILATED_CFGS[-1][0] and t_new <= DILATED_CFGS[1][1] and w_buf % tail == 0
    split = (depth, bsz, w_buf // d_max, d_max, nh, hd)
    new_spec = pl.BlockSpec((nb, t_new, nh, hd), lambda i: (i, 0, 0, 0))
    tail_spec = pl.BlockSpec((None, nb, tail, nh, hd), lambda i: (layer, i, w_buf // tail - 1, 0, 0))
    str_spec = pl.BlockSpec((None, nb, w_buf // d_max, t_new, nh, hd), lambda i: (layer, i, 0, 0, 0, 0))
    tile_bytes = 8 * LANES * 4
    vmem = 2 * 2 * nb * (tail + w_buf // d_max * t_new) * tile_bytes + 24 * 1024 * 1024
    return pl.pallas_call(
        _attn_window_body,
        grid=(bsz // nb,),
        in_specs=[new_spec, new_spec, new_spec, tail_spec, str_spec, tail_spec, str_spec],
        out_specs=new_spec,
        out_shape=jax.ShapeDtypeStruct(q.shape, F32),
        compiler_params=_params(("parallel",), vmem),
        name="attn_window",
    )(q, k_new, v_new, cache_k, cache_k.reshape(split), cache_v, cache_v.reshape(split))


def _ssm_prep_body(a_re_ref, a_im_ref, log_dt_ref, b_re_ref, b_im_ref, lam_re_ref, lam_im_ref,
                   bbar_re_ref, bbar_im_ref):
    ar = a_re_ref[...]
    ai = a_im_ref[...]
    dt = jnp.exp(log_dt_ref[...])
    mag = jnp.exp(dt * ar)
    lr = mag * jnp.cos(dt * ai)
    li = mag * jnp.sin(dt * ai)
    den = ar * ar + ai * ai
    nr, ni = lr - 1.0, li
    cr = (nr * ar + ni * ai) / den
    ci = (ni * ar - nr * ai) / den
    for h in range(b_re_ref.shape[0]):
        br, bi = b_re_ref[h], b_im_ref[h]
        bbar_re_ref[h] = cr * br - ci * bi
        bbar_im_ref[h] = cr * bi + ci * br
    pr, pi = lr, li
    for j in range(lam_re_ref.shape[0]):
        lam_re_ref[j] = pr
        lam_im_ref[j] = pi
        pr, pi = pr * lr - pi * li, pr * li + pi * lr


def _ssm_prep(a_re, a_im, log_dt, b_re, b_im, n_pow):
    g, p = a_re.shape
    h = b_re.shape[-1]
    b_re_t = jnp.transpose(b_re, (2, 0, 1))
    b_im_t = jnp.transpose(b_im, (2, 0, 1))
    gp = jax.ShapeDtypeStruct((n_pow, g, p), F32)
    hgp = jax.ShapeDtypeStruct((h, g, p), F32)
    return pl.pallas_call(
        _ssm_prep_body,
        out_shape=[gp, gp, hgp, hgp],
        name="ssm_prep",
    )(a_re, a_im, log_dt.reshape(g, 1), b_re_t, b_im_t)


def _ssm_layouts(lam_re, lam_im, bbar_re_t, bbar_im_t, c_re, c_im, d_skip):
    n_pow, g, p = lam_re.shape
    h = bbar_re_t.shape[0]
    nq = g // SSM_GB
    eye = jnp.eye(SSM_GB, dtype=bool)
    bb = jnp.stack([bbar_re_t, bbar_im_t], 0).reshape(2, h, nq, SSM_GB, p)
    bb = jnp.transpose(bb, (2, 3, 1, 0, 4))
    bmat = jnp.where(eye[None, :, None, None, :, None], bb[:, :, :, :, None, :], 0.0)
    bmat = bmat.reshape(nq, SSM_GB * h, 2 * SSM_GB * p).astype(BF16)
    def c_layout(c):
        cc = jnp.transpose(c.reshape(nq, SSM_GB, h, p), (0, 1, 3, 2))
        m = jnp.where(eye[None, :, None, :, None], cc[:, :, :, None, :], 0.0)
        return m.reshape(nq, SSM_GB * p, SSM_GB * h).astype(BF16)
    lam_re_q = lam_re.reshape(n_pow, nq, 1, SSM_GB * p)
    lam_im_q = lam_im.reshape(n_pow, nq, 1, SSM_GB * p)
    d_q = d_skip.reshape(nq, 1, SSM_GB * h)
    return bmat, c_layout(c_re), c_layout(c_im), lam_re_q, lam_im_q, d_q


def _ssm_body(u_ref, h0r_ref, h0i_ref, bmat_ref, cre_ref, cim_ref, lamr_ref, lami_ref, d_ref,
              y_ref, hfr_ref, hfi_ref, hloc_ref, hs_ref, *, t1, chained):
    rows = u_ref.shape[0]
    d_ssm = y_ref.shape[1] // t1
    nq = bmat_ref.shape[0]
    ns = lamr_ref.shape[-1]
    rt = SSM_ROW_TILE
    for q in range(nq):
        def u_lanes(i, q=q):
            return slice(i * d_ssm + q * LANES, i * d_ssm + (q + 1) * LANES)
        s_lanes = slice(q * ns, (q + 1) * ns)
        for i in range(t1):
            hloc_ref[i] = _dot(u_ref[:, u_lanes(i)].astype(BF16), bmat_ref[q])
        lr, li = lamr_ref[0, q], lami_ref[0, q]

        def local_scan(tile, carry, lr=lr, li=li):
            r = pl.ds(pl.multiple_of(tile * rt, rt), rt)
            hr = hloc_ref[0, r, 0:ns]
            hi = hloc_ref[0, r, ns:2 * ns]
            for i in range(1, t1):
                hr, hi = (lr * hr - li * hi + hloc_ref[i, r, 0:ns],
                          lr * hi + li * hr + hloc_ref[i, r, ns:2 * ns])
                hloc_ref[i, r, 0:ns] = hr
                hloc_ref[i, r, ns:2 * ns] = hi
            return carry

        lax.fori_loop(0, rows // rt, local_scan, 0)
        if chained:
            cr, ci = lamr_ref[t1 - 1, q], lami_ref[t1 - 1, q]

            def chain(c, g, cr=cr, ci=ci):
                gr, gi = g
                row = pl.ds(c, 1)
                hs_ref[row, 0:ns] = gr
                hs_ref[row, ns:2 * ns] = gi
                er = hloc_ref[t1 - 1, row, 0:ns]
                ei = hloc_ref[t1 - 1, row, ns:2 * ns]
                return cr * gr - ci * gi + er, cr * gi + ci * gr + ei

            gr, gi = lax.fori_loop(0, rows, chain, (h0r_ref[:, s_lanes], h0i_ref[:, s_lanes]))
            hfr_ref[:, s_lanes] = gr
            hfi_ref[:, s_lanes] = gi
        else:
            hs_ref[:, 0:ns] = h0r_ref[:, s_lanes]
            hs_ref[:, ns:2 * ns] = h0i_ref[:, s_lanes]

        def add_carry(tile, carry, q=q):
            r = pl.ds(pl.multiple_of(tile * rt, rt), rt)
            hr0 = hs_ref[r, 0:ns]
            hi0 = hs_ref[r, ns:2 * ns]
            for i in range(t1):
                pr, pi = lamr_ref[i, q], lami_ref[i, q]
                hloc_ref[i, r, 0:ns] = hloc_ref[i, r, 0:ns] + (pr * hr0 - pi * hi0)
                hloc_ref[i, r, ns:2 * ns] = hloc_ref[i, r, ns:2 * ns] + (pr * hi0 + pi * hr0)
            return carry

        lax.fori_loop(0, rows // rt, add_carry, 0)
        if not chained:
            hfr_ref[:, s_lanes] = hloc_ref[t1 - 1, :, 0:ns]
            hfi_ref[:, s_lanes] = hloc_ref[t1 - 1, :, ns:2 * ns]
        for i in range(t1):
            y_ref[:, u_lanes(i)] = (_dot(hloc_ref[i, :, 0:ns].astype(BF16), cre_ref[q])
                                    - _dot(hloc_ref[i, :, ns:2 * ns].astype(BF16), cim_ref[q])
                                    + d_ref[q] * u_ref[:, u_lanes(i)])


def _ssm(u_chunks, h0_re, h0_im, layouts, t1, rows, chained):
    bmat, cre, cim, lam_re_q, lam_im_q, d_q = layouts
    lam_re_q, lam_im_q = lam_re_q[:t1], lam_im_q[:t1]
    n_blocks, rows_h, n_state = h0_re.shape
    width = u_chunks.shape[1]
    assert rows % SSM_ROW_TILE == 0 and u_chunks.shape[0] == n_blocks * rows
    ns2 = bmat.shape[2]
    u_spec = pl.BlockSpec((rows, width), lambda b: (b, 0))
    h_spec = pl.BlockSpec((None, rows_h, n_state), lambda b: (b, 0, 0))
    consts = [bmat, cre, cim, lam_re_q, lam_im_q, d_q]
    vmem = (4 * rows * width * 4 + (t1 + 1) * rows * ns2 * 4 + 2 * sum(c.size * c.dtype.itemsize for c in consts)
            + 8 * rows_h * n_state * 4 + 8 * 1024 * 1024)
    h_out = jax.ShapeDtypeStruct(h0_re.shape, F32)
    return pl.pallas_call(
        functools.partial(_ssm_body, t1=t1, chained=chained),
        grid=(n_blocks,),
        in_specs=[u_spec, h_spec, h_spec] + [_const_spec(c.shape) for c in consts],
        out_specs=[u_spec, h_spec, h_spec],
        out_shape=[jax.ShapeDtypeStruct(u_chunks.shape, F32), h_out, h_out],
        scratch_shapes=[pltpu.VMEM((t1, rows, ns2), F32), pltpu.VMEM((rows, ns2), F32)],
        compiler_params=_params(("parallel",), vmem),
        name="ssm_chained" if chained else "ssm_rows",
    )(u_chunks, h0_re, h0_im, *consts)


def _mix_body(x_ref, oatt_ref, y_ref, wglu_ref, bglu_ref, gatt_ref, gssm_ref, wout_ref, lng_ref, lnb_ref,
              o_ref, *, alpha):
    d_att = oatt_ref.shape[1]
    g = jax.nn.gelu(y_ref[...])
    z = g * jax.nn.sigmoid(_dot(g.astype(BF16), wglu_ref[...]) + bglu_ref[...])
    ra = _rms_norm(oatt_ref[...], gatt_ref[...]).astype(BF16)
    rz = _rms_norm(z, gssm_ref[...]).astype(BF16)
    mixed = _dot(ra, wout_ref[0:d_att, :]) + _dot(rz, wout_ref[d_att:, :])
    o_ref[...] = _layer_norm(alpha * x_ref[...] + mixed, lng_ref[...], lnb_ref[...])


def _mix(x2d, o_att, y_ssm, w_glu, b_glu, g_att, g_ssm, w_out, ln_g, ln_b, alpha, tm):
    rows, dm = x2d.shape
    d_att, d_ssm = o_att.shape[1], y_ssm.shape[1]
    consts = [w_glu, b_glu.reshape(1, -1), g_att.reshape(1, -1), g_ssm.reshape(1, -1), w_out,
              ln_g.reshape(1, -1), ln_b.reshape(1, -1)]
    row = lambda width: pl.BlockSpec((tm, width), lambda i: (i, 0))
    vmem = 2 * tm * (2 * dm + d_att + d_ssm) * 4 + 4 * (w_glu.size + w_out.size) + 12 * tm * dm * 4
    return pl.pallas_call(
        functools.partial(_mix_body, alpha=alpha),
        grid=(rows // tm,),
        in_specs=[row(dm), row(d_att), row(d_ssm)] + [_const_spec(c.shape) for c in consts],
        out_specs=row(dm),
        out_shape=jax.ShapeDtypeStruct(x2d.shape, F32),
        compiler_params=_params(("parallel",), vmem),
        name="mix_out_ln1",
    )(x2d, o_att, y_ssm, *consts)


def _memkv_body(m_ref, wk_ref, wv_ref, k_ref, v_ref):
    mb = m_ref[...].astype(BF16)
    k_ref[...] = _dot(mb, wk_ref[...])
    v_ref[...] = _dot(mb, wv_ref[...])


def _memkv(mem2d, wk, wv, tm):
    rows, dm = mem2d.shape
    row = pl.BlockSpec((tm, dm), lambda i: (i, 0))
    out = jax.ShapeDtypeStruct((rows, wk.shape[1]), F32)
    vmem = 2 * 3 * tm * dm * 4 + 4 * (wk.size + wv.size) + 4 * tm * dm * 4
    return pl.pallas_call(
        _memkv_body,
        grid=(rows // tm,),
        in_specs=[row, _const_spec(wk.shape), _const_spec(wv.shape)],
        out_specs=[row, row],
        out_shape=[out, out],
        compiler_params=_params(("parallel",), vmem),
        name="mem_kv",
    )(mem2d, wk, wv)


def _memattn_body(x_ref, mk_ref, mv_ref, wq_ref, wo_ref, lng_ref, lnb_ref, o_ref, q_scr, a_scr, *, alpha, tl):
    nseq = mk_ref.shape[0]
    dm = x_ref.shape[1]
    hd = dm // N_MEM_HEADS
    x = x_ref[...]
    q_scr[...] = _dot(x.astype(BF16), wq_ref[...]) * (hd ** -0.5)
    for j in range(nseq):
        r = slice(j * tl, (j + 1) * tl)
        for h in range(N_MEM_HEADS):
            c = slice(h * hd, (h + 1) * hd)
            s = _dot_nt(q_scr[r, c].astype(BF16), mk_ref[j, :, c].astype(BF16))
            p = jnp.exp(s - jnp.max(s, axis=1, keepdims=True))
            l = jnp.sum(p, axis=1, keepdims=True)
            a_scr[r, c] = _dot(p.astype(BF16), mv_ref[j, :, c].astype(BF16)) / l
    att = _dot(a_scr[...].astype(BF16), wo_ref[...])
    o_ref[...] = _layer_norm(alpha * x + att, lng_ref[...], lnb_ref[...])


def _memattn(x2d, mem_k, mem_v, wq, wo, ln_g, ln_b, alpha, nseq, tl):
    rows, dm = x2d.shape
    n_mem = mem_k.shape[1]
    per_mem = rows // mem_k.shape[0]
    steps_per_mem = per_mem // tl if nseq == 1 else 1
    tm = nseq * tl
    row = pl.BlockSpec((tm, dm), lambda i: (i, 0))
    mem_spec = pl.BlockSpec((nseq, n_mem, dm), lambda i: (i // steps_per_mem, 0, 0))
    consts = [wq, wo, ln_g.reshape(1, -1), ln_b.reshape(1, -1)]
    vmem = 2 * 2 * tm * dm * 4 + 2 * 2 * nseq * n_mem * dm * 4 + 4 * (wq.size + wo.size) + 10 * tm * dm * 4
    return pl.pallas_call(
        functools.partial(_memattn_body, alpha=alpha, tl=tl),
        grid=(rows // tm,),
        in_specs=[row, mem_spec, mem_spec] + [_const_spec(c.shape) for c in consts],
        out_specs=row,
        out_shape=jax.ShapeDtypeStruct(x2d.shape, F32),
        scratch_shapes=[pltpu.VMEM((tm, dm), F32), pltpu.VMEM((tm, dm), F32)],
        compiler_params=_params(("parallel",), vmem),
        name="mem_attn_ln2",
    )(x2d, mem_k, mem_v, *consts)


def _ffn_body(x_ref, wg_ref, wu_ref, wd_ref, lng_ref, lnb_ref, o_ref, acc_ref, *, alpha, tf):
    x = x_ref[...]
    xb = x.astype(BF16)
    d_ff = wg_ref.shape[1]
    for c in range(d_ff // tf):
        cols = slice(c * tf, (c + 1) * tf)
        hid = (jax.nn.silu(_dot(xb, wg_ref[:, cols])) * _dot(xb, wu_ref[:, cols])).astype(BF16)
        part = _dot(hid, wd_ref[cols, :])
        if c == 0:
            acc_ref[...] = part
        else:
            acc_ref[...] += part
    o_ref[...] = _layer_norm(alpha * x + acc_ref[...], lng_ref[...], lnb_ref[...])


def _ffn(x2d, wg, wu, wd, ln_g, ln_b, alpha, tm, tf):
    rows, dm = x2d.shape
    d_ff = wg.shape[1]
    assert d_ff % tf == 0 and tf % LANES == 0
    row = pl.BlockSpec((tm, dm), lambda i: (i, 0))
    consts = [wg, wu, wd, ln_g.reshape(1, -1), ln_b.reshape(1, -1)]
    vmem = 2 * 2 * tm * dm * 4 + 4 * 3 * wg.size + tm * dm * 4 + 6 * tm * max(tf, dm) * 4
    return pl.pallas_call(
        functools.partial(_ffn_body, alpha=alpha, tf=tf),
        grid=(rows // tm,),
        in_specs=[row] + [_const_spec(c.shape) for c in consts],
        out_specs=row,
        out_shape=jax.ShapeDtypeStruct(x2d.shape, F32),
        scratch_shapes=[pltpu.VMEM((tm, dm), F32)],
        compiler_params=_params(("parallel",), vmem),
        name="swiglu_ln3",
    )(x2d, *consts)


PROMPT_CHUNK = 8
ROW_TILE = 512
FFN_COL_TILE = 256
SAMPLE_ATTN_SEQS = 2
SAMPLE_MEM_SEQS = 4


def kernel(x_prompt, x_sample, cache_win_k, cache_win_v, state_ssm_re, state_ssm_im, cache_mem_k, cache_mem_v, mem_prompt, w_in, g_att, g_ssm, ssm_a_re, ssm_a_im, ssm_log_dt, ssm_b_re, ssm_b_im, ssm_c_re, ssm_c_im, ssm_d, w_glu, b_glu, w_out, ln1_g, ln1_b, w_mem_q, w_mem_k, w_mem_v, w_mem_o, ln2_g, ln2_b, w_gate, w_up, w_down, ln3_g, ln3_b):
    depth = w_in.shape[0]
    bp, seq, dm = x_prompt.shape
    bs, t_new, _ = x_sample.shape
    n_groups, n_state = ssm_a_re.shape[1:]
    n_mem = mem_prompt.shape[1]
    alpha = (2 * depth) ** 0.25
    keep = min(DILATED_CFGS[-1][0], seq)
    assert keep == seq
    n_chunks = seq // PROMPT_CHUNK

    y_p = x_prompt.reshape(bp * seq, dm)
    y_s = x_sample.reshape(bs * t_new, dm)
    mem2d = mem_prompt.reshape(bp * n_mem, dm)
    outs = [[] for _ in range(10)]
    for l in range(depth):
        bf = lambda w: w[l].astype(BF16)
        w_in_l, w_glu_l, w_out_l = bf(w_in), bf(w_glu), bf(w_out)
        wq_l, wk_l, wv_l, wo_l = bf(w_mem_q), bf(w_mem_k), bf(w_mem_v), bf(w_mem_o)
        wg_l, wu_l, wd_l = bf(w_gate), bf(w_up), bf(w_down)
        lam_re, lam_im, bbar_re_t, bbar_im_t = _ssm_prep(
            ssm_a_re[l], ssm_a_im[l], ssm_log_dt[l], ssm_b_re[l], ssm_b_im[l], max(PROMPT_CHUNK, t_new))
        layouts = _ssm_layouts(lam_re, lam_im, bbar_re_t, bbar_im_t, ssm_c_re[l], ssm_c_im[l], ssm_d[l])
        mix_w = (w_glu_l, b_glu[l], g_att[l], g_ssm[l], w_out_l, ln1_g[l], ln1_b[l])

        q, k, v, u = _inproj(y_p, w_in_l, ROW_TILE)
        d_att = q.shape[1]
        o_att = _attn_prompt(q.reshape(bp, seq, d_att), k.reshape(bp, seq, d_att), v.reshape(bp, seq, d_att))
        zeros = jnp.zeros((bp, 1, n_groups * n_state), F32)
        y_ssm, hr_p, hi_p = _ssm(u.reshape(bp * n_chunks, PROMPT_CHUNK * u.shape[1]), zeros, zeros, layouts,
                                 PROMPT_CHUNK, n_chunks, True)
        x1 = _mix(y_p, o_att.reshape(bp * seq, d_att), y_ssm.reshape(bp * seq, -1), *mix_w, alpha, ROW_TILE)
        mk_p, mv_p = _memkv(mem2d, wk_l, wv_l, ROW_TILE)
        x2 = _memattn(x1, mk_p.reshape(bp, n_mem, dm), mv_p.reshape(bp, n_mem, dm), wq_l, wo_l,
                      ln2_g[l], ln2_b[l], alpha, 1, ROW_TILE)
        y_p = _ffn(x2, wg_l, wu_l, wd_l, ln3_g[l], ln3_b[l], alpha, ROW_TILE, FFN_COL_TILE)

        qs, ks, vs, us = _inproj(y_s, w_in_l, bs * t_new)
        shp = (bs, t_new, N_ATT_HEADS, ATT_HEAD_DIM)
        o_att_s = _attn_window(qs.reshape(shp), ks.reshape(shp), vs.reshape(shp), cache_win_k, cache_win_v, l,
                               SAMPLE_ATTN_SEQS)
        y_ssm_s, hr_s, hi_s = _ssm(us.reshape(bs, t_new * us.shape[1]),
                                   state_ssm_re[l].reshape(1, bs, -1), state_ssm_im[l].reshape(1, bs, -1),
                                   layouts, t_new, bs, False)
        x1s = _mix(y_s, o_att_s.reshape(bs * t_new, d_att), y_ssm_s.reshape(bs * t_new, -1), *mix_w, alpha,
                   bs * t_new)
        x2s = _memattn(x1s, cache_mem_k[l].reshape(bs, n_mem, dm), cache_mem_v[l].reshape(bs, n_mem, dm),
                       wq_l, wo_l, ln2_g[l], ln2_b[l], alpha, SAMPLE_MEM_SEQS, t_new)
        y_s = _ffn(x2s, wg_l, wu_l, wd_l, ln3_g[l], ln3_b[l], alpha, bs * t_new, FFN_COL_TILE)

        head_shape = (N_ATT_HEADS, ATT_HEAD_DIM)
        state_shape = (n_groups, n_state)
        mem_shape = (bp, n_mem, N_MEM_HEADS, dm // N_MEM_HEADS)
        for lst, val in zip(outs, (
                k.reshape(bp, seq, *head_shape)[:, seq - keep:], v.reshape(bp, seq, *head_shape)[:, seq - keep:],
                ks.reshape(bs, t_new, *head_shape), vs.reshape(bs, t_new, *head_shape),
                hr_p.reshape(bp, *state_shape), hi_p.reshape(bp, *state_shape),
                hr_s.reshape(bs, *state_shape), hi_s.reshape(bs, *state_shape),
                mk_p.reshape(mem_shape), mv_p.reshape(mem_shape))):
            lst.append(val)
    return (y_p.reshape(bp, seq, dm), y_s.reshape(bs, t_new, dm)) + tuple(jnp.stack(o) for o in outs)
```

```python
import functools
import math

import jax
import jax.numpy as jnp
from jax import lax
from jax.experimental import pallas as pl
from jax.experimental.pallas import tpu as pltpu

F32 = jnp.float32
BF16 = jnp.bfloat16

N_ATT_HEADS = 8
ATT_HEAD_DIM = 64
D_ATT = N_ATT_HEADS * ATT_HEAD_DIM
DILATED_CFGS = ((128, 1), (512, 4), (2048, 16))
ATT_BLK = 128
ATT_GROUP = 4
SSM_CH = 16
SSM_STATE = 64
N_MEM_HEADS = 4
EPS = 1e-5
NEG = -1e30

LANES = 128
V7X_VMEM_CAP_BYTES = 56 * 1024 * 1024

SSM_GB = LANES // SSM_CH
SSM_ROW_TILE = 32


def _params(sem, vmem_bytes):
    return pltpu.CompilerParams(
        dimension_semantics=sem,
        vmem_limit_bytes=int(min(max(vmem_bytes, 16 * 1024 * 1024), V7X_VMEM_CAP_BYTES)),
    )


def _dot(a, b):
    return jnp.dot(a, b, preferred_element_type=F32)


def _dot_nt(a, b):
    return lax.dot_general(a, b, (((1,), (1,)), ((), ())), preferred_element_type=F32)


def _layer_norm(x, g, b):
    mu = jnp.mean(x, axis=-1, keepdims=True)
    xc = x - mu
    var = jnp.mean(xc * xc, axis=-1, keepdims=True)
    return xc * lax.rsqrt(var + EPS) * g + b


def _rms_norm(x, g):
    return x * lax.rsqrt(jnp.mean(x * x, axis=-1, keepdims=True) + EPS) * g


def _const_spec(shape):
    n = len(shape)
    return pl.BlockSpec(shape, lambda *_: (0,) * n)


def _inproj_body(x_ref, w_ref, q_ref, k_ref, v_ref, u_ref):
    xb = x_ref[...].astype(BF16)
    d = q_ref.shape[1]
    q_ref[...] = _dot(xb, w_ref[:, 0 * d:1 * d])
    k_ref[...] = _dot(xb, w_ref[:, 1 * d:2 * d])
    v_ref[...] = _dot(xb, w_ref[:, 2 * d:3 * d])
    u_ref[...] = _dot(xb, w_ref[:, 3 * d:4 * d])


def _inproj(x2d, w_in_bf16, tm):
    rows, dm = x2d.shape
    d = w_in_bf16.shape[1] // 4
    out = jax.ShapeDtypeStruct((rows, d), F32)
    row_spec = pl.BlockSpec((tm, d), lambda i: (i, 0))
    vmem = 2 * (tm * dm * 4 + 4 * tm * d * 4) + 2 * w_in_bf16.size * 2 + 8 * tm * d * 4
    return pl.pallas_call(
        _inproj_body,
        grid=(rows // tm,),
        in_specs=[pl.BlockSpec((tm, dm), lambda i: (i, 0)), _const_spec(w_in_bf16.shape)],
        out_specs=[row_spec] * 4,
        out_shape=[out] * 4,
        compiler_params=_params(("parallel",), vmem),
        name="inproj",
    )(x2d, w_in_bf16)


def _attn_prompt_body(q_ref, k_ref, v_ref, o_ref, acc_ref, m_ref, l_ref):
    seq = q_ref.shape[0]
    blk = ATT_BLK
    scale = ATT_HEAD_DIM ** -0.5
    lane = lax.broadcasted_iota(jnp.int32, (blk, LANES), 1)
    head0 = lane < ATT_HEAD_DIM
    qi = lax.broadcasted_iota(jnp.int32, (blk, blk), 0)
    kj = lax.broadcasted_iota(jnp.int32, (blk, blk), 1)
    tri = kj <= qi
    mask_rest = jnp.concatenate([kj >= qi, tri], axis=1)

    def rows_of(start, d):
        return pl.ds(start, blk) if d == 1 else pl.ds(start, blk, stride=d)

    def attend(cfg, d, q_start, prev_start):
        rows = rows_of(q_start, d)
        q = q_ref[rows, :] * scale
        k2 = k_ref[rows, :].astype(BF16)
        v2 = v_ref[rows, :].astype(BF16)
        mask = tri
        if prev_start is not None:
            prow = rows_of(prev_start, d)
            k2 = jnp.concatenate([k_ref[prow, :].astype(BF16), k2], axis=0)
            v2 = jnp.concatenate([v_ref[prow, :].astype(BF16), v2], axis=0)
            mask = mask_rest
        parts = []
        for h in range(2):
            sel = head0 if h == 0 else jnp.logical_not(head0)
            qh = jnp.where(sel, q, 0.0).astype(BF16)
            s = jnp.where(mask, _dot_nt(qh, k2), NEG)
            m = jnp.max(s, axis=1, keepdims=True)
            p = jnp.exp(s - m)
            l = jnp.sum(p, axis=1, keepdims=True)
            parts.append((m, l, _dot(p.astype(BF16), v2)))
        m = jnp.where(head0, parts[0][0], parts[1][0])
        l = jnp.where(head0, parts[0][1], parts[1][1])
        num = jnp.where(head0, parts[0][2], parts[1][2])
        if cfg == 0:
            acc_ref[rows, :] = num
            m_ref[rows, :] = m
            l_ref[rows, :] = l
            return
        m_old = m_ref[rows, :]
        m_new = jnp.maximum(m_old, m)
        a = jnp.exp(m_old - m_new)
        b = jnp.exp(m - m_new)
        num = a * acc_ref[rows, :] + b * num
        l = a * l_ref[rows, :] + b * l
        if cfg == len(DILATED_CFGS) - 1:
            o_ref[rows, :] = num / l
        else:
            acc_ref[rows, :] = num
            m_ref[rows, :] = m_new
            l_ref[rows, :] = l

    grp = ATT_GROUP
    for cfg, (window, d) in enumerate(DILATED_CFGS):
        span = d * blk
        nb = seq // span
        if d >= grp:
            def residue_group(g, carry, cfg=cfg, d=d, span=span, nb=nb):
                for j in range(grp):
                    attend(cfg, d, g * grp + j, None)
                if nb > 1:
                    def later(n, c):
                        for j in range(grp):
                            r = g * grp + j
                            attend(cfg, d, r + n * span, r + (n - 1) * span)
                        return c
                    lax.fori_loop(1, nb, later, 0)
                return carry

            lax.fori_loop(0, d // grp, residue_group, 0)
        else:
            assert d == 1 and nb % grp == 0
            attend(cfg, d, 0, None)
            for n in range(1, grp):
                attend(cfg, d, n * span, (n - 1) * span)

            def block_group(g, carry, cfg=cfg, d=d, span=span):
                for j in range(grp):
                    n = g * grp + j
                    attend(cfg, d, n * span, (n - 1) * span)
                return carry

            lax.fori_loop(1, nb // grp, block_group, 0)


def _attn_prompt(q, k, v):
    bsz, seq, d_att = q.shape
    assert d_att % LANES == 0 and LANES == 2 * ATT_HEAD_DIM
    for window, d in DILATED_CFGS:
        assert window // d == ATT_BLK and seq % (d * ATT_BLK) == 0
    spec = pl.BlockSpec((None, seq, LANES), lambda b, h: (b, 0, h))
    blk_bytes = seq * LANES * 4
    return pl.pallas_call(
        _attn_prompt_body,
        grid=(bsz, d_att // LANES),
        in_specs=[spec, spec, spec],
        out_specs=spec,
        out_shape=jax.ShapeDtypeStruct(q.shape, F32),
        scratch_shapes=[pltpu.VMEM((seq, LANES), F32)] * 3,
        compiler_params=_params(("parallel", "parallel"), 11 * blk_bytes + 8 * 1024 * 1024),
        name="attn_prompt",
    )(q, k, v)


def _attn_window_body(q_ref, kn_ref, vn_ref, kt_ref, vt_ref, o_ref):
    nb, t_new, d_att = q_ref.shape
    w_buf = kt_ref.shape[2]
    nh = N_ATT_HEADS
    rows = t_new * nh
    scale = ATT_HEAD_DIM ** -0.5
    (win1, dil1), (win4, dil4), (_, dil16) = DILATED_CFGS
    tail = win4
    near = win1
    hbits = nh.bit_length() - 1
    row_id = lax.broadcasted_iota(jnp.int32, (rows, 1), 0)
    trow = jnp.right_shift(row_id, hbits)
    head_mask = (jnp.right_shift(lax.broadcasted_iota(jnp.int32, (nh, d_att), 1), ATT_HEAD_DIM.bit_length() - 1)
                 == lax.broadcasted_iota(jnp.int32, (nh, d_att), 0))
    lane = lambda n: lax.broadcasted_iota(jnp.int32, (rows, n), 1)
    mask16 = (lane(w_buf) & (dil16 - 1)) == trow
    mask4 = (lane(tail) & (dil4 - 1)) == trow
    mask1 = lane(near) >= trow
    new_self = lane(t_new) == trow
    new_causal = lane(t_new) <= trow

    def one_sequence(b, carry):
        q = q_ref[b] * scale
        qbd = jnp.concatenate(
            [jnp.where(head_mask, jnp.broadcast_to(q[t:t + 1], (nh, d_att)), 0.0) for t in range(t_new)],
            axis=0).astype(BF16)
        s_all = _dot(qbd, kt_ref[b].astype(BF16))
        s_new = _dot_nt(qbd, kn_ref[b].astype(BF16))

        def softmax_parts(main, main_mask, new_mask):
            sm = jnp.where(main_mask, main, NEG)
            sn = jnp.where(new_mask, s_new, NEG)
            m = jnp.maximum(jnp.max(sm, axis=1, keepdims=True), jnp.max(sn, axis=1, keepdims=True))
            p = jnp.exp(sm - m)
            pn = jnp.exp(sn - m)
            den = jnp.sum(p, axis=1, keepdims=True) + jnp.sum(pn, axis=1, keepdims=True)
            return m, p, pn, den

        m16, p16, pn16, den16 = softmax_parts(s_all, mask16, new_self)
        m4, p4, pn4, den4 = softmax_parts(s_all[:, w_buf - tail:], mask4, new_self)
        m1, p1, pn1, den1 = softmax_parts(s_all[:, w_buf - near:], mask1, new_causal)
        m = jnp.maximum(jnp.maximum(m1, m4), m16)
        w1, w4, w16 = jnp.exp(m1 - m), jnp.exp(m4 - m), jnp.exp(m16 - m)
        den = w1 * den1 + w4 * den4 + w16 * den16
        p16, p4, p1 = w16 * p16, w4 * p4, w1 * p1
        p_all = jnp.concatenate(
            [p16[:, :w_buf - tail],
             p16[:, w_buf - tail:w_buf - near] + p4[:, :tail - near],
             p16[:, w_buf - near:] + p4[:, tail - near:] + p1], axis=1).astype(BF16)
        pn_all = (w16 * pn16 + w4 * pn4 + w1 * pn1).astype(BF16)
        num = _dot_nt(p_all, vt_ref[b].astype(BF16)) + _dot(pn_all, vn_ref[b].astype(BF16))
        out = num / den
        o_ref[b] = jnp.concatenate(
            [jnp.sum(jnp.where(head_mask, out[t * nh:(t + 1) * nh], 0.0), axis=0, keepdims=True)
             for t in range(t_new)], axis=0)
        return carry

    lax.fori_loop(0, nb, one_sequence, 0)


def _attn_window(q, k_new, v_new, cache_k, cache_v, layer, nb):
    bsz, t_new, d_att = q.shape
    depth, _, w_buf, nh, hd = cache_k.shape
    assert w_buf == DILATED_CFGS[-1][0] and t_new <= DILATED_CFGS[1][1] and nh * hd == d_att
    rows_last = lambda c: jnp.transpose(c, (0, 1, 3, 4, 2)).reshape(depth, bsz, d_att, w_buf)
    new_spec = pl.BlockSpec((nb, t_new, d_att), lambda i: (i, 0, 0))
    cache_spec = pl.BlockSpec((None, nb, d_att, w_buf), lambda i: (layer, i, 0, 0))
    vmem = 2 * 2 * nb * d_att * w_buf * 4 + 2 * d_att * w_buf * 2 + 16 * 1024 * 1024
    return pl.pallas_call(
        _attn_window_body,
        grid=(bsz // nb,),
        in_specs=[new_spec, new_spec, new_spec, cache_spec, cache_spec],
        out_specs=new_spec,
        out_shape=jax.ShapeDtypeStruct(q.shape, F32),
        compiler_params=_params(("parallel",), vmem),
        name="attn_window",
    )(q, k_new, v_new, rows_last(cache_k), rows_last(cache_v))


def _ssm_prep_body(a_re_ref, a_im_ref, log_dt_ref, b_re_ref, b_im_ref, lam_re_ref, lam_im_ref,
                   bbar_re_ref, bbar_im_ref):
    ar = a_re_ref[...]
    ai = a_im_ref[...]
    dt = jnp.exp(log_dt_ref[...])
    mag = jnp.exp(dt * ar)
    lr = mag * jnp.cos(dt * ai)
    li = mag * jnp.sin(dt * ai)
    den = ar * ar + ai * ai
    nr, ni = lr - 1.0, li
    cr = (nr * ar + ni * ai) / den
    ci = (ni * ar - nr * ai) / den
    for h in range(b_re_ref.shape[0]):
        br, bi = b_re_ref[h], b_im_ref[h]
        bbar_re_ref[h] = cr * br - ci * bi
        bbar_im_ref[h] = cr * bi + ci * br
    pr, pi = lr, li
    for j in range(lam_re_ref.shape[0]):
        lam_re_ref[j] = pr
        lam_im_ref[j] = pi
        pr, pi = pr * lr - pi * li, pr * li + pi * lr


def _ssm_prep(a_re, a_im, log_dt, b_re, b_im, n_pow):
    g, p = a_re.shape
    h = b_re.shape[-1]
    b_re_t = jnp.transpose(b_re, (2, 0, 1))
    b_im_t = jnp.transpose(b_im, (2, 0, 1))
    gp = jax.ShapeDtypeStruct((n_pow, g, p), F32)
    hgp = jax.ShapeDtypeStruct((h, g, p), F32)
    return pl.pallas_call(
        _ssm_prep_body,
        out_shape=[gp, gp, hgp, hgp],
        name="ssm_prep",
    )(a_re, a_im, log_dt.reshape(g, 1), b_re_t, b_im_t)


def _ssm_layouts(lam_re, lam_im, bbar_re_t, bbar_im_t, c_re, c_im, d_skip):
    n_pow, g, p = lam_re.shape
    h = bbar_re_t.shape[0]
    nq = g // SSM_GB
    eye = jnp.eye(SSM_GB, dtype=bool)
    bb = jnp.stack([bbar_re_t, bbar_im_t], 0).reshape(2, h, nq, SSM_GB, p)
    bb = jnp.transpose(bb, (2, 3, 1, 0, 4))
    bmat = jnp.where(eye[None, :, None, None, :, None], bb[:, :, :, :, None, :], 0.0)
    bmat = bmat.reshape(nq, SSM_GB * h, 2 * SSM_GB * p).astype(BF16)
    def c_layout(c):
        cc = jnp.transpose(c.reshape(nq, SSM_GB, h, p), (0, 1, 3, 2))
        m = jnp.where(eye[None, :, None, :, None], cc[:, :, :, None, :], 0.0)
        return m.reshape(nq, SSM_GB * p, SSM_GB * h).astype(BF16)
    lam_re_q = lam_re.reshape(n_pow, nq, 1, SSM_GB * p)
    lam_im_q = lam_im.reshape(n_pow, nq, 1, SSM_GB * p)
    d_q = d_skip.reshape(nq, 1, SSM_GB * h)
    return bmat, c_layout(c_re), c_layout(c_im), lam_re_q, lam_im_q, d_q


def _ssm_body(u_ref, h0r_ref, h0i_ref, bmat_ref, cre_ref, cim_ref, lamr_ref, lami_ref, d_ref,
              y_ref, hfr_ref, hfi_ref, hloc_ref, hs_ref, *, t1, chained):
    rows = u_ref.shape[0]
    d_ssm = y_ref.shape[1] // t1
    nq = bmat_ref.shape[0]
    ns = lamr_ref.shape[-1]
    rt = SSM_ROW_TILE
    for q in range(nq):
        def u_lanes(i, q=q):
            return slice(i * d_ssm + q * LANES, i * d_ssm + (q + 1) * LANES)
        s_lanes = slice(q * ns, (q + 1) * ns)
        for i in range(t1):
            hloc_ref[i] = _dot(u_ref[:, u_lanes(i)].astype(BF16), bmat_ref[q])
        lr, li = lamr_ref[0, q], lami_ref[0, q]

        def local_scan(tile, carry, lr=lr, li=li):
            r = pl.ds(pl.multiple_of(tile * rt, rt), rt)
            hr = hloc_ref[0, r, 0:ns]
            hi = hloc_ref[0, r, ns:2 * ns]
            for i in range(1, t1):
                hr, hi = (lr * hr - li * hi + hloc_ref[i, r, 0:ns],
                          lr * hi + li * hr + hloc_ref[i, r, ns:2 * ns])
                hloc_ref[i, r, 0:ns] = hr
                hloc_ref[i, r, ns:2 * ns] = hi
            return carry

        lax.fori_loop(0, rows // rt, local_scan, 0)
        if chained:
            cr, ci = lamr_ref[t1 - 1, q], lami_ref[t1 - 1, q]

            def chain(c, g, cr=cr, ci=ci):
                gr, gi = g
                row = pl.ds(c, 1)
                hs_ref[row, 0:ns] = gr
                hs_ref[row, ns:2 * ns] = gi
                er = hloc_ref[t1 - 1, row, 0:ns]
                ei = hloc_ref[t1 - 1, row, ns:2 * ns]
                return cr * gr - ci * gi + er, cr * gi + ci * gr + ei

            gr, gi = lax.fori_loop(0, rows, chain, (h0r_ref[:, s_lanes], h0i_ref[:, s_lanes]))
            hfr_ref[:, s_lanes] = gr
            hfi_ref[:, s_lanes] = gi
        else:
            hs_ref[:, 0:ns] = h0r_ref[:, s_lanes]
            hs_ref[:, ns:2 * ns] = h0i_ref[:, s_lanes]

        def add_carry(tile, carry, q=q):
            r = pl.ds(pl.multiple_of(tile * rt, rt), rt)
            hr0 = hs_ref[r, 0:ns]
            hi0 = hs_ref[r, ns:2 * ns]
            for i in range(t1):
                pr, pi = lamr_ref[i, q], lami_ref[i, q]
                hloc_ref[i, r, 0:ns] = hloc_ref[i, r, 0:ns] + (pr * hr0 - pi * hi0)
                hloc_ref[i, r, ns:2 * ns] = hloc_ref[i, r, ns:2 * ns] + (pr * hi0 + pi * hr0)
            return carry

        lax.fori_loop(0, rows // rt, add_carry, 0)
        if not chained:
            hfr_ref[:, s_lanes] = hloc_ref[t1 - 1, :, 0:ns]
            hfi_ref[:, s_lanes] = hloc_ref[t1 - 1, :, ns:2 * ns]
        for i in range(t1):
            y_ref[:, u_lanes(i)] = (_dot(hloc_ref[i, :, 0:ns].astype(BF16), cre_ref[q])
                                    - _dot(hloc_ref[i, :, ns:2 * ns].astype(BF16), cim_ref[q])
                                    + d_ref[q] * u_ref[:, u_lanes(i)])


def _ssm(u_chunks, h0_re, h0_im, layouts, t1, rows, chained):
    bmat, cre, cim, lam_re_q, lam_im_q, d_q = layouts
    lam_re_q, lam_im_q = lam_re_q[:t1], lam_im_q[:t1]
    n_blocks, rows_h, n_state = h0_re.shape
    width = u_chunks.shape[1]
    assert rows % SSM_ROW_TILE == 0 and u_chunks.shape[0] == n_blocks * rows
    ns2 = bmat.shape[2]
    u_spec = pl.BlockSpec((rows, width), lambda b: (b, 0))
    h_spec = pl.BlockSpec((None, rows_h, n_state), lambda b: (b, 0, 0))
    consts = [bmat, cre, cim, lam_re_q, lam_im_q, d_q]
    vmem = (4 * rows * width * 4 + (t1 + 1) * rows * ns2 * 4 + 2 * sum(c.size * c.dtype.itemsize for c in consts)
            + 8 * rows_h * n_state * 4 + 8 * 1024 * 1024)
    h_out = jax.ShapeDtypeStruct(h0_re.shape, F32)
    return pl.pallas_call(
        functools.partial(_ssm_body, t1=t1, chained=chained),
        grid=(n_blocks,),
        in_specs=[u_spec, h_spec, h_spec] + [_const_spec(c.shape) for c in consts],
        out_specs=[u_spec, h_spec, h_spec],
        out_shape=[jax.ShapeDtypeStruct(u_chunks.shape, F32), h_out, h_out],
        scratch_shapes=[pltpu.VMEM((t1, rows, ns2), F32), pltpu.VMEM((rows, ns2), F32)],
        compiler_params=_params(("parallel",), vmem),
        name="ssm_chained" if chained else "ssm_rows",
    )(u_chunks, h0_re, h0_im, *consts)


def _mix_body(x_ref, oatt_ref, y_ref, wglu_ref, bglu_ref, gatt_ref, gssm_ref, wout_ref, lng_ref, lnb_ref,
              o_ref, *, alpha):
    d_att = oatt_ref.shape[1]
    g = jax.nn.gelu(y_ref[...])
    z = g * jax.nn.sigmoid(_dot(g.astype(BF16), wglu_ref[...]) + bglu_ref[...])
    ra = _rms_norm(oatt_ref[...], gatt_ref[...]).astype(BF16)
    rz = _rms_norm(z, gssm_ref[...]).astype(BF16)
    mixed = _dot(ra, wout_ref[0:d_att, :]) + _dot(rz, wout_ref[d_att:, :])
    o_ref[...] = _layer_norm(alpha * x_ref[...] + mixed, lng_ref[...], lnb_ref[...])


def _mix(x2d, o_att, y_ssm, w_glu, b_glu, g_att, g_ssm, w_out, ln_g, ln_b, alpha, tm):
    rows, dm = x2d.shape
    d_att, d_ssm = o_att.shape[1], y_ssm.shape[1]
    consts = [w_glu, b_glu.reshape(1, -1), g_att.reshape(1, -1), g_ssm.reshape(1, -1), w_out,
              ln_g.reshape(1, -1), ln_b.reshape(1, -1)]
    row = lambda width: pl.BlockSpec((tm, width), lambda i: (i, 0))
    vmem = 2 * tm * (2 * dm + d_att + d_ssm) * 4 + 4 * (w_glu.size + w_out.size) + 12 * tm * dm * 4
    return pl.pallas_call(
        functools.partial(_mix_body, alpha=alpha),
        grid=(rows // tm,),
        in_specs=[row(dm), row(d_att), row(d_ssm)] + [_const_spec(c.shape) for c in consts],
        out_specs=row(dm),
        out_shape=jax.ShapeDtypeStruct(x2d.shape, F32),
        compiler_params=_params(("parallel",), vmem),
        name="mix_out_ln1",
    )(x2d, o_att, y_ssm, *consts)


def _memkv_body(m_ref, wk_ref, wv_ref, k_ref, v_ref):
    mb = m_ref[...].astype(BF16)
    k_ref[...] = _dot(mb, wk_ref[...])
    v_ref[...] = _dot(mb, wv_ref[...])


def _memkv(mem2d, wk, wv, tm):
    rows, dm = mem2d.shape
    row = pl.BlockSpec((tm, dm), lambda i: (i, 0))
    out = jax.ShapeDtypeStruct((rows, wk.shape[1]), F32)
    vmem = 2 * 3 * tm * dm * 4 + 4 * (wk.size + wv.size) + 4 * tm * dm * 4
    return pl.pallas_call(
        _memkv_body,
        grid=(rows // tm,),
        in_specs=[row, _const_spec(wk.shape), _const_spec(wv.shape)],
        out_specs=[row, row],
        out_shape=[out, out],
        compiler_params=_params(("parallel",), vmem),
        name="mem_kv",
    )(mem2d, wk, wv)


def _memattn_body(x_ref, mk_ref, mv_ref, wq_ref, wo_ref, lng_ref, lnb_ref, o_ref, q_scr, a_scr, *, alpha, tl):
    nseq = mk_ref.shape[0]
    dm = x_ref.shape[1]
    hd = dm // N_MEM_HEADS
    x = x_ref[...]
    q_scr[...] = _dot(x.astype(BF16), wq_ref[...]) * (hd ** -0.5)
    for j in range(nseq):
        r = slice(j * tl, (j + 1) * tl)
        for h in range(N_MEM_HEADS):
            c = slice(h * hd, (h + 1) * hd)
            s = _dot_nt(q_scr[r, c].astype(BF16), mk_ref[j, :, c].astype(BF16))
            p = jnp.exp(s - jnp.max(s, axis=1, keepdims=True))
            l = jnp.sum(p, axis=1, keepdims=True)
            a_scr[r, c] = _dot(p.astype(BF16), mv_ref[j, :, c].astype(BF16)) / l
    att = _dot(a_scr[...].astype(BF16), wo_ref[...])
    o_ref[...] = _layer_norm(alpha * x + att, lng_ref[...], lnb_ref[...])


def _memattn(x2d, mem_k, mem_v, wq, wo, ln_g, ln_b, alpha, nseq, tl):
    rows, dm = x2d.shape
    n_mem = mem_k.shape[1]
    per_mem = rows // mem_k.shape[0]
    steps_per_mem = per_mem // tl if nseq == 1 else 1
    tm = nseq * tl
    row = pl.BlockSpec((tm, dm), lambda i: (i, 0))
    mem_spec = pl.BlockSpec((nseq, n_mem, dm), lambda i: (i // steps_per_mem, 0, 0))
    consts = [wq, wo, ln_g.reshape(1, -1), ln_b.reshape(1, -1)]
    vmem = 2 * 2 * tm * dm * 4 + 2 * 2 * nseq * n_mem * dm * 4 + 4 * (wq.size + wo.size) + 10 * tm * dm * 4
    return pl.pallas_call(
        functools.partial(_memattn_body, alpha=alpha, tl=tl),
        grid=(rows // tm,),
        in_specs=[row, mem_spec, mem_spec] + [_const_spec(c.shape) for c in consts],
        out_specs=row,
        out_shape=jax.ShapeDtypeStruct(x2d.shape, F32),
        scratch_shapes=[pltpu.VMEM((tm, dm), F32), pltpu.VMEM((tm, dm), F32)],
        compiler_params=_params(("parallel",), vmem),
        name="mem_attn_ln2",
    )(x2d, mem_k, mem_v, *consts)


def _ffn_body(x_ref, wg_ref, wu_ref, wd_ref, lng_ref, lnb_ref, o_ref, acc_ref, *, alpha, tf):
    x = x_ref[...]
    xb = x.astype(BF16)
    d_ff = wg_ref.shape[1]
    for c in range(d_ff // tf):
        cols = slice(c * tf, (c + 1) * tf)
        hid = (jax.nn.silu(_dot(xb, wg_ref[:, cols])) * _dot(xb, wu_ref[:, cols])).astype(BF16)
        part = _dot(hid, wd_ref[cols, :])
        if c == 0:
            acc_ref[...] = part
        else:
            acc_ref[...] += part
    o_ref[...] = _layer_norm(alpha * x + acc_ref[...], lng_ref[...], lnb_ref[...])


def _ffn(x2d, wg, wu, wd, ln_g, ln_b, alpha, tm, tf):
    rows, dm = x2d.shape
    d_ff = wg.shape[1]
    assert d_ff % tf == 0 and tf % LANES == 0
    row = pl.BlockSpec((tm, dm), lambda i: (i, 0))
    consts = [wg, wu, wd, ln_g.reshape(1, -1), ln_b.reshape(1, -1)]
    vmem = 2 * 2 * tm * dm * 4 + 4 * 3 * wg.size + tm * dm * 4 + 6 * tm * max(tf, dm) * 4
    return pl.pallas_call(
        functools.partial(_ffn_body, alpha=alpha, tf=tf),
        grid=(rows // tm,),
        in_specs=[row] + [_const_spec(c.shape) for c in consts],
        out_specs=row,
        out_shape=jax.ShapeDtypeStruct(x2d.shape, F32),
        scratch_shapes=[pltpu.VMEM((tm, dm), F32)],
        compiler_params=_params(("parallel",), vmem),
        name="swiglu_ln3",
    )(x2d, *consts)


PROMPT_CHUNK = 8
ROW_TILE = 512
FFN_COL_TILE = 256
SAMPLE_ATTN_SEQS = 2
SAMPLE_MEM_SEQS = 4


def kernel(x_prompt, x_sample, cache_win_k, cache_win_v, state_ssm_re, state_ssm_im, cache_mem_k, cache_mem_v, mem_prompt, w_in, g_att, g_ssm, ssm_a_re, ssm_a_im, ssm_log_dt, ssm_b_re, ssm_b_im, ssm_c_re, ssm_c_im, ssm_d, w_glu, b_glu, w_out, ln1_g, ln1_b, w_mem_q, w_mem_k, w_mem_v, w_mem_o, ln2_g, ln2_b, w_gate, w_up, w_down, ln3_g, ln3_b):
    depth = w_in.shape[0]
    bp, seq, dm = x_prompt.shape
    bs, t_new, _ = x_sample.shape
    n_groups, n_state = ssm_a_re.shape[1:]
    n_mem = mem_prompt.shape[1]
    alpha = (2 * depth) ** 0.25
    keep = min(DILATED_CFGS[-1][0], seq)
    assert keep == seq
    n_chunks = seq // PROMPT_CHUNK

    y_p = x_prompt.reshape(bp * seq, dm)
    y_s = x_sample.reshape(bs * t_new, dm)
    mem2d = mem_prompt.reshape(bp * n_mem, dm)
    outs = [[] for _ in range(10)]
    for l in range(depth):
        bf = lambda w: w[l].astype(BF16)
        w_in_l, w_glu_l, w_out_l = bf(w_in), bf(w_glu), bf(w_out)
        wq_l, wk_l, wv_l, wo_l = bf(w_mem_q), bf(w_mem_k), bf(w_mem_v), bf(w_mem_o)
        wg_l, wu_l, wd_l = bf(w_gate), bf(w_up), bf(w_down)
        lam_re, lam_im, bbar_re_t, bbar_im_t = _ssm_prep(
            ssm_a_re[l], ssm_a_im[l], ssm_log_dt[l], ssm_b_re[l], ssm_b_im[l], max(PROMPT_CHUNK, t_new))
        layouts = _ssm_layouts(lam_re, lam_im, bbar_re_t, bbar_im_t, ssm_c_re[l], ssm_c_im[l], ssm_d[l])
        mix_w = (w_glu_l, b_glu[l], g_att[l], g_ssm[l], w_out_l, ln1_g[l], ln1_b[l])

        q, k, v, u = _inproj(y_p, w_in_l, ROW_TILE)
        d_att = q.shape[1]
        o_att = _attn_prompt(q.reshape(bp, seq, d_att), k.reshape(bp, seq, d_att), v.reshape(bp, seq, d_att))
        zeros = jnp.zeros((bp, 1, n_groups * n_state), F32)
        y_ssm, hr_p, hi_p = _ssm(u.reshape(bp * n_chunks, PROMPT_CHUNK * u.shape[1]), zeros, zeros, layouts,
                                 PROMPT_CHUNK, n_chunks, True)
        x1 = _mix(y_p, o_att.reshape(bp * seq, d_att), y_ssm.reshape(bp * seq, -1), *mix_w, alpha, ROW_TILE)
        mk_p, mv_p = _memkv(mem2d, wk_l, wv_l, ROW_TILE)
        x2 = _memattn(x1, mk_p.reshape(bp, n_mem, dm), mv_p.reshape(bp, n_mem, dm), wq_l, wo_l,
                      ln2_g[l], ln2_b[l], alpha, 1, ROW_TILE)
        y_p = _ffn(x2, wg_l, wu_l, wd_l, ln3_g[l], ln3_b[l], alpha, ROW_TILE, FFN_COL_TILE)

        qs, ks, vs, us = _inproj(y_s, w_in_l, bs * t_new)
        shp = (bs, t_new, d_att)
        o_att_s = _attn_window(qs.reshape(shp), ks.reshape(shp), vs.reshape(shp), cache_win_k, cache_win_v, l,
                               SAMPLE_ATTN_SEQS)
        y_ssm_s, hr_s, hi_s = _ssm(us.reshape(bs, t_new * us.shape[1]),
                                   state_ssm_re[l].reshape(1, bs, -1), state_ssm_im[l].reshape(1, bs, -1),
                                   layouts, t_new, bs, False)
        x1s = _mix(y_s, o_att_s.reshape(bs * t_new, d_att), y_ssm_s.reshape(bs * t_new, -1), *mix_w, alpha,
                   bs * t_new)
        x2s = _memattn(x1s, cache_mem_k[l].reshape(bs, n_mem, dm), cache_mem_v[l].reshape(bs, n_mem, dm),
                       wq_l, wo_l, ln2_g[l], ln2_b[l], alpha, SAMPLE_MEM_SEQS, t_new)
        y_s = _ffn(x2s, wg_l, wu_l, wd_l, ln3_g[l], ln3_b[l], alpha, bs * t_new, FFN_COL_TILE)

        head_shape = (N_ATT_HEADS, ATT_HEAD_DIM)
        state_shape = (n_groups, n_state)
        mem_shape = (bp, n_mem, N_MEM_HEADS, dm // N_MEM_HEADS)
        for lst, val in zip(outs, (
                k.reshape(bp, seq, *head_shape)[:, seq - keep:], v.reshape(bp, seq, *head_shape)[:, seq - keep:],
                ks.reshape(bs, t_new, *head_shape), vs.reshape(bs, t_new, *head_shape),
                hr_p.reshape(bp, *state_shape), hi_p.reshape(bp, *state_shape),
                hr_s.reshape(bs, *state_shape), hi_s.reshape(bs, *state_shape),
                mk_p.reshape(mem_shape), mv_p.reshape(mem_shape))):
            lst.append(val)
    return (y_p.reshape(bp, seq, dm), y_s.reshape(bs, t_new, dm)) + tuple(jnp.stack(o) for o in outs)
```

```python
import functools
import math

import jax
import jax.numpy as jnp
from jax import lax
from jax.experimental import pallas as pl
from jax.experimental.pallas import tpu as pltpu

F32 = jnp.float32
BF16 = jnp.bfloat16

N_ATT_HEADS = 8
ATT_HEAD_DIM = 64
D_ATT = N_ATT_HEADS * ATT_HEAD_DIM
DILATED_CFGS = ((128, 1), (512, 4), (2048, 16))
ATT_BLK = 128
ATT_GROUP = 4
SSM_CH = 16
SSM_STATE = 64
N_MEM_HEADS = 4
EPS = 1e-5
NEG = -1e30

LANES = 128
V7X_VMEM_CAP_BYTES = 56 * 1024 * 1024

SSM_GB = LANES // SSM_CH
SSM_ROW_TILE = 32


def _params(sem, vmem_bytes):
    return pltpu.CompilerParams(
        dimension_semantics=sem,
        vmem_limit_bytes=int(min(max(vmem_bytes, 16 * 1024 * 1024), V7X_VMEM_CAP_BYTES)),
    )


def _dot(a, b):
    return jnp.dot(a, b, preferred_element_type=F32)


def _dot_nt(a, b):
    return lax.dot_general(a, b, (((1,), (1,)), ((), ())), preferred_element_type=F32)


def _layer_norm(x, g, b):
    mu = jnp.mean(x, axis=-1, keepdims=True)
    xc = x - mu
    var = jnp.mean(xc * xc, axis=-1, keepdims=True)
    return xc * lax.rsqrt(var + EPS) * g + b


def _rms_norm(x, g):
    return x * lax.rsqrt(jnp.mean(x * x, axis=-1, keepdims=True) + EPS) * g


def _const_spec(shape):
    n = len(shape)
    return pl.BlockSpec(shape, lambda *_: (0,) * n)


def _inproj_body(x_ref, w_ref, q_ref, k_ref, v_ref, u_ref):
    xb = x_ref[...].astype(BF16)
    d = q_ref.shape[1]
    q_ref[...] = _dot(xb, w_ref[:, 0 * d:1 * d])
    k_ref[...] = _dot(xb, w_ref[:, 1 * d:2 * d])
    v_ref[...] = _dot(xb, w_ref[:, 2 * d:3 * d])
    u_ref[...] = _dot(xb, w_ref[:, 3 * d:4 * d])


def _inproj(x2d, w_in_bf16, tm):
    rows, dm = x2d.shape
    d = w_in_bf16.shape[1] // 4
    out = jax.ShapeDtypeStruct((rows, d), F32)
    row_spec = pl.BlockSpec((tm, d), lambda i: (i, 0))
    vmem = 2 * (tm * dm * 4 + 4 * tm * d * 4) + 2 * w_in_bf16.size * 2 + 8 * tm * d * 4
    return pl.pallas_call(
        _inproj_body,
        grid=(rows // tm,),
        in_specs=[pl.BlockSpec((tm, dm), lambda i: (i, 0)), _const_spec(w_in_bf16.shape)],
        out_specs=[row_spec] * 4,
        out_shape=[out] * 4,
        compiler_params=_params(("parallel",), vmem),
        name="inproj",
    )(x2d, w_in_bf16)


def _attn_prompt_body(q_ref, k_ref, v_ref, o_ref, acc_ref, m_ref, l_ref):
    seq = q_ref.shape[0]
    blk = ATT_BLK
    scale = ATT_HEAD_DIM ** -0.5
    lane = lax.broadcasted_iota(jnp.int32, (blk, LANES), 1)
    head0 = lane < ATT_HEAD_DIM
    qi = lax.broadcasted_iota(jnp.int32, (blk, blk), 0)
    kj = lax.broadcasted_iota(jnp.int32, (blk, blk), 1)
    tri = kj <= qi
    mask_rest = jnp.concatenate([kj >= qi, tri], axis=1)

    def rows_of(start, d):
        return pl.ds(start, blk) if d == 1 else pl.ds(start, blk, stride=d)

    def load_block(d, q_start, prev_start):
        rows = rows_of(q_start, d)
        q = q_ref[rows, :] * scale
        k2 = k_ref[rows, :].astype(BF16)
        v2 = v_ref[rows, :].astype(BF16)
        mask = tri
        if prev_start is not None:
            prow = rows_of(prev_start, d)
            k2 = jnp.concatenate([k_ref[prow, :].astype(BF16), k2], axis=0)
            v2 = jnp.concatenate([v_ref[prow, :].astype(BF16), v2], axis=0)
            mask = mask_rest
        qh = [jnp.where(head0 if h == 0 else jnp.logical_not(head0), q, 0.0).astype(BF16) for h in range(2)]
        return rows, qh, k2, v2, mask

    def attend_group(cfg, d, starts):
        blocks = [load_block(d, q_start, prev_start) for q_start, prev_start in starts]
        scores = [[jnp.where(mask, _dot_nt(qh[h], k2), NEG) for h in range(2)] for _, qh, k2, _, mask in blocks]
        maxes = [[jnp.max(s, axis=1, keepdims=True) for s in sb] for sb in scores]
        probs = [[jnp.exp(s - m) for s, m in zip(sb, mb)] for sb, mb in zip(scores, maxes)]
        sums = [[jnp.sum(p, axis=1, keepdims=True) for p in pb] for pb in probs]
        pvs = [[_dot(p.astype(BF16), blk_[3]) for p in pb] for pb, blk_ in zip(probs, blocks)]
        new = [(blk_[0], jnp.where(head0, mb[0], mb[1]), jnp.where(head0, lb[0], lb[1]),
                jnp.where(head0, ob[0], ob[1])) for blk_, mb, lb, ob in zip(blocks, maxes, sums, pvs)]
        if cfg == 0:
            for rows, m, l, num in new:
                acc_ref[rows, :] = num
                m_ref[rows, :] = m
                l_ref[rows, :] = l
            return
        old = [(m_ref[rows, :], l_ref[rows, :], acc_ref[rows, :]) for rows, _, _, _ in new]
        for (rows, m, l, num), (m_old, l_old, acc_old) in zip(new, old):
            m_new = jnp.maximum(m_old, m)
            a = jnp.exp(m_old - m_new)
            b = jnp.exp(m - m_new)
            num = a * acc_old + b * num
            l = a * l_old + b * l
            if cfg == len(DILATED_CFGS) - 1:
                o_ref[rows, :] = num / l
            else:
                acc_ref[rows, :] = num
                m_ref[rows, :] = m_new
                l_ref[rows, :] = l

    grp = ATT_GROUP
    for cfg, (window, d) in enumerate(DILATED_CFGS):
        span = d * blk
        nb = seq // span
        if d >= grp:
            def residue_group(g, carry, cfg=cfg, d=d, span=span, nb=nb):
                res = [g * grp + j for j in range(grp)]
                attend_group(cfg, d, [(r, None) for r in res])
                if nb > 1:
                    def later(n, c):
                        attend_group(cfg, d, [(r + n * span, r + (n - 1) * span) for r in res])
                        return c
                    lax.fori_loop(1, nb, later, 0)
                return carry

            lax.fori_loop(0, d // grp, residue_group, 0)
        else:
            assert d == 1 and nb % grp == 0
            attend_group(cfg, d, [(0, None)] + [(n * span, (n - 1) * span) for n in range(1, grp)])

            def block_group(g, carry, cfg=cfg, d=d, span=span):
                attend_group(cfg, d, [((g * grp + j) * span, (g * grp + j - 1) * span) for j in range(grp)])
                return carry

            lax.fori_loop(1, nb // grp, block_group, 0)


def _attn_prompt(q, k, v):
    bsz, seq, d_att = q.shape
    assert d_att % LANES == 0 and LANES == 2 * ATT_HEAD_DIM
    for window, d in DILATED_CFGS:
        assert window // d == ATT_BLK and seq % (d * ATT_BLK) == 0
    spec = pl.BlockSpec((None, seq, LANES), lambda b, h: (b, 0, h))
    blk_bytes = seq * LANES * 4
    return pl.pallas_call(
        _attn_prompt_body,
        grid=(bsz, d_att // LANES),
        in_specs=[spec, spec, spec],
        out_specs=spec,
        out_shape=jax.ShapeDtypeStruct(q.shape, F32),
        scratch_shapes=[pltpu.VMEM((seq, LANES), F32)] * 3,
        compiler_params=_params(("parallel", "parallel"), 11 * blk_bytes + 8 * 1024 * 1024),
        name="attn_prompt",
    )(q, k, v)


def _attn_window_body(q_ref, kn_ref, vn_ref, kt_ref, vt_ref, o_ref):
    nb, t_new, d_att = q_ref.shape
    w_buf = kt_ref.shape[2]
    nh = N_ATT_HEADS
    rows = t_new * nh
    scale = ATT_HEAD_DIM ** -0.5
    (win1, dil1), (win4, dil4), (_, dil16) = DILATED_CFGS
    tail = win4
    near = win1
    hbits = nh.bit_length() - 1
    row_id = lax.broadcasted_iota(jnp.int32, (rows, 1), 0)
    trow = jnp.right_shift(row_id, hbits)
    head_mask = (jnp.right_shift(lax.broadcasted_iota(jnp.int32, (nh, d_att), 1), ATT_HEAD_DIM.bit_length() - 1)
                 == lax.broadcasted_iota(jnp.int32, (nh, d_att), 0))
    lane = lambda n: lax.broadcasted_iota(jnp.int32, (rows, n), 1)
    mask16 = (lane(w_buf) & (dil16 - 1)) == trow
    mask4 = (lane(tail) & (dil4 - 1)) == trow
    mask1 = lane(near) >= trow
    new_self = lane(t_new) == trow
    new_causal = lane(t_new) <= trow

    def one_sequence(b, carry):
        q = q_ref[b] * scale
        qbd = jnp.concatenate(
            [jnp.where(head_mask, jnp.broadcast_to(q[t:t + 1], (nh, d_att)), 0.0) for t in range(t_new)],
            axis=0).astype(BF16)
        s_all = _dot(qbd, kt_ref[b].astype(BF16))
        s_new = _dot_nt(qbd, kn_ref[b].astype(BF16))

        def softmax_parts(main, main_mask, new_mask):
            sm = jnp.where(main_mask, main, NEG)
            sn = jnp.where(new_mask, s_new, NEG)
            m = jnp.maximum(jnp.max(sm, axis=1, keepdims=True), jnp.max(sn, axis=1, keepdims=True))
            p = jnp.exp(sm - m)
            pn = jnp.exp(sn - m)
            den = jnp.sum(p, axis=1, keepdims=True) + jnp.sum(pn, axis=1, keepdims=True)
            return m, p, pn, den

        m16, p16, pn16, den16 = softmax_parts(s_all, mask16, new_self)
        m4, p4, pn4, den4 = softmax_parts(s_all[:, w_buf - tail:], mask4, new_self)
        m1, p1, pn1, den1 = softmax_parts(s_all[:, w_buf - near:], mask1, new_causal)
        m = jnp.maximum(jnp.maximum(m1, m4), m16)
        w1, w4, w16 = jnp.exp(m1 - m), jnp.exp(m4 - m), jnp.exp(m16 - m)
        den = w1 * den1 + w4 * den4 + w16 * den16
        p16, p4, p1 = w16 * p16, w4 * p4, w1 * p1
        p_all = jnp.concatenate(
            [p16[:, :w_buf - tail],
             p16[:, w_buf - tail:w_buf - near] + p4[:, :tail - near],
             p16[:, w_buf - near:] + p4[:, tail - near:] + p1], axis=1).astype(BF16)
        pn_all = (w16 * pn16 + w4 * pn4 + w1 * pn1).astype(BF16)
        num = _dot_nt(p_all, vt_ref[b].astype(BF16)) + _dot(pn_all, vn_ref[b].astype(BF16))
        out = num / den
        o_ref[b] = jnp.concatenate(
            [jnp.sum(jnp.where(head_mask, out[t * nh:(t + 1) * nh], 0.0), axis=0, keepdims=True)
             for t in range(t_new)], axis=0)
        return carry

    lax.fori_loop(0, nb, one_sequence, 0)


def _attn_window(q, k_new, v_new, cache_k, cache_v, layer, nb):
    bsz, t_new, d_att = q.shape
    depth, _, w_buf, nh, hd = cache_k.shape
    assert w_buf == DILATED_CFGS[-1][0] and t_new <= DILATED_CFGS[1][1] and nh * hd == d_att
    rows_last = lambda c: jnp.transpose(c, (0, 1, 3, 4, 2)).reshape(depth, bsz, d_att, w_buf)
    new_spec = pl.BlockSpec((nb, t_new, d_att), lambda i: (i, 0, 0))
    cache_spec = pl.BlockSpec((None, nb, d_att, w_buf), lambda i: (layer, i, 0, 0))
    vmem = 2 * 2 * nb * d_att * w_buf * 4 + 2 * d_att * w_buf * 2 + 16 * 1024 * 1024
    return pl.pallas_call(
        _attn_window_body,
        grid=(bsz // nb,),
        in_specs=[new_spec, new_spec, new_spec, cache_spec, cache_spec],
        out_specs=new_spec,
        out_shape=jax.ShapeDtypeStruct(q.shape, F32),
        compiler_params=_params(("parallel",), vmem),
        name="attn_window",
    )(q, k_new, v_new, rows_last(cache_k), rows_last(cache_v))


def _ssm_prep_body(a_re_ref, a_im_ref, log_dt_ref, b_re_ref, b_im_ref, lam_re_ref, lam_im_ref,
                   bbar_re_ref, bbar_im_ref):
    ar = a_re_ref[...]
    ai = a_im_ref[...]
    dt = jnp.exp(log_dt_ref[...])
    mag = jnp.exp(dt * ar)
    lr = mag * jnp.cos(dt * ai)
    li = mag * jnp.sin(dt * ai)
    den = ar * ar + ai * ai
    nr, ni = lr - 1.0, li
    cr = (nr * ar + ni * ai) / den
    ci = (ni * ar - nr * ai) / den
    for h in range(b_re_ref.shape[0]):
        br, bi = b_re_ref[h], b_im_ref[h]
        bbar_re_ref[h] = cr * br - ci * bi
        bbar_im_ref[h] = cr * bi + ci * br
    pr, pi = lr, li
    for j in range(lam_re_ref.shape[0]):
        lam_re_ref[j] = pr
        lam_im_ref[j] = pi
        pr, pi = pr * lr - pi * li, pr * li + pi * lr


def _ssm_prep(a_re, a_im, log_dt, b_re, b_im, n_pow):
    g, p = a_re.shape
    h = b_re.shape[-1]
    b_re_t = jnp.transpose(b_re, (2, 0, 1))
    b_im_t = jnp.transpose(b_im, (2, 0, 1))
    gp = jax.ShapeDtypeStruct((n_pow, g, p), F32)
    hgp = jax.ShapeDtypeStruct((h, g, p), F32)
    return pl.pallas_call(
        _ssm_prep_body,
        out_shape=[gp, gp, hgp, hgp],
        name="ssm_prep",
    )(a_re, a_im, log_dt.reshape(g, 1), b_re_t, b_im_t)


def _ssm_layouts(lam_re, lam_im, bbar_re_t, bbar_im_t, c_re, c_im, d_skip):
    n_pow, g, p = lam_re.shape
    h = bbar_re_t.shape[0]
    nq = g // SSM_GB
    eye = jnp.eye(SSM_GB, dtype=bool)
    bb = jnp.stack([bbar_re_t, bbar_im_t], 0).reshape(2, h, nq, SSM_GB, p)
    bb = jnp.transpose(bb, (2, 3, 1, 0, 4))
    bmat = jnp.where(eye[None, :, None, None, :, None], bb[:, :, :, :, None, :], 0.0)
    bmat = bmat.reshape(nq, SSM_GB * h, 2 * SSM_GB * p).astype(BF16)
    def c_layout(c):
        cc = jnp.transpose(c.reshape(nq, SSM_GB, h, p), (0, 1, 3, 2))
        m = jnp.where(eye[None, :, None, :, None], cc[:, :, :, None, :], 0.0)
        return m.reshape(nq, SSM_GB * p, SSM_GB * h).astype(BF16)
    lam_re_q = lam_re.reshape(n_pow, nq, 1, SSM_GB * p)
    lam_im_q = lam_im.reshape(n_pow, nq, 1, SSM_GB * p)
    d_q = d_skip.reshape(nq, 1, SSM_GB * h)
    return bmat, c_layout(c_re), c_layout(c_im), lam_re_q, lam_im_q, d_q


def _ssm_body(u_ref, h0r_ref, h0i_ref, bmat_ref, cre_ref, cim_ref, lamr_ref, lami_ref, d_ref,
              y_ref, hfr_ref, hfi_ref, hloc_ref, hs_ref, *, t1, chained):
    rows = u_ref.shape[0]
    d_ssm = y_ref.shape[1] // t1
    nq = bmat_ref.shape[0]
    ns = lamr_ref.shape[-1]
    rt = SSM_ROW_TILE
    for q in range(nq):
        def u_lanes(i, q=q):
            return slice(i * d_ssm + q * LANES, i * d_ssm + (q + 1) * LANES)
        s_lanes = slice(q * ns, (q + 1) * ns)
        for i in range(t1):
            hloc_ref[i] = _dot(u_ref[:, u_lanes(i)].astype(BF16), bmat_ref[q])
        lr, li = lamr_ref[0, q], lami_ref[0, q]

        def local_scan(tile, carry, lr=lr, li=li):
            r = pl.ds(pl.multiple_of(tile * rt, rt), rt)
            hr = hloc_ref[0, r, 0:ns]
            hi = hloc_ref[0, r, ns:2 * ns]
            for i in range(1, t1):
                hr, hi = (lr * hr - li * hi + hloc_ref[i, r, 0:ns],
                          lr * hi + li * hr + hloc_ref[i, r, ns:2 * ns])
                hloc_ref[i, r, 0:ns] = hr
                hloc_ref[i, r, ns:2 * ns] = hi
            return carry

        lax.fori_loop(0, rows // rt, local_scan, 0)
        if chained:
            cr, ci = lamr_ref[t1 - 1, q], lami_ref[t1 - 1, q]

            def chain(c, g, cr=cr, ci=ci):
                gr, gi = g
                row = pl.ds(c, 1)
                hs_ref[row, 0:ns] = gr
                hs_ref[row, ns:2 * ns] = gi
                er = hloc_ref[t1 - 1, row, 0:ns]
                ei = hloc_ref[t1 - 1, row, ns:2 * ns]
                return cr * gr - ci * gi + er, cr * gi + ci * gr + ei

            gr, gi = lax.fori_loop(0, rows, chain, (h0r_ref[:, s_lanes], h0i_ref[:, s_lanes]))
            hfr_ref[:, s_lanes] = gr
            hfi_ref[:, s_lanes] = gi
        else:
            hs_ref[:, 0:ns] = h0r_ref[:, s_lanes]
            hs_ref[:, ns:2 * ns] = h0i_ref[:, s_lanes]

        def add_carry(tile, carry, q=q):
            r = pl.ds(pl.multiple_of(tile * rt, rt), rt)
            hr0 = hs_ref[r, 0:ns]
            hi0 = hs_ref[r, ns:2 * ns]
            for i in range(t1):
                pr, pi = lamr_ref[i, q], lami_ref[i, q]
                hloc_ref[i, r, 0:ns] = hloc_ref[i, r, 0:ns] + (pr * hr0 - pi * hi0)
                hloc_ref[i, r, ns:2 * ns] = hloc_ref[i, r, ns:2 * ns] + (pr * hi0 + pi * hr0)
            return carry

        lax.fori_loop(0, rows // rt, add_carry, 0)
        if not chained:
            hfr_ref[:, s_lanes] = hloc_ref[t1 - 1, :, 0:ns]
            hfi_ref[:, s_lanes] = hloc_ref[t1 - 1, :, ns:2 * ns]
        for i in range(t1):
            y_ref[:, u_lanes(i)] = (_dot(hloc_ref[i, :, 0:ns].astype(BF16), cre_ref[q])
                                    - _dot(hloc_ref[i, :, ns:2 * ns].astype(BF16), cim_ref[q])
                                    + d_ref[q] * u_ref[:, u_lanes(i)])


def _ssm(u_chunks, h0_re, h0_im, layouts, t1, rows, chained):
    bmat, cre, cim, lam_re_q, lam_im_q, d_q = layouts
    lam_re_q, lam_im_q = lam_re_q[:t1], lam_im_q[:t1]
    n_blocks, rows_h, n_state = h0_re.shape
    width = u_chunks.shape[1]
    assert rows % SSM_ROW_TILE == 0 and u_chunks.shape[0] == n_blocks * rows
    ns2 = bmat.shape[2]
    u_spec = pl.BlockSpec((rows, width), lambda b: (b, 0))
    h_spec = pl.BlockSpec((None, rows_h, n_state), lambda b: (b, 0, 0))
    consts = [bmat, cre, cim, lam_re_q, lam_im_q, d_q]
    vmem = (4 * rows * width * 4 + (t1 + 1) * rows * ns2 * 4 + 2 * sum(c.size * c.dtype.itemsize for c in consts)
            + 8 * rows_h * n_state * 4 + 8 * 1024 * 1024)
    h_out = jax.ShapeDtypeStruct(h0_re.shape, F32)
    return pl.pallas_call(
        functools.partial(_ssm_body, t1=t1, chained=chained),
        grid=(n_blocks,),
        in_specs=[u_spec, h_spec, h_spec] + [_const_spec(c.shape) for c in consts],
        out_specs=[u_spec, h_spec, h_spec],
        out_shape=[jax.ShapeDtypeStruct(u_chunks.shape, F32), h_out, h_out],
        scratch_shapes=[pltpu.VMEM((t1, rows, ns2), F32), pltpu.VMEM((rows, ns2), F32)],
        compiler_params=_params(("parallel",), vmem),
        name="ssm_chained" if chained else "ssm_rows",
    )(u_chunks, h0_re, h0_im, *consts)


def _mix_body(x_ref, oatt_ref, y_ref, wglu_ref, bglu_ref, gatt_ref, gssm_ref, wout_ref, lng_ref, lnb_ref,
              o_ref, *, alpha):
    d_att = oatt_ref.shape[1]
    g = jax.nn.gelu(y_ref[...])
    z = g * jax.nn.sigmoid(_dot(g.astype(BF16), wglu_ref[...]) + bglu_ref[...])
    ra = _rms_norm(oatt_ref[...], gatt_ref[...]).astype(BF16)
    rz = _rms_norm(z, gssm_ref[...]).astype(BF16)
    mixed = _dot(ra, wout_ref[0:d_att, :]) + _dot(rz, wout_ref[d_att:, :])
    o_ref[...] = _layer_norm(alpha * x_ref[...] + mixed, lng_ref[...], lnb_ref[...])


def _mix(x2d, o_att, y_ssm, w_glu, b_glu, g_att, g_ssm, w_out, ln_g, ln_b, alpha, tm):
    rows, dm = x2d.shape
    d_att, d_ssm = o_att.shape[1], y_ssm.shape[1]
    consts = [w_glu, b_glu.reshape(1, -1), g_att.reshape(1, -1), g_ssm.reshape(1, -1), w_out,
              ln_g.reshape(1, -1), ln_b.reshape(1, -1)]
    row = lambda width: pl.BlockSpec((tm, width), lambda i: (i, 0))
    vmem = 2 * tm * (2 * dm + d_att + d_ssm) * 4 + 4 * (w_glu.size + w_out.size) + 12 * tm * dm * 4
    return pl.pallas_call(
        functools.partial(_mix_body, alpha=alpha),
        grid=(rows // tm,),
        in_specs=[row(dm), row(d_att), row(d_ssm)] + [_const_spec(c.shape) for c in consts],
        out_specs=row(dm),
        out_shape=jax.ShapeDtypeStruct(x2d.shape, F32),
        compiler_params=_params(("parallel",), vmem),
        name="mix_out_ln1",
    )(x2d, o_att, y_ssm, *consts)


def _memkv_body(m_ref, wk_ref, wv_ref, k_ref, v_ref):
    mb = m_ref[...].astype(BF16)
    k_ref[...] = _dot(mb, wk_ref[...])
    v_ref[...] = _dot(mb, wv_ref[...])


def _memkv(mem2d, wk, wv, tm):
    rows, dm = mem2d.shape
    row = pl.BlockSpec((tm, dm), lambda i: (i, 0))
    out = jax.ShapeDtypeStruct((rows, wk.shape[1]), F32)
    vmem = 2 * 3 * tm * dm * 4 + 4 * (wk.size + wv.size) + 4 * tm * dm * 4
    return pl.pallas_call(
        _memkv_body,
        grid=(rows // tm,),
        in_specs=[row, _const_spec(wk.shape), _const_spec(wv.shape)],
        out_specs=[row, row],
        out_shape=[out, out],
        compiler_params=_params(("parallel",), vmem),
        name="mem_kv",
    )(mem2d, wk, wv)


def _memattn_body(x_ref, mk_ref, mv_ref, wq_ref, wo_ref, lng_ref, lnb_ref, o_ref, *, alpha):
    dm = x_ref.shape[1]
    hd = dm // N_MEM_HEADS
    heads = [slice(h * hd, (h + 1) * hd) for h in range(N_MEM_HEADS)]
    x = x_ref[...]
    q = (_dot(x.astype(BF16), wq_ref[...]) * (hd ** -0.5)).astype(BF16)
    scores = [_dot_nt(q[:, c], mk_ref[:, c].astype(BF16)) for c in heads]
    probs = [jnp.exp(s - jnp.max(s, axis=1, keepdims=True)) for s in scores]
    sums = [jnp.sum(p, axis=1, keepdims=True) for p in probs]
    outs = [_dot(p.astype(BF16), mv_ref[:, c].astype(BF16)) / l for p, l, c in zip(probs, sums, heads)]
    att = _dot(jnp.concatenate(outs, axis=1).astype(BF16), wo_ref[...])
    o_ref[...] = _layer_norm(alpha * x + att, lng_ref[...], lnb_ref[...])


def _memattn(x2d, mem_k, mem_v, wq, wo, ln_g, ln_b, alpha, tm):
    rows, dm = x2d.shape
    n_seq, n_mem, _ = mem_k.shape
    steps_per_mem = rows // n_seq // tm
    row = pl.BlockSpec((tm, dm), lambda i: (i, 0))
    mem_spec = pl.BlockSpec((None, n_mem, dm), lambda i: (i // steps_per_mem, 0, 0))
    consts = [wq, wo, ln_g.reshape(1, -1), ln_b.reshape(1, -1)]
    vmem = 2 * 2 * tm * dm * 4 + 2 * 2 * n_mem * dm * 4 + 4 * (wq.size + wo.size) + 12 * tm * dm * 4
    return pl.pallas_call(
        functools.partial(_memattn_body, alpha=alpha),
        grid=(rows // tm,),
        in_specs=[row, mem_spec, mem_spec] + [_const_spec(c.shape) for c in consts],
        out_specs=row,
        out_shape=jax.ShapeDtypeStruct(x2d.shape, F32),
        compiler_params=_params(("parallel",), vmem),
        name="mem_attn_ln2",
    )(x2d, mem_k, mem_v, *consts)


def _rows_matmul_body(x_ref, w_ref, o_ref, *, scale):
    o_ref[...] = _dot(x_ref[...].astype(BF16), w_ref[...]) * scale


def _rows_matmul(x2d, w, scale):
    rows, dm = x2d.shape
    vmem = 4 * rows * (dm + w.shape[1]) * 4 + 4 * w.size
    return pl.pallas_call(
        functools.partial(_rows_matmul_body, scale=scale),
        out_shape=jax.ShapeDtypeStruct((rows, w.shape[1]), F32),
        compiler_params=_params(None, vmem),
        name="rows_matmul",
    )(x2d, w)


def _memattn_cache_body(q_ref, mk_ref, mv_ref, o_ref):
    nb, rows, hd = q_ref.shape
    n_mem, nh = mk_ref.shape[1:3]
    cols = n_mem * nh
    head_ok = ((lax.broadcasted_iota(jnp.int32, (rows, cols), 0) & (nh - 1))
               == (lax.broadcasted_iota(jnp.int32, (rows, cols), 1) & (nh - 1)))
    scores = [jnp.where(head_ok, _dot_nt(q_ref[j].astype(BF16), mk_ref[j].reshape(cols, hd).astype(BF16)), NEG)
              for j in range(nb)]
    probs = [jnp.exp(s - jnp.max(s, axis=1, keepdims=True)) for s in scores]
    sums = [jnp.sum(p, axis=1, keepdims=True) for p in probs]
    for j in range(nb):
        o_ref[j] = _dot(probs[j].astype(BF16), mv_ref[j].reshape(cols, hd).astype(BF16)) / sums[j]


def _memattn_cache(q3, cache_k, cache_v, layer, nb):
    bsz, rows, hd = q3.shape
    _, _, n_mem, nh, _ = cache_k.shape
    q_spec = pl.BlockSpec((nb, rows, hd), lambda i: (i, 0, 0))
    c_spec = pl.BlockSpec((None, nb, n_mem, nh, hd), lambda i: (layer, i, 0, 0, 0))
    vmem = 2 * 2 * nb * n_mem * 8 * hd * 4 + 8 * nb * rows * n_mem * nh * 4 + 8 * 1024 * 1024
    return pl.pallas_call(
        _memattn_cache_body,
        grid=(bsz // nb,),
        in_specs=[q_spec, c_spec, c_spec],
        out_specs=q_spec,
        out_shape=jax.ShapeDtypeStruct(q3.shape, F32),
        compiler_params=_params(("parallel",), vmem),
        name="mem_attn_cache",
    )(q3, cache_k, cache_v)


def _proj_ln_body(x_ref, a_ref, w_ref, lng_ref, lnb_ref, o_ref, *, alpha):
    att = _dot(a_ref[...].astype(BF16), w_ref[...])
    o_ref[...] = _layer_norm(alpha * x_ref[...] + att, lng_ref[...], lnb_ref[...])


def _proj_ln(x2d, a2d, w, ln_g, ln_b, alpha):
    rows, dm = x2d.shape
    vmem = 8 * rows * dm * 4 + 4 * w.size
    return pl.pallas_call(
        functools.partial(_proj_ln_body, alpha=alpha),
        out_shape=jax.ShapeDtypeStruct(x2d.shape, F32),
        compiler_params=_params(None, vmem),
        name="proj_ln",
    )(x2d, a2d, w, ln_g.reshape(1, -1), ln_b.reshape(1, -1))


def _ffn_body(x_ref, wg_ref, wu_ref, wd_ref, lng_ref, lnb_ref, o_ref, acc_ref, *, alpha, tf):
    x = x_ref[...]
    xb = x.astype(BF16)
    d_ff = wg_ref.shape[1]
    for c in range(d_ff // tf):
        cols = slice(c * tf, (c + 1) * tf)
        hid = (jax.nn.silu(_dot(xb, wg_ref[:, cols])) * _dot(xb, wu_ref[:, cols])).astype(BF16)
        part = _dot(hid, wd_ref[cols, :])
        if c == 0:
            acc_ref[...] = part
        else:
            acc_ref[...] += part
    o_ref[...] = _layer_norm(alpha * x + acc_ref[...], lng_ref[...], lnb_ref[...])


def _ffn(x2d, wg, wu, wd, ln_g, ln_b, alpha, tm, tf):
    rows, dm = x2d.shape
    d_ff = wg.shape[1]
    assert d_ff % tf == 0 and tf % LANES == 0
    row = pl.BlockSpec((tm, dm), lambda i: (i, 0))
    consts = [wg, wu, wd, ln_g.reshape(1, -1), ln_b.reshape(1, -1)]
    vmem = 2 * 2 * tm * dm * 4 + 4 * 3 * wg.size + tm * dm * 4 + 6 * tm * max(tf, dm) * 4
    return pl.pallas_call(
        functools.partial(_ffn_body, alpha=alpha, tf=tf),
        grid=(rows // tm,),
        in_specs=[row] + [_const_spec(c.shape) for c in consts],
        out_specs=row,
        out_shape=jax.ShapeDtypeStruct(x2d.shape, F32),
        scratch_shapes=[pltpu.VMEM((tm, dm), F32)],
        compiler_params=_params(("parallel",), vmem),
        name="swiglu_ln3",
    )(x2d, *consts)


PROMPT_CHUNK = 8
ROW_TILE = 512
FFN_COL_TILE = 256
SAMPLE_ATTN_SEQS = 2
SAMPLE_MEM_SEQS = 4


def kernel(x_prompt, x_sample, cache_win_k, cache_win_v, state_ssm_re, state_ssm_im, cache_mem_k, cache_mem_v, mem_prompt, w_in, g_att, g_ssm, ssm_a_re, ssm_a_im, ssm_log_dt, ssm_b_re, ssm_b_im, ssm_c_re, ssm_c_im, ssm_d, w_glu, b_glu, w_out, ln1_g, ln1_b, w_mem_q, w_mem_k, w_mem_v, w_mem_o, ln2_g, ln2_b, w_gate, w_up, w_down, ln3_g, ln3_b):
    depth = w_in.shape[0]
    bp, seq, dm = x_prompt.shape
    bs, t_new, _ = x_sample.shape
    n_groups, n_state = ssm_a_re.shape[1:]
    n_mem = mem_prompt.shape[1]
    alpha = (2 * depth) ** 0.25
    keep = min(DILATED_CFGS[-1][0], seq)
    assert keep == seq
    n_chunks = seq // PROMPT_CHUNK

    y_p = x_prompt.reshape(bp * seq, dm)
    y_s = x_sample.reshape(bs * t_new, dm)
    mem2d = mem_prompt.reshape(bp * n_mem, dm)
    outs = [[] for _ in range(10)]
    for l in range(depth):
        bf = lambda w: w[l].astype(BF16)
        w_in_l, w_glu_l, w_out_l = bf(w_in), bf(w_glu), bf(w_out)
        wq_l, wk_l, wv_l, wo_l = bf(w_mem_q), bf(w_mem_k), bf(w_mem_v), bf(w_mem_o)
        wg_l, wu_l, wd_l = bf(w_gate), bf(w_up), bf(w_down)
        lam_re, lam_im, bbar_re_t, bbar_im_t = _ssm_prep(
            ssm_a_re[l], ssm_a_im[l], ssm_log_dt[l], ssm_b_re[l], ssm_b_im[l], max(PROMPT_CHUNK, t_new))
        layouts = _ssm_layouts(lam_re, lam_im, bbar_re_t, bbar_im_t, ssm_c_re[l], ssm_c_im[l], ssm_d[l])
        mix_w = (w_glu_l, b_glu[l], g_att[l], g_ssm[l], w_out_l, ln1_g[l], ln1_b[l])

        q, k, v, u = _inproj(y_p, w_in_l, ROW_TILE)
        d_att = q.shape[1]
        o_att = _attn_prompt(q.reshape(bp, seq, d_att), k.reshape(bp, seq, d_att), v.reshape(bp, seq, d_att))
        zeros = jnp.zeros((bp, 1, n_groups * n_state), F32)
        y_ssm, hr_p, hi_p = _ssm(u.reshape(bp * n_chunks, PROMPT_CHUNK * u.shape[1]), zeros, zeros, layouts,
                                 PROMPT_CHUNK, n_chunks, True)
        x1 = _mix(y_p, o_att.reshape(bp * seq, d_att), y_ssm.reshape(bp * seq, -1), *mix_w, alpha, ROW_TILE)
        mk_p, mv_p = _memkv(mem2d, wk_l, wv_l, ROW_TILE)
        x2 = _memattn(x1, mk_p.reshape(bp, n_mem, dm), mv_p.reshape(bp, n_mem, dm), wq_l, wo_l,
                      ln2_g[l], ln2_b[l], alpha, ROW_TILE)
        y_p = _ffn(x2, wg_l, wu_l, wd_l, ln3_g[l], ln3_b[l], alpha, ROW_TILE, FFN_COL_TILE)

        qs, ks, vs, us = _inproj(y_s, w_in_l, bs * t_new)
        shp = (bs, t_new, d_att)
        o_att_s = _attn_window(qs.reshape(shp), ks.reshape(shp), vs.reshape(shp), cache_win_k, cache_win_v, l,
                               SAMPLE_ATTN_SEQS)
        y_ssm_s, hr_s, hi_s = _ssm(us.reshape(bs, t_new * us.shape[1]),
                                   state_ssm_re[l].reshape(1, bs, -1), state_ssm_im[l].reshape(1, bs, -1),
                                   layouts, t_new, bs, False)
        x1s = _mix(y_s, o_att_s.reshape(bs * t_new, d_att), y_ssm_s.reshape(bs * t_new, -1), *mix_w, alpha,
                   bs * t_new)
        mem_hd = dm // N_MEM_HEADS
        q_mem = _rows_matmul(x1s, wq_l, mem_hd ** -0.5).reshape(bs, t_new * N_MEM_HEADS, mem_hd)
        a_mem = _memattn_cache(q_mem, cache_mem_k, cache_mem_v, l, SAMPLE_MEM_SEQS)
        x2s = _proj_ln(x1s, a_mem.reshape(bs * t_new, dm), wo_l, ln2_g[l], ln2_b[l], alpha)
        y_s = _ffn(x2s, wg_l, wu_l, wd_l, ln3_g[l], ln3_b[l], alpha, bs * t_new, FFN_COL_TILE)

        head_shape = (N_ATT_HEADS, ATT_HEAD_DIM)
        state_shape = (n_groups, n_state)
        mem_shape = (bp, n_mem, N_MEM_HEADS, dm // N_MEM_HEADS)
        for lst, val in zip(outs, (
                k.reshape(bp, seq, *head_shape)[:, seq - keep:], v.reshape(bp, seq, *head_shape)[:, seq - keep:],
                ks.reshape(bs, t_new, *head_shape), vs.reshape(bs, t_new, *head_shape),
                hr_p.reshape(bp, *state_shape), hi_p.reshape(bp, *state_shape),
                hr_s.reshape(bs, *state_shape), hi_s.reshape(bs, *state_shape),
                mk_p.reshape(mem_shape), mv_p.reshape(mem_shape))):
            lst.append(val)
    return (y_p.reshape(bp, seq, dm), y_s.reshape(bs, t_new, dm)) + tuple(jnp.stack(o) for o in outs)
```

```python
import functools
import math

import jax
import jax.numpy as jnp
from jax import lax
from jax.experimental import pallas as pl
from jax.experimental.pallas import tpu as pltpu

F32 = jnp.float32
BF16 = jnp.bfloat16

N_ATT_HEADS = 8
ATT_HEAD_DIM = 64
D_ATT = N_ATT_HEADS * ATT_HEAD_DIM
DILATED_CFGS = ((128, 1), (512, 4), (2048, 16))
ATT_BLK = 128
ATT_GROUP = 4
SSM_CH = 16
SSM_STATE = 64
N_MEM_HEADS = 4
EPS = 1e-5
NEG = -1e30

LANES = 128
V7X_VMEM_CAP_BYTES = 56 * 1024 * 1024

SSM_GB = LANES // SSM_CH
SSM_ROW_TILE = 32


def _params(sem, vmem_bytes):
    return pltpu.CompilerParams(
        dimension_semantics=sem,
        vmem_limit_bytes=int(min(max(vmem_bytes, 16 * 1024 * 1024), V7X_VMEM_CAP_BYTES)),
    )


def _dot(a, b):
    return jnp.dot(a, b, preferred_element_type=F32)


def _dot_nt(a, b):
    return lax.dot_general(a, b, (((1,), (1,)), ((), ())), preferred_element_type=F32)


def _layer_norm(x, g, b):
    mu = jnp.mean(x, axis=-1, keepdims=True)
    xc = x - mu
    var = jnp.mean(xc * xc, axis=-1, keepdims=True)
    return xc * lax.rsqrt(var + EPS) * g + b


def _rms_norm(x, g):
    return x * lax.rsqrt(jnp.mean(x * x, axis=-1, keepdims=True) + EPS) * g


def _const_spec(shape):
    n = len(shape)
    return pl.BlockSpec(shape, lambda *_: (0,) * n)


def _inproj_body(x_ref, w_ref, q_ref, k_ref, v_ref, uc_ref, *rest, t1, transposed):
    u_scr = rest[-1]
    xb = x_ref[...].astype(BF16)
    d = q_ref.shape[1]
    q_ref[...] = _dot(xb, w_ref[:, 0 * d:1 * d])
    k = _dot(xb, w_ref[:, 1 * d:2 * d])
    v = _dot(xb, w_ref[:, 2 * d:3 * d])
    k_ref[...] = k
    v_ref[...] = v
    if transposed:
        kt_ref, vt_ref = rest[:2]
        kt_ref[...] = k.T
        vt_ref[...] = v.T
    u = _dot(xb, w_ref[:, 3 * d:4 * d])
    chunks = u_scr.shape[1] // t1
    for c in range(d // LANES):
        u_scr[c] = u[:, c * LANES:(c + 1) * LANES]
        for i in range(t1):
            uc_ref[:, i * d + c * LANES:i * d + (c + 1) * LANES] = u_scr[c, pl.ds(i, chunks, stride=t1), :]


def _inproj(x2d, w_in_bf16, tm, t1, seq=None):
    rows, dm = x2d.shape
    d = w_in_bf16.shape[1] // 4
    out = jax.ShapeDtypeStruct((rows, d), F32)
    row_spec = pl.BlockSpec((tm, d), lambda i: (i, 0))
    out_specs = [row_spec] * 3 + [pl.BlockSpec((tm // t1, t1 * d), lambda i: (i, 0))]
    out_shape = [out] * 3 + [jax.ShapeDtypeStruct((rows // t1, t1 * d), F32)]
    if seq is not None:
        steps = seq // tm
        out_specs += [pl.BlockSpec((None, d, tm), lambda i: (i // steps, 0, i % steps))] * 2
        out_shape += [jax.ShapeDtypeStruct((rows // seq, d, seq), F32)] * 2
    vmem = 2 * (tm * dm * 4 + 6 * tm * d * 4) + 2 * w_in_bf16.size * 2 + 10 * tm * d * 4
    return pl.pallas_call(
        functools.partial(_inproj_body, t1=t1, transposed=seq is not None),
        grid=(rows // tm,),
        in_specs=[pl.BlockSpec((tm, dm), lambda i: (i, 0)), _const_spec(w_in_bf16.shape)],
        out_specs=out_specs,
        out_shape=out_shape,
        scratch_shapes=[pltpu.VMEM((d // LANES, tm, LANES), F32)],
        compiler_params=_params(("parallel",), vmem),
        name="inproj",
    )(x2d, w_in_bf16)


def _attn_prompt_body(q_ref, k_ref, v_ref, o_ref, acc_ref, m_ref, l_ref):
    seq = q_ref.shape[0]
    blk = ATT_BLK
    scale = ATT_HEAD_DIM ** -0.5
    lane = lax.broadcasted_iota(jnp.int32, (blk, LANES), 1)
    head0 = lane < ATT_HEAD_DIM
    qi = lax.broadcasted_iota(jnp.int32, (blk, blk), 0)
    kj = lax.broadcasted_iota(jnp.int32, (blk, blk), 1)
    tri = kj <= qi
    mask_rest = jnp.concatenate([kj >= qi, tri], axis=1)

    def rows_of(start, d):
        return pl.ds(start, blk) if d == 1 else pl.ds(start, blk, stride=d)

    def load_block(d, q_start, prev_start):
        rows = rows_of(q_start, d)
        q = q_ref[rows, :] * scale
        k2 = k_ref[rows, :].astype(BF16)
        v2 = v_ref[rows, :].astype(BF16)
        mask = tri
        if prev_start is not None:
            prow = rows_of(prev_start, d)
            k2 = jnp.concatenate([k_ref[prow, :].astype(BF16), k2], axis=0)
            v2 = jnp.concatenate([v_ref[prow, :].astype(BF16), v2], axis=0)
            mask = mask_rest
        qh = [jnp.where(head0 if h == 0 else jnp.logical_not(head0), q, 0.0).astype(BF16) for h in range(2)]
        return rows, qh, k2, v2, mask

    def attend_group(cfg, d, starts):
        blocks = [load_block(d, q_start, prev_start) for q_start, prev_start in starts]
        scores = [[jnp.where(mask, _dot_nt(qh[h], k2), NEG) for h in range(2)] for _, qh, k2, _, mask in blocks]
        maxes = [[jnp.max(s, axis=1, keepdims=True) for s in sb] for sb in scores]
        probs = [[jnp.exp(s - m) for s, m in zip(sb, mb)] for sb, mb in zip(scores, maxes)]
        sums = [[jnp.sum(p, axis=1, keepdims=True) for p in pb] for pb in probs]
        pvs = [[_dot(p.astype(BF16), blk_[3]) for p in pb] for pb, blk_ in zip(probs, blocks)]
        new = [(blk_[0], jnp.where(head0, mb[0], mb[1]), jnp.where(head0, lb[0], lb[1]),
                jnp.where(head0, ob[0], ob[1])) for blk_, mb, lb, ob in zip(blocks, maxes, sums, pvs)]
        if cfg == 0:
            for rows, m, l, num in new:
                acc_ref[rows, :] = num
                m_ref[rows, :] = m
                l_ref[rows, :] = l
            return
        old = [(m_ref[rows, :], l_ref[rows, :], acc_ref[rows, :]) for rows, _, _, _ in new]
        for (rows, m, l, num), (m_old, l_old, acc_old) in zip(new, old):
            m_new = jnp.maximum(m_old, m)
            a = jnp.exp(m_old - m_new)
            b = jnp.exp(m - m_new)
            num = a * acc_old + b * num
            l = a * l_old + b * l
            if cfg == len(DILATED_CFGS) - 1:
                o_ref[rows, :] = num / l
            else:
                acc_ref[rows, :] = num
                m_ref[rows, :] = m_new
                l_ref[rows, :] = l

    grp = ATT_GROUP
    for cfg, (window, d) in enumerate(DILATED_CFGS):
        span = d * blk
        nb = seq // span
        if d >= grp:
            def residue_group(g, carry, cfg=cfg, d=d, span=span, nb=nb):
                res = [g * grp + j for j in range(grp)]
                attend_group(cfg, d, [(r, None) for r in res])
                if nb > 1:
                    def later(n, c):
                        attend_group(cfg, d, [(r + n * span, r + (n - 1) * span) for r in res])
                        return c
                    lax.fori_loop(1, nb, later, 0)
                return carry

            lax.fori_loop(0, d // grp, residue_group, 0)
        else:
            assert d == 1 and nb % grp == 0
            attend_group(cfg, d, [(0, None)] + [(n * span, (n - 1) * span) for n in range(1, grp)])

            def block_group(g, carry, cfg=cfg, d=d, span=span):
                attend_group(cfg, d, [((g * grp + j) * span, (g * grp + j - 1) * span) for j in range(grp)])
                return carry

            lax.fori_loop(1, nb // grp, block_group, 0)


def _attn_prompt(q, k, v):
    bsz, seq, d_att = q.shape
    assert d_att % LANES == 0 and LANES == 2 * ATT_HEAD_DIM
    for window, d in DILATED_CFGS:
        assert window // d == ATT_BLK and seq % (d * ATT_BLK) == 0
    spec = pl.BlockSpec((None, seq, LANES), lambda b, h: (b, 0, h))
    blk_bytes = seq * LANES * 4
    return pl.pallas_call(
        _attn_prompt_body,
        grid=(bsz, d_att // LANES),
        in_specs=[spec, spec, spec],
        out_specs=spec,
        out_shape=jax.ShapeDtypeStruct(q.shape, F32),
        scratch_shapes=[pltpu.VMEM((seq, LANES), F32)] * 3,
        compiler_params=_params(("parallel", "parallel"), 11 * blk_bytes + 8 * 1024 * 1024),
        name="attn_prompt",
    )(q, k, v)


def _attn_window_body(q_ref, kn_ref, vn_ref, kt_ref, vt_ref, o_ref):
    nb, t_new, d_att = q_ref.shape
    w_buf = kt_ref.shape[2]
    nh = N_ATT_HEADS
    rows = t_new * nh
    scale = ATT_HEAD_DIM ** -0.5
    (win1, dil1), (win4, dil4), (_, dil16) = DILATED_CFGS
    tail = win4
    near = win1
    hbits = nh.bit_length() - 1
    row_id = lax.broadcasted_iota(jnp.int32, (rows, 1), 0)
    trow = jnp.right_shift(row_id, hbits)
    head_mask = (jnp.right_shift(lax.broadcasted_iota(jnp.int32, (nh, d_att), 1), ATT_HEAD_DIM.bit_length() - 1)
                 == lax.broadcasted_iota(jnp.int32, (nh, d_att), 0))
    lane = lambda n: lax.broadcasted_iota(jnp.int32, (rows, n), 1)
    mask16 = (lane(w_buf) & (dil16 - 1)) == trow
    mask4 = (lane(tail) & (dil4 - 1)) == trow
    mask1 = lane(near) >= trow
    new_self = lane(t_new) == trow
    new_causal = lane(t_new) <= trow

    def one_sequence(b, carry):
        q = q_ref[b] * scale
        qbd = jnp.concatenate(
            [jnp.where(head_mask, jnp.broadcast_to(q[t:t + 1], (nh, d_att)), 0.0) for t in range(t_new)],
            axis=0).astype(BF16)
        s_all = _dot(qbd, kt_ref[b].astype(BF16))
        s_new = _dot_nt(qbd, kn_ref[b].astype(BF16))

        def softmax_parts(main, main_mask, new_mask):
            sm = jnp.where(main_mask, main, NEG)
            sn = jnp.where(new_mask, s_new, NEG)
            m = jnp.maximum(jnp.max(sm, axis=1, keepdims=True), jnp.max(sn, axis=1, keepdims=True))
            p = jnp.exp(sm - m)
            pn = jnp.exp(sn - m)
            den = jnp.sum(p, axis=1, keepdims=True) + jnp.sum(pn, axis=1, keepdims=True)
            return m, p, pn, den

        m16, p16, pn16, den16 = softmax_parts(s_all, mask16, new_self)
        m4, p4, pn4, den4 = softmax_parts(s_all[:, w_buf - tail:], mask4, new_self)
        m1, p1, pn1, den1 = softmax_parts(s_all[:, w_buf - near:], mask1, new_causal)
        m = jnp.maximum(jnp.maximum(m1, m4), m16)
        w1, w4, w16 = jnp.exp(m1 - m), jnp.exp(m4 - m), jnp.exp(m16 - m)
        den = w1 * den1 + w4 * den4 + w16 * den16
        p16, p4, p1 = w16 * p16, w4 * p4, w1 * p1
        p_all = jnp.concatenate(
            [p16[:, :w_buf - tail],
             p16[:, w_buf - tail:w_buf - near] + p4[:, :tail - near],
             p16[:, w_buf - near:] + p4[:, tail - near:] + p1], axis=1).astype(BF16)
        pn_all = (w16 * pn16 + w4 * pn4 + w1 * pn1).astype(BF16)
        num = _dot_nt(p_all, vt_ref[b].astype(BF16)) + _dot(pn_all, vn_ref[b].astype(BF16))
        out = num / den
        o_ref[b] = jnp.concatenate(
            [jnp.sum(jnp.where(head_mask, out[t * nh:(t + 1) * nh], 0.0), axis=0, keepdims=True)
             for t in range(t_new)], axis=0)
        return carry

    lax.fori_loop(0, nb, one_sequence, 0)


def _attn_window(q, k_new, v_new, cache_k, cache_v, layer, nb):
    bsz, t_new, d_att = q.shape
    depth, _, w_buf, nh, hd = cache_k.shape
    assert w_buf == DILATED_CFGS[-1][0] and t_new <= DILATED_CFGS[1][1] and nh * hd == d_att
    rows_last = lambda c: jnp.transpose(c, (0, 1, 3, 4, 2)).reshape(depth, bsz, d_att, w_buf)
    new_spec = pl.BlockSpec((nb, t_new, d_att), lambda i: (i, 0, 0))
    cache_spec = pl.BlockSpec((None, nb, d_att, w_buf), lambda i: (layer, i, 0, 0))
    vmem = 2 * 2 * nb * d_att * w_buf * 4 + 2 * d_att * w_buf * 2 + 16 * 1024 * 1024
    return pl.pallas_call(
        _attn_window_body,
        grid=(bsz // nb,),
        in_specs=[new_spec, new_spec, new_spec, cache_spec, cache_spec],
        out_specs=new_spec,
        out_shape=jax.ShapeDtypeStruct(q.shape, F32),
        compiler_params=_params(("parallel",), vmem),
        name="attn_window",
    )(q, k_new, v_new, rows_last(cache_k), rows_last(cache_v))


def _ssm_prep_body(a_re_ref, a_im_ref, log_dt_ref, b_re_ref, b_im_ref, lam_re_ref, lam_im_ref,
                   bbar_re_ref, bbar_im_ref):
    ar = a_re_ref[...]
    ai = a_im_ref[...]
    dt = jnp.exp(log_dt_ref[...])
    mag = jnp.exp(dt * ar)
    lr = mag * jnp.cos(dt * ai)
    li = mag * jnp.sin(dt * ai)
    den = ar * ar + ai * ai
    nr, ni = lr - 1.0, li
    cr = (nr * ar + ni * ai) / den
    ci = (ni * ar - nr * ai) / den
    for h in range(b_re_ref.shape[0]):
        br, bi = b_re_ref[h], b_im_ref[h]
        bbar_re_ref[h] = cr * br - ci * bi
        bbar_im_ref[h] = cr * bi + ci * br
    pr, pi = lr, li
    for j in range(lam_re_ref.shape[0]):
        lam_re_ref[j] = pr
        lam_im_ref[j] = pi
        pr, pi = pr * lr - pi * li, pr * li + pi * lr


def _ssm_prep(a_re, a_im, log_dt, b_re, b_im, n_pow):
    g, p = a_re.shape
    h = b_re.shape[-1]
    b_re_t = jnp.transpose(b_re, (2, 0, 1))
    b_im_t = jnp.transpose(b_im, (2, 0, 1))
    gp = jax.ShapeDtypeStruct((n_pow, g, p), F32)
    hgp = jax.ShapeDtypeStruct((h, g, p), F32)
    return pl.pallas_call(
        _ssm_prep_body,
        out_shape=[gp, gp, hgp, hgp],
        name="ssm_prep",
    )(a_re, a_im, log_dt.reshape(g, 1), b_re_t, b_im_t)


def _ssm_layouts(lam_re, lam_im, bbar_re_t, bbar_im_t, c_re, c_im, d_skip):
    n_pow, g, p = lam_re.shape
    h = bbar_re_t.shape[0]
    nq = g // SSM_GB
    eye = jnp.eye(SSM_GB, dtype=bool)
    bb = jnp.stack([bbar_re_t, bbar_im_t], 0).reshape(2, h, nq, SSM_GB, p)
    bb = jnp.transpose(bb, (2, 3, 1, 0, 4))
    bmat = jnp.where(eye[None, :, None, None, :, None], bb[:, :, :, :, None, :], 0.0)
    bmat = bmat.reshape(nq, SSM_GB * h, 2 * SSM_GB * p).astype(BF16)
    def c_layout(c):
        cc = jnp.transpose(c.reshape(nq, SSM_GB, h, p), (0, 1, 3, 2))
        m = jnp.where(eye[None, :, None, :, None], cc[:, :, :, None, :], 0.0)
        return m.reshape(nq, SSM_GB * p, SSM_GB * h).astype(BF16)
    lam_re_q = lam_re.reshape(n_pow, nq, 1, SSM_GB * p)
    lam_im_q = lam_im.reshape(n_pow, nq, 1, SSM_GB * p)
    d_q = d_skip.reshape(nq, 1, SSM_GB * h)
    return bmat, c_layout(c_re), c_layout(c_im), lam_re_q, lam_im_q, d_q


def _ssm_body(u_ref, h0r_ref, h0i_ref, bmat_ref, cre_ref, cim_ref, lamr_ref, lami_ref, d_ref,
              y_ref, hfr_ref, hfi_ref, hloc_ref, hs_ref, *, t1, chained):
    rows = u_ref.shape[0]
    d_ssm = y_ref.shape[1] // t1
    nq = bmat_ref.shape[0]
    ns = lamr_ref.shape[-1]
    rt = SSM_ROW_TILE
    for q in range(nq):
        def u_lanes(i, q=q):
            return slice(i * d_ssm + q * LANES, i * d_ssm + (q + 1) * LANES)
        s_lanes = slice(q * ns, (q + 1) * ns)
        for i in range(t1):
            hloc_ref[i] = _dot(u_ref[:, u_lanes(i)].astype(BF16), bmat_ref[q])
        lr, li = lamr_ref[0, q], lami_ref[0, q]

        def local_scan(tile, carry, lr=lr, li=li):
            r = pl.ds(pl.multiple_of(tile * rt, rt), rt)
            hr = hloc_ref[0, r, 0:ns]
            hi = hloc_ref[0, r, ns:2 * ns]
            for i in range(1, t1):
                hr, hi = (lr * hr - li * hi + hloc_ref[i, r, 0:ns],
                          lr * hi + li * hr + hloc_ref[i, r, ns:2 * ns])
                hloc_ref[i, r, 0:ns] = hr
                hloc_ref[i, r, ns:2 * ns] = hi
            return carry

        lax.fori_loop(0, rows // rt, local_scan, 0)
        if chained:
            cr, ci = lamr_ref[t1 - 1, q], lami_ref[t1 - 1, q]

            def chain(c, g, cr=cr, ci=ci):
                gr, gi = g
                row = pl.ds(c, 1)
                hs_ref[row, 0:ns] = gr
                hs_ref[row, ns:2 * ns] = gi
                er = hloc_ref[t1 - 1, row, 0:ns]
                ei = hloc_ref[t1 - 1, row, ns:2 * ns]
                return cr * gr - ci * gi + er, cr * gi + ci * gr + ei

            gr, gi = lax.fori_loop(0, rows, chain, (h0r_ref[:, s_lanes], h0i_ref[:, s_lanes]))
            hfr_ref[:, s_lanes] = gr
            hfi_ref[:, s_lanes] = gi
        else:
            hs_ref[:, 0:ns] = h0r_ref[:, s_lanes]
            hs_ref[:, ns:2 * ns] = h0i_ref[:, s_lanes]

        def add_carry(tile, carry, q=q):
            r = pl.ds(pl.multiple_of(tile * rt, rt), rt)
            hr0 = hs_ref[r, 0:ns]
            hi0 = hs_ref[r, ns:2 * ns]
            for i in range(t1):
                pr, pi = lamr_ref[i, q], lami_ref[i, q]
                hloc_ref[i, r, 0:ns] = hloc_ref[i, r, 0:ns] + (pr * hr0 - pi * hi0)
                hloc_ref[i, r, ns:2 * ns] = hloc_ref[i, r, ns:2 * ns] + (pr * hi0 + pi * hr0)
            return carry

        lax.fori_loop(0, rows // rt, add_carry, 0)
        if not chained:
            hfr_ref[:, s_lanes] = hloc_ref[t1 - 1, :, 0:ns]
            hfi_ref[:, s_lanes] = hloc_ref[t1 - 1, :, ns:2 * ns]
        for i in range(t1):
            y_ref[:, u_lanes(i)] = (_dot(hloc_ref[i, :, 0:ns].astype(BF16), cre_ref[q])
                                    - _dot(hloc_ref[i, :, ns:2 * ns].astype(BF16), cim_ref[q])
                                    + d_ref[q] * u_ref[:, u_lanes(i)])


def _ssm(u_chunks, h0_re, h0_im, layouts, t1, rows, chained):
    bmat, cre, cim, lam_re_q, lam_im_q, d_q = layouts
    lam_re_q, lam_im_q = lam_re_q[:t1], lam_im_q[:t1]
    n_blocks, rows_h, n_state = h0_re.shape
    width = u_chunks.shape[1]
    assert rows % SSM_ROW_TILE == 0 and u_chunks.shape[0] == n_blocks * rows
    ns2 = bmat.shape[2]
    u_spec = pl.BlockSpec((rows, width), lambda b: (b, 0))
    h_spec = pl.BlockSpec((None, rows_h, n_state), lambda b: (b, 0, 0))
    consts = [bmat, cre, cim, lam_re_q, lam_im_q, d_q]
    vmem = (4 * rows * width * 4 + (t1 + 1) * rows * ns2 * 4 + 2 * sum(c.size * c.dtype.itemsize for c in consts)
            + 8 * rows_h * n_state * 4 + 8 * 1024 * 1024)
    h_out = jax.ShapeDtypeStruct(h0_re.shape, F32)
    return pl.pallas_call(
        functools.partial(_ssm_body, t1=t1, chained=chained),
        grid=(n_blocks,),
        in_specs=[u_spec, h_spec, h_spec] + [_const_spec(c.shape) for c in consts],
        out_specs=[u_spec, h_spec, h_spec],
        out_shape=[jax.ShapeDtypeStruct(u_chunks.shape, F32), h_out, h_out],
        scratch_shapes=[pltpu.VMEM((t1, rows, ns2), F32), pltpu.VMEM((rows, ns2), F32)],
        compiler_params=_params(("parallel",), vmem),
        name="ssm_chained" if chained else "ssm_rows",
    )(u_chunks, h0_re, h0_im, *consts)


def _mix_body(x_ref, oatt_ref, yc_ref, wglu_ref, bglu_ref, gatt_ref, gssm_ref, wout_ref, lng_ref, lnb_ref,
              o_ref, y_scr, *, alpha, t1):
    d_att = oatt_ref.shape[1]
    n_lane_blocks, rows, _ = y_scr.shape
    d_ssm = n_lane_blocks * LANES
    chunks = rows // t1
    for c in range(n_lane_blocks):
        for i in range(t1):
            y_scr[c, pl.ds(i, chunks, stride=t1), :] = yc_ref[:, i * d_ssm + c * LANES:i * d_ssm + (c + 1) * LANES]
    g = jax.nn.gelu(jnp.concatenate([y_scr[c] for c in range(n_lane_blocks)], axis=1))
    z = g * jax.nn.sigmoid(_dot(g.astype(BF16), wglu_ref[...]) + bglu_ref[...])
    ra = _rms_norm(oatt_ref[...], gatt_ref[...]).astype(BF16)
    rz = _rms_norm(z, gssm_ref[...]).astype(BF16)
    mixed = _dot(ra, wout_ref[0:d_att, :]) + _dot(rz, wout_ref[d_att:, :])
    o_ref[...] = _layer_norm(alpha * x_ref[...] + mixed, lng_ref[...], lnb_ref[...])


def _mix(x2d, o_att, y_chunks, w_glu, b_glu, g_att, g_ssm, w_out, ln_g, ln_b, alpha, tm, t1):
    rows, dm = x2d.shape
    d_att, d_ssm = o_att.shape[1], y_chunks.shape[1] // t1
    consts = [w_glu, b_glu.reshape(1, -1), g_att.reshape(1, -1), g_ssm.reshape(1, -1), w_out,
              ln_g.reshape(1, -1), ln_b.reshape(1, -1)]
    row = lambda width: pl.BlockSpec((tm, width), lambda i: (i, 0))
    chunk_spec = pl.BlockSpec((tm // t1, t1 * d_ssm), lambda i: (i, 0))
    vmem = 2 * tm * (2 * dm + d_att + d_ssm) * 4 + 4 * (w_glu.size + w_out.size) + 12 * tm * dm * 4
    return pl.pallas_call(
        functools.partial(_mix_body, alpha=alpha, t1=t1),
        grid=(rows // tm,),
        in_specs=[row(dm), row(d_att), chunk_spec] + [_const_spec(c.shape) for c in consts],
        out_specs=row(dm),
        out_shape=jax.ShapeDtypeStruct(x2d.shape, F32),
        scratch_shapes=[pltpu.VMEM((d_ssm // LANES, tm, LANES), F32)],
        compiler_params=_params(("parallel",), vmem),
        name="mix_out_ln1",
    )(x2d, o_att, y_chunks, *consts)


def _memkv_body(m_ref, wk_ref, wv_ref, k_ref, v_ref):
    mb = m_ref[...].astype(BF16)
    k_ref[...] = _dot(mb, wk_ref[...])
    v_ref[...] = _dot(mb, wv_ref[...])


def _memkv(mem2d, wk, wv, tm):
    rows, dm = mem2d.shape
    row = pl.BlockSpec((tm, dm), lambda i: (i, 0))
    out = jax.ShapeDtypeStruct((rows, wk.shape[1]), F32)
    vmem = 2 * 3 * tm * dm * 4 + 4 * (wk.size + wv.size) + 4 * tm * dm * 4
    return pl.pallas_call(
        _memkv_body,
        grid=(rows // tm,),
        in_specs=[row, _const_spec(wk.shape), _const_spec(wv.shape)],
        out_specs=[row, row],
        out_shape=[out, out],
        compiler_params=_params(("parallel",), vmem),
        name="mem_kv",
    )(mem2d, wk, wv)


def _memattn_body(x_ref, mk_ref, mv_ref, wq_ref, wo_ref, lng_ref, lnb_ref, o_ref, *, alpha):
    dm = x_ref.shape[1]
    hd = dm // N_MEM_HEADS
    heads = [slice(h * hd, (h + 1) * hd) for h in range(N_MEM_HEADS)]
    x = x_ref[...]
    q = (_dot(x.astype(BF16), wq_ref[...]) * (hd ** -0.5)).astype(BF16)
    scores = [_dot_nt(q[:, c], mk_ref[:, c].astype(BF16)) for c in heads]
    probs = [jnp.exp(s - jnp.max(s, axis=1, keepdims=True)) for s in scores]
    sums = [jnp.sum(p, axis=1, keepdims=True) for p in probs]
    outs = [_dot(p.astype(BF16), mv_ref[:, c].astype(BF16)) / l for p, l, c in zip(probs, sums, heads)]
    att = _dot(jnp.concatenate(outs, axis=1).astype(BF16), wo_ref[...])
    o_ref[...] = _layer_norm(alpha * x + att, lng_ref[...], lnb_ref[...])


def _memattn(x2d, mem_k, mem_v, wq, wo, ln_g, ln_b, alpha, tm):
    rows, dm = x2d.shape
    n_seq, n_mem, _ = mem_k.shape
    steps_per_mem = rows // n_seq // tm
    row = pl.BlockSpec((tm, dm), lambda i: (i, 0))
    mem_spec = pl.BlockSpec((None, n_mem, dm), lambda i: (i // steps_per_mem, 0, 0))
    consts = [wq, wo, ln_g.reshape(1, -1), ln_b.reshape(1, -1)]
    vmem = 2 * 2 * tm * dm * 4 + 2 * 2 * n_mem * dm * 4 + 4 * (wq.size + wo.size) + 12 * tm * dm * 4
    return pl.pallas_call(
        functools.partial(_memattn_body, alpha=alpha),
        grid=(rows // tm,),
        in_specs=[row, mem_spec, mem_spec] + [_const_spec(c.shape) for c in consts],
        out_specs=row,
        out_shape=jax.ShapeDtypeStruct(x2d.shape, F32),
        compiler_params=_params(("parallel",), vmem),
        name="mem_attn_ln2",
    )(x2d, mem_k, mem_v, *consts)


def _rows_matmul_body(x_ref, w_ref, o_ref, *, scale):
    o_ref[...] = _dot(x_ref[...].astype(BF16), w_ref[...]) * scale


def _rows_matmul(x2d, w, scale):
    rows, dm = x2d.shape
    vmem = 4 * rows * (dm + w.shape[1]) * 4 + 4 * w.size
    return pl.pallas_call(
        functools.partial(_rows_matmul_body, scale=scale),
        out_shape=jax.ShapeDtypeStruct((rows, w.shape[1]), F32),
        compiler_params=_params(None, vmem),
        name="rows_matmul",
    )(x2d, w)


def _memattn_cache_body(q_ref, mk_ref, mv_ref, o_ref):
    nb, rows, hd = q_ref.shape
    n_mem, nh = mk_ref.shape[1:3]
    cols = n_mem * nh
    head_ok = ((lax.broadcasted_iota(jnp.int32, (rows, cols), 0) & (nh - 1))
               == (lax.broadcasted_iota(jnp.int32, (rows, cols), 1) & (nh - 1)))
    scores = [jnp.where(head_ok, _dot_nt(q_ref[j].astype(BF16), mk_ref[j].reshape(cols, hd).astype(BF16)), NEG)
              for j in range(nb)]
    probs = [jnp.exp(s - jnp.max(s, axis=1, keepdims=True)) for s in scores]
    sums = [jnp.sum(p, axis=1, keepdims=True) for p in probs]
    for j in range(nb):
        o_ref[j] = _dot(probs[j].astype(BF16), mv_ref[j].reshape(cols, hd).astype(BF16)) / sums[j]


def _memattn_cache(q3, cache_k, cache_v, layer, nb):
    bsz, rows, hd = q3.shape
    _, _, n_mem, nh, _ = cache_k.shape
    q_spec = pl.BlockSpec((nb, rows, hd), lambda i: (i, 0, 0))
    c_spec = pl.BlockSpec((None, nb, n_mem, nh, hd), lambda i: (layer, i, 0, 0, 0))
    vmem = 2 * 2 * nb * n_mem * 8 * hd * 4 + 8 * nb * rows * n_mem * nh * 4 + 8 * 1024 * 1024
    return pl.pallas_call(
        _memattn_cache_body,
        grid=(bsz // nb,),
        in_specs=[q_spec, c_spec, c_spec],
        out_specs=q_spec,
        out_shape=jax.ShapeDtypeStruct(q3.shape, F32),
        compiler_params=_params(("parallel",), vmem),
        name="mem_attn_cache",
    )(q3, cache_k, cache_v)


def _proj_ln_body(x_ref, a_ref, w_ref, lng_ref, lnb_ref, o_ref, *, alpha):
    att = _dot(a_ref[...].astype(BF16), w_ref[...])
    o_ref[...] = _layer_norm(alpha * x_ref[...] + att, lng_ref[...], lnb_ref[...])


def _proj_ln(x2d, a2d, w, ln_g, ln_b, alpha):
    rows, dm = x2d.shape
    vmem = 8 * rows * dm * 4 + 4 * w.size
    return pl.pallas_call(
        functools.partial(_proj_ln_body, alpha=alpha),
        out_shape=jax.ShapeDtypeStruct(x2d.shape, F32),
        compiler_params=_params(None, vmem),
        name="proj_ln",
    )(x2d, a2d, w, ln_g.reshape(1, -1), ln_b.reshape(1, -1))


def _ffn_body(x_ref, wg_ref, wu_ref, wd_ref, lng_ref, lnb_ref, o_ref, acc_ref, *, alpha, tf):
    x = x_ref[...]
    xb = x.astype(BF16)
    d_ff = wg_ref.shape[1]
    for c in range(d_ff // tf):
        cols = slice(c * tf, (c + 1) * tf)
        hid = (jax.nn.silu(_dot(xb, wg_ref[:, cols])) * _dot(xb, wu_ref[:, cols])).astype(BF16)
        part = _dot(hid, wd_ref[cols, :])
        if c == 0:
            acc_ref[...] = part
        else:
            acc_ref[...] += part
    o_ref[...] = _layer_norm(alpha * x + acc_ref[...], lng_ref[...], lnb_ref[...])


def _ffn(x2d, wg, wu, wd, ln_g, ln_b, alpha, tm, tf):
    rows, dm = x2d.shape
    d_ff = wg.shape[1]
    assert d_ff % tf == 0 and tf % LANES == 0
    row = pl.BlockSpec((tm, dm), lambda i: (i, 0))
    consts = [wg, wu, wd, ln_g.reshape(1, -1), ln_b.reshape(1, -1)]
    vmem = 2 * 2 * tm * dm * 4 + 4 * 3 * wg.size + tm * dm * 4 + 6 * tm * max(tf, dm) * 4
    return pl.pallas_call(
        functools.partial(_ffn_body, alpha=alpha, tf=tf),
        grid=(rows // tm,),
        in_specs=[row] + [_const_spec(c.shape) for c in consts],
        out_specs=row,
        out_shape=jax.ShapeDtypeStruct(x2d.shape, F32),
        scratch_shapes=[pltpu.VMEM((tm, dm), F32)],
        compiler_params=_params(("parallel",), vmem),
        name="swiglu_ln3",
    )(x2d, *consts)


PROMPT_CHUNK = 8
ROW_TILE = 512
FFN_COL_TILE = 256
SAMPLE_ATTN_SEQS = 2
SAMPLE_MEM_SEQS = 4


def kernel(x_prompt, x_sample, cache_win_k, cache_win_v, state_ssm_re, state_ssm_im, cache_mem_k, cache_mem_v, mem_prompt, w_in, g_att, g_ssm, ssm_a_re, ssm_a_im, ssm_log_dt, ssm_b_re, ssm_b_im, ssm_c_re, ssm_c_im, ssm_d, w_glu, b_glu, w_out, ln1_g, ln1_b, w_mem_q, w_mem_k, w_mem_v, w_mem_o, ln2_g, ln2_b, w_gate, w_up, w_down, ln3_g, ln3_b):
    depth = w_in.shape[0]
    bp, seq, dm = x_prompt.shape
    bs, t_new, _ = x_sample.shape
    n_groups, n_state = ssm_a_re.shape[1:]
    n_mem = mem_prompt.shape[1]
    alpha = (2 * depth) ** 0.25
    keep = min(DILATED_CFGS[-1][0], seq)
    assert keep == seq
    n_chunks = seq // PROMPT_CHUNK

    y_p = x_prompt.reshape(bp * seq, dm)
    y_s = x_sample.reshape(bs * t_new, dm)
    mem2d = mem_prompt.reshape(bp * n_mem, dm)
    outs = [[] for _ in range(10)]
    for l in range(depth):
        bf = lambda w: w[l].astype(BF16)
        w_in_l, w_glu_l, w_out_l = bf(w_in), bf(w_glu), bf(w_out)
        wq_l, wk_l, wv_l, wo_l = bf(w_mem_q), bf(w_mem_k), bf(w_mem_v), bf(w_mem_o)
        wg_l, wu_l, wd_l = bf(w_gate), bf(w_up), bf(w_down)
        lam_re, lam_im, bbar_re_t, bbar_im_t = _ssm_prep(
            ssm_a_re[l], ssm_a_im[l], ssm_log_dt[l], ssm_b_re[l], ssm_b_im[l], max(PROMPT_CHUNK, t_new))
        layouts = _ssm_layouts(lam_re, lam_im, bbar_re_t, bbar_im_t, ssm_c_re[l], ssm_c_im[l], ssm_d[l])
        mix_w = (w_glu_l, b_glu[l], g_att[l], g_ssm[l], w_out_l, ln1_g[l], ln1_b[l])

        q, k, v, u, k_t, v_t = _inproj(y_p, w_in_l, ROW_TILE, PROMPT_CHUNK, seq)
        d_att = q.shape[1]
        o_att = _attn_prompt(q.reshape(bp, seq, d_att), k.reshape(bp, seq, d_att), v.reshape(bp, seq, d_att))
        zeros = jnp.zeros((bp, 1, n_groups * n_state), F32)
        y_ssm, hr_p, hi_p = _ssm(u, zeros, zeros, layouts, PROMPT_CHUNK, n_chunks, True)
        x1 = _mix(y_p, o_att.reshape(bp * seq, d_att), y_ssm, *mix_w, alpha, ROW_TILE, PROMPT_CHUNK)
        mk_p, mv_p = _memkv(mem2d, wk_l, wv_l, ROW_TILE)
        x2 = _memattn(x1, mk_p.reshape(bp, n_mem, dm), mv_p.reshape(bp, n_mem, dm), wq_l, wo_l,
                      ln2_g[l], ln2_b[l], alpha, ROW_TILE)
        y_p = _ffn(x2, wg_l, wu_l, wd_l, ln3_g[l], ln3_b[l], alpha, ROW_TILE, FFN_COL_TILE)

        qs, ks, vs, us = _inproj(y_s, w_in_l, bs * t_new, t_new)
        shp = (bs, t_new, d_att)
        o_att_s = _attn_window(qs.reshape(shp), ks.reshape(shp), vs.reshape(shp), cache_win_k, cache_win_v, l,
                               SAMPLE_ATTN_SEQS)
        y_ssm_s, hr_s, hi_s = _ssm(us, state_ssm_re[l].reshape(1, bs, -1), state_ssm_im[l].reshape(1, bs, -1),
                                   layouts, t_new, bs, False)
        x1s = _mix(y_s, o_att_s.reshape(bs * t_new, d_att), y_ssm_s, *mix_w, alpha, bs * t_new, t_new)
        mem_hd = dm // N_MEM_HEADS
        q_mem = _rows_matmul(x1s, wq_l, mem_hd ** -0.5).reshape(bs, t_new * N_MEM_HEADS, mem_hd)
        a_mem = _memattn_cache(q_mem, cache_mem_k, cache_mem_v, l, SAMPLE_MEM_SEQS)
        x2s = _proj_ln(x1s, a_mem.reshape(bs * t_new, dm), wo_l, ln2_g[l], ln2_b[l], alpha)
        y_s = _ffn(x2s, wg_l, wu_l, wd_l, ln3_g[l], ln3_b[l], alpha, bs * t_new, FFN_COL_TILE)

        head_shape = (N_ATT_HEADS, ATT_HEAD_DIM)
        state_shape = (n_groups, n_state)
        mem_shape = (bp, n_mem, N_MEM_HEADS, dm // N_MEM_HEADS)
        rows_first = lambda t: jnp.transpose(t.reshape(bp, *head_shape, seq), (0, 3, 1, 2))
        for lst, val in zip(outs, (
                rows_first(k_t)[:, seq - keep:], rows_first(v_t)[:, seq - keep:],
                ks.reshape(bs, t_new, *head_shape), vs.reshape(bs, t_new, *head_shape),
                hr_p.reshape(bp, *state_shape), hi_p.reshape(bp, *state_shape),
                hr_s.reshape(bs, *state_shape), hi_s.reshape(bs, *state_shape),
                mk_p.reshape(mem_shape), mv_p.reshape(mem_shape))):
            lst.append(val)
    return (y_p.reshape(bp, seq, dm), y_s.reshape(bs, t_new, dm)) + tuple(jnp.stack(o) for o in outs)
```

```python
import functools
import math

import jax
import jax.numpy as jnp
from jax import lax
from jax.experimental import pallas as pl
from jax.experimental.pallas import tpu as pltpu

F32 = jnp.float32
BF16 = jnp.bfloat16

N_ATT_HEADS = 8
ATT_HEAD_DIM = 64
D_ATT = N_ATT_HEADS * ATT_HEAD_DIM
DILATED_CFGS = ((128, 1), (512, 4), (2048, 16))
ATT_BLK = 128
ATT_GROUP = 4
SSM_CH = 16
SSM_STATE = 64
N_MEM_HEADS = 4
EPS = 1e-5
NEG = -1e30

LANES = 128
V7X_VMEM_CAP_BYTES = 56 * 1024 * 1024

SSM_GB = LANES // SSM_CH
SSM_ROW_TILE = 32


def _params(sem, vmem_bytes):
    return pltpu.CompilerParams(
        dimension_semantics=sem,
        vmem_limit_bytes=int(min(max(vmem_bytes, 16 * 1024 * 1024), V7X_VMEM_CAP_BYTES)),
    )


def _dot(a, b):
    return jnp.dot(a, b, preferred_element_type=F32)


def _dot_nt(a, b):
    return lax.dot_general(a, b, (((1,), (1,)), ((), ())), preferred_element_type=F32)


def _layer_norm(x, g, b):
    mu = jnp.mean(x, axis=-1, keepdims=True)
    xc = x - mu
    var = jnp.mean(xc * xc, axis=-1, keepdims=True)
    return xc * lax.rsqrt(var + EPS) * g + b


def _rms_norm(x, g):
    return x * lax.rsqrt(jnp.mean(x * x, axis=-1, keepdims=True) + EPS) * g


def _const_spec(shape):
    n = len(shape)
    return pl.BlockSpec(shape, lambda *_: (0,) * n, pipeline_mode=pl.Buffered(1))


def _inproj_body(x_ref, w_ref, q_ref, k_ref, v_ref, uc_ref, *rest, t1, transposed):
    u_scr = rest[-1]
    xb = x_ref[...].astype(BF16)
    d = q_ref.shape[1]
    q_ref[...] = _dot(xb, w_ref[:, 0 * d:1 * d])
    k = _dot(xb, w_ref[:, 1 * d:2 * d])
    v = _dot(xb, w_ref[:, 2 * d:3 * d])
    k_ref[...] = k
    v_ref[...] = v
    if transposed:
        kt_ref, vt_ref = rest[:2]
        kt_ref[...] = k.T
        vt_ref[...] = v.T
    u = _dot(xb, w_ref[:, 3 * d:4 * d])
    chunks = u_scr.shape[1] // t1
    for c in range(d // LANES):
        u_scr[c] = u[:, c * LANES:(c + 1) * LANES]
        for i in range(t1):
            uc_ref[:, i * d + c * LANES:i * d + (c + 1) * LANES] = u_scr[c, pl.ds(i, chunks, stride=t1), :]


def _inproj(x2d, w_in_bf16, tm, t1, seq=None):
    rows, dm = x2d.shape
    d = w_in_bf16.shape[1] // 4
    out = jax.ShapeDtypeStruct((rows, d), F32)
    row_spec = pl.BlockSpec((tm, d), lambda i: (i, 0))
    out_specs = [row_spec] * 3 + [pl.BlockSpec((tm // t1, t1 * d), lambda i: (i, 0))]
    out_shape = [out] * 3 + [jax.ShapeDtypeStruct((rows // t1, t1 * d), F32)]
    if seq is not None:
        steps = seq // tm
        out_specs += [pl.BlockSpec((None, d, tm), lambda i: (i // steps, 0, i % steps))] * 2
        out_shape += [jax.ShapeDtypeStruct((rows // seq, d, seq), F32)] * 2
    vmem = 2 * (tm * dm * 4 + 6 * tm * d * 4) + 2 * w_in_bf16.size * 2 + 10 * tm * d * 4
    return pl.pallas_call(
        functools.partial(_inproj_body, t1=t1, transposed=seq is not None),
        grid=(rows // tm,),
        in_specs=[pl.BlockSpec((tm, dm), lambda i: (i, 0)), _const_spec(w_in_bf16.shape)],
        out_specs=out_specs,
        out_shape=out_shape,
        scratch_shapes=[pltpu.VMEM((d // LANES, tm, LANES), F32)],
        compiler_params=_params(("parallel",), vmem),
        name="inproj",
    )(x2d, w_in_bf16)


def _attn_prompt_body(q_ref, k_ref, v_ref, o_ref, acc_ref, m_ref, l_ref):
    seq = q_ref.shape[0]
    blk = ATT_BLK
    scale = ATT_HEAD_DIM ** -0.5
    lane = lax.broadcasted_iota(jnp.int32, (blk, LANES), 1)
    head0 = lane < ATT_HEAD_DIM
    qi = lax.broadcasted_iota(jnp.int32, (blk, blk), 0)
    kj = lax.broadcasted_iota(jnp.int32, (blk, blk), 1)
    tri = kj <= qi
    mask_rest = jnp.concatenate([kj >= qi, tri], axis=1)

    def rows_of(start, d):
        return pl.ds(start, blk) if d == 1 else pl.ds(start, blk, stride=d)

    def load_block(d, q_start, prev_start):
        rows = rows_of(q_start, d)
        q = q_ref[rows, :] * scale
        k2 = k_ref[rows, :].astype(BF16)
        v2 = v_ref[rows, :].astype(BF16)
        mask = tri
        if prev_start is not None:
            prow = rows_of(prev_start, d)
            k2 = jnp.concatenate([k_ref[prow, :].astype(BF16), k2], axis=0)
            v2 = jnp.concatenate([v_ref[prow, :].astype(BF16), v2], axis=0)
            mask = mask_rest
        qh = [jnp.where(head0 if h == 0 else jnp.logical_not(head0), q, 0.0).astype(BF16) for h in range(2)]
        return rows, qh, k2, v2, mask

    def attend_group(cfg, d, starts):
        blocks = [load_block(d, q_start, prev_start) for q_start, prev_start in starts]
        scores = [[jnp.where(mask, _dot_nt(qh[h], k2), NEG) for h in range(2)] for _, qh, k2, _, mask in blocks]
        maxes = [[jnp.max(s, axis=1, keepdims=True) for s in sb] for sb in scores]
        probs = [[jnp.exp(s - m) for s, m in zip(sb, mb)] for sb, mb in zip(scores, maxes)]
        sums = [[jnp.sum(p, axis=1, keepdims=True) for p in pb] for pb in probs]
        pvs = [[_dot(p.astype(BF16), blk_[3]) for p in pb] for pb, blk_ in zip(probs, blocks)]
        new = [(blk_[0], jnp.where(head0, mb[0], mb[1]), jnp.where(head0, lb[0], lb[1]),
                jnp.where(head0, ob[0], ob[1])) for blk_, mb, lb, ob in zip(blocks, maxes, sums, pvs)]
        if cfg == 0:
            for rows, m, l, num in new:
                acc_ref[rows, :] = num
                m_ref[rows, :] = m
                l_ref[rows, :] = l
            return
        old = [(m_ref[rows, :], l_ref[rows, :], acc_ref[rows, :]) for rows, _, _, _ in new]
        for (rows, m, l, num), (m_old, l_old, acc_old) in zip(new, old):
            m_new = jnp.maximum(m_old, m)
            a = jnp.exp(m_old - m_new)
            b = jnp.exp(m - m_new)
            num = a * acc_old + b * num
            l = a * l_old + b * l
            if cfg == len(DILATED_CFGS) - 1:
                o_ref[rows, :] = num / l
            else:
                acc_ref[rows, :] = num
                m_ref[rows, :] = m_new
                l_ref[rows, :] = l

    grp = ATT_GROUP
    for cfg, (window, d) in enumerate(DILATED_CFGS):
        span = d * blk
        nb = seq // span
        if d >= grp:
            def residue_group(g, carry, cfg=cfg, d=d, span=span, nb=nb):
                res = [g * grp + j for j in range(grp)]
                attend_group(cfg, d, [(r, None) for r in res])
                if nb > 1:
                    def later(n, c):
                        attend_group(cfg, d, [(r + n * span, r + (n - 1) * span) for r in res])
                        return c
                    lax.fori_loop(1, nb, later, 0)
                return carry

            lax.fori_loop(0, d // grp, residue_group, 0)
        else:
            assert d == 1 and nb % grp == 0
            attend_group(cfg, d, [(0, None)] + [(n * span, (n - 1) * span) for n in range(1, grp)])

            def block_group(g, carry, cfg=cfg, d=d, span=span):
                attend_group(cfg, d, [((g * grp + j) * span, (g * grp + j - 1) * span) for j in range(grp)])
                return carry

            lax.fori_loop(1, nb // grp, block_group, 0)


def _attn_prompt(q, k, v):
    bsz, seq, d_att = q.shape
    assert d_att % LANES == 0 and LANES == 2 * ATT_HEAD_DIM
    for window, d in DILATED_CFGS:
        assert window // d == ATT_BLK and seq % (d * ATT_BLK) == 0
    spec = pl.BlockSpec((None, seq, LANES), lambda b, h: (b, 0, h))
    blk_bytes = seq * LANES * 4
    return pl.pallas_call(
        _attn_prompt_body,
        grid=(bsz, d_att // LANES),
        in_specs=[spec, spec, spec],
        out_specs=spec,
        out_shape=jax.ShapeDtypeStruct(q.shape, F32),
        scratch_shapes=[pltpu.VMEM((seq, LANES), F32)] * 3,
        compiler_params=_params(("parallel", "parallel"), 11 * blk_bytes + 8 * 1024 * 1024),
        name="attn_prompt",
    )(q, k, v)


def _window_head_mask(d_att):
    nh = N_ATT_HEADS
    return (jnp.right_shift(lax.broadcasted_iota(jnp.int32, (nh, d_att), 1), ATT_HEAD_DIM.bit_length() - 1)
            == lax.broadcasted_iota(jnp.int32, (nh, d_att), 0))


def _window_probs(q, kn, kt):
    t_new, d_att = q.shape
    w_buf = kt.shape[1]
    nh = N_ATT_HEADS
    rows = t_new * nh
    (win1, _), (win4, dil4), (_, dil16) = DILATED_CFGS
    tail, near = win4, win1
    head_mask = _window_head_mask(d_att)
    trow = jnp.right_shift(lax.broadcasted_iota(jnp.int32, (rows, 1), 0), nh.bit_length() - 1)
    lane = lambda n: lax.broadcasted_iota(jnp.int32, (rows, n), 1)
    mask16 = (lane(w_buf) & (dil16 - 1)) == trow
    mask4 = (lane(tail) & (dil4 - 1)) == trow
    mask1 = lane(near) >= trow
    new_self = lane(t_new) == trow
    new_causal = lane(t_new) <= trow

    q = q * (ATT_HEAD_DIM ** -0.5)
    qbd = jnp.concatenate(
        [jnp.where(head_mask, jnp.broadcast_to(q[t:t + 1], (nh, d_att)), 0.0) for t in range(t_new)],
        axis=0).astype(BF16)
    s_all = _dot(qbd, kt.astype(BF16))
    s_new = _dot_nt(qbd, kn.astype(BF16))

    def softmax_parts(main, main_mask, new_mask):
        sm = jnp.where(main_mask, main, NEG)
        sn = jnp.where(new_mask, s_new, NEG)
        m = jnp.maximum(jnp.max(sm, axis=1, keepdims=True), jnp.max(sn, axis=1, keepdims=True))
        p = jnp.exp(sm - m)
        pn = jnp.exp(sn - m)
        den = jnp.sum(p, axis=1, keepdims=True) + jnp.sum(pn, axis=1, keepdims=True)
        return m, p, pn, den

    m16, p16, pn16, den16 = softmax_parts(s_all, mask16, new_self)
    m4, p4, pn4, den4 = softmax_parts(s_all[:, w_buf - tail:], mask4, new_self)
    m1, p1, pn1, den1 = softmax_parts(s_all[:, w_buf - near:], mask1, new_causal)
    m = jnp.maximum(jnp.maximum(m1, m4), m16)
    w1, w4, w16 = jnp.exp(m1 - m), jnp.exp(m4 - m), jnp.exp(m16 - m)
    den = w1 * den1 + w4 * den4 + w16 * den16
    p16, p4, p1 = w16 * p16, w4 * p4, w1 * p1
    p_all = jnp.concatenate(
        [p16[:, :w_buf - tail],
         p16[:, w_buf - tail:w_buf - near] + p4[:, :tail - near],
         p16[:, w_buf - near:] + p4[:, tail - near:] + p1], axis=1).astype(BF16)
    pn_all = (w16 * pn16 + w4 * pn4 + w1 * pn1).astype(BF16)
    return p_all, pn_all, den


def _window_output(p_all, pn_all, den, vn, vt):
    t_new, d_att = vn.shape
    nh = N_ATT_HEADS
    head_mask = _window_head_mask(d_att)
    out = (_dot_nt(p_all, vt.astype(BF16)) + _dot(pn_all, vn.astype(BF16))) / den
    return jnp.concatenate(
        [jnp.sum(jnp.where(head_mask, out[t * nh:(t + 1) * nh], 0.0), axis=0, keepdims=True)
         for t in range(t_new)], axis=0)


def _window_operands(q, k_new, v_new, cache_k, cache_v, layer, first, count):
    bsz, t_new, d_att = q.shape
    depth, _, w_buf, nh, hd = cache_k.shape
    assert w_buf == DILATED_CFGS[-1][0] and t_new <= DILATED_CFGS[1][1] and nh * hd == d_att
    rows_last = lambda c: jnp.transpose(c, (0, 1, 3, 4, 2)).reshape(depth, bsz, d_att, w_buf)
    new_spec = pl.BlockSpec((None, t_new, d_att), lambda i: (first + i, 0, 0))
    cache_spec = pl.BlockSpec((None, None, d_att, w_buf), lambda i: (layer, first + i, 0, 0))
    out_spec = pl.BlockSpec((None, t_new, d_att), lambda i: (i, 0, 0))
    arrays = [q, k_new, v_new, rows_last(cache_k), rows_last(cache_v)]
    vmem = 2 * 2 * d_att * w_buf * 4 + 4 * d_att * w_buf * 2 + 8 * 1024 * 1024
    return (arrays, [new_spec, new_spec, new_spec, cache_spec, cache_spec], out_spec,
            jax.ShapeDtypeStruct((count, t_new, d_att), F32), vmem)


def _host_window(window, steps, operands, in_specs, out_specs, out_shape):
    w_arrays, w_specs, w_out_spec, w_out_shape, w_vmem = window
    assert w_out_shape.shape[0] == steps
    return operands + w_arrays, in_specs + w_specs, [out_specs, w_out_spec], [out_shape, w_out_shape], w_vmem


def _ssm_prep_body(a_re_ref, a_im_ref, log_dt_ref, b_re_ref, b_im_ref, lam_re_ref, lam_im_ref,
                   bbar_re_ref, bbar_im_ref):
    ar = a_re_ref[...]
    ai = a_im_ref[...]
    dt = jnp.exp(log_dt_ref[...])
    mag = jnp.exp(dt * ar)
    lr = mag * jnp.cos(dt * ai)
    li = mag * jnp.sin(dt * ai)
    den = ar * ar + ai * ai
    nr, ni = lr - 1.0, li
    cr = (nr * ar + ni * ai) / den
    ci = (ni * ar - nr * ai) / den
    for h in range(b_re_ref.shape[0]):
        br, bi = b_re_ref[h], b_im_ref[h]
        bbar_re_ref[h] = cr * br - ci * bi
        bbar_im_ref[h] = cr * bi + ci * br
    pr, pi = lr, li
    for j in range(lam_re_ref.shape[0]):
        lam_re_ref[j] = pr
        lam_im_ref[j] = pi
        pr, pi = pr * lr - pi * li, pr * li + pi * lr


def _ssm_prep(a_re, a_im, log_dt, b_re, b_im, n_pow):
    g, p = a_re.shape
    h = b_re.shape[-1]
    b_re_t = jnp.transpose(b_re, (2, 0, 1))
    b_im_t = jnp.transpose(b_im, (2, 0, 1))
    gp = jax.ShapeDtypeStruct((n_pow, g, p), F32)
    hgp = jax.ShapeDtypeStruct((h, g, p), F32)
    return pl.pallas_call(
        _ssm_prep_body,
        out_shape=[gp, gp, hgp, hgp],
        name="ssm_prep",
    )(a_re, a_im, log_dt.reshape(g, 1), b_re_t, b_im_t)


def _ssm_layouts(lam_re, lam_im, bbar_re_t, bbar_im_t, c_re, c_im, d_skip):
    n_pow, g, p = lam_re.shape
    h = bbar_re_t.shape[0]
    nq = g // SSM_GB
    eye = jnp.eye(SSM_GB, dtype=bool)
    bb = jnp.stack([bbar_re_t, bbar_im_t], 0).reshape(2, h, nq, SSM_GB, p)
    bb = jnp.transpose(bb, (2, 3, 1, 0, 4))
    bmat = jnp.where(eye[None, :, None, None, :, None], bb[:, :, :, :, None, :], 0.0)
    bmat = bmat.reshape(nq, SSM_GB * h, 2 * SSM_GB * p).astype(BF16)
    def c_layout(c):
        cc = jnp.transpose(c.reshape(nq, SSM_GB, h, p), (0, 1, 3, 2))
        m = jnp.where(eye[None, :, None, :, None], cc[:, :, :, None, :], 0.0)
        return m.reshape(nq, SSM_GB * p, SSM_GB * h).astype(BF16)
    lam_re_q = lam_re.reshape(n_pow, nq, 1, SSM_GB * p)
    lam_im_q = lam_im.reshape(n_pow, nq, 1, SSM_GB * p)
    d_q = d_skip.reshape(nq, 1, SSM_GB * h)
    return bmat, c_layout(c_re), c_layout(c_im), lam_re_q, lam_im_q, d_q


def _ssm_body(u_ref, h0r_ref, h0i_ref, bmat_ref, cre_ref, cim_ref, lamr_ref, lami_ref, d_ref,
              y_ref, hfr_ref, hfi_ref, hloc_ref, hs_ref, *, t1, chained):
    rows = u_ref.shape[0]
    d_ssm = y_ref.shape[1] // t1
    nq = bmat_ref.shape[0]
    ns = lamr_ref.shape[-1]
    rt = SSM_ROW_TILE
    for q in range(nq):
        def u_lanes(i, q=q):
            return slice(i * d_ssm + q * LANES, i * d_ssm + (q + 1) * LANES)
        s_lanes = slice(q * ns, (q + 1) * ns)
        for i in range(t1):
            hloc_ref[i] = _dot(u_ref[:, u_lanes(i)].astype(BF16), bmat_ref[q])
        lr, li = lamr_ref[0, q], lami_ref[0, q]

        def local_scan(tile, carry, lr=lr, li=li):
            r = pl.ds(pl.multiple_of(tile * rt, rt), rt)
            hr = hloc_ref[0, r, 0:ns]
            hi = hloc_ref[0, r, ns:2 * ns]
            for i in range(1, t1):
                hr, hi = (lr * hr - li * hi + hloc_ref[i, r, 0:ns],
                          lr * hi + li * hr + hloc_ref[i, r, ns:2 * ns])
                hloc_ref[i, r, 0:ns] = hr
                hloc_ref[i, r, ns:2 * ns] = hi
            return carry

        lax.fori_loop(0, rows // rt, local_scan, 0)
        if chained:
            cr, ci = lamr_ref[t1 - 1, q], lami_ref[t1 - 1, q]

            def chain(c, g, cr=cr, ci=ci):
                gr, gi = g
                row = pl.ds(c, 1)
                hs_ref[row, 0:ns] = gr
                hs_ref[row, ns:2 * ns] = gi
                er = hloc_ref[t1 - 1, row, 0:ns]
                ei = hloc_ref[t1 - 1, row, ns:2 * ns]
                return cr * gr - ci * gi + er, cr * gi + ci * gr + ei

            gr, gi = lax.fori_loop(0, rows, chain, (h0r_ref[:, s_lanes], h0i_ref[:, s_lanes]))
            hfr_ref[:, s_lanes] = gr
            hfi_ref[:, s_lanes] = gi
        else:
            hs_ref[:, 0:ns] = h0r_ref[:, s_lanes]
            hs_ref[:, ns:2 * ns] = h0i_ref[:, s_lanes]

        def add_carry(tile, carry, q=q):
            r = pl.ds(pl.multiple_of(tile * rt, rt), rt)
            hr0 = hs_ref[r, 0:ns]
            hi0 = hs_ref[r, ns:2 * ns]
            for i in range(t1):
                pr, pi = lamr_ref[i, q], lami_ref[i, q]
                hloc_ref[i, r, 0:ns] = hloc_ref[i, r, 0:ns] + (pr * hr0 - pi * hi0)
                hloc_ref[i, r, ns:2 * ns] = hloc_ref[i, r, ns:2 * ns] + (pr * hi0 + pi * hr0)
            return carry

        lax.fori_loop(0, rows // rt, add_carry, 0)
        if not chained:
            hfr_ref[:, s_lanes] = hloc_ref[t1 - 1, :, 0:ns]
            hfi_ref[:, s_lanes] = hloc_ref[t1 - 1, :, ns:2 * ns]
        for i in range(t1):
            y_ref[:, u_lanes(i)] = (_dot(hloc_ref[i, :, 0:ns].astype(BF16), cre_ref[q])
                                    - _dot(hloc_ref[i, :, ns:2 * ns].astype(BF16), cim_ref[q])
                                    + d_ref[q] * u_ref[:, u_lanes(i)])


def _ssm(u_chunks, h0_re, h0_im, layouts, t1, rows, chained):
    bmat, cre, cim, lam_re_q, lam_im_q, d_q = layouts
    lam_re_q, lam_im_q = lam_re_q[:t1], lam_im_q[:t1]
    n_blocks, rows_h, n_state = h0_re.shape
    width = u_chunks.shape[1]
    assert rows % SSM_ROW_TILE == 0 and u_chunks.shape[0] == n_blocks * rows
    ns2 = bmat.shape[2]
    u_spec = pl.BlockSpec((rows, width), lambda b: (b, 0))
    h_spec = pl.BlockSpec((None, rows_h, n_state), lambda b: (b, 0, 0))
    consts = [bmat, cre, cim, lam_re_q, lam_im_q, d_q]
    vmem = (4 * rows * width * 4 + (t1 + 1) * rows * ns2 * 4 + 2 * sum(c.size * c.dtype.itemsize for c in consts)
            + 8 * rows_h * n_state * 4 + 8 * 1024 * 1024)
    h_out = jax.ShapeDtypeStruct(h0_re.shape, F32)
    return pl.pallas_call(
        functools.partial(_ssm_body, t1=t1, chained=chained),
        grid=(n_blocks,),
        in_specs=[u_spec, h_spec, h_spec] + [_const_spec(c.shape) for c in consts],
        out_specs=[u_spec, h_spec, h_spec],
        out_shape=[jax.ShapeDtypeStruct(u_chunks.shape, F32), h_out, h_out],
        scratch_shapes=[pltpu.VMEM((t1, rows, ns2), F32), pltpu.VMEM((rows, ns2), F32)],
        compiler_params=_params(("parallel",), vmem),
        name="ssm_chained" if chained else "ssm_rows",
    )(u_chunks, h0_re, h0_im, *consts)


def _mix_body(x_ref, oatt_ref, yc_ref, wglu_ref, bglu_ref, gatt_ref, gssm_ref, wout_ref, lng_ref, lnb_ref,
              o_ref, y_scr, *, alpha, t1):
    d_att = oatt_ref.shape[1]
    n_lane_blocks, rows, _ = y_scr.shape
    d_ssm = n_lane_blocks * LANES
    chunks = rows // t1
    for c in range(n_lane_blocks):
        for i in range(t1):
            y_scr[c, pl.ds(i, chunks, stride=t1), :] = yc_ref[:, i * d_ssm + c * LANES:i * d_ssm + (c + 1) * LANES]
    g = jax.nn.gelu(jnp.concatenate([y_scr[c] for c in range(n_lane_blocks)], axis=1))
    z = g * jax.nn.sigmoid(_dot(g.astype(BF16), wglu_ref[...]) + bglu_ref[...])
    ra = _rms_norm(oatt_ref[...], gatt_ref[...]).astype(BF16)
    rz = _rms_norm(z, gssm_ref[...]).astype(BF16)
    mixed = _dot(ra, wout_ref[0:d_att, :]) + _dot(rz, wout_ref[d_att:, :])
    o_ref[...] = _layer_norm(alpha * x_ref[...] + mixed, lng_ref[...], lnb_ref[...])


def _mix(x2d, o_att, y_chunks, w_glu, b_glu, g_att, g_ssm, w_out, ln_g, ln_b, alpha, tm, t1):
    rows, dm = x2d.shape
    d_att, d_ssm = o_att.shape[1], y_chunks.shape[1] // t1
    consts = [w_glu, b_glu.reshape(1, -1), g_att.reshape(1, -1), g_ssm.reshape(1, -1), w_out,
              ln_g.reshape(1, -1), ln_b.reshape(1, -1)]
    row = lambda width: pl.BlockSpec((tm, width), lambda i: (i, 0))
    chunk_spec = pl.BlockSpec((tm // t1, t1 * d_ssm), lambda i: (i, 0))
    vmem = 2 * tm * (2 * dm + d_att + d_ssm) * 4 + 4 * (w_glu.size + w_out.size) + 12 * tm * dm * 4
    return pl.pallas_call(
        functools.partial(_mix_body, alpha=alpha, t1=t1),
        grid=(rows // tm,),
        in_specs=[row(dm), row(d_att), chunk_spec] + [_const_spec(c.shape) for c in consts],
        out_specs=row(dm),
        out_shape=jax.ShapeDtypeStruct(x2d.shape, F32),
        scratch_shapes=[pltpu.VMEM((d_ssm // LANES, tm, LANES), F32)],
        compiler_params=_params(("parallel",), vmem),
        name="mix_out_ln1",
    )(x2d, o_att, y_chunks, *consts)


def _memkv_body(m_ref, wk_ref, wv_ref, k_ref, v_ref):
    mb = m_ref[...].astype(BF16)
    k_ref[...] = _dot(mb, wk_ref[...])
    v_ref[...] = _dot(mb, wv_ref[...])


def _memkv(mem2d, wk, wv, tm):
    rows, dm = mem2d.shape
    row = pl.BlockSpec((tm, dm), lambda i: (i, 0))
    out = jax.ShapeDtypeStruct((rows, wk.shape[1]), F32)
    vmem = 2 * 3 * tm * dm * 4 + 4 * (wk.size + wv.size) + 4 * tm * dm * 4
    return pl.pallas_call(
        _memkv_body,
        grid=(rows // tm,),
        in_specs=[row, _const_spec(wk.shape), _const_spec(wv.shape)],
        out_specs=[row, row],
        out_shape=[out, out],
        compiler_params=_params(("parallel",), vmem),
        name="mem_kv",
    )(mem2d, wk, wv)


def _memattn_body(x_ref, mk_ref, mv_ref, wq_ref, wo_ref, lng_ref, lnb_ref, *rest, alpha, hosts_window):
    if hosts_window:
        q_ref, kn_ref, vn_ref, kt_ref, vt_ref, o_ref, ow_ref = rest
        w_probs = _window_probs(q_ref[...], kn_ref[...], kt_ref[...])
    else:
        o_ref, = rest
    dm = x_ref.shape[1]
    hd = dm // N_MEM_HEADS
    heads = [slice(h * hd, (h + 1) * hd) for h in range(N_MEM_HEADS)]
    x = x_ref[...]
    q = (_dot(x.astype(BF16), wq_ref[...]) * (hd ** -0.5)).astype(BF16)
    scores = [_dot_nt(q[:, c], mk_ref[:, c].astype(BF16)) for c in heads]
    if hosts_window:
        ow_ref[...] = _window_output(*w_probs, vn_ref[...], vt_ref[...])
    probs = [jnp.exp(s - jnp.max(s, axis=1, keepdims=True)) for s in scores]
    sums = [jnp.sum(p, axis=1, keepdims=True) for p in probs]
    outs = [_dot(p.astype(BF16), mv_ref[:, c].astype(BF16)) / l for p, l, c in zip(probs, sums, heads)]
    att = _dot(jnp.concatenate(outs, axis=1).astype(BF16), wo_ref[...])
    o_ref[...] = _layer_norm(alpha * x + att, lng_ref[...], lnb_ref[...])


def _memattn(x2d, mem_k, mem_v, wq, wo, ln_g, ln_b, alpha, tm, window=None):
    rows, dm = x2d.shape
    n_seq, n_mem, _ = mem_k.shape
    steps_per_mem = rows // n_seq // tm
    row = pl.BlockSpec((tm, dm), lambda i: (i, 0))
    mem_spec = pl.BlockSpec((None, n_mem, dm), lambda i: (i // steps_per_mem, 0, 0))
    consts = [wq, wo, ln_g.reshape(1, -1), ln_b.reshape(1, -1)]
    operands = [x2d, mem_k, mem_v, *consts]
    in_specs = [row, mem_spec, mem_spec] + [_const_spec(c.shape) for c in consts]
    out_specs, out_shape = row, jax.ShapeDtypeStruct(x2d.shape, F32)
    vmem = 2 * 2 * tm * dm * 4 + 2 * 2 * n_mem * dm * 4 + 2 * (wq.size + wo.size) + 12 * tm * dm * 4
    if window is not None:
        operands, in_specs, out_specs, out_shape, w_vmem = _host_window(
            window, rows // tm, operands, in_specs, out_specs, out_shape)
        vmem += w_vmem
    return pl.pallas_call(
        functools.partial(_memattn_body, alpha=alpha, hosts_window=window is not None),
        grid=(rows // tm,),
        in_specs=in_specs,
        out_specs=out_specs,
        out_shape=out_shape,
        compiler_params=_params(("parallel",), vmem),
        name="mem_attn_ln2_window" if window is not None else "mem_attn_ln2",
    )(*operands)


def _rows_matmul_body(x_ref, w_ref, o_ref, *, scale):
    o_ref[...] = _dot(x_ref[...].astype(BF16), w_ref[...]) * scale


def _rows_matmul(x2d, w, scale):
    rows, dm = x2d.shape
    vmem = 4 * rows * (dm + w.shape[1]) * 4 + 4 * w.size
    return pl.pallas_call(
        functools.partial(_rows_matmul_body, scale=scale),
        out_shape=jax.ShapeDtypeStruct((rows, w.shape[1]), F32),
        compiler_params=_params(None, vmem),
        name="rows_matmul",
    )(x2d, w)


def _memattn_cache_body(q_ref, mk_ref, mv_ref, o_ref):
    nb, rows, hd = q_ref.shape
    n_mem, nh = mk_ref.shape[1:3]
    cols = n_mem * nh
    head_ok = ((lax.broadcasted_iota(jnp.int32, (rows, cols), 0) & (nh - 1))
               == (lax.broadcasted_iota(jnp.int32, (rows, cols), 1) & (nh - 1)))
    scores = [jnp.where(head_ok, _dot_nt(q_ref[j].astype(BF16), mk_ref[j].reshape(cols, hd).astype(BF16)), NEG)
              for j in range(nb)]
    probs = [jnp.exp(s - jnp.max(s, axis=1, keepdims=True)) for s in scores]
    sums = [jnp.sum(p, axis=1, keepdims=True) for p in probs]
    for j in range(nb):
        o_ref[j] = _dot(probs[j].astype(BF16), mv_ref[j].reshape(cols, hd).astype(BF16)) / sums[j]


def _memattn_cache(q3, cache_k, cache_v, layer, nb):
    bsz, rows, hd = q3.shape
    _, _, n_mem, nh, _ = cache_k.shape
    q_spec = pl.BlockSpec((nb, rows, hd), lambda i: (i, 0, 0))
    c_spec = pl.BlockSpec((None, nb, n_mem, nh, hd), lambda i: (layer, i, 0, 0, 0))
    vmem = 2 * 2 * nb * n_mem * 8 * hd * 4 + 8 * nb * rows * n_mem * nh * 4 + 8 * 1024 * 1024
    return pl.pallas_call(
        _memattn_cache_body,
        grid=(bsz // nb,),
        in_specs=[q_spec, c_spec, c_spec],
        out_specs=q_spec,
        out_shape=jax.ShapeDtypeStruct(q3.shape, F32),
        compiler_params=_params(("parallel",), vmem),
        name="mem_attn_cache",
    )(q3, cache_k, cache_v)


def _proj_ln_body(x_ref, a_ref, w_ref, lng_ref, lnb_ref, o_ref, *, alpha):
    att = _dot(a_ref[...].astype(BF16), w_ref[...])
    o_ref[...] = _layer_norm(alpha * x_ref[...] + att, lng_ref[...], lnb_ref[...])


def _proj_ln(x2d, a2d, w, ln_g, ln_b, alpha):
    rows, dm = x2d.shape
    vmem = 8 * rows * dm * 4 + 4 * w.size
    return pl.pallas_call(
        functools.partial(_proj_ln_body, alpha=alpha),
        out_shape=jax.ShapeDtypeStruct(x2d.shape, F32),
        compiler_params=_params(None, vmem),
        name="proj_ln",
    )(x2d, a2d, w, ln_g.reshape(1, -1), ln_b.reshape(1, -1))


def _ffn_body(x_ref, wg_ref, wu_ref, wd_ref, lng_ref, lnb_ref, *rest, alpha, tf, hosts_window):
    if hosts_window:
        q_ref, kn_ref, vn_ref, kt_ref, vt_ref, o_ref, ow_ref, acc_ref = rest
        probs = _window_probs(q_ref[...], kn_ref[...], kt_ref[...])
    else:
        o_ref, acc_ref = rest
    x = x_ref[...]
    xb = x.astype(BF16)
    n_chunks = wg_ref.shape[1] // tf
    for c in range(n_chunks):
        if hosts_window and c == n_chunks // 2:
            ow_ref[...] = _window_output(*probs, vn_ref[...], vt_ref[...])
        cols = slice(c * tf, (c + 1) * tf)
        hid = (jax.nn.silu(_dot(xb, wg_ref[:, cols])) * _dot(xb, wu_ref[:, cols])).astype(BF16)
        part = _dot(hid, wd_ref[cols, :])
        if c == 0:
            acc_ref[...] = part
        else:
            acc_ref[...] += part
    o_ref[...] = _layer_norm(alpha * x + acc_ref[...], lng_ref[...], lnb_ref[...])


def _ffn(x2d, wg, wu, wd, ln_g, ln_b, alpha, tm, tf, window=None):
    rows, dm = x2d.shape
    d_ff = wg.shape[1]
    assert d_ff % tf == 0 and tf % LANES == 0 and rows % tm == 0
    row = pl.BlockSpec((tm, dm), lambda i: (i, 0))
    consts = [wg, wu, wd, ln_g.reshape(1, -1), ln_b.reshape(1, -1)]
    operands = [x2d, *consts]
    in_specs = [row] + [_const_spec(c.shape) for c in consts]
    out_specs, out_shape = row, jax.ShapeDtypeStruct(x2d.shape, F32)
    vmem = 2 * 2 * tm * dm * 4 + 2 * 3 * wg.size + tm * dm * 4 + 8 * tm * max(tf, dm) * 4
    if window is not None:
        operands, in_specs, out_specs, out_shape, w_vmem = _host_window(
            window, rows // tm, operands, in_specs, out_specs, out_shape)
        vmem += w_vmem
    return pl.pallas_call(
        functools.partial(_ffn_body, alpha=alpha, tf=tf, hosts_window=window is not None),
        grid=(rows // tm,),
        in_specs=in_specs,
        out_specs=out_specs,
        out_shape=out_shape,
        scratch_shapes=[pltpu.VMEM((tm, dm), F32)],
        compiler_params=_params(("parallel",), vmem),
        name="swiglu_ln3_window" if window is not None else "swiglu_ln3",
    )(*operands)


PROMPT_CHUNK = 8
ROW_TILE = 512
FFN_COL_TILE = 256
SAMPLE_MEM_SEQS = 4


def kernel(x_prompt, x_sample, cache_win_k, cache_win_v, state_ssm_re, state_ssm_im, cache_mem_k, cache_mem_v, mem_prompt, w_in, g_att, g_ssm, ssm_a_re, ssm_a_im, ssm_log_dt, ssm_b_re, ssm_b_im, ssm_c_re, ssm_c_im, ssm_d, w_glu, b_glu, w_out, ln1_g, ln1_b, w_mem_q, w_mem_k, w_mem_v, w_mem_o, ln2_g, ln2_b, w_gate, w_up, w_down, ln3_g, ln3_b):
    depth = w_in.shape[0]
    bp, seq, dm = x_prompt.shape
    bs, t_new, _ = x_sample.shape
    n_groups, n_state = ssm_a_re.shape[1:]
    n_mem = mem_prompt.shape[1]
    alpha = (2 * depth) ** 0.25
    keep = min(DILATED_CFGS[-1][0], seq)
    assert keep == seq
    n_chunks = seq // PROMPT_CHUNK

    y_p = x_prompt.reshape(bp * seq, dm)
    y_s = x_sample.reshape(bs * t_new, dm)
    mem2d = mem_prompt.reshape(bp * n_mem, dm)
    outs = [[] for _ in range(10)]
    for l in range(depth):
        bf = lambda w: w[l].astype(BF16)
        w_in_l, w_glu_l, w_out_l = bf(w_in), bf(w_glu), bf(w_out)
        wq_l, wk_l, wv_l, wo_l = bf(w_mem_q), bf(w_mem_k), bf(w_mem_v), bf(w_mem_o)
        wg_l, wu_l, wd_l = bf(w_gate), bf(w_up), bf(w_down)
        lam_re, lam_im, bbar_re_t, bbar_im_t = _ssm_prep(
            ssm_a_re[l], ssm_a_im[l], ssm_log_dt[l], ssm_b_re[l], ssm_b_im[l], max(PROMPT_CHUNK, t_new))
        layouts = _ssm_layouts(lam_re, lam_im, bbar_re_t, bbar_im_t, ssm_c_re[l], ssm_c_im[l], ssm_d[l])
        mix_w = (w_glu_l, b_glu[l], g_att[l], g_ssm[l], w_out_l, ln1_g[l], ln1_b[l])

        q, k, v, u, k_t, v_t = _inproj(y_p, w_in_l, ROW_TILE, PROMPT_CHUNK, seq)
        d_att = q.shape[1]
        o_att = _attn_prompt(q.reshape(bp, seq, d_att), k.reshape(bp, seq, d_att), v.reshape(bp, seq, d_att))
        zeros = jnp.zeros((bp, 1, n_groups * n_state), F32)
        y_ssm, hr_p, hi_p = _ssm(u, zeros, zeros, layouts, PROMPT_CHUNK, n_chunks, True)
        x1 = _mix(y_p, o_att.reshape(bp * seq, d_att), y_ssm, *mix_w, alpha, ROW_TILE, PROMPT_CHUNK)
        mk_p, mv_p = _memkv(mem2d, wk_l, wv_l, ROW_TILE)
        qs, ks, vs, us = _inproj(y_s, w_in_l, bs * t_new, t_new)
        shp = (bs, t_new, d_att)
        host_steps = bs // 2
        host_tile = bp * seq // host_steps
        win = [_window_operands(qs.reshape(shp), ks.reshape(shp), vs.reshape(shp), cache_win_k, cache_win_v, l,
                                first, host_steps) for first in (0, host_steps)]
        x2, o_att_s0 = _memattn(x1, mk_p.reshape(bp, n_mem, dm), mv_p.reshape(bp, n_mem, dm), wq_l, wo_l,
                                ln2_g[l], ln2_b[l], alpha, host_tile, win[0])
        y_p, o_att_s1 = _ffn(x2, wg_l, wu_l, wd_l, ln3_g[l], ln3_b[l], alpha, host_tile, FFN_COL_TILE, win[1])
        o_att_s = jnp.concatenate([o_att_s0, o_att_s1], axis=0)
        y_ssm_s, hr_s, hi_s = _ssm(us, state_ssm_re[l].reshape(1, bs, -1), state_ssm_im[l].reshape(1, bs, -1),
                                   layouts, t_new, bs, False)
        x1s = _mix(y_s, o_att_s.reshape(bs * t_new, d_att), y_ssm_s, *mix_w, alpha, bs * t_new, t_new)
        mem_hd = dm // N_MEM_HEADS
        q_mem = _rows_matmul(x1s, wq_l, mem_hd ** -0.5).reshape(bs, t_new * N_MEM_HEADS, mem_hd)
        a_mem = _memattn_cache(q_mem, cache_mem_k, cache_mem_v, l, SAMPLE_MEM_SEQS)
        x2s = _proj_ln(x1s, a_mem.reshape(bs * t_new, dm), wo_l, ln2_g[l], ln2_b[l], alpha)
        y_s = _ffn(x2s, wg_l, wu_l, wd_l, ln3_g[l], ln3_b[l], alpha, bs * t_new, FFN_COL_TILE)

        head_shape = (N_ATT_HEADS, ATT_HEAD_DIM)
        state_shape = (n_groups, n_state)
        mem_shape = (bp, n_mem, N_MEM_HEADS, dm // N_MEM_HEADS)
        rows_first = lambda t: jnp.transpose(t.reshape(bp, *head_shape, seq), (0, 3, 1, 2))
        for lst, val in zip(outs, (
                rows_first(k_t)[:, seq - keep:], rows_first(v_t)[:, seq - keep:],
                ks.reshape(bs, t_new, *head_shape), vs.reshape(bs, t_new, *head_shape),
                hr_p.reshape(bp, *state_shape), hi_p.reshape(bp, *state_shape),
                hr_s.reshape(bs, *state_shape), hi_s.reshape(bs, *state_shape),
                mk_p.reshape(mem_shape), mv_p.reshape(mem_shape))):
            lst.append(val)
    return (y_p.reshape(bp, seq, dm), y_s.reshape(bs, t_new, dm)) + tuple(jnp.stack(o) for o in outs)
```

```python
import functools
import math

import jax
import jax.numpy as jnp
from jax import lax
from jax.experimental import pallas as pl
from jax.experimental.pallas import tpu as pltpu

F32 = jnp.float32
BF16 = jnp.bfloat16

N_ATT_HEADS = 8
ATT_HEAD_DIM = 64
D_ATT = N_ATT_HEADS * ATT_HEAD_DIM
DILATED_CFGS = ((128, 1), (512, 4), (2048, 16))
ATT_BLK = 128
ATT_GROUP = 4
SSM_CH = 16
SSM_STATE = 64
N_MEM_HEADS = 4
EPS = 1e-5
NEG = -1e30

LANES = 128
V7X_VMEM_CAP_BYTES = 56 * 1024 * 1024

SSM_GB = LANES // SSM_CH
SSM_ROW_TILE = 32


def _params(sem, vmem_bytes):
    return pltpu.CompilerParams(
        dimension_semantics=sem,
        vmem_limit_bytes=int(min(max(vmem_bytes, 16 * 1024 * 1024), V7X_VMEM_CAP_BYTES)),
    )


def _dot(a, b):
    return jnp.dot(a, b, preferred_element_type=F32)


def _dot_nt(a, b):
    return lax.dot_general(a, b, (((1,), (1,)), ((), ())), preferred_element_type=F32)


def _layer_norm(x, g, b):
    mu = jnp.mean(x, axis=-1, keepdims=True)
    xc = x - mu
    var = jnp.mean(xc * xc, axis=-1, keepdims=True)
    return xc * lax.rsqrt(var + EPS) * g + b


def _rms_norm(x, g):
    return x * lax.rsqrt(jnp.mean(x * x, axis=-1, keepdims=True) + EPS) * g


def _const_spec(shape):
    n = len(shape)
    return pl.BlockSpec(shape, lambda *_: (0,) * n, pipeline_mode=pl.Buffered(1))


def _inproj_body(x_ref, w_ref, q_ref, k_ref, v_ref, uc_ref, *rest, t1, transposed):
    u_scr = rest[-1]
    xb = x_ref[...].astype(BF16)
    d = q_ref.shape[1]
    q_ref[...] = _dot(xb, w_ref[:, 0 * d:1 * d])
    k = _dot(xb, w_ref[:, 1 * d:2 * d])
    v = _dot(xb, w_ref[:, 2 * d:3 * d])
    k_ref[...] = k
    v_ref[...] = v
    if transposed:
        kt_ref, vt_ref = rest[:2]
        kt_ref[...] = k.T
        vt_ref[...] = v.T
    u = _dot(xb, w_ref[:, 3 * d:4 * d])
    chunks = u_scr.shape[1] // t1
    for c in range(d // LANES):
        u_scr[c] = u[:, c * LANES:(c + 1) * LANES]
        for i in range(t1):
            uc_ref[:, i * d + c * LANES:i * d + (c + 1) * LANES] = u_scr[c, pl.ds(i, chunks, stride=t1), :]


def _inproj(x2d, w_in_bf16, tm, t1, seq=None):
    rows, dm = x2d.shape
    d = w_in_bf16.shape[1] // 4
    out = jax.ShapeDtypeStruct((rows, d), F32)
    row_spec = pl.BlockSpec((tm, d), lambda i: (i, 0))
    out_specs = [row_spec] * 3 + [pl.BlockSpec((tm // t1, t1 * d), lambda i: (i, 0))]
    out_shape = [out] * 3 + [jax.ShapeDtypeStruct((rows // t1, t1 * d), F32)]
    if seq is not None:
        steps = seq // tm
        out_specs += [pl.BlockSpec((None, d, tm), lambda i: (i // steps, 0, i % steps))] * 2
        out_shape += [jax.ShapeDtypeStruct((rows // seq, d, seq), F32)] * 2
    vmem = 2 * (tm * dm * 4 + 6 * tm * d * 4) + 2 * w_in_bf16.size * 2 + 10 * tm * d * 4
    return pl.pallas_call(
        functools.partial(_inproj_body, t1=t1, transposed=seq is not None),
        grid=(rows // tm,),
        in_specs=[pl.BlockSpec((tm, dm), lambda i: (i, 0)), _const_spec(w_in_bf16.shape)],
        out_specs=out_specs,
        out_shape=out_shape,
        scratch_shapes=[pltpu.VMEM((d // LANES, tm, LANES), F32)],
        compiler_params=_params(("parallel",), vmem),
        name="inproj",
    )(x2d, w_in_bf16)


def _attn_prompt_body(q_ref, k_ref, v_ref, o_ref, acc_ref, m_ref, l_ref):
    seq = q_ref.shape[0]
    blk = ATT_BLK
    scale = ATT_HEAD_DIM ** -0.5
    lane = lax.broadcasted_iota(jnp.int32, (blk, LANES), 1)
    head0 = lane < ATT_HEAD_DIM
    qi = lax.broadcasted_iota(jnp.int32, (blk, blk), 0)
    kj = lax.broadcasted_iota(jnp.int32, (blk, blk), 1)
    tri = kj <= qi
    mask_rest = jnp.concatenate([kj >= qi, tri], axis=1)

    def rows_of(start, d):
        return pl.ds(start, blk) if d == 1 else pl.ds(start, blk, stride=d)

    def load_block(d, q_start, prev_start):
        rows = rows_of(q_start, d)
        q = q_ref[rows, :] * scale
        k2 = k_ref[rows, :].astype(BF16)
        v2 = v_ref[rows, :].astype(BF16)
        mask = tri
        if prev_start is not None:
            prow = rows_of(prev_start, d)
            k2 = jnp.concatenate([k_ref[prow, :].astype(BF16), k2], axis=0)
            v2 = jnp.concatenate([v_ref[prow, :].astype(BF16), v2], axis=0)
            mask = mask_rest
        qh = [jnp.where(head0 if h == 0 else jnp.logical_not(head0), q, 0.0).astype(BF16) for h in range(2)]
        return rows, qh, k2, v2, mask

    def attend_group(cfg, d, starts):
        blocks = [load_block(d, q_start, prev_start) for q_start, prev_start in starts]
        scores = [[jnp.where(mask, _dot_nt(qh[h], k2), NEG) for h in range(2)] for _, qh, k2, _, mask in blocks]
        maxes = [[jnp.max(s, axis=1, keepdims=True) for s in sb] for sb in scores]
        probs = [[jnp.exp(s - m) for s, m in zip(sb, mb)] for sb, mb in zip(scores, maxes)]
        sums = [[jnp.sum(p, axis=1, keepdims=True) for p in pb] for pb in probs]
        pvs = [[_dot(p.astype(BF16), blk_[3]) for p in pb] for pb, blk_ in zip(probs, blocks)]
        new = [(blk_[0], jnp.where(head0, mb[0], mb[1]), jnp.where(head0, lb[0], lb[1]),
                jnp.where(head0, ob[0], ob[1])) for blk_, mb, lb, ob in zip(blocks, maxes, sums, pvs)]
        if cfg == 0:
            for rows, m, l, num in new:
                acc_ref[rows, :] = num
                m_ref[rows, :] = m
                l_ref[rows, :] = l
            return
        old = [(m_ref[rows, :], l_ref[rows, :], acc_ref[rows, :]) for rows, _, _, _ in new]
        for (rows, m, l, num), (m_old, l_old, acc_old) in zip(new, old):
            m_new = jnp.maximum(m_old, m)
            a = jnp.exp(m_old - m_new)
            b = jnp.exp(m - m_new)
            num = a * acc_old + b * num
            l = a * l_old + b * l
            if cfg == len(DILATED_CFGS) - 1:
                o_ref[rows, :] = num / l
            else:
                acc_ref[rows, :] = num
                m_ref[rows, :] = m_new
                l_ref[rows, :] = l

    grp = ATT_GROUP
    for cfg, (window, d) in enumerate(DILATED_CFGS):
        span = d * blk
        nb = seq // span
        if d >= grp:
            def residue_group(g, carry, cfg=cfg, d=d, span=span, nb=nb):
                res = [g * grp + j for j in range(grp)]
                attend_group(cfg, d, [(r, None) for r in res])
                if nb > 1:
                    def later(n, c):
                        attend_group(cfg, d, [(r + n * span, r + (n - 1) * span) for r in res])
                        return c
                    lax.fori_loop(1, nb, later, 0)
                return carry

            lax.fori_loop(0, d // grp, residue_group, 0)
        else:
            assert d == 1 and nb % grp == 0
            attend_group(cfg, d, [(0, None)] + [(n * span, (n - 1) * span) for n in range(1, grp)])

            def block_group(g, carry, cfg=cfg, d=d, span=span):
                attend_group(cfg, d, [((g * grp + j) * span, (g * grp + j - 1) * span) for j in range(grp)])
                return carry

            lax.fori_loop(1, nb // grp, block_group, 0)


def _attn_prompt(q, k, v):
    bsz, seq, d_att = q.shape
    assert d_att % LANES == 0 and LANES == 2 * ATT_HEAD_DIM
    for window, d in DILATED_CFGS:
        assert window // d == ATT_BLK and seq % (d * ATT_BLK) == 0
    spec = pl.BlockSpec((None, seq, LANES), lambda b, h: (b, 0, h))
    blk_bytes = seq * LANES * 4
    return pl.pallas_call(
        _attn_prompt_body,
        grid=(bsz, d_att // LANES),
        in_specs=[spec, spec, spec],
        out_specs=spec,
        out_shape=jax.ShapeDtypeStruct(q.shape, F32),
        scratch_shapes=[pltpu.VMEM((seq, LANES), F32)] * 3,
        compiler_params=_params(("parallel", "parallel"), 11 * blk_bytes + 8 * 1024 * 1024),
        name="attn_prompt",
    )(q, k, v)


def _window_head_mask(d_att):
    nh = N_ATT_HEADS
    return (jnp.right_shift(lax.broadcasted_iota(jnp.int32, (nh, d_att), 1), ATT_HEAD_DIM.bit_length() - 1)
            == lax.broadcasted_iota(jnp.int32, (nh, d_att), 0))


def _window_probs(q, kn, kt):
    t_new, d_att = q.shape
    w_buf = kt.shape[1]
    nh = N_ATT_HEADS
    rows = t_new * nh
    (win1, _), (win4, dil4), (_, dil16) = DILATED_CFGS
    tail, near = win4, win1
    head_mask = _window_head_mask(d_att)
    trow = jnp.right_shift(lax.broadcasted_iota(jnp.int32, (rows, 1), 0), nh.bit_length() - 1)
    lane = lambda n: lax.broadcasted_iota(jnp.int32, (rows, n), 1)
    mask16 = (lane(w_buf) & (dil16 - 1)) == trow
    mask4 = (lane(tail) & (dil4 - 1)) == trow
    mask1 = lane(near) >= trow
    new_self = lane(t_new) == trow
    new_causal = lane(t_new) <= trow

    q = q * (ATT_HEAD_DIM ** -0.5)
    qbd = jnp.concatenate(
        [jnp.where(head_mask, jnp.broadcast_to(q[t:t + 1], (nh, d_att)), 0.0) for t in range(t_new)],
        axis=0).astype(BF16)
    s_all = _dot(qbd, kt.astype(BF16))
    s_new = _dot_nt(qbd, kn.astype(BF16))

    def softmax_parts(main, main_mask, new_mask):
        sm = jnp.where(main_mask, main, NEG)
        sn = jnp.where(new_mask, s_new, NEG)
        m = jnp.maximum(jnp.max(sm, axis=1, keepdims=True), jnp.max(sn, axis=1, keepdims=True))
        p = jnp.exp(sm - m)
        pn = jnp.exp(sn - m)
        den = jnp.sum(p, axis=1, keepdims=True) + jnp.sum(pn, axis=1, keepdims=True)
        return m, p, pn, den

    m16, p16, pn16, den16 = softmax_parts(s_all, mask16, new_self)
    m4, p4, pn4, den4 = softmax_parts(s_all[:, w_buf - tail:], mask4, new_self)
    m1, p1, pn1, den1 = softmax_parts(s_all[:, w_buf - near:], mask1, new_causal)
    m = jnp.maximum(jnp.maximum(m1, m4), m16)
    w1, w4, w16 = jnp.exp(m1 - m), jnp.exp(m4 - m), jnp.exp(m16 - m)
    den = w1 * den1 + w4 * den4 + w16 * den16
    p16, p4, p1 = w16 * p16, w4 * p4, w1 * p1
    p_all = jnp.concatenate(
        [p16[:, :w_buf - tail],
         p16[:, w_buf - tail:w_buf - near] + p4[:, :tail - near],
         p16[:, w_buf - near:] + p4[:, tail - near:] + p1], axis=1).astype(BF16)
    pn_all = (w16 * pn16 + w4 * pn4 + w1 * pn1).astype(BF16)
    return p_all, pn_all, den


def _window_output(p_all, pn_all, den, vn, vt):
    t_new, d_att = vn.shape
    nh = N_ATT_HEADS
    head_mask = _window_head_mask(d_att)
    out = (_dot_nt(p_all, vt.astype(BF16)) + _dot(pn_all, vn.astype(BF16))) / den
    return jnp.concatenate(
        [jnp.sum(jnp.where(head_mask, out[t * nh:(t + 1) * nh], 0.0), axis=0, keepdims=True)
         for t in range(t_new)], axis=0)


def _attn_window_body(q_ref, kn_ref, vn_ref, kt_ref, vt_ref, o_ref):
    nb = q_ref.shape[0]
    probs = [_window_probs(q_ref[b], kn_ref[b], kt_ref[b]) for b in range(nb)]
    for b in range(nb):
        o_ref[b] = _window_output(*probs[b], vn_ref[b], vt_ref[b])


def _attn_window(q, k_new, v_new, cache_k, cache_v, layer, nb):
    bsz, t_new, d_att = q.shape
    depth, _, w_buf, nh, hd = cache_k.shape
    assert w_buf == DILATED_CFGS[-1][0] and t_new <= DILATED_CFGS[1][1] and nh * hd == d_att
    rows_last = lambda c: jnp.transpose(c, (0, 1, 3, 4, 2)).reshape(depth, bsz, d_att, w_buf)
    new_spec = pl.BlockSpec((nb, t_new, d_att), lambda i: (i, 0, 0))
    cache_spec = pl.BlockSpec((None, nb, d_att, w_buf), lambda i: (layer, i, 0, 0))
    vmem = 2 * 2 * nb * d_att * w_buf * 4 + 4 * nb * d_att * w_buf * 2 + 12 * 1024 * 1024
    return pl.pallas_call(
        _attn_window_body,
        grid=(bsz // nb,),
        in_specs=[new_spec, new_spec, new_spec, cache_spec, cache_spec],
        out_specs=new_spec,
        out_shape=jax.ShapeDtypeStruct(q.shape, F32),
        compiler_params=_params(("parallel",), vmem),
        name="attn_window",
    )(q, k_new, v_new, rows_last(cache_k), rows_last(cache_v))


def _ssm_prep_body(a_re_ref, a_im_ref, log_dt_ref, b_re_ref, b_im_ref, lam_re_ref, lam_im_ref,
                   bbar_re_ref, bbar_im_ref):
    ar = a_re_ref[...]
    ai = a_im_ref[...]
    dt = jnp.exp(log_dt_ref[...])
    mag = jnp.exp(dt * ar)
    lr = mag * jnp.cos(dt * ai)
    li = mag * jnp.sin(dt * ai)
    den = ar * ar + ai * ai
    nr, ni = lr - 1.0, li
    cr = (nr * ar + ni * ai) / den
    ci = (ni * ar - nr * ai) / den
    for h in range(b_re_ref.shape[0]):
        br, bi = b_re_ref[h], b_im_ref[h]
        bbar_re_ref[h] = cr * br - ci * bi
        bbar_im_ref[h] = cr * bi + ci * br
    pr, pi = lr, li
    for j in range(lam_re_ref.shape[0]):
        lam_re_ref[j] = pr
        lam_im_ref[j] = pi
        pr, pi = pr * lr - pi * li, pr * li + pi * lr


def _ssm_prep(a_re, a_im, log_dt, b_re, b_im, n_pow):
    g, p = a_re.shape
    h = b_re.shape[-1]
    b_re_t = jnp.transpose(b_re, (2, 0, 1))
    b_im_t = jnp.transpose(b_im, (2, 0, 1))
    gp = jax.ShapeDtypeStruct((n_pow, g, p), F32)
    hgp = jax.ShapeDtypeStruct((h, g, p), F32)
    return pl.pallas_call(
        _ssm_prep_body,
        out_shape=[gp, gp, hgp, hgp],
        name="ssm_prep",
    )(a_re, a_im, log_dt.reshape(g, 1), b_re_t, b_im_t)


def _ssm_layouts(lam_re, lam_im, bbar_re_t, bbar_im_t, c_re, c_im, d_skip):
    n_pow, g, p = lam_re.shape
    h = bbar_re_t.shape[0]
    nq = g // SSM_GB
    eye = jnp.eye(SSM_GB, dtype=bool)
    bb = jnp.stack([bbar_re_t, bbar_im_t], 0).reshape(2, h, nq, SSM_GB, p)
    bb = jnp.transpose(bb, (2, 3, 1, 0, 4))
    bmat = jnp.where(eye[None, :, None, None, :, None], bb[:, :, :, :, None, :], 0.0)
    bmat = bmat.reshape(nq, SSM_GB * h, 2 * SSM_GB * p).astype(BF16)
    def c_layout(c):
        cc = jnp.transpose(c.reshape(nq, SSM_GB, h, p), (0, 1, 3, 2))
        m = jnp.where(eye[None, :, None, :, None], cc[:, :, :, None, :], 0.0)
        return m.reshape(nq, SSM_GB * p, SSM_GB * h).astype(BF16)
    lam_re_q = lam_re.reshape(n_pow, nq, 1, SSM_GB * p)
    lam_im_q = lam_im.reshape(n_pow, nq, 1, SSM_GB * p)
    d_q = d_skip.reshape(nq, 1, SSM_GB * h)
    return bmat, c_layout(c_re), c_layout(c_im), lam_re_q, lam_im_q, d_q


def _ssm_body(u_ref, h0r_ref, h0i_ref, bmat_ref, cre_ref, cim_ref, lamr_ref, lami_ref, d_ref,
              y_ref, hfr_ref, hfi_ref, hloc_ref, hs_ref, *, t1, chained):
    rows = u_ref.shape[0]
    d_ssm = y_ref.shape[1] // t1
    nq = bmat_ref.shape[0]
    ns = lamr_ref.shape[-1]
    rt = SSM_ROW_TILE
    re, im = slice(0, ns), slice(ns, 2 * ns)

    def u_lanes(i, q):
        return slice(i * d_ssm + q * LANES, i * d_ssm + (q + 1) * LANES)

    def row_tiles(body):
        lax.fori_loop(0, rows // rt, lambda t, c: (body(pl.ds(pl.multiple_of(t * rt, rt), rt)), c)[1], 0,
                      unroll=True)

    def input_drive(q):
        for i in range(t1):
            hloc_ref[q % 2, i] = _dot(u_ref[:, u_lanes(i, q)].astype(BF16), bmat_ref[q])

    def local_scan(q):
        h = hloc_ref.at[q % 2]
        lr, li = lamr_ref[0, q], lami_ref[0, q]

        def tile(r):
            hr, hi = h[0, r, re], h[0, r, im]
            for i in range(1, t1):
                hr, hi = lr * hr - li * hi + h[i, r, re], lr * hi + li * hr + h[i, r, im]
                h[i, r, re] = hr
                h[i, r, im] = hi
        row_tiles(tile)

    def chunk_starts(q):
        h = hloc_ref.at[q % 2]
        s_lanes = slice(q * ns, (q + 1) * ns)
        if not chained:
            hs_ref[:, re] = h0r_ref[:, s_lanes]
            hs_ref[:, im] = h0i_ref[:, s_lanes]
            return
        cr, ci = lamr_ref[t1 - 1, q], lami_ref[t1 - 1, q]

        def chain(c, g):
            gr, gi = g
            row = pl.ds(c, 1)
            hs_ref[row, re] = gr
            hs_ref[row, im] = gi
            return cr * gr - ci * gi + h[t1 - 1, row, re], cr * gi + ci * gr + h[t1 - 1, row, im]

        gr, gi = lax.fori_loop(0, rows, chain, (h0r_ref[:, s_lanes], h0i_ref[:, s_lanes]))
        hfr_ref[:, s_lanes] = gr
        hfi_ref[:, s_lanes] = gi

    def add_carry(q):
        h = hloc_ref.at[q % 2]

        def tile(r):
            hr0, hi0 = hs_ref[r, re], hs_ref[r, im]
            for i in range(t1):
                pr, pi = lamr_ref[i, q], lami_ref[i, q]
                h[i, r, re] = h[i, r, re] + (pr * hr0 - pi * hi0)
                h[i, r, im] = h[i, r, im] + (pr * hi0 + pi * hr0)
        row_tiles(tile)

    def readout(q):
        h = hloc_ref.at[q % 2]
        if not chained:
            s_lanes = slice(q * ns, (q + 1) * ns)
            hfr_ref[:, s_lanes] = h[t1 - 1, :, re]
            hfi_ref[:, s_lanes] = h[t1 - 1, :, im]
        for i in range(t1):
            y_ref[:, u_lanes(i, q)] = (_dot(h[i, :, re].astype(BF16), cre_ref[q])
                                       - _dot(h[i, :, im].astype(BF16), cim_ref[q])
                                       + d_ref[q] * u_ref[:, u_lanes(i, q)])

    input_drive(0)
    local_scan(0)
    chunk_starts(0)
    for q in range(nq):
        if q + 1 < nq:
            input_drive(q + 1)
        add_carry(q)
        readout(q)
        if q + 1 < nq:
            local_scan(q + 1)
            chunk_starts(q + 1)


def _ssm(u_chunks, h0_re, h0_im, layouts, t1, rows, chained):
    bmat, cre, cim, lam_re_q, lam_im_q, d_q = layouts
    lam_re_q, lam_im_q = lam_re_q[:t1], lam_im_q[:t1]
    n_blocks, rows_h, n_state = h0_re.shape
    width = u_chunks.shape[1]
    assert rows % SSM_ROW_TILE == 0 and u_chunks.shape[0] == n_blocks * rows
    ns2 = bmat.shape[2]
    u_spec = pl.BlockSpec((rows, width), lambda b: (b, 0))
    h_spec = pl.BlockSpec((None, rows_h, n_state), lambda b: (b, 0, 0))
    consts = [bmat, cre, cim, lam_re_q, lam_im_q, d_q]
    vmem = (4 * rows * width * 4 + (2 * t1 + 1) * rows * ns2 * 4 + sum(c.size * c.dtype.itemsize for c in consts)
            + 8 * rows_h * n_state * 4 + 8 * 1024 * 1024)
    h_out = jax.ShapeDtypeStruct(h0_re.shape, F32)
    return pl.pallas_call(
        functools.partial(_ssm_body, t1=t1, chained=chained),
        grid=(n_blocks,),
        in_specs=[u_spec, h_spec, h_spec] + [_const_spec(c.shape) for c in consts],
        out_specs=[u_spec, h_spec, h_spec],
        out_shape=[jax.ShapeDtypeStruct(u_chunks.shape, F32), h_out, h_out],
        scratch_shapes=[pltpu.VMEM((2, t1, rows, ns2), F32), pltpu.VMEM((rows, ns2), F32)],
        compiler_params=_params(("parallel",), vmem),
        name="ssm_chained" if chained else "ssm_rows",
    )(u_chunks, h0_re, h0_im, *consts)


def _mix_body(x_ref, oatt_ref, yc_ref, wglu_ref, bglu_ref, gatt_ref, gssm_ref, wout_ref, lng_ref, lnb_ref,
              o_ref, y_scr, *, alpha, t1):
    d_att = oatt_ref.shape[1]
    n_lane_blocks, rows, _ = y_scr.shape
    d_ssm = n_lane_blocks * LANES
    chunks = rows // t1
    for c in range(n_lane_blocks):
        for i in range(t1):
            y_scr[c, pl.ds(i, chunks, stride=t1), :] = yc_ref[:, i * d_ssm + c * LANES:i * d_ssm + (c + 1) * LANES]
    g = jax.nn.gelu(jnp.concatenate([y_scr[c] for c in range(n_lane_blocks)], axis=1))
    z = g * jax.nn.sigmoid(_dot(g.astype(BF16), wglu_ref[...]) + bglu_ref[...])
    ra = _rms_norm(oatt_ref[...], gatt_ref[...]).astype(BF16)
    rz = _rms_norm(z, gssm_ref[...]).astype(BF16)
    mixed = _dot(ra, wout_ref[0:d_att, :]) + _dot(rz, wout_ref[d_att:, :])
    o_ref[...] = _layer_norm(alpha * x_ref[...] + mixed, lng_ref[...], lnb_ref[...])


def _mix(x2d, o_att, y_chunks, w_glu, b_glu, g_att, g_ssm, w_out, ln_g, ln_b, alpha, tm, t1):
    rows, dm = x2d.shape
    d_att, d_ssm = o_att.shape[1], y_chunks.shape[1] // t1
    consts = [w_glu, b_glu.reshape(1, -1), g_att.reshape(1, -1), g_ssm.reshape(1, -1), w_out,
              ln_g.reshape(1, -1), ln_b.reshape(1, -1)]
    row = lambda width: pl.BlockSpec((tm, width), lambda i: (i, 0))
    chunk_spec = pl.BlockSpec((tm // t1, t1 * d_ssm), lambda i: (i, 0))
    vmem = 2 * tm * (2 * dm + d_att + d_ssm) * 4 + 4 * (w_glu.size + w_out.size) + 12 * tm * dm * 4
    return pl.pallas_call(
        functools.partial(_mix_body, alpha=alpha, t1=t1),
        grid=(rows // tm,),
        in_specs=[row(dm), row(d_att), chunk_spec] + [_const_spec(c.shape) for c in consts],
        out_specs=row(dm),
        out_shape=jax.ShapeDtypeStruct(x2d.shape, F32),
        scratch_shapes=[pltpu.VMEM((d_ssm // LANES, tm, LANES), F32)],
        compiler_params=_params(("parallel",), vmem),
        name="mix_out_ln1",
    )(x2d, o_att, y_chunks, *consts)


def _memkv_body(m_ref, wk_ref, wv_ref, k_ref, v_ref):
    mb = m_ref[...].astype(BF16)
    k_ref[...] = _dot(mb, wk_ref[...])
    v_ref[...] = _dot(mb, wv_ref[...])


def _memkv(mem2d, wk, wv, tm):
    rows, dm = mem2d.shape
    row = pl.BlockSpec((tm, dm), lambda i: (i, 0))
    out = jax.ShapeDtypeStruct((rows, wk.shape[1]), F32)
    vmem = 2 * 3 * tm * dm * 4 + 4 * (wk.size + wv.size) + 4 * tm * dm * 4
    return pl.pallas_call(
        _memkv_body,
        grid=(rows // tm,),
        in_specs=[row, _const_spec(wk.shape), _const_spec(wv.shape)],
        out_specs=[row, row],
        out_shape=[out, out],
        compiler_params=_params(("parallel",), vmem),
        name="mem_kv",
    )(mem2d, wk, wv)


def _memattn_body(x_ref, mk_ref, mv_ref, wq_ref, wo_ref, lng_ref, lnb_ref, o_ref, *, alpha):
    dm = x_ref.shape[1]
    hd = dm // N_MEM_HEADS
    heads = [slice(h * hd, (h + 1) * hd) for h in range(N_MEM_HEADS)]
    x = x_ref[...]
    q = (_dot(x.astype(BF16), wq_ref[...]) * (hd ** -0.5)).astype(BF16)
    scores = [_dot_nt(q[:, c], mk_ref[:, c].astype(BF16)) for c in heads]
    probs =[jnp.exp(s - jnp.max(s, axis=1, keepdims=True)) for s in scores]
    sums = [jnp.sum(p, axis=1, keepdims=True) for p in probs]
    outs = [_dot(p.astype(BF16), mv_ref[:, c].astype(BF16)) / l for p, l, c in zip(probs, sums, heads)]
    att = _dot(jnp.concatenate(outs, axis=1).astype(BF16), wo_ref[...])
    o_ref[...] = _layer_norm(alpha * x + att, lng_ref[...], lnb_ref[...])


def _memattn(x2d, mem_k, mem_v, wq, wo, ln_g, ln_b, alpha, tm):
    rows, dm = x2d.shape
    n_seq, n_mem, _ = mem_k.shape
    steps_per_mem = rows // n_seq // tm
    row = pl.BlockSpec((tm, dm), lambda i: (i, 0))
    mem_spec = pl.BlockSpec((None, n_mem, dm), lambda i: (i // steps_per_mem, 0, 0))
    consts = [wq, wo, ln_g.reshape(1, -1), ln_b.reshape(1, -1)]
    vmem = 2 * 2 * tm * dm * 4 + 2 * 2 * n_mem * dm * 4 + 2 * (wq.size + wo.size) + 12 * tm * dm * 4
    return pl.pallas_call(
        functools.partial(_memattn_body, alpha=alpha),
        grid=(rows // tm,),
        in_specs=[row, mem_spec, mem_spec] + [_const_spec(c.shape) for c in consts],
        out_specs=row,
        out_shape=jax.ShapeDtypeStruct(x2d.shape, F32),
        compiler_params=_params(("parallel",), vmem),
        name="mem_attn_ln2",
    )(x2d, mem_k, mem_v, *consts)


def _rows_matmul_body(x_ref, w_ref, o_ref, *, scale):
    o_ref[...] = _dot(x_ref[...].astype(BF16), w_ref[...]) * scale


def _rows_matmul(x2d, w, scale):
    rows, dm = x2d.shape
    vmem = 4 * rows * (dm + w.shape[1]) * 4 + 4 * w.size
    return pl.pallas_call(
        functools.partial(_rows_matmul_body, scale=scale),
        out_shape=jax.ShapeDtypeStruct((rows, w.shape[1]), F32),
        compiler_params=_params(None, vmem),
        name="rows_matmul",
    )(x2d, w)


def _memattn_cache_body(q_ref, mk_ref, mv_ref, o_ref):
    nb, rows, hd = q_ref.shape
    n_mem, nh = mk_ref.shape[1:3]
    cols = n_mem * nh
    head_ok = ((lax.broadcasted_iota(jnp.int32, (rows, cols), 0) & (nh - 1))
               == (lax.broadcasted_iota(jnp.int32, (rows, cols), 1) & (nh - 1)))
    scores = [jnp.where(head_ok, _dot_nt(q_ref[j].astype(BF16), mk_ref[j].reshape(cols, hd).astype(BF16)), NEG)
              for j in range(nb)]
    probs = [jnp.exp(s - jnp.max(s, axis=1, keepdims=True)) for s in scores]
    sums = [jnp.sum(p, axis=1, keepdims=True) for p in probs]
    for j in range(nb):
        o_ref[j] = _dot(probs[j].astype(BF16), mv_ref[j].reshape(cols, hd).astype(BF16)) / sums[j]


def _memattn_cache(q3, cache_k, cache_v, layer, nb):
    bsz, rows, hd = q3.shape
    _, _, n_mem, nh, _ = cache_k.shape
    q_spec = pl.BlockSpec((nb, rows, hd), lambda i: (i, 0, 0))
    c_spec = pl.BlockSpec((None, nb, n_mem, nh, hd), lambda i: (layer, i, 0, 0, 0))
    vmem = 2 * 2 * nb * n_mem * 8 * hd * 4 + 8 * nb * rows * n_mem * nh * 4 + 8 * 1024 * 1024
    return pl.pallas_call(
        _memattn_cache_body,
        grid=(bsz // nb,),
        in_specs=[q_spec, c_spec, c_spec],
        out_specs=q_spec,
        out_shape=jax.ShapeDtypeStruct(q3.shape, F32),
        compiler_params=_params(("parallel",), vmem),
        name="mem_attn_cache",
    )(q3, cache_k, cache_v)


def _proj_ln_body(x_ref, a_ref, w_ref, lng_ref, lnb_ref, o_ref, *, alpha):
    att = _dot(a_ref[...].astype(BF16), w_ref[...])
    o_ref[...] = _layer_norm(alpha * x_ref[...] + att, lng_ref[...], lnb_ref[...])


def _proj_ln(x2d, a2d, w, ln_g, ln_b, alpha):
    rows, dm = x2d.shape
    vmem = 8 * rows * dm * 4 + 4 * w.size
    return pl.pallas_call(
        functools.partial(_proj_ln_body, alpha=alpha),
        out_shape=jax.ShapeDtypeStruct(x2d.shape, F32),
        compiler_params=_params(None, vmem),
        name="proj_ln",
    )(x2d, a2d, w, ln_g.reshape(1, -1), ln_b.reshape(1, -1))


def _ffn_body(x_ref, wg_ref, wu_ref, wd_ref, lng_ref, lnb_ref, o_ref, acc_ref, *, alpha, tf):
    x = x_ref[...]
    xb = x.astype(BF16)
    for c in range(wg_ref.shape[1] // tf):
        cols = slice(c * tf, (c + 1) * tf)
        hid = (jax.nn.silu(_dot(xb, wg_ref[:, cols])) * _dot(xb, wu_ref[:, cols])).astype(BF16)
        part = _dot(hid, wd_ref[cols, :])
        if c == 0:
            acc_ref[...] = part
        else:
            acc_ref[...] += part
    o_ref[...] = _layer_norm(alpha * x + acc_ref[...], lng_ref[...], lnb_ref[...])


def _ffn(x2d, wg, wu, wd, ln_g, ln_b, alpha, tm, tf):
    rows, dm = x2d.shape
    d_ff = wg.shape[1]
    assert d_ff % tf == 0 and tf % LANES == 0 and rows % tm == 0
    row = pl.BlockSpec((tm, dm), lambda i: (i, 0))
    consts = [wg, wu, wd, ln_g.reshape(1, -1), ln_b.reshape(1, -1)]
    vmem = 2 * 2 * tm * dm * 4 + 2 * 3 * wg.size + tm * dm * 4 + 8 * tm * max(tf, dm) * 4
    return pl.pallas_call(
        functools.partial(_ffn_body, alpha=alpha, tf=tf),
        grid=(rows // tm,),
        in_specs=[row] + [_const_spec(c.shape) for c in consts],
        out_specs=row,
        out_shape=jax.ShapeDtypeStruct(x2d.shape, F32),
        scratch_shapes=[pltpu.VMEM((tm, dm), F32)],
        compiler_params=_params(("parallel",), vmem),
        name="swiglu_ln3",
    )(x2d, *consts)


PROMPT_CHUNK = 8
ROW_TILE = 512
FFN_COL_TILE = 256
SAMPLE_ATTN_SEQS = 2
SAMPLE_MEM_SEQS = 4


def kernel(x_prompt, x_sample, cache_win_k, cache_win_v, state_ssm_re, state_ssm_im, cache_mem_k, cache_mem_v, mem_prompt, w_in, g_att, g_ssm, ssm_a_re, ssm_a_im, ssm_log_dt, ssm_b_re, ssm_b_im, ssm_c_re, ssm_c_im, ssm_d, w_glu, b_glu, w_out, ln1_g, ln1_b, w_mem_q, w_mem_k, w_mem_v, w_mem_o, ln2_g, ln2_b, w_gate, w_up, w_down, ln3_g, ln3_b):
    depth = w_in.shape[0]
    bp, seq, dm = x_prompt.shape
    bs, t_new, _ = x_sample.shape
    n_groups, n_state = ssm_a_re.shape[1:]
    n_mem = mem_prompt.shape[1]
    alpha = (2 * depth) ** 0.25
    keep = min(DILATED_CFGS[-1][0], seq)
    assert keep == seq
    n_chunks = seq // PROMPT_CHUNK

    y_p = x_prompt.reshape(bp * seq, dm)
    y_s = x_sample.reshape(bs * t_new, dm)
    mem2d = mem_prompt.reshape(bp * n_mem, dm)
    outs = [[] for _ in range(10)]
    for l in range(depth):
        bf = lambda w: w[l].astype(BF16)
        w_in_l, w_glu_l, w_out_l = bf(w_in), bf(w_glu), bf(w_out)
        wq_l, wk_l, wv_l, wo_l = bf(w_mem_q), bf(w_mem_k), bf(w_mem_v), bf(w_mem_o)
        wg_l, wu_l, wd_l = bf(w_gate), bf(w_up), bf(w_down)
        lam_re, lam_im, bbar_re_t, bbar_im_t = _ssm_prep(
            ssm_a_re[l], ssm_a_im[l], ssm_log_dt[l], ssm_b_re[l], ssm_b_im[l], max(PROMPT_CHUNK, t_new))
        layouts = _ssm_layouts(lam_re, lam_im, bbar_re_t, bbar_im_t, ssm_c_re[l], ssm_c_im[l], ssm_d[l])
        mix_w = (w_glu_l, b_glu[l], g_att[l], g_ssm[l], w_out_l, ln1_g[l], ln1_b[l])

        q, k, v, u, k_t, v_t = _inproj(y_p, w_in_l, ROW_TILE, PROMPT_CHUNK, seq)
        d_att = q.shape[1]
        o_att = _attn_prompt(q.reshape(bp, seq, d_att), k.reshape(bp, seq, d_att), v.reshape(bp, seq, d_att))
        zeros = jnp.zeros((bp, 1, n_groups * n_state), F32)
        y_ssm, hr_p, hi_p = _ssm(u, zeros, zeros, layouts, PROMPT_CHUNK, n_chunks, True)
        x1 = _mix(y_p, o_att.reshape(bp * seq, d_att), y_ssm, *mix_w, alpha, ROW_TILE, PROMPT_CHUNK)
        mk_p, mv_p = _memkv(mem2d, wk_l, wv_l, ROW_TILE)
        x2 = _memattn(x1, mk_p.reshape(bp, n_mem, dm), mv_p.reshape(bp, n_mem, dm), wq_l, wo_l,
                      ln2_g[l], ln2_b[l], alpha, ROW_TILE)
        y_p = _ffn(x2, wg_l, wu_l, wd_l, ln3_g[l], ln3_b[l], alpha, ROW_TILE, FFN_COL_TILE)

        qs, ks, vs, us = _inproj(y_s, w_in_l, bs * t_new, t_new)
        shp = (bs, t_new, d_att)
        o_att_s = _attn_window(qs.reshape(shp), ks.reshape(shp), vs.reshape(shp), cache_win_k, cache_win_v, l,
                               SAMPLE_ATTN_SEQS)
        y_ssm_s, hr_s, hi_s = _ssm(us, state_ssm_re[l].reshape(1, bs, -1), state_ssm_im[l].reshape(1, bs, -1),
                                   layouts, t_new, bs, False)
        x1s = _mix(y_s, o_att_s.reshape(bs * t_new, d_att), y_ssm_s, *mix_w, alpha, bs * t_new, t_new)
        mem_hd = dm // N_MEM_HEADS
        q_mem = _rows_matmul(x1s, wq_l, mem_hd ** -0.5).reshape(bs, t_new * N_MEM_HEADS, mem_hd)
        a_mem = _memattn_cache(q_mem, cache_mem_k, cache_mem_v, l, SAMPLE_MEM_SEQS)
        x2s = _proj_ln(x1s, a_mem.reshape(bs * t_new, dm), wo_l, ln2_g[l], ln2_b[l], alpha)
        y_s = _ffn(x2s, wg_l, wu_l, wd_l, ln3_g[l], ln3_b[l], alpha, bs * t_new, FFN_COL_TILE)

        head_shape = (N_ATT_HEADS, ATT_HEAD_DIM)
        state_shape = (n_groups, n_state)
        mem_shape = (bp, n_mem, N_MEM_HEADS, dm // N_MEM_HEADS)
        rows_first = lambda t: jnp.transpose(t.reshape(bp, *head_shape, seq), (0, 3, 1, 2))
        for lst, val in zip(outs, (
                rows_first(k_t)[:, seq - keep:], rows_first(v_t)[:, seq - keep:],
                ks.reshape(bs, t_new, *head_shape), vs.reshape(bs, t_new, *head_shape),
                hr_p.reshape(bp, *state_shape), hi_p.reshape(bp, *state_shape),
                hr_s.reshape(bs, *state_shape), hi_s.reshape(bs, *state_shape),
                mk_p.reshape(mem_shape), mv_p.reshape(mem_shape))):
            lst.append(val)
    return (y_p.reshape(bp, seq, dm), y_s.reshape(bs, t_new, dm)) + tuple(jnp.stack(o) for o in outs)
```

```python
import functools
import math

import jax
import jax.numpy as jnp
from jax import lax
from jax.experimental import pallas as pl
from jax.experimental.pallas import tpu as pltpu

F32 = jnp.float32
BF16 = jnp.bfloat16

N_ATT_HEADS = 8
ATT_HEAD_DIM = 64
D_ATT = N_ATT_HEADS * ATT_HEAD_DIM
DILATED_CFGS = ((128, 1), (512, 4), (2048, 16))
ATT_BLK = 128
ATT_GROUP = 4
SSM_CH = 16
SSM_STATE = 64
N_MEM_HEADS = 4
EPS = 1e-5
NEG = -1e30

LANES = 128
V7X_VMEM_CAP_BYTES = 56 * 1024 * 1024

SSM_GB = LANES // SSM_CH
SSM_ROW_TILE = 32
SSM_CHAIN_GROUP = 8


def _params(sem, vmem_bytes):
    return pltpu.CompilerParams(
        dimension_semantics=sem,
        vmem_limit_bytes=int(min(max(vmem_bytes, 16 * 1024 * 1024), V7X_VMEM_CAP_BYTES)),
    )


def _dot(a, b):
    return jnp.dot(a, b, preferred_element_type=F32)


def _dot_nt(a, b):
    return lax.dot_general(a, b, (((1,), (1,)), ((), ())), preferred_element_type=F32)


def _layer_norm(x, g, b):
    mu = jnp.mean(x, axis=-1, keepdims=True)
    xc = x - mu
    var = jnp.mean(xc * xc, axis=-1, keepdims=True)
    return xc * lax.rsqrt(var + EPS) * g + b


def _rms_norm(x, g):
    return x * lax.rsqrt(jnp.mean(x * x, axis=-1, keepdims=True) + EPS) * g


def _const_spec(shape):
    n = len(shape)
    return pl.BlockSpec(shape, lambda *_: (0,) * n, pipeline_mode=pl.Buffered(1))


def _inproj_body(x_ref, w_ref, q_ref, k_ref, v_ref, uc_ref, *rest, t1, transposed):
    u_scr = rest[-1]
    xb = x_ref[...].astype(BF16)
    d = q_ref.shape[1]
    q_ref[...] = _dot(xb, w_ref[:, 0 * d:1 * d])
    k = _dot(xb, w_ref[:, 1 * d:2 * d])
    v = _dot(xb, w_ref[:, 2 * d:3 * d])
    k_ref[...] = k
    v_ref[...] = v
    if transposed:
        kt_ref, vt_ref = rest[:2]
        kt_ref[...] = k.T
        vt_ref[...] = v.T
    u = _dot(xb, w_ref[:, 3 * d:4 * d])
    chunks = u_scr.shape[1] // t1
    for c in range(d // LANES):
        u_scr[c] = u[:, c * LANES:(c + 1) * LANES]
        for i in range(t1):
            uc_ref[:, i * d + c * LANES:i * d + (c + 1) * LANES] = u_scr[c, pl.ds(i, chunks, stride=t1), :]


def _inproj(x2d, w_in_bf16, tm, t1, seq=None):
    rows, dm = x2d.shape
    d = w_in_bf16.shape[1] // 4
    out = jax.ShapeDtypeStruct((rows, d), F32)
    row_spec = pl.BlockSpec((tm, d), lambda i: (i, 0))
    out_specs = [row_spec] * 3 + [pl.BlockSpec((tm // t1, t1 * d), lambda i: (i, 0))]
    out_shape = [out] * 3 + [jax.ShapeDtypeStruct((rows // t1, t1 * d), F32)]
    if seq is not None:
        steps = seq // tm
        out_specs += [pl.BlockSpec((None, d, tm), lambda i: (i // steps, 0, i % steps))] * 2
        out_shape += [jax.ShapeDtypeStruct((rows // seq, d, seq), F32)] * 2
    vmem = 2 * (tm * dm * 4 + 6 * tm * d * 4) + 2 * w_in_bf16.size * 2 + 10 * tm * d * 4
    return pl.pallas_call(
        functools.partial(_inproj_body, t1=t1, transposed=seq is not None),
        grid=(rows // tm,),
        in_specs=[pl.BlockSpec((tm, dm), lambda i: (i, 0)), _const_spec(w_in_bf16.shape)],
        out_specs=out_specs,
        out_shape=out_shape,
        scratch_shapes=[pltpu.VMEM((d // LANES, tm, LANES), F32)],
        compiler_params=_params(("parallel",), vmem),
        name="inproj",
    )(x2d, w_in_bf16)


def _attn_prompt_body(q_ref, k_ref, v_ref, o_ref, acc_ref, m_ref, l_ref):
    seq = q_ref.shape[0]
    blk = ATT_BLK
    scale = ATT_HEAD_DIM ** -0.5 * math.log2(math.e)
    lane = lax.broadcasted_iota(jnp.int32, (blk, LANES), 1)
    head0 = lane < ATT_HEAD_DIM
    qi = lax.broadcasted_iota(jnp.int32, (blk, blk), 0)
    kj = lax.broadcasted_iota(jnp.int32, (blk, blk), 1)
    tri = kj <= qi
    mask_rest = jnp.concatenate([kj >= qi, tri], axis=1)

    def rows_of(start, d):
        return pl.ds(start, blk) if d == 1 else pl.ds(start, blk, stride=d)

    def load_block(d, q_start, prev_start):
        rows = rows_of(q_start, d)
        q = q_ref[rows, :] * scale
        k2 = k_ref[rows, :].astype(BF16)
        v2 = v_ref[rows, :].astype(BF16)
        mask = tri
        if prev_start is not None:
            prow = rows_of(prev_start, d)
            k2 = jnp.concatenate([k_ref[prow, :].astype(BF16), k2], axis=0)
            v2 = jnp.concatenate([v_ref[prow, :].astype(BF16), v2], axis=0)
            mask = mask_rest
        qh = [jnp.where(head0 if h == 0 else jnp.logical_not(head0), q, 0.0).astype(BF16) for h in range(2)]
        return rows, qh, k2, v2, mask

    def attend_group(cfg, d, starts):
        blocks = [load_block(d, q_start, prev_start) for q_start, prev_start in starts]
        scores = [[jnp.where(mask, _dot_nt(qh[h], k2), NEG) for h in range(2)] for _, qh, k2, _, mask in blocks]
        maxes = [[jnp.max(s, axis=1, keepdims=True) for s in sb] for sb in scores]
        probs = [[jnp.exp2(s - m) for s, m in zip(sb, mb)] for sb, mb in zip(scores, maxes)]
        sums = [[jnp.sum(p, axis=1, keepdims=True) for p in pb] for pb in probs]
        pvs = [[_dot(p.astype(BF16), blk_[3]) for p in pb] for pb, blk_ in zip(probs, blocks)]
        new = [(blk_[0], jnp.where(head0, mb[0], mb[1]), jnp.where(head0, lb[0], lb[1]),
                jnp.where(head0, ob[0], ob[1])) for blk_, mb, lb, ob in zip(blocks, maxes, sums, pvs)]
        if cfg == 0:
            for rows, m, l, num in new:
                acc_ref[rows, :] = num
                m_ref[rows, :] = m
                l_ref[rows, :] = l
            return
        old = [(m_ref[rows, :], l_ref[rows, :], acc_ref[rows, :]) for rows, _, _, _ in new]
        for (rows, m, l, num), (m_old, l_old, acc_old) in zip(new, old):
            m_new = jnp.maximum(m_old, m)
            a = jnp.exp2(m_old - m_new)
            b = jnp.exp2(m - m_new)
            num = a * acc_old + b * num
            l = a * l_old + b * l
            if cfg == len(DILATED_CFGS) - 1:
                o_ref[rows, :] = num / l
            else:
                acc_ref[rows, :] = num
                m_ref[rows, :] = m_new
                l_ref[rows, :] = l

    grp = ATT_GROUP
    for cfg, (window, d) in enumerate(DILATED_CFGS):
        span = d * blk
        nb = seq // span
        if d >= grp:
            def residue_group(g, carry, cfg=cfg, d=d, span=span, nb=nb):
                res = [g * grp + j for j in range(grp)]
                attend_group(cfg, d, [(r, None) for r in res])
                if nb > 1:
                    def later(n, c):
                        attend_group(cfg, d, [(r + n * span, r + (n - 1) * span) for r in res])
                        return c
                    lax.fori_loop(1, nb, later, 0)
                return carry

            lax.fori_loop(0, d // grp, residue_group, 0)
        else:
            assert d == 1 and nb % grp == 0
            attend_group(cfg, d, [(0, None)] + [(n * span, (n - 1) * span) for n in range(1, grp)])

            def block_group(g, carry, cfg=cfg, d=d, span=span):
                attend_group(cfg, d, [((g * grp + j) * span, (g * grp + j - 1) * span) for j in range(grp)])
                return carry

            lax.fori_loop(1, nb // grp, block_group, 0)


def _attn_prompt(q, k, v):
    bsz, seq, d_att = q.shape
    assert d_att % LANES == 0 and LANES == 2 * ATT_HEAD_DIM
    for window, d in DILATED_CFGS:
        assert window // d == ATT_BLK and seq % (d * ATT_BLK) == 0
    spec = pl.BlockSpec((None, seq, LANES), lambda b, h: (b, 0, h))
    blk_bytes = seq * LANES * 4
    return pl.pallas_call(
        _attn_prompt_body,
        grid=(bsz, d_att // LANES),
        in_specs=[spec, spec, spec],
        out_specs=spec,
        out_shape=jax.ShapeDtypeStruct(q.shape, F32),
        scratch_shapes=[pltpu.VMEM((seq, LANES), F32)] * 3,
        compiler_params=_params(("parallel", "parallel"), 11 * blk_bytes + 8 * 1024 * 1024),
        name="attn_prompt",
    )(q, k, v)


def _window_head_mask(d_att):
    nh = N_ATT_HEADS
    return (jnp.right_shift(lax.broadcasted_iota(jnp.int32, (nh, d_att), 1), ATT_HEAD_DIM.bit_length() - 1)
            == lax.broadcasted_iota(jnp.int32, (nh, d_att), 0))


def _window_probs(q, kn, kt):
    t_new, d_att = q.shape
    w_buf = kt.shape[1]
    nh = N_ATT_HEADS
    rows = t_new * nh
    (win1, _), (win4, dil4), (_, dil16) = DILATED_CFGS
    tail, near = win4, win1
    head_mask = _window_head_mask(d_att)
    trow = jnp.right_shift(lax.broadcasted_iota(jnp.int32, (rows, 1), 0), nh.bit_length() - 1)
    lane = lambda n: lax.broadcasted_iota(jnp.int32, (rows, n), 1)
    mask16 = (lane(w_buf) & (dil16 - 1)) == trow
    mask4 = (lane(tail) & (dil4 - 1)) == trow
    mask1 = lane(near) >= trow
    new_self = lane(t_new) == trow
    new_causal = lane(t_new) <= trow

    q = q * (ATT_HEAD_DIM ** -0.5)
    qbd = jnp.concatenate(
        [jnp.where(head_mask, jnp.broadcast_to(q[t:t + 1], (nh, d_att)), 0.0) for t in range(t_new)],
        axis=0).astype(BF16)
    s_all = _dot(qbd, kt.astype(BF16))
    s_new = _dot_nt(qbd, kn.astype(BF16))

    def softmax_parts(main, main_mask, new_mask):
        sm = jnp.where(main_mask, main, NEG)
        sn = jnp.where(new_mask, s_new, NEG)
        m = jnp.maximum(jnp.max(sm, axis=1, keepdims=True), jnp.max(sn, axis=1, keepdims=True))
        p = jnp.exp(sm - m)
        pn = jnp.exp(sn - m)
        den = jnp.sum(p, axis=1, keepdims=True) + jnp.sum(pn, axis=1, keepdims=True)
        return m, p, pn, den

    m16, p16, pn16, den16 = softmax_parts(s_all, mask16, new_self)
    m4, p4, pn4, den4 = softmax_parts(s_all[:, w_buf - tail:], mask4, new_self)
    m1, p1, pn1, den1 = softmax_parts(s_all[:, w_buf - near:], mask1, new_causal)
    m = jnp.maximum(jnp.maximum(m1, m4), m16)
    w1, w4, w16 = jnp.exp(m1 - m), jnp.exp(m4 - m), jnp.exp(m16 - m)
    den = w1 * den1 + w4 * den4 + w16 * den16
    p16, p4, p1 = w16 * p16, w4 * p4, w1 * p1
    p_all = jnp.concatenate(
        [p16[:, :w_buf - tail],
         p16[:, w_buf - tail:w_buf - near] + p4[:, :tail - near],
         p16[:, w_buf - near:] + p4[:, tail - near:] + p1], axis=1).astype(BF16)
    pn_all = (w16 * pn16 + w4 * pn4 + w1 * pn1).astype(BF16)
    return p_all, pn_all, den


def _window_output(p_all, pn_all, den, vn, vt):
    t_new, d_att = vn.shape
    nh = N_ATT_HEADS
    head_mask = _window_head_mask(d_att)
    out = (_dot_nt(p_all, vt.astype(BF16)) + _dot(pn_all, vn.astype(BF16))) / den
    return jnp.concatenate(
        [jnp.sum(jnp.where(head_mask, out[t * nh:(t + 1) * nh], 0.0), axis=0, keepdims=True)
         for t in range(t_new)], axis=0)


def _attn_window_body(q_ref, kn_ref, vn_ref, kt_ref, vt_ref, o_ref):
    nb = q_ref.shape[0]
    probs = [_window_probs(q_ref[b], kn_ref[b], kt_ref[b]) for b in range(nb)]
    for b in range(nb):
        o_ref[b] = _window_output(*probs[b], vn_ref[b], vt_ref[b])


def _attn_window(q, k_new, v_new, cache_k, cache_v, layer, nb):
    bsz, t_new, d_att = q.shape
    depth, _, w_buf, nh, hd = cache_k.shape
    assert w_buf == DILATED_CFGS[-1][0] and t_new <= DILATED_CFGS[1][1] and nh * hd == d_att
    rows_last = lambda c: jnp.transpose(c, (0, 1, 3, 4, 2)).reshape(depth, bsz, d_att, w_buf)
    new_spec = pl.BlockSpec((nb, t_new, d_att), lambda i: (i, 0, 0))
    cache_spec = pl.BlockSpec((None, nb, d_att, w_buf), lambda i: (layer, i, 0, 0))
    vmem = 2 * 2 * nb * d_att * w_buf * 4 + 4 * nb * d_att * w_buf * 2 + 12 * 1024 * 1024
    return pl.pallas_call(
        _attn_window_body,
        grid=(bsz // nb,),
        in_specs=[new_spec, new_spec, new_spec, cache_spec, cache_spec],
        out_specs=new_spec,
        out_shape=jax.ShapeDtypeStruct(q.shape, F32),
        compiler_params=_params(("parallel",), vmem),
        name="attn_window",
    )(q, k_new, v_new, rows_last(cache_k), rows_last(cache_v))


def _ssm_prep_body(a_re_ref, a_im_ref, log_dt_ref, b_re_ref, b_im_ref, lam_re_ref, lam_im_ref,
                   bbar_re_ref, bbar_im_ref, step_re_ref, step_im_ref, *, chunk):
    ar = a_re_ref[...]
    ai = a_im_ref[...]
    dt = jnp.exp(log_dt_ref[...])
    mag = jnp.exp(dt * ar)
    lr = mag * jnp.cos(dt * ai)
    li = mag * jnp.sin(dt * ai)
    den = ar * ar + ai * ai
    nr, ni = lr - 1.0, li
    cr = (nr * ar + ni * ai) / den
    ci = (ni * ar - nr * ai) / den
    for h in range(b_re_ref.shape[0]):
        br, bi = b_re_ref[h], b_im_ref[h]
        bbar_re_ref[h] = cr * br - ci * bi
        bbar_im_ref[h] = cr * bi + ci * br
    pr, pi = lr, li
    for j in range(lam_re_ref.shape[0]):
        lam_re_ref[j] = pr
        lam_im_ref[j] = pi
        if j == chunk - 1:
            cr, ci = pr, pi
        pr, pi = pr * lr - pi * li, pr * li + pi * lr
    pr, pi = cr, ci
    for j in range(step_re_ref.shape[0]):
        step_re_ref[j] = pr
        step_im_ref[j] = pi
        pr, pi = pr * cr - pi * ci, pr * ci + pi * cr


def _ssm_prep(a_re, a_im, log_dt, b_re, b_im, n_pow, chunk):
    g, p = a_re.shape
    h = b_re.shape[-1]
    assert chunk <= n_pow
    b_re_t = jnp.transpose(b_re, (2, 0, 1))
    b_im_t = jnp.transpose(b_im, (2, 0, 1))
    gp = jax.ShapeDtypeStruct((n_pow, g, p), F32)
    hgp = jax.ShapeDtypeStruct((h, g, p), F32)
    sgp = jax.ShapeDtypeStruct((SSM_CHAIN_GROUP, g, p), F32)
    return pl.pallas_call(
        functools.partial(_ssm_prep_body, chunk=chunk),
        out_shape=[gp, gp, hgp, hgp, sgp, sgp],
        name="ssm_prep",
    )(a_re, a_im, log_dt.reshape(g, 1), b_re_t, b_im_t)


def _ssm_layouts(lam_re, lam_im, bbar_re_t, bbar_im_t, step_re, step_im, c_re, c_im, d_skip):
    n_pow, g, p = lam_re.shape
    h = bbar_re_t.shape[0]
    nq = g // SSM_GB
    eye = jnp.eye(SSM_GB, dtype=bool)
    bb = jnp.stack([bbar_re_t, bbar_im_t], 0).reshape(2, h, nq, SSM_GB, p)
    bb = jnp.transpose(bb, (2, 3, 1, 0, 4))
    bmat = jnp.where(eye[None, :, None, None, :, None], bb[:, :, :, :, None, :], 0.0)
    bmat = bmat.reshape(nq, SSM_GB * h, 2 * SSM_GB * p).astype(BF16)
    def c_layout(c):
        cc = jnp.transpose(c.reshape(nq, SSM_GB, h, p), (0, 1, 3, 2))
        m = jnp.where(eye[None, :, None, :, None], cc[:, :, :, None, :], 0.0)
        return m.reshape(nq, SSM_GB * p, SSM_GB * h).astype(BF16)
    per_block = lambda t: t.reshape(t.shape[0], nq, 1, SSM_GB * p)
    d_q = d_skip.reshape(nq, 1, SSM_GB * h)
    return (bmat, c_layout(c_re), c_layout(c_im), per_block(lam_re), per_block(lam_im),
            per_block(step_re), per_block(step_im), d_q)


def _ssm_body(u_ref, h0r_ref, h0i_ref, bmat_ref, cre_ref, cim_ref, lamr_ref, lami_ref, stepr_ref, stepi_ref, d_ref,
              y_ref, hfr_ref, hfi_ref, hloc_ref, hs_ref, ends_ref, grp_ref, *, t1, chained):
    rows = u_ref.shape[0]
    d_ssm = y_ref.shape[1] // t1
    nq = bmat_ref.shape[0]
    ns = lamr_ref.shape[-1]
    rt = SSM_ROW_TILE
    re, im = slice(0, ns), slice(ns, 2 * ns)

    def u_lanes(i, q):
        return slice(i * d_ssm + q * LANES, i * d_ssm + (q + 1) * LANES)

    def row_tiles(body):
        lax.fori_loop(0, rows // rt, lambda t, c: (body(pl.ds(pl.multiple_of(t * rt, rt), rt)), c)[1], 0,
                      unroll=True)

    def input_drive(q):
        for i in range(t1):
            hloc_ref[q % 2, i] = _dot(u_ref[:, u_lanes(i, q)].astype(BF16), bmat_ref[q])

    def local_scan(q):
        h = hloc_ref.at[q % 2]
        lr, li = lamr_ref[0, q], lami_ref[0, q]

        def tile(r):
            hr, hi = h[0, r, re], h[0, r, im]
            for i in range(1, t1):
                hr, hi = lr * hr - li * hi + h[i, r, re], lr * hi + li * hr + h[i, r, im]
                h[i, r, re] = hr
                h[i, r, im] = hi
        row_tiles(tile)

    nlb = ns // LANES

    def put_blocks(ref, rows_idx, vr, vi):
        for c in range(nlb):
            ref[c, rows_idx, :] = vr[:, c * LANES:(c + 1) * LANES]
            ref[nlb + c, rows_idx, :] = vi[:, c * LANES:(c + 1) * LANES]

    def get_blocks(ref, rows_idx):
        return (jnp.concatenate([ref[c, rows_idx, :] for c in range(nlb)], axis=1),
                jnp.concatenate([ref[nlb + c, rows_idx, :] for c in range(nlb)], axis=1))

    def chunk_starts(q):
        h = hloc_ref.at[q % 2]
        s_lanes = slice(q * ns, (q + 1) * ns)
        if not chained:
            put_blocks(hs_ref, slice(None), h0r_ref[:, s_lanes], h0i_ref[:, s_lanes])
            return
        grp = SSM_CHAIN_GROUP
        n_grp = rows // grp
        put_blocks(ends_ref, slice(None), h[t1 - 1, :, re], h[t1 - 1, :, im])
        member = lambda j: pl.ds(j, n_grp, stride=grp)
        cr, ci = stepr_ref[0, q], stepi_ref[0, q]
        gr, gi = get_blocks(ends_ref, member(0))
        inside = [(gr, gi)]
        for j in range(1, grp):
            er, ei = get_blocks(ends_ref, member(j))
            gr, gi = cr * gr - ci * gi + er, cr * gi + ci * gr + ei
            inside.append((gr, gi))
        grp_ref[0, :, re] = gr
        grp_ref[0, :, im] = gi
        wr, wi = stepr_ref[grp - 1, q], stepi_ref[grp - 1, q]

        def chain(c, g):
            sr, si = g
            row = pl.ds(c, 1)
            grp_ref[1, row, re] = sr
            grp_ref[1, row, im] = si
            return wr * sr - wi * si + grp_ref[0, row, re], wr * si + wi * sr + grp_ref[0, row, im]

        sr, si = lax.fori_loop(0, n_grp, chain, (h0r_ref[:, s_lanes], h0i_ref[:, s_lanes]))
        hfr_ref[:, s_lanes] = sr
        hfi_ref[:, s_lanes] = si
        sr, si = grp_ref[1, :, re], grp_ref[1, :, im]
        put_blocks(hs_ref, member(0), sr, si)
        for j in range(1, grp):
            pr, pi = stepr_ref[j - 1, q], stepi_ref[j - 1, q]
            gr, gi = inside[j - 1]
            put_blocks(hs_ref, member(j), gr + (pr * sr - pi * si), gi + (pr * si + pi * sr))

    def add_carry(q):
        h = hloc_ref.at[q % 2]

        def tile(r):
            hr0, hi0 = get_blocks(hs_ref, r)
            for i in range(t1):
                pr, pi = lamr_ref[i, q], lami_ref[i, q]
                h[i, r, re] = h[i, r, re] + (pr * hr0 - pi * hi0)
                h[i, r, im] = h[i, r, im] + (pr * hi0 + pi * hr0)
        row_tiles(tile)

    def readout(q):
        h = hloc_ref.at[q % 2]
        if not chained:
            s_lanes = slice(q * ns, (q + 1) * ns)
            hfr_ref[:, s_lanes] = h[t1 - 1, :, re]
            hfi_ref[:, s_lanes] = h[t1 - 1, :, im]
        for i in range(t1):
            y_ref[:, u_lanes(i, q)] = (_dot(h[i, :, re].astype(BF16), cre_ref[q])
                                       - _dot(h[i, :, im].astype(BF16), cim_ref[q])
                                       + d_ref[q] * u_ref[:, u_lanes(i, q)])

    input_drive(0)
    local_scan(0)
    chunk_starts(0)
    for q in range(nq):
        if q + 1 < nq:
            input_drive(q + 1)
        add_carry(q)
        readout(q)
        if q + 1 < nq:
            local_scan(q + 1)
            chunk_starts(q + 1)


def _ssm(u_chunks, h0_re, h0_im, layouts, t1, rows, chained):
    bmat, cre, cim, lam_re_q, lam_im_q, step_re_q, step_im_q, d_q = layouts
    lam_re_q, lam_im_q = lam_re_q[:t1], lam_im_q[:t1]
    n_blocks, rows_h, n_state = h0_re.shape
    width = u_chunks.shape[1]
    assert rows % SSM_ROW_TILE == 0 and u_chunks.shape[0] == n_blocks * rows
    assert not chained or rows % SSM_CHAIN_GROUP == 0
    ns2 = bmat.shape[2]
    u_spec = pl.BlockSpec((rows, width), lambda b: (b, 0))
    h_spec = pl.BlockSpec((None, rows_h, n_state), lambda b: (b, 0, 0))
    consts = [bmat, cre, cim, lam_re_q, lam_im_q, step_re_q, step_im_q, d_q]
    chain_rows = rows if chained else 8
    blocked = lambda r: pltpu.VMEM((ns2 // LANES, r, LANES), F32)
    vmem = (4 * rows * width * 4 + (2 * t1 + 1) * rows * ns2 * 4 + sum(c.size * c.dtype.itemsize for c in consts)
            + 8 * rows_h * n_state * 4 + 8 * 1024 * 1024)
    h_out = jax.ShapeDtypeStruct(h0_re.shape, F32)
    return pl.pallas_call(
        functools.partial(_ssm_body, t1=t1, chained=chained),
        grid=(n_blocks,),
        in_specs=[u_spec, h_spec, h_spec] + [_const_spec(c.shape) for c in consts],
        out_specs=[u_spec, h_spec, h_spec],
        out_shape=[jax.ShapeDtypeStruct(u_chunks.shape, F32), h_out, h_out],
        scratch_shapes=[pltpu.VMEM((2, t1, rows, ns2), F32), blocked(rows), blocked(chain_rows),
                        pltpu.VMEM((2, chain_rows // SSM_CHAIN_GROUP, ns2), F32)],
        compiler_params=_params(("parallel",), vmem),
        name="ssm_chained" if chained else "ssm_rows",
    )(u_chunks, h0_re, h0_im, *consts)


def _mix_body(x_ref, oatt_ref, yc_ref, wglu_ref, bglu_ref, gatt_ref, gssm_ref, wout_ref, lng_ref, lnb_ref,
              o_ref, y_scr, *, alpha, t1):
    d_att = oatt_ref.shape[1]
    n_lane_blocks, rows, _ = y_scr.shape
    d_ssm = n_lane_blocks * LANES
    chunks = rows // t1
    for c in range(n_lane_blocks):
        for i in range(t1):
            y_scr[c, pl.ds(i, chunks, stride=t1), :] = yc_ref[:, i * d_ssm + c * LANES:i * d_ssm + (c + 1) * LANES]
    g = jax.nn.gelu(jnp.concatenate([y_scr[c] for c in range(n_lane_blocks)], axis=1))
    z = g * jax.nn.sigmoid(_dot(g.astype(BF16), wglu_ref[...]) + bglu_ref[...])
    ra = _rms_norm(oatt_ref[...], gatt_ref[...]).astype(BF16)
    rz = _rms_norm(z, gssm_ref[...]).astype(BF16)
    mixed = _dot(ra, wout_ref[0:d_att, :]) + _dot(rz, wout_ref[d_att:, :])
    o_ref[...] = _layer_norm(alpha * x_ref[...] + mixed, lng_ref[...], lnb_ref[...])


def _mix(x2d, o_att, y_chunks, w_glu, b_glu, g_att, g_ssm, w_out, ln_g, ln_b, alpha, tm, t1):
    rows, dm = x2d.shape
    d_att, d_ssm = o_att.shape[1], y_chunks.shape[1] // t1
    consts = [w_glu, b_glu.reshape(1, -1), g_att.reshape(1, -1), g_ssm.reshape(1, -1), w_out,
              ln_g.reshape(1, -1), ln_b.reshape(1, -1)]
    row = lambda width: pl.BlockSpec((tm, width), lambda i: (i, 0))
    chunk_spec = pl.BlockSpec((tm // t1, t1 * d_ssm), lambda i: (i, 0))
    vmem = 2 * tm * (2 * dm + d_att + d_ssm) * 4 + 4 * (w_glu.size + w_out.size) + 12 * tm * dm * 4
    return pl.pallas_call(
        functools.partial(_mix_body, alpha=alpha, t1=t1),
        grid=(rows // tm,),
        in_specs=[row(dm), row(d_att), chunk_spec] + [_const_spec(c.shape) for c in consts],
        out_specs=row(dm),
        out_shape=jax.ShapeDtypeStruct(x2d.shape, F32),
        scratch_shapes=[pltpu.VMEM((d_ssm // LANES, tm, LANES), F32)],
        compiler_params=_params(("parallel",), vmem),
        name="mix_out_ln1",
    )(x2d, o_att, y_chunks, *consts)


def _memkv_body(m_ref, wk_ref, wv_ref, k_ref, v_ref):
    mb = m_ref[...].astype(BF16)
    k_ref[...] = _dot(mb, wk_ref[...])
    v_ref[...] = _dot(mb, wv_ref[...])


def _memkv(mem2d, wk, wv, tm):
    rows, dm = mem2d.shape
    row = pl.BlockSpec((tm, dm), lambda i: (i, 0))
    out = jax.ShapeDtypeStruct((rows, wk.shape[1]), F32)
    vmem = 2 * 3 * tm * dm * 4 + 4 * (wk.size + wv.size) + 4 * tm * dm * 4
    return pl.pallas_call(
        _memkv_body,
        grid=(rows // tm,),
        in_specs=[row, _const_spec(wk.shape), _const_spec(wv.shape)],
        out_specs=[row, row],
        out_shape=[out, out],
        compiler_params=_params(("parallel",), vmem),
        name="mem_kv",
    )(mem2d, wk, wv)


def _memattn_body(x_ref, mk_ref, mv_ref, wq_ref, wo_ref, lng_ref, lnb_ref, o_ref, *, alpha):
    dm = x_ref.shape[1]
    hd = dm // N_MEM_HEADS
    heads = [slice(h * hd, (h + 1) * hd) for h in range(N_MEM_HEADS)]
    x = x_ref[...]
    q = (_dot(x.astype(BF16), wq_ref[...]) * (hd ** -0.5)).astype(BF16)
    scores = [_dot_nt(q[:, c], mk_ref[:, c].astype(BF16)) for c in heads]
    probs =[jnp.exp(s - jnp.max(s, axis=1, keepdims=True)) for s in scores]
    sums = [jnp.sum(p, axis=1, keepdims=True) for p in probs]
    outs = [_dot(p.astype(BF16), mv_ref[:, c].astype(BF16)) / l for p, l, c in zip(probs, sums, heads)]
    att = _dot(jnp.concatenate(outs, axis=1).astype(BF16), wo_ref[...])
    o_ref[...] = _layer_norm(alpha * x + att, lng_ref[...], lnb_ref[...])


def _memattn(x2d, mem_k, mem_v, wq, wo, ln_g, ln_b, alpha, tm):
    rows, dm = x2d.shape
    n_seq, n_mem, _ = mem_k.shape
    steps_per_mem = rows // n_seq // tm
    row = pl.BlockSpec((tm, dm), lambda i: (i, 0))
    mem_spec = pl.BlockSpec((None, n_mem, dm), lambda i: (i // steps_per_mem, 0, 0))
    consts = [wq, wo, ln_g.reshape(1, -1), ln_b.reshape(1, -1)]
    vmem = 2 * 2 * tm * dm * 4 + 2 * 2 * n_mem * dm * 4 + 2 * (wq.size + wo.size) + 12 * tm * dm * 4
    return pl.pallas_call(
        functools.partial(_memattn_body, alpha=alpha),
        grid=(rows // tm,),
        in_specs=[row, mem_spec, mem_spec] + [_const_spec(c.shape) for c in consts],
        out_specs=row,
        out_shape=jax.ShapeDtypeStruct(x2d.shape, F32),
        compiler_params=_params(("parallel",), vmem),
        name="mem_attn_ln2",
    )(x2d, mem_k, mem_v, *consts)


def _rows_matmul_body(x_ref, w_ref, o_ref, *, scale):
    o_ref[...] = _dot(x_ref[...].astype(BF16), w_ref[...]) * scale


def _rows_matmul(x2d, w, scale):
    rows, dm = x2d.shape
    vmem = 4 * rows * (dm + w.shape[1]) * 4 + 4 * w.size
    return pl.pallas_call(
        functools.partial(_rows_matmul_body, scale=scale),
        out_shape=jax.ShapeDtypeStruct((rows, w.shape[1]), F32),
        compiler_params=_params(None, vmem),
        name="rows_matmul",
    )(x2d, w)


def _memattn_cache_body(q_ref, mk_ref, mv_ref, o_ref):
    nb, rows, hd = q_ref.shape
    n_mem, nh = mk_ref.shape[1:3]
    cols = n_mem * nh
    head_ok = ((lax.broadcasted_iota(jnp.int32, (rows, cols), 0) & (nh - 1))
               == (lax.broadcasted_iota(jnp.int32, (rows, cols), 1) & (nh - 1)))
    scores = [jnp.where(head_ok, _dot_nt(q_ref[j].astype(BF16), mk_ref[j].reshape(cols, hd).astype(BF16)), NEG)
              for j in range(nb)]
    probs = [jnp.exp(s - jnp.max(s, axis=1, keepdims=True)) for s in scores]
    sums = [jnp.sum(p, axis=1, keepdims=True) for p in probs]
    for j in range(nb):
        o_ref[j] = _dot(probs[j].astype(BF16), mv_ref[j].reshape(cols, hd).astype(BF16)) / sums[j]


def _memattn_cache(q3, cache_k, cache_v, layer, nb):
    bsz, rows, hd = q3.shape
    _, _, n_mem, nh, _ = cache_k.shape
    q_spec = pl.BlockSpec((nb, rows, hd), lambda i: (i, 0, 0))
    c_spec = pl.BlockSpec((None, nb, n_mem, nh, hd), lambda i: (layer, i, 0, 0, 0))
    vmem = 2 * 2 * nb * n_mem * 8 * hd * 4 + 8 * nb * rows * n_mem * nh * 4 + 8 * 1024 * 1024
    return pl.pallas_call(
        _memattn_cache_body,
        grid=(bsz // nb,),
        in_specs=[q_spec, c_spec, c_spec],
        out_specs=q_spec,
        out_shape=jax.ShapeDtypeStruct(q3.shape, F32),
        compiler_params=_params(("parallel",), vmem),
        name="mem_attn_cache",
    )(q3, cache_k, cache_v)


def _proj_ln_body(x_ref, a_ref, w_ref, lng_ref, lnb_ref, o_ref, *, alpha):
    att = _dot(a_ref[...].astype(BF16), w_ref[...])
    o_ref[...] = _layer_norm(alpha * x_ref[...] + att, lng_ref[...], lnb_ref[...])


def _proj_ln(x2d, a2d, w, ln_g, ln_b, alpha):
    rows, dm = x2d.shape
    vmem = 8 * rows * dm * 4 + 4 * w.size
    return pl.pallas_call(
        functools.partial(_proj_ln_body, alpha=alpha),
        out_shape=jax.ShapeDtypeStruct(x2d.shape, F32),
        compiler_params=_params(None, vmem),
        name="proj_ln",
    )(x2d, a2d, w, ln_g.reshape(1, -1), ln_b.reshape(1, -1))


def _ffn_body(x_ref, wg_ref, wu_ref, wd_ref, lng_ref, lnb_ref, o_ref, acc_ref, *, alpha, tf):
    x = x_ref[...]
    xb = x.astype(BF16)
    for c in range(wg_ref.shape[1] // tf):
        cols = slice(c * tf, (c + 1) * tf)
        hid = (jax.nn.silu(_dot(xb, wg_ref[:, cols])) * _dot(xb, wu_ref[:, cols])).astype(BF16)
        part = _dot(hid, wd_ref[cols, :])
        if c == 0:
            acc_ref[...] = part
        else:
            acc_ref[...] += part
    o_ref[...] = _layer_norm(alpha * x + acc_ref[...], lng_ref[...], lnb_ref[...])


def _ffn(x2d, wg, wu, wd, ln_g, ln_b, alpha, tm, tf):
    rows, dm = x2d.shape
    d_ff = wg.shape[1]
    assert d_ff % tf == 0 and tf % LANES == 0 and rows % tm == 0
    row = pl.BlockSpec((tm, dm), lambda i: (i, 0))
    consts = [wg, wu, wd, ln_g.reshape(1, -1), ln_b.reshape(1, -1)]
    vmem = 2 * 2 * tm * dm * 4 + 2 * 3 * wg.size + tm * dm * 4 + 8 * tm * max(tf, dm) * 4
    return pl.pallas_call(
        functools.partial(_ffn_body, alpha=alpha, tf=tf),
        grid=(rows // tm,),
        in_specs=[row] + [_const_spec(c.shape) for c in consts],
        out_specs=row,
        out_shape=jax.ShapeDtypeStruct(x2d.shape, F32),
        scratch_shapes=[pltpu.VMEM((tm, dm), F32)],
        compiler_params=_params(("parallel",), vmem),
        name="swiglu_ln3",
    )(x2d, *consts)


PROMPT_CHUNK = 8
ROW_TILE = 512
FFN_COL_TILE = 256
SAMPLE_ATTN_SEQS = 2
SAMPLE_MEM_SEQS = 4


def kernel(x_prompt, x_sample, cache_win_k, cache_win_v, state_ssm_re, state_ssm_im, cache_mem_k, cache_mem_v, mem_prompt, w_in, g_att, g_ssm, ssm_a_re, ssm_a_im, ssm_log_dt, ssm_b_re, ssm_b_im, ssm_c_re, ssm_c_im, ssm_d, w_glu, b_glu, w_out, ln1_g, ln1_b, w_mem_q, w_mem_k, w_mem_v, w_mem_o, ln2_g, ln2_b, w_gate, w_up, w_down, ln3_g, ln3_b):
    depth = w_in.shape[0]
    bp, seq, dm = x_prompt.shape
    bs, t_new, _ = x_sample.shape
    n_groups, n_state = ssm_a_re.shape[1:]
    n_mem = mem_prompt.shape[1]
    alpha = (2 * depth) ** 0.25
    keep = min(DILATED_CFGS[-1][0], seq)
    assert keep == seq
    n_chunks = seq // PROMPT_CHUNK

    y_p = x_prompt.reshape(bp * seq, dm)
    y_s = x_sample.reshape(bs * t_new, dm)
    mem2d = mem_prompt.reshape(bp * n_mem, dm)
    outs = [[] for _ in range(10)]
    for l in range(depth):
        bf = lambda w: w[l].astype(BF16)
        w_in_l, w_glu_l, w_out_l = bf(w_in), bf(w_glu), bf(w_out)
        wq_l, wk_l, wv_l, wo_l = bf(w_mem_q), bf(w_mem_k), bf(w_mem_v), bf(w_mem_o)
        wg_l, wu_l, wd_l = bf(w_gate), bf(w_up), bf(w_down)
        prepped = _ssm_prep(ssm_a_re[l], ssm_a_im[l], ssm_log_dt[l], ssm_b_re[l], ssm_b_im[l],
                            max(PROMPT_CHUNK, t_new), PROMPT_CHUNK)
        layouts = _ssm_layouts(*prepped, ssm_c_re[l], ssm_c_im[l], ssm_d[l])
        mix_w = (w_glu_l, b_glu[l], g_att[l], g_ssm[l], w_out_l, ln1_g[l], ln1_b[l])

        q, k, v, u, k_t, v_t = _inproj(y_p, w_in_l, ROW_TILE, PROMPT_CHUNK, seq)
        d_att = q.shape[1]
        o_att = _attn_prompt(q.reshape(bp, seq, d_att), k.reshape(bp, seq, d_att), v.reshape(bp, seq, d_att))
        zeros = jnp.zeros((bp, 1, n_groups * n_state), F32)
        y_ssm, hr_p, hi_p = _ssm(u, zeros, zeros, layouts, PROMPT_CHUNK, n_chunks, True)
        x1 = _mix(y_p, o_att.reshape(bp * seq, d_att), y_ssm, *mix_w, alpha, ROW_TILE, PROMPT_CHUNK)
        mk_p, mv_p = _memkv(mem2d, wk_l, wv_l, ROW_TILE)
        x2 = _memattn(x1, mk_p.reshape(bp, n_mem, dm), mv_p.reshape(bp, n_mem, dm), wq_l, wo_l,
                      ln2_g[l], ln2_b[l], alpha, ROW_TILE)
        y_p = _ffn(x2, wg_l, wu_l, wd_l, ln3_g[l], ln3_b[l], alpha, ROW_TILE, FFN_COL_TILE)

        qs, ks, vs, us = _inproj(y_s, w_in_l, bs * t_new, t_new)
        shp = (bs, t_new, d_att)
        o_att_s = _attn_window(qs.reshape(shp), ks.reshape(shp), vs.reshape(shp), cache_win_k, cache_win_v, l,
                               SAMPLE_ATTN_SEQS)
        y_ssm_s, hr_s, hi_s = _ssm(us, state_ssm_re[l].reshape(1, bs, -1), state_ssm_im[l].reshape(1, bs, -1),
                                   layouts, t_new, bs, False)
        x1s = _mix(y_s, o_att_s.reshape(bs * t_new, d_att), y_ssm_s, *mix_w, alpha, bs * t_new, t_new)
        mem_hd = dm // N_MEM_HEADS
        q_mem = _rows_matmul(x1s, wq_l, mem_hd ** -0.5).reshape(bs, t_new * N_MEM_HEADS, mem_hd)
        a_mem = _memattn_cache(q_mem, cache_mem_k, cache_mem_v, l, SAMPLE_MEM_SEQS)
        x2s = _proj_ln(x1s, a_mem.reshape(bs * t_new, dm), wo_l, ln2_g[l], ln2_b[l], alpha)
        y_s = _ffn(x2s, wg_l, wu_l, wd_l, ln3_g[l], ln3_b[l], alpha, bs * t_new, FFN_COL_TILE)

        head_shape = (N_ATT_HEADS, ATT_HEAD_DIM)
        state_shape = (n_groups, n_state)
        mem_shape = (bp, n_mem, N_MEM_HEADS, dm // N_MEM_HEADS)
        rows_first = lambda t: jnp.transpose(t.reshape(bp, *head_shape, seq), (0, 3, 1, 2))
        for lst, val in zip(outs, (
                rows_first(k_t)[:, seq - keep:], rows_first(v_t)[:, seq - keep:],
                ks.reshape(bs, t_new, *head_shape), vs.reshape(bs, t_new, *head_shape),
                hr_p.reshape(bp, *state_shape), hi_p.reshape(bp, *state_shape),
                hr_s.reshape(bs, *state_shape), hi_s.reshape(bs, *state_shape),
                mk_p.reshape(mem_shape), mv_p.reshape(mem_shape))):
            lst.append(val)
    return (y_p.reshape(bp, seq, dm), y_s.reshape(bs, t_new, dm)) + tuple(jnp.stack(o) for o in outs)
```

```python
import functools
import math

import jax
import jax.numpy as jnp
from jax import lax
from jax.experimental import pallas as pl
from jax.experimental.pallas import tpu as pltpu

F32 = jnp.float32
BF16 = jnp.bfloat16

N_ATT_HEADS = 8
ATT_HEAD_DIM = 64
D_ATT = N_ATT_HEADS * ATT_HEAD_DIM
DILATED_CFGS = ((128, 1), (512, 4), (2048, 16))
ATT_BLK = 128
ATT_GROUP = 4
SSM_CH = 16
SSM_STATE = 64
N_MEM_HEADS = 4
EPS = 1e-5
NEG = -1e30

LANES = 128
V7X_VMEM_CAP_BYTES = 56 * 1024 * 1024

SSM_GB = LANES // SSM_CH
SSM_ROW_TILE = 32
SSM_CHAIN_GROUP = 8


def _params(sem, vmem_bytes):
    return pltpu.CompilerParams(
        dimension_semantics=sem,
        vmem_limit_bytes=int(min(max(vmem_bytes, 16 * 1024 * 1024), V7X_VMEM_CAP_BYTES)),
    )


def _dot(a, b):
    return jnp.dot(a, b, preferred_element_type=F32)


def _dot_nt(a, b):
    return lax.dot_general(a, b, (((1,), (1,)), ((), ())), preferred_element_type=F32)


def _layer_norm(x, g, b):
    mu = jnp.mean(x, axis=-1, keepdims=True)
    xc = x - mu
    var = jnp.mean(xc * xc, axis=-1, keepdims=True)
    return xc * lax.rsqrt(var + EPS) * g + b


def _rms_norm(x, g):
    return x * lax.rsqrt(jnp.mean(x * x, axis=-1, keepdims=True) + EPS) * g


def _const_spec(shape):
    n = len(shape)
    return pl.BlockSpec(shape, lambda *_: (0,) * n, pipeline_mode=pl.Buffered(1))


def _inproj_body(x_ref, w_ref, q_ref, k_ref, v_ref, uc_ref, *rest, t1, transposed):
    u_scr = rest[-1]
    xb = x_ref[...].astype(BF16)
    d = q_ref.shape[1]
    q_ref[...] = _dot(xb, w_ref[:, 0 * d:1 * d])
    k = _dot(xb, w_ref[:, 1 * d:2 * d])
    v = _dot(xb, w_ref[:, 2 * d:3 * d])
    k_ref[...] = k
    v_ref[...] = v
    if transposed:
        kt_ref, vt_ref = rest[:2]
        kt_ref[...] = k.T
        vt_ref[...] = v.T
    u = _dot(xb, w_ref[:, 3 * d:4 * d])
    chunks = u_scr.shape[1] // t1
    for c in range(d // LANES):
        u_scr[c] = u[:, c * LANES:(c + 1) * LANES]
        for i in range(t1):
            uc_ref[:, i * d + c * LANES:i * d + (c + 1) * LANES] = u_scr[c, pl.ds(i, chunks, stride=t1), :]


def _inproj(x2d, w_in_bf16, tm, t1, seq=None):
    rows, dm = x2d.shape
    d = w_in_bf16.shape[1] // 4
    out = jax.ShapeDtypeStruct((rows, d), F32)
    row_spec = pl.BlockSpec((tm, d), lambda i: (i, 0))
    out_specs = [row_spec] * 3 + [pl.BlockSpec((tm // t1, t1 * d), lambda i: (i, 0))]
    out_shape = [out] * 3 + [jax.ShapeDtypeStruct((rows // t1, t1 * d), F32)]
    if seq is not None:
        steps = seq // tm
        out_specs += [pl.BlockSpec((None, d, tm), lambda i: (i // steps, 0, i % steps))] * 2
        out_shape += [jax.ShapeDtypeStruct((rows // seq, d, seq), F32)] * 2
    vmem = 2 * (tm * dm * 4 + 6 * tm * d * 4) + 2 * w_in_bf16.size * 2 + 10 * tm * d * 4
    return pl.pallas_call(
        functools.partial(_inproj_body, t1=t1, transposed=seq is not None),
        grid=(rows // tm,),
        in_specs=[pl.BlockSpec((tm, dm), lambda i: (i, 0)), _const_spec(w_in_bf16.shape)],
        out_specs=out_specs,
        out_shape=out_shape,
        scratch_shapes=[pltpu.VMEM((d // LANES, tm, LANES), F32)],
        compiler_params=_params(("parallel",), vmem),
        name="inproj",
    )(x2d, w_in_bf16)


def _attn_prompt_body(q_ref, k_ref, v_ref, o_ref, acc_ref, m_ref, l_ref):
    seq = q_ref.shape[0]
    blk = ATT_BLK
    scale = ATT_HEAD_DIM ** -0.5 * math.log2(math.e)
    lane = lax.broadcasted_iota(jnp.int32, (blk, LANES), 1)
    head0 = lane < ATT_HEAD_DIM
    qi = lax.broadcasted_iota(jnp.int32, (blk, blk), 0)
    kj = lax.broadcasted_iota(jnp.int32, (blk, blk), 1)
    tri = kj <= qi
    mask_rest = jnp.concatenate([kj >= qi, tri], axis=1)

    def rows_of(start, d):
        return pl.ds(start, blk) if d == 1 else pl.ds(start, blk, stride=d)

    def load_block(d, q_start, prev_start):
        rows = rows_of(q_start, d)
        q = q_ref[rows, :] * scale
        k2 = k_ref[rows, :].astype(BF16)
        v2 = v_ref[rows, :].astype(BF16)
        mask = tri
        if prev_start is not None:
            prow = rows_of(prev_start, d)
            k2 = jnp.concatenate([k_ref[prow, :].astype(BF16), k2], axis=0)
            v2 = jnp.concatenate([v_ref[prow, :].astype(BF16), v2], axis=0)
            mask = mask_rest
        qh = [jnp.where(head0 if h == 0 else jnp.logical_not(head0), q, 0.0).astype(BF16) for h in range(2)]
        return rows, qh, k2, v2, mask

    def attend_group(cfg, d, starts):
        blocks = [load_block(d, q_start, prev_start) for q_start, prev_start in starts]
        scores = [[jnp.where(mask, _dot_nt(qh[h], k2), NEG) for h in range(2)] for _, qh, k2, _, mask in blocks]
        maxes = [[jnp.max(s, axis=1, keepdims=True) for s in sb] for sb in scores]
        probs = [[jnp.exp2(s - m) for s, m in zip(sb, mb)] for sb, mb in zip(scores, maxes)]
        sums = [[jnp.sum(p, axis=1, keepdims=True) for p in pb] for pb in probs]
        pvs = [[_dot(p.astype(BF16), blk_[3]) for p in pb] for pb, blk_ in zip(probs, blocks)]
        new = [(blk_[0], jnp.where(head0, mb[0], mb[1]), jnp.where(head0, lb[0], lb[1]),
                jnp.where(head0, ob[0], ob[1])) for blk_, mb, lb, ob in zip(blocks, maxes, sums, pvs)]
        if cfg == 0:
            for rows, m, l, num in new:
                acc_ref[rows, :] = num
                m_ref[rows, :] = m
                l_ref[rows, :] = l
            return
        old = [(m_ref[rows, :], l_ref[rows, :], acc_ref[rows, :]) for rows, _, _, _ in new]
        for (rows, m, l, num), (m_old, l_old, acc_old) in zip(new, old):
            m_new = jnp.maximum(m_old, m)
            a = jnp.exp2(m_old - m_new)
            b = jnp.exp2(m - m_new)
            num = a * acc_old + b * num
            l = a * l_old + b * l
            if cfg == len(DILATED_CFGS) - 1:
                o_ref[rows, :] = num / l
            else:
                acc_ref[rows, :] = num
                m_ref[rows, :] = m_new
                l_ref[rows, :] = l

    grp = ATT_GROUP
    for cfg, (window, d) in enumerate(reversed(DILATED_CFGS)):
        span = d * blk
        nb = seq // span
        if d >= grp:
            def residue_group(g, carry, cfg=cfg, d=d, span=span, nb=nb):
                res = [g * grp + j for j in range(grp)]
                attend_group(cfg, d, [(r, None) for r in res])
                if nb > 1:
                    def later(n, c):
                        attend_group(cfg, d, [(r + n * span, r + (n - 1) * span) for r in res])
                        return c
                    lax.fori_loop(1, nb, later, 0)
                return carry

            lax.fori_loop(0, d // grp, residue_group, 0)
        else:
            assert d == 1 and nb % grp == 0
            attend_group(cfg, d, [(0, None)] + [(n * span, (n - 1) * span) for n in range(1, grp)])

            def block_group(g, carry, cfg=cfg, d=d, span=span):
                attend_group(cfg, d, [((g * grp + j) * span, (g * grp + j - 1) * span) for j in range(grp)])
                return carry

            lax.fori_loop(1, nb // grp, block_group, 0)


def _attn_prompt(q, k, v):
    bsz, seq, d_att = q.shape
    assert d_att % LANES == 0 and LANES == 2 * ATT_HEAD_DIM
    for window, d in DILATED_CFGS:
        assert window // d == ATT_BLK and seq % (d * ATT_BLK) == 0
    spec = pl.BlockSpec((None, seq, LANES), lambda b, h: (b, 0, h))
    blk_bytes = seq * LANES * 4
    return pl.pallas_call(
        _attn_prompt_body,
        grid=(bsz, d_att // LANES),
        in_specs=[spec, spec, spec],
        out_specs=spec,
        out_shape=jax.ShapeDtypeStruct(q.shape, F32),
        scratch_shapes=[pltpu.VMEM((seq, LANES), F32)] * 3,
        compiler_params=_params(("parallel", "parallel"), 11 * blk_bytes + 8 * 1024 * 1024),
        name="attn_prompt",
    )(q, k, v)


def _window_head_mask(d_att):
    nh = N_ATT_HEADS
    return (jnp.right_shift(lax.broadcasted_iota(jnp.int32, (nh, d_att), 1), ATT_HEAD_DIM.bit_length() - 1)
            == lax.broadcasted_iota(jnp.int32, (nh, d_att), 0))


def _window_probs(q, kn, kt):
    t_new, d_att = q.shape
    w_buf = kt.shape[1]
    nh = N_ATT_HEADS
    rows = t_new * nh
    (win1, _), (win4, dil4), (_, dil16) = DILATED_CFGS
    tail, near = win4, win1
    head_mask = _window_head_mask(d_att)
    trow = jnp.right_shift(lax.broadcasted_iota(jnp.int32, (rows, 1), 0), nh.bit_length() - 1)
    lane = lambda n: lax.broadcasted_iota(jnp.int32, (rows, n), 1)
    mask16 = (lane(w_buf) & (dil16 - 1)) == trow
    mask4 = (lane(tail) & (dil4 - 1)) == trow
    mask1 = lane(near) >= trow
    new_self = lane(t_new) == trow
    new_causal = lane(t_new) <= trow

    q = q * (ATT_HEAD_DIM ** -0.5)
    qbd = jnp.concatenate(
        [jnp.where(head_mask, jnp.broadcast_to(q[t:t + 1], (nh, d_att)), 0.0) for t in range(t_new)],
        axis=0).astype(BF16)
    s_all = _dot(qbd, kt.astype(BF16))
    s_new = _dot_nt(qbd, kn.astype(BF16))

    def softmax_parts(main, main_mask, new_mask):
        sm = jnp.where(main_mask, main, NEG)
        sn = jnp.where(new_mask, s_new, NEG)
        m = jnp.maximum(jnp.max(sm, axis=1, keepdims=True), jnp.max(sn, axis=1, keepdims=True))
        p = jnp.exp(sm - m)
        pn = jnp.exp(sn - m)
        den = jnp.sum(p, axis=1, keepdims=True) + jnp.sum(pn, axis=1, keepdims=True)
        return m, p, pn, den

    m16, p16, pn16, den16 = softmax_parts(s_all, mask16, new_self)
    m4, p4, pn4, den4 = softmax_parts(s_all[:, w_buf - tail:], mask4, new_self)
    m1, p1, pn1, den1 = softmax_parts(s_all[:, w_buf - near:], mask1, new_causal)
    m = jnp.maximum(jnp.maximum(m1, m4), m16)
    w1, w4, w16 = jnp.exp(m1 - m), jnp.exp(m4 - m), jnp.exp(m16 - m)
    den = w1 * den1 + w4 * den4 + w16 * den16
    p16, p4, p1 = w16 * p16, w4 * p4, w1 * p1
    p_all = jnp.concatenate(
        [p16[:, :w_buf - tail],
         p16[:, w_buf - tail:w_buf - near] + p4[:, :tail - near],
         p16[:, w_buf - near:] + p4[:, tail - near:] + p1], axis=1).astype(BF16)
    pn_all = (w16 * pn16 + w4 * pn4 + w1 * pn1).astype(BF16)
    return p_all, pn_all, den


def _window_output(p_all, pn_all, den, vn, vt):
    t_new, d_att = vn.shape
    nh = N_ATT_HEADS
    head_mask = _window_head_mask(d_att)
    out = (_dot_nt(p_all, vt.astype(BF16)) + _dot(pn_all, vn.astype(BF16))) / den
    return jnp.concatenate(
        [jnp.sum(jnp.where(head_mask, out[t * nh:(t + 1) * nh], 0.0), axis=0, keepdims=True)
         for t in range(t_new)], axis=0)


def _attn_window_body(q_ref, kn_ref, vn_ref, kt_ref, vt_ref, o_ref):
    nb = q_ref.shape[0]
    probs = [_window_probs(q_ref[b], kn_ref[b], kt_ref[b]) for b in range(nb)]
    for b in range(nb):
        o_ref[b] = _window_output(*probs[b], vn_ref[b], vt_ref[b])


def _attn_window(q, k_new, v_new, cache_k, cache_v, layer, nb):
    bsz, t_new, d_att = q.shape
    depth, _, w_buf, nh, hd = cache_k.shape
    assert w_buf == DILATED_CFGS[-1][0] and t_new <= DILATED_CFGS[1][1] and nh * hd == d_att
    rows_last = lambda c: jnp.transpose(c, (0, 1, 3, 4, 2)).reshape(depth, bsz, d_att, w_buf)
    new_spec = pl.BlockSpec((nb, t_new, d_att), lambda i: (i, 0, 0))
    cache_spec = pl.BlockSpec((None, nb, d_att, w_buf), lambda i: (layer, i, 0, 0))
    vmem = 2 * 2 * nb * d_att * w_buf * 4 + 4 * nb * d_att * w_buf * 2 + 12 * 1024 * 1024
    return pl.pallas_call(
        _attn_window_body,
        grid=(bsz // nb,),
        in_specs=[new_spec, new_spec, new_spec, cache_spec, cache_spec],
        out_specs=new_spec,
        out_shape=jax.ShapeDtypeStruct(q.shape, F32),
        compiler_params=_params(("parallel",), vmem),
        name="attn_window",
    )(q, k_new, v_new, rows_last(cache_k), rows_last(cache_v))


def _ssm_prep_body(a_re_ref, a_im_ref, log_dt_ref, b_re_ref, b_im_ref, lam_re_ref, lam_im_ref,
                   bbar_re_ref, bbar_im_ref, step_re_ref, step_im_ref, bpow_re_ref, bpow_im_ref, *, chunk):
    ar = a_re_ref[...]
    ai = a_im_ref[...]
    dt = jnp.exp(log_dt_ref[...])
    mag = jnp.exp(dt * ar)
    lr = mag * jnp.cos(dt * ai)
    li = mag * jnp.sin(dt * ai)
    den = ar * ar + ai * ai
    nr, ni = lr - 1.0, li
    cr = (nr * ar + ni * ai) / den
    ci = (ni * ar - nr * ai) / den
    for h in range(b_re_ref.shape[0]):
        br, bi = b_re_ref[h], b_im_ref[h]
        bbar_re_ref[h] = cr * br - ci * bi
        bbar_im_ref[h] = cr * bi + ci * br
    n_h = b_re_ref.shape[0]
    pr, pi = lr, li
    for j in range(lam_re_ref.shape[0]):
        if j < chunk:
            for h in range(n_h):
                br, bi = bbar_re_ref[h], bbar_im_ref[h]
                if j == 0:
                    bpow_re_ref[h], bpow_im_ref[h] = br, bi
                else:
                    qr, qi = lam_re_ref[j - 1], lam_im_ref[j - 1]
                    bpow_re_ref[j * n_h + h] = qr * br - qi * bi
                    bpow_im_ref[j * n_h + h] = qr * bi + qi * br
        lam_re_ref[j] = pr
        lam_im_ref[j] = pi
        if j == chunk - 1:
            cr, ci = pr, pi
        pr, pi = pr * lr - pi * li, pr * li + pi * lr
    pr, pi = cr, ci
    for j in range(step_re_ref.shape[0]):
        step_re_ref[j] = pr
        step_im_ref[j] = pi
        pr, pi = pr * cr - pi * ci, pr * ci + pi * cr


def _ssm_prep(a_re, a_im, log_dt, b_re, b_im, n_pow, chunk):
    g, p = a_re.shape
    h = b_re.shape[-1]
    assert chunk <= n_pow
    b_re_t = jnp.transpose(b_re, (2, 0, 1))
    b_im_t = jnp.transpose(b_im, (2, 0, 1))
    gp = jax.ShapeDtypeStruct((n_pow, g, p), F32)
    hgp = jax.ShapeDtypeStruct((h, g, p), F32)
    sgp = jax.ShapeDtypeStruct((SSM_CHAIN_GROUP, g, p), F32)
    bgp = jax.ShapeDtypeStruct((chunk * h, g, p), F32)
    return pl.pallas_call(
        functools.partial(_ssm_prep_body, chunk=chunk),
        out_shape=[gp, gp, hgp, hgp, sgp, sgp, bgp, bgp],
        name="ssm_prep",
    )(a_re, a_im, log_dt.reshape(g, 1), b_re_t, b_im_t)


def _ssm_layouts(lam_re, lam_im, bbar_re_t, bbar_im_t, step_re, step_im, bpow_re, bpow_im, c_re, c_im, d_skip):
    n_pow, g, p = lam_re.shape
    h = bbar_re_t.shape[0]
    nq = g // SSM_GB
    eye = jnp.eye(SSM_GB, dtype=bool)
    bb = jnp.stack([bbar_re_t, bbar_im_t], 0).reshape(2, h, nq, SSM_GB, p)
    bb = jnp.transpose(bb, (2, 3, 1, 0, 4))
    bmat = jnp.where(eye[None, :, None, None, :, None], bb[:, :, :, :, None, :], 0.0)
    bmat = bmat.reshape(nq, SSM_GB * h, 2 * SSM_GB * p).astype(BF16)
    chunk = bpow_re.shape[0] // h
    bp = jnp.stack([bpow_re, bpow_im], 0).reshape(2, chunk, h, nq, SSM_GB, p)[:, ::-1]
    bp = jnp.transpose(bp, (3, 1, 4, 2, 0, 5))
    fmat = jnp.where(eye[None, None, :, None, None, :, None], bp[:, :, :, :, :, None, :], 0.0)
    fmat = fmat.reshape(nq, chunk * SSM_GB * h, 2 * SSM_GB * p).astype(BF16)
    def c_layout(c):
        cc = jnp.transpose(c.reshape(nq, SSM_GB, h, p), (0, 1, 3, 2))
        m = jnp.where(eye[None, :, None, :, None], cc[:, :, :, None, :], 0.0)
        return m.reshape(nq, SSM_GB * p, SSM_GB * h).astype(BF16)
    per_block = lambda t: t.reshape(t.shape[0], nq, 1, SSM_GB * p)
    d_q = d_skip.reshape(nq, 1, SSM_GB * h)
    return (bmat, fmat, c_layout(c_re), c_layout(c_im), per_block(lam_re), per_block(lam_im),
            per_block(step_re), per_block(step_im), d_q)


def _ssm_body(u_ref, h0r_ref, h0i_ref, bmat_ref, fmat_ref, cre_ref, cim_ref, lamr_ref, lami_ref, stepr_ref,
              stepi_ref, d_ref, y_ref, hfr_ref, hfi_ref, hloc_ref, hs_ref, ends_ref, grp_ref, *, t1, chained):
    rows = u_ref.shape[0]
    d_ssm = y_ref.shape[1] // t1
    nq = bmat_ref.shape[0]
    ns = lamr_ref.shape[-1]
    rt = SSM_ROW_TILE
    re, im = slice(0, ns), slice(ns, 2 * ns)

    def u_lanes(i, q):
        return slice(i * d_ssm + q * LANES, i * d_ssm + (q + 1) * LANES)

    def row_tiles(body):
        lax.fori_loop(0, rows // rt, lambda t, c: (body(pl.ds(pl.multiple_of(t * rt, rt), rt)), c)[1], 0,
                      unroll=True)

    nlb = ns // LANES

    def input_drive(q):
        ub = [u_ref[:, u_lanes(i, q)].astype(BF16) for i in range(t1)]
        for i in range(t1):
            hloc_ref[q % 2, i] = _dot(ub[i], bmat_ref[q])
        if chained:
            ends = _dot(jnp.concatenate(ub, axis=1), fmat_ref[q])
            put_blocks(ends_ref, slice(None), ends[:, re], ends[:, im])

    def put_blocks(ref, rows_idx, vr, vi):
        for c in range(nlb):
            ref[c, rows_idx, :] = vr[:, c * LANES:(c + 1) * LANES]
            ref[nlb + c, rows_idx, :] = vi[:, c * LANES:(c + 1) * LANES]

    def get_blocks(ref, rows_idx):
        return (jnp.concatenate([ref[c, rows_idx, :] for c in range(nlb)], axis=1),
                jnp.concatenate([ref[nlb + c, rows_idx, :] for c in range(nlb)], axis=1))

    def chunk_starts(q):
        s_lanes = slice(q * ns, (q + 1) * ns)
        if not chained:
            put_blocks(hs_ref, slice(None), h0r_ref[:, s_lanes], h0i_ref[:, s_lanes])
            return
        grp = SSM_CHAIN_GROUP
        n_grp = rows // grp
        member = lambda j: pl.ds(j, n_grp, stride=grp)
        cr, ci = stepr_ref[0, q], stepi_ref[0, q]
        gr, gi = get_blocks(ends_ref, member(0))
        inside = [(gr, gi)]
        for j in range(1, grp):
            er, ei = get_blocks(ends_ref, member(j))
            gr, gi = cr * gr - ci * gi + er, cr * gi + ci * gr + ei
            inside.append((gr, gi))
        grp_ref[0, :, re] = gr
        grp_ref[0, :, im] = gi
        wr, wi = stepr_ref[grp - 1, q], stepi_ref[grp - 1, q]

        def chain(c, g):
            sr, si = g
            row = pl.ds(c, 1)
            grp_ref[1, row, re] = sr
            grp_ref[1, row, im] = si
            return wr * sr - wi * si + grp_ref[0, row, re], wr * si + wi * sr + grp_ref[0, row, im]

        sr, si = lax.fori_loop(0, n_grp, chain, (h0r_ref[:, s_lanes], h0i_ref[:, s_lanes]))
        hfr_ref[:, s_lanes] = sr
        hfi_ref[:, s_lanes] = si
        sr, si = grp_ref[1, :, re], grp_ref[1, :, im]
        put_blocks(hs_ref, member(0), sr, si)
        for j in range(1, grp):
            pr, pi = stepr_ref[j - 1, q], stepi_ref[j - 1, q]
            gr, gi = inside[j - 1]
            put_blocks(hs_ref, member(j), gr + (pr * sr - pi * si), gi + (pr * si + pi * sr))

    def scan(q):
        h = hloc_ref.at[q % 2]
        lr, li = lamr_ref[0, q], lami_ref[0, q]

        def tile(r):
            hr, hi = get_blocks(hs_ref, r)
            for i in range(t1):
                hr, hi = lr * hr - li * hi + h[i, r, re], lr * hi + li * hr + h[i, r, im]
                h[i, r, re] = hr
                h[i, r, im] = hi
        row_tiles(tile)

    def readout(q):
        h = hloc_ref.at[q % 2]
        if not chained:
            s_lanes = slice(q * ns, (q + 1) * ns)
            hfr_ref[:, s_lanes] = h[t1 - 1, :, re]
            hfi_ref[:, s_lanes] = h[t1 - 1, :, im]
        for i in range(t1):
            y_ref[:, u_lanes(i, q)] = (_dot(h[i, :, re].astype(BF16), cre_ref[q])
                                       - _dot(h[i, :, im].astype(BF16), cim_ref[q])
                                       + d_ref[q] * u_ref[:, u_lanes(i, q)])

    input_drive(0)
    chunk_starts(0)
    for q in range(nq):
        if q + 1 < nq:
            input_drive(q + 1)
        scan(q)
        readout(q)
        if q + 1 < nq:
            chunk_starts(q + 1)


def _ssm(u_chunks, h0_re, h0_im, layouts, t1, rows, chained):
    bmat, fmat, cre, cim, lam_re_q, lam_im_q, step_re_q, step_im_q, d_q = layouts
    lam_re_q, lam_im_q = lam_re_q[:t1], lam_im_q[:t1]
    if not chained:
        fmat = fmat[:, :8]
    else:
        assert fmat.shape[1] == t1 * LANES
    n_blocks, rows_h, n_state = h0_re.shape
    width = u_chunks.shape[1]
    assert rows % SSM_ROW_TILE == 0 and u_chunks.shape[0] == n_blocks * rows
    assert not chained or rows % SSM_CHAIN_GROUP == 0
    ns2 = bmat.shape[2]
    u_spec = pl.BlockSpec((rows, width), lambda b: (b, 0))
    h_spec = pl.BlockSpec((None, rows_h, n_state), lambda b: (b, 0, 0))
    consts = [bmat, fmat, cre, cim, lam_re_q, lam_im_q, step_re_q, step_im_q, d_q]
    chain_rows = rows if chained else 8
    blocked = lambda r: pltpu.VMEM((ns2 // LANES, r, LANES), F32)
    vmem = (4 * rows * width * 4 + (2 * t1 + 1) * rows * ns2 * 4 + sum(c.size * c.dtype.itemsize for c in consts)
            + 8 * rows_h * n_state * 4 + 8 * 1024 * 1024)
    h_out = jax.ShapeDtypeStruct(h0_re.shape, F32)
    return pl.pallas_call(
        functools.partial(_ssm_body, t1=t1, chained=chained),
        grid=(n_blocks,),
        in_specs=[u_spec, h_spec, h_spec] + [_const_spec(c.shape) for c in consts],
        out_specs=[u_spec, h_spec, h_spec],
        out_shape=[jax.ShapeDtypeStruct(u_chunks.shape, F32), h_out, h_out],
        scratch_shapes=[pltpu.VMEM((2, t1, rows, ns2), F32), blocked(rows), blocked(chain_rows),
                        pltpu.VMEM((2, chain_rows // SSM_CHAIN_GROUP, ns2), F32)],
        compiler_params=_params(("parallel",), vmem),
        name="ssm_chained" if chained else "ssm_rows",
    )(u_chunks, h0_re, h0_im, *consts)


def _mix_body(x_ref, oatt_ref, yc_ref, wglu_ref, bglu_ref, gatt_ref, gssm_ref, wout_ref, lng_ref, lnb_ref,
              o_ref, y_scr, *, alpha, t1):
    d_att = oatt_ref.shape[1]
    n_lane_blocks, rows, _ = y_scr.shape
    d_ssm = n_lane_blocks * LANES
    chunks = rows // t1
    for c in range(n_lane_blocks):
        for i in range(t1):
            y_scr[c, pl.ds(i, chunks, stride=t1), :] = yc_ref[:, i * d_ssm + c * LANES:i * d_ssm + (c + 1) * LANES]
    g = jax.nn.gelu(jnp.concatenate([y_scr[c] for c in range(n_lane_blocks)], axis=1))
    z = g * jax.nn.sigmoid(_dot(g.astype(BF16), wglu_ref[...]) + bglu_ref[...])
    ra = _rms_norm(oatt_ref[...], gatt_ref[...]).astype(BF16)
    rz = _rms_norm(z, gssm_ref[...]).astype(BF16)
    mixed = _dot(ra, wout_ref[0:d_att, :]) + _dot(rz, wout_ref[d_att:, :])
    o_ref[...] = _layer_norm(alpha * x_ref[...] + mixed, lng_ref[...], lnb_ref[...])


def _mix(x2d, o_att, y_chunks, w_glu, b_glu, g_att, g_ssm, w_out, ln_g, ln_b, alpha, tm, t1):
    rows, dm = x2d.shape
    d_att, d_ssm = o_att.shape[1], y_chunks.shape[1] // t1
    consts = [w_glu, b_glu.reshape(1, -1), g_att.reshape(1, -1), g_ssm.reshape(1, -1), w_out,
              ln_g.reshape(1, -1), ln_b.reshape(1, -1)]
    row = lambda width: pl.BlockSpec((tm, width), lambda i: (i, 0))
    chunk_spec = pl.BlockSpec((tm // t1, t1 * d_ssm), lambda i: (i, 0))
    vmem = 2 * tm * (2 * dm + d_att + d_ssm) * 4 + 4 * (w_glu.size + w_out.size) + 12 * tm * dm * 4
    return pl.pallas_call(
        functools.partial(_mix_body, alpha=alpha, t1=t1),
        grid=(rows // tm,),
        in_specs=[row(dm), row(d_att), chunk_spec] + [_const_spec(c.shape) for c in consts],
        out_specs=row(dm),
        out_shape=jax.ShapeDtypeStruct(x2d.shape, F32),
        scratch_shapes=[pltpu.VMEM((d_ssm // LANES, tm, LANES), F32)],
        compiler_params=_params(("parallel",), vmem),
        name="mix_out_ln1",
    )(x2d, o_att, y_chunks, *consts)


def _memkv_body(m_ref, wk_ref, wv_ref, k_ref, v_ref):
    mb = m_ref[...].astype(BF16)
    k_ref[...] = _dot(mb, wk_ref[...])
    v_ref[...] = _dot(mb, wv_ref[...])


def _memkv(mem2d, wk, wv, tm):
    rows, dm = mem2d.shape
    row = pl.BlockSpec((tm, dm), lambda i: (i, 0))
    out = jax.ShapeDtypeStruct((rows, wk.shape[1]), F32)
    vmem = 2 * 3 * tm * dm * 4 + 4 * (wk.size + wv.size) + 4 * tm * dm * 4
    return pl.pallas_call(
        _memkv_body,
        grid=(rows // tm,),
        in_specs=[row, _const_spec(wk.shape), _const_spec(wv.shape)],
        out_specs=[row, row],
        out_shape=[out, out],
        compiler_params=_params(("parallel",), vmem),
        name="mem_kv",
    )(mem2d, wk, wv)


def _memattn_body(x_ref, mk_ref, mv_ref, wq_ref, wo_ref, lng_ref, lnb_ref, o_ref, *, alpha):
    dm = x_ref.shape[1]
    hd = dm // N_MEM_HEADS
    heads = [slice(h * hd, (h + 1) * hd) for h in range(N_MEM_HEADS)]
    x = x_ref[...]
    q = (_dot(x.astype(BF16), wq_ref[...]) * (hd ** -0.5)).astype(BF16)
    scores = [_dot_nt(q[:, c], mk_ref[:, c].astype(BF16)) for c in heads]
    probs =[jnp.exp(s - jnp.max(s, axis=1, keepdims=True)) for s in scores]
    sums = [jnp.sum(p, axis=1, keepdims=True) for p in probs]
    outs = [_dot(p.astype(BF16), mv_ref[:, c].astype(BF16)) / l for p, l, c in zip(probs, sums, heads)]
    att = _dot(jnp.concatenate(outs, axis=1).astype(BF16), wo_ref[...])
    o_ref[...] = _layer_norm(alpha * x + att, lng_ref[...], lnb_ref[...])


def _memattn(x2d, mem_k, mem_v, wq, wo, ln_g, ln_b, alpha, tm):
    rows, dm = x2d.shape
    n_seq, n_mem, _ = mem_k.shape
    steps_per_mem = rows // n_seq // tm
    row = pl.BlockSpec((tm, dm), lambda i: (i, 0))
    mem_spec = pl.BlockSpec((None, n_mem, dm), lambda i: (i // steps_per_mem, 0, 0))
    consts = [wq, wo, ln_g.reshape(1, -1), ln_b.reshape(1, -1)]
    vmem = 2 * 2 * tm * dm * 4 + 2 * 2 * n_mem * dm * 4 + 2 * (wq.size + wo.size) + 12 * tm * dm * 4
    return pl.pallas_call(
        functools.partial(_memattn_body, alpha=alpha),
        grid=(rows // tm,),
        in_specs=[row, mem_spec, mem_spec] + [_const_spec(c.shape) for c in consts],
        out_specs=row,
        out_shape=jax.ShapeDtypeStruct(x2d.shape, F32),
        compiler_params=_params(("parallel",), vmem),
        name="mem_attn_ln2",
    )(x2d, mem_k, mem_v, *consts)


def _rows_matmul_body(x_ref, w_ref, o_ref, *, scale):
    o_ref[...] = _dot(x_ref[...].astype(BF16), w_ref[...]) * scale


def _rows_matmul(x2d, w, scale):
    rows, dm = x2d.shape
    vmem = 4 * rows * (dm + w.shape[1]) * 4 + 4 * w.size
    return pl.pallas_call(
        functools.partial(_rows_matmul_body, scale=scale),
        out_shape=jax.ShapeDtypeStruct((rows, w.shape[1]), F32),
        compiler_params=_params(None, vmem),
        name="rows_matmul",
    )(x2d, w)


def _memattn_cache_body(q_ref, mk_ref, mv_ref, o_ref):
    nb, rows, hd = q_ref.shape
    n_mem, nh = mk_ref.shape[1:3]
    cols = n_mem * nh
    head_ok = ((lax.broadcasted_iota(jnp.int32, (rows, cols), 0) & (nh - 1))
               == (lax.broadcasted_iota(jnp.int32, (rows, cols), 1) & (nh - 1)))
    scores = [jnp.where(head_ok, _dot_nt(q_ref[j].astype(BF16), mk_ref[j].reshape(cols, hd).astype(BF16)), NEG)
              for j in range(nb)]
    probs = [jnp.exp(s - jnp.max(s, axis=1, keepdims=True)) for s in scores]
    sums = [jnp.sum(p, axis=1, keepdims=True) for p in probs]
    for j in range(nb):
        o_ref[j] = _dot(probs[j].astype(BF16), mv_ref[j].reshape(cols, hd).astype(BF16)) / sums[j]


def _memattn_cache(q3, cache_k, cache_v, layer, nb):
    bsz, rows, hd = q3.shape
    _, _, n_mem, nh, _ = cache_k.shape
    q_spec = pl.BlockSpec((nb, rows, hd), lambda i: (i, 0, 0))
    c_spec = pl.BlockSpec((None, nb, n_mem, nh, hd), lambda i: (layer, i, 0, 0, 0))
    vmem = 2 * 2 * nb * n_mem * 8 * hd * 4 + 8 * nb * rows * n_mem * nh * 4 + 8 * 1024 * 1024
    return pl.pallas_call(
        _memattn_cache_body,
        grid=(bsz // nb,),
        in_specs=[q_spec, c_spec, c_spec],
        out_specs=q_spec,
        out_shape=jax.ShapeDtypeStruct(q3.shape, F32),
        compiler_params=_params(("parallel",), vmem),
        name="mem_attn_cache",
    )(q3, cache_k, cache_v)


def _proj_ln_body(x_ref, a_ref, w_ref, lng_ref, lnb_ref, o_ref, *, alpha):
    att = _dot(a_ref[...].astype(BF16), w_ref[...])
    o_ref[...] = _layer_norm(alpha * x_ref[...] + att, lng_ref[...], lnb_ref[...])


def _proj_ln(x2d, a2d, w, ln_g, ln_b, alpha):
    rows, dm = x2d.shape
    vmem = 8 * rows * dm * 4 + 4 * w.size
    return pl.pallas_call(
        functools.partial(_proj_ln_body, alpha=alpha),
        out_shape=jax.ShapeDtypeStruct(x2d.shape, F32),
        compiler_params=_params(None, vmem),
        name="proj_ln",
    )(x2d, a2d, w, ln_g.reshape(1, -1), ln_b.reshape(1, -1))


def _ffn_body(x_ref, wg_ref, wu_ref, wd_ref, lng_ref, lnb_ref, o_ref, acc_ref, *, alpha, tf):
    x = x_ref[...]
    xb = x.astype(BF16)
    for c in range(wg_ref.shape[1] // tf):
        cols = slice(c * tf, (c + 1) * tf)
        hid = (jax.nn.silu(_dot(xb, wg_ref[:, cols])) * _dot(xb, wu_ref[:, cols])).astype(BF16)
        part = _dot(hid, wd_ref[cols, :])
        if c == 0:
            acc_ref[...] = part
        else:
            acc_ref[...] += part
    o_ref[...] = _layer_norm(alpha * x + acc_ref[...], lng_ref[...], lnb_ref[...])


def _ffn(x2d, wg, wu, wd, ln_g, ln_b, alpha, tm, tf):
    rows, dm = x2d.shape
    d_ff = wg.shape[1]
    assert d_ff % tf == 0 and tf % LANES == 0 and rows % tm == 0
    row = pl.BlockSpec((tm, dm), lambda i: (i, 0))
    consts = [wg, wu, wd, ln_g.reshape(1, -1), ln_b.reshape(1, -1)]
    vmem = 2 * 2 * tm * dm * 4 + 2 * 3 * wg.size + tm * dm * 4 + 8 * tm * max(tf, dm) * 4
    return pl.pallas_call(
        functools.partial(_ffn_body, alpha=alpha, tf=tf),
        grid=(rows // tm,),
        in_specs=[row] + [_const_spec(c.shape) for c in consts],
        out_specs=row,
        out_shape=jax.ShapeDtypeStruct(x2d.shape, F32),
        scratch_shapes=[pltpu.VMEM((tm, dm), F32)],
        compiler_params=_params(("parallel",), vmem),
        name="swiglu_ln3",
    )(x2d, *consts)


PROMPT_CHUNK = 8
ROW_TILE = 512
FFN_ROW_TILE = 1024
FFN_COL_TILE = 256
SAMPLE_ATTN_SEQS = 2
SAMPLE_MEM_SEQS = 4


def kernel(x_prompt, x_sample, cache_win_k, cache_win_v, state_ssm_re, state_ssm_im, cache_mem_k, cache_mem_v, mem_prompt, w_in, g_att, g_ssm, ssm_a_re, ssm_a_im, ssm_log_dt, ssm_b_re, ssm_b_im, ssm_c_re, ssm_c_im, ssm_d, w_glu, b_glu, w_out, ln1_g, ln1_b, w_mem_q, w_mem_k, w_mem_v, w_mem_o, ln2_g, ln2_b, w_gate, w_up, w_down, ln3_g, ln3_b):
    depth = w_in.shape[0]
    bp, seq, dm = x_prompt.shape
    bs, t_new, _ = x_sample.shape
    n_groups, n_state = ssm_a_re.shape[1:]
    n_mem = mem_prompt.shape[1]
    alpha = (2 * depth) ** 0.25
    keep = min(DILATED_CFGS[-1][0], seq)
    assert keep == seq
    n_chunks = seq // PROMPT_CHUNK

    y_p = x_prompt.reshape(bp * seq, dm)
    y_s = x_sample.reshape(bs * t_new, dm)
    mem2d = mem_prompt.reshape(bp * n_mem, dm)
    outs = [[] for _ in range(10)]
    for l in range(depth):
        bf = lambda w: w[l].astype(BF16)
        w_in_l, w_glu_l, w_out_l = bf(w_in), bf(w_glu), bf(w_out)
        wq_l, wk_l, wv_l, wo_l = bf(w_mem_q), bf(w_mem_k), bf(w_mem_v), bf(w_mem_o)
        wg_l, wu_l, wd_l = bf(w_gate), bf(w_up), bf(w_down)
        prepped = _ssm_prep(ssm_a_re[l], ssm_a_im[l], ssm_log_dt[l], ssm_b_re[l], ssm_b_im[l],
                            max(PROMPT_CHUNK, t_new), PROMPT_CHUNK)
        layouts = _ssm_layouts(*prepped, ssm_c_re[l], ssm_c_im[l], ssm_d[l])
        mix_w = (w_glu_l, b_glu[l], g_att[l], g_ssm[l], w_out_l, ln1_g[l], ln1_b[l])

        q, k, v, u, k_t, v_t = _inproj(y_p, w_in_l, ROW_TILE, PROMPT_CHUNK, seq)
        d_att = q.shape[1]
        o_att = _attn_prompt(q.reshape(bp, seq, d_att), k.reshape(bp, seq, d_att), v.reshape(bp, seq, d_att))
        zeros = jnp.zeros((bp, 1, n_groups * n_state), F32)
        y_ssm, hr_p, hi_p = _ssm(u, zeros, zeros, layouts, PROMPT_CHUNK, n_chunks, True)
        x1 = _mix(y_p, o_att.reshape(bp * seq, d_att), y_ssm, *mix_w, alpha, ROW_TILE, PROMPT_CHUNK)
        mk_p, mv_p = _memkv(mem2d, wk_l, wv_l, ROW_TILE)
        x2 = _memattn(x1, mk_p.reshape(bp, n_mem, dm), mv_p.reshape(bp, n_mem, dm), wq_l, wo_l,
                      ln2_g[l], ln2_b[l], alpha, ROW_TILE)
        y_p = _ffn(x2, wg_l, wu_l, wd_l, ln3_g[l], ln3_b[l], alpha, FFN_ROW_TILE, FFN_COL_TILE)

        qs, ks, vs, us = _inproj(y_s, w_in_l, bs * t_new, t_new)
        shp = (bs, t_new, d_att)
        o_att_s = _attn_window(qs.reshape(shp), ks.reshape(shp), vs.reshape(shp), cache_win_k, cache_win_v, l,
                               SAMPLE_ATTN_SEQS)
        y_ssm_s, hr_s, hi_s = _ssm(us, state_ssm_re[l].reshape(1, bs, -1), state_ssm_im[l].reshape(1, bs, -1),
                                   layouts, t_new, bs, False)
        x1s = _mix(y_s, o_att_s.reshape(bs * t_new, d_att), y_ssm_s, *mix_w, alpha, bs * t_new, t_new)
        mem_hd = dm // N_MEM_HEADS
        q_mem = _rows_matmul(x1s, wq_l, mem_hd ** -0.5).reshape(bs, t_new * N_MEM_HEADS, mem_hd)
        a_mem = _memattn_cache(q_mem, cache_mem_k, cache_mem_v, l, SAMPLE_MEM_SEQS)
        x2s = _proj_ln(x1s, a_mem.reshape(bs * t_new, dm), wo_l, ln2_g[l], ln2_b[l], alpha)
        y_s = _ffn(x2s, wg_l, wu_l, wd_l, ln3_g[l], ln3_b[l], alpha, bs * t_new, FFN_COL_TILE)

        head_shape = (N_ATT_HEADS, ATT_HEAD_DIM)
        state_shape = (n_groups, n_state)
        mem_shape = (bp, n_mem, N_MEM_HEADS, dm // N_MEM_HEADS)
        rows_first = lambda t: jnp.transpose(t.reshape(bp, *head_shape, seq), (0, 3, 1, 2))
        for lst, val in zip(outs, (
                rows_first(k_t)[:, seq - keep:], rows_first(v_t)[:, seq - keep:],
                ks.reshape(bs, t_new, *head_shape), vs.reshape(bs, t_new, *head_shape),
                hr_p.reshape(bp, *state_shape), hi_p.reshape(bp, *state_shape),
                hr_s.reshape(bs, *state_shape), hi_s.reshape(bs, *state_shape),
                mk_p.reshape(mem_shape), mv_p.reshape(mem_shape))):
            lst.append(val)
    return (y_p.reshape(bp, seq, dm), y_s.reshape(bs, t_new, dm)) + tuple(jnp.stack(o) for o in outs)
```

```python
import functools
import math

import jax
import jax.numpy as jnp
from jax import lax
from jax.experimental import pallas as pl
from jax.experimental.pallas import tpu as pltpu

F32 = jnp.float32
BF16 = jnp.bfloat16

N_ATT_HEADS = 8
ATT_HEAD_DIM = 64
D_ATT = N_ATT_HEADS * ATT_HEAD_DIM
DILATED_CFGS = ((128, 1), (512, 4), (2048, 16))
ATT_BLK = 128
ATT_GROUP = 4
SSM_CH = 16
SSM_STATE = 64
N_MEM_HEADS = 4
EPS = 1e-5
NEG = -1e30

LANES = 128
V7X_VMEM_CAP_BYTES = 56 * 1024 * 1024

SSM_GB = LANES // SSM_CH
SSM_ROW_TILE = 32
SSM_CHAIN_GROUP = 8


def _params(sem, vmem_bytes):
    return pltpu.CompilerParams(
        dimension_semantics=sem,
        vmem_limit_bytes=int(min(max(vmem_bytes, 16 * 1024 * 1024), V7X_VMEM_CAP_BYTES)),
    )


def _dot(a, b):
    return jnp.dot(a, b, preferred_element_type=F32)


def _dot_nt(a, b):
    return lax.dot_general(a, b, (((1,), (1,)), ((), ())), preferred_element_type=F32)


def _layer_norm(x, g, b):
    mu = jnp.mean(x, axis=-1, keepdims=True)
    xc = x - mu
    var = jnp.mean(xc * xc, axis=-1, keepdims=True)
    return xc * lax.rsqrt(var + EPS) * g + b


def _rms_norm(x, g):
    return x * lax.rsqrt(jnp.mean(x * x, axis=-1, keepdims=True) + EPS) * g


def _const_spec(shape):
    n = len(shape)
    return pl.BlockSpec(shape, lambda *_: (0,) * n, pipeline_mode=pl.Buffered(1))


def _inproj_body(x_ref, w_ref, q_ref, k_ref, v_ref, uc_ref, *rest, t1, transposed):
    u_scr = rest[-1]
    xb = x_ref[...].astype(BF16)
    d = q_ref.shape[1]
    q_ref[...] = _dot(xb, w_ref[:, 0 * d:1 * d])
    k = _dot(xb, w_ref[:, 1 * d:2 * d])
    v = _dot(xb, w_ref[:, 2 * d:3 * d])
    k_ref[...] = k
    v_ref[...] = v
    if transposed:
        kt_ref, vt_ref = rest[:2]
        kt_ref[...] = k.T
        vt_ref[...] = v.T
    u = _dot(xb, w_ref[:, 3 * d:4 * d])
    chunks = u_scr.shape[1] // t1
    for c in range(d // LANES):
        u_scr[c] = u[:, c * LANES:(c + 1) * LANES]
        for i in range(t1):
            uc_ref[:, i * d + c * LANES:i * d + (c + 1) * LANES] = u_scr[c, pl.ds(i, chunks, stride=t1), :]


def _inproj(x2d, w_in_bf16, tm, t1, seq=None):
    rows, dm = x2d.shape
    d = w_in_bf16.shape[1] // 4
    out = jax.ShapeDtypeStruct((rows, d), F32)
    row_spec = pl.BlockSpec((tm, d), lambda i: (i, 0))
    out_specs = [row_spec] * 3 + [pl.BlockSpec((tm // t1, t1 * d), lambda i: (i, 0))]
    out_shape = [out] * 3 + [jax.ShapeDtypeStruct((rows // t1, t1 * d), F32)]
    if seq is not None:
        steps = seq // tm
        out_specs += [pl.BlockSpec((None, d, tm), lambda i: (i // steps, 0, i % steps))] * 2
        out_shape += [jax.ShapeDtypeStruct((rows // seq, d, seq), F32)] * 2
    vmem = 2 * (tm * dm * 4 + 6 * tm * d * 4) + 2 * w_in_bf16.size * 2 + 10 * tm * d * 4
    return pl.pallas_call(
        functools.partial(_inproj_body, t1=t1, transposed=seq is not None),
        grid=(rows // tm,),
        in_specs=[pl.BlockSpec((tm, dm), lambda i: (i, 0)), _const_spec(w_in_bf16.shape)],
        out_specs=out_specs,
        out_shape=out_shape,
        scratch_shapes=[pltpu.VMEM((d // LANES, tm, LANES), F32)],
        compiler_params=_params(("parallel",), vmem),
        name="inproj",
    )(x2d, w_in_bf16)


def _attn_prompt_body(q_ref, k_ref, v_ref, o_ref, acc_ref, m_ref, l_ref):
    seq = q_ref.shape[0]
    blk = ATT_BLK
    scale = ATT_HEAD_DIM ** -0.5 * math.log2(math.e)
    lane = lax.broadcasted_iota(jnp.int32, (blk, LANES), 1)
    head0 = lane < ATT_HEAD_DIM
    qi = lax.broadcasted_iota(jnp.int32, (blk, blk), 0)
    kj = lax.broadcasted_iota(jnp.int32, (blk, blk), 1)
    tri = kj <= qi
    mask_rest = jnp.concatenate([kj >= qi, tri], axis=1)

    def rows_of(start, d):
        return pl.ds(start, blk) if d == 1 else pl.ds(start, blk, stride=d)

    def load_block(d, q_start, prev_start):
        rows = rows_of(q_start, d)
        q = q_ref[rows, :] * scale
        k2 = k_ref[rows, :].astype(BF16)
        v2 = v_ref[rows, :].astype(BF16)
        mask = tri
        if prev_start is not None:
            prow = rows_of(prev_start, d)
            k2 = jnp.concatenate([k_ref[prow, :].astype(BF16), k2], axis=0)
            v2 = jnp.concatenate([v_ref[prow, :].astype(BF16), v2], axis=0)
            mask = mask_rest
        qh = [jnp.where(head0 if h == 0 else jnp.logical_not(head0), q, 0.0).astype(BF16) for h in range(2)]
        return rows, qh, k2, v2, mask

    def attend_group(cfg, d, starts):
        blocks = [load_block(d, q_start, prev_start) for q_start, prev_start in starts]
        scores = [[jnp.where(mask, _dot_nt(qh[h], k2), NEG) for h in range(2)] for _, qh, k2, _, mask in blocks]
        maxes = [[jnp.max(s, axis=1, keepdims=True) for s in sb] for sb in scores]
        probs = [[jnp.exp2(s - m) for s, m in zip(sb, mb)] for sb, mb in zip(scores, maxes)]
        sums = [[jnp.sum(p, axis=1, keepdims=True) for p in pb] for pb in probs]
        pvs = [[_dot(p.astype(BF16), blk_[3]) for p in pb] for pb, blk_ in zip(probs, blocks)]
        new = [(blk_[0], jnp.where(head0, mb[0], mb[1]), jnp.where(head0, lb[0], lb[1]),
                jnp.where(head0, ob[0], ob[1])) for blk_, mb, lb, ob in zip(blocks, maxes, sums, pvs)]
        if cfg == 0:
            for rows, m, l, num in new:
                acc_ref[rows, :] = num
                m_ref[rows, :] = m
                l_ref[rows, :] = l
            return
        old = [(m_ref[rows, :], l_ref[rows, :], acc_ref[rows, :]) for rows, _, _, _ in new]
        for (rows, m, l, num), (m_old, l_old, acc_old) in zip(new, old):
            m_new = jnp.maximum(m_old, m)
            a = jnp.exp2(m_old - m_new)
            b = jnp.exp2(m - m_new)
            num = a * acc_old + b * num
            l = a * l_old + b * l
            if cfg == len(DILATED_CFGS) - 1:
                o_ref[rows, :] = num / l
            else:
                acc_ref[rows, :] = num
                m_ref[rows, :] = m_new
                l_ref[rows, :] = l

    grp = ATT_GROUP
    for cfg, (window, d) in enumerate(DILATED_CFGS):
        span = d * blk
        nb = seq // span
        if d >= grp:
            def residue_group(g, carry, cfg=cfg, d=d, span=span, nb=nb):
                res = [g * grp + j for j in range(grp)]
                attend_group(cfg, d, [(r, None) for r in res])
                if nb > 1:
                    def later(n, c):
                        attend_group(cfg, d, [(r + n * span, r + (n - 1) * span) for r in res])
                        return c
                    lax.fori_loop(1, nb, later, 0)
                return carry

            lax.fori_loop(0, d // grp, residue_group, 0)
        else:
            assert d == 1 and nb % grp == 0
            attend_group(cfg, d, [(0, None)] + [(n * span, (n - 1) * span) for n in range(1, grp)])

            def block_group(g, carry, cfg=cfg, d=d, span=span):
                attend_group(cfg, d, [((g * grp + j) * span, (g * grp + j - 1) * span) for j in range(grp)])
                return carry

            lax.fori_loop(1, nb // grp, block_group, 0)


def _attn_prompt(q, k, v, seq):
    rows, d_att = q.shape
    bsz = rows // seq
    assert d_att % LANES == 0 and LANES == 2 * ATT_HEAD_DIM
    for window, d in DILATED_CFGS:
        assert window // d == ATT_BLK and seq % (d * ATT_BLK) == 0
    spec = pl.BlockSpec((seq, LANES), lambda b, h: (b, h))
    blk_bytes = seq * LANES * 4
    return pl.pallas_call(
        _attn_prompt_body,
        grid=(bsz, d_att // LANES),
        in_specs=[spec, spec, spec],
        out_specs=spec,
        out_shape=jax.ShapeDtypeStruct(q.shape, F32),
        scratch_shapes=[pltpu.VMEM((seq, LANES), F32)] * 3,
        compiler_params=_params(("parallel", "parallel"), 11 * blk_bytes + 8 * 1024 * 1024),
        name="attn_prompt",
    )(q, k, v)


def _window_head_mask(d_att):
    nh = N_ATT_HEADS
    return (jnp.right_shift(lax.broadcasted_iota(jnp.int32, (nh, d_att), 1), ATT_HEAD_DIM.bit_length() - 1)
            == lax.broadcasted_iota(jnp.int32, (nh, d_att), 0))


def _window_probs(q, kn, kt):
    t_new, d_att = q.shape
    w_buf = kt.shape[1]
    nh = N_ATT_HEADS
    rows = t_new * nh
    (win1, _), (win4, dil4), (_, dil16) = DILATED_CFGS
    tail, near = win4, win1
    head_mask = _window_head_mask(d_att)
    trow = jnp.right_shift(lax.broadcasted_iota(jnp.int32, (rows, 1), 0), nh.bit_length() - 1)
    lane = lambda n: lax.broadcasted_iota(jnp.int32, (rows, n), 1)
    mask16 = (lane(w_buf) & (dil16 - 1)) == trow
    mask4 = (lane(tail) & (dil4 - 1)) == trow
    mask1 = lane(near) >= trow
    new_self = lane(t_new) == trow
    new_causal = lane(t_new) <= trow

    q = q * (ATT_HEAD_DIM ** -0.5)
    qbd = jnp.concatenate(
        [jnp.where(head_mask, jnp.broadcast_to(q[t:t + 1], (nh, d_att)), 0.0) for t in range(t_new)],
        axis=0).astype(BF16)
    s_all = _dot(qbd, kt.astype(BF16))
    s_new = _dot_nt(qbd, kn.astype(BF16))

    def softmax_parts(main, main_mask, new_mask):
        sm = jnp.where(main_mask, main, NEG)
        sn = jnp.where(new_mask, s_new, NEG)
        m = jnp.maximum(jnp.max(sm, axis=1, keepdims=True), jnp.max(sn, axis=1, keepdims=True))
        p = jnp.exp(sm - m)
        pn = jnp.exp(sn - m)
        den = jnp.sum(p, axis=1, keepdims=True) + jnp.sum(pn, axis=1, keepdims=True)
        return m, p, pn, den

    m16, p16, pn16, den16 = softmax_parts(s_all, mask16, new_self)
    m4, p4, pn4, den4 = softmax_parts(s_all[:, w_buf - tail:], mask4, new_self)
    m1, p1, pn1, den1 = softmax_parts(s_all[:, w_buf - near:], mask1, new_causal)
    m = jnp.maximum(jnp.maximum(m1, m4), m16)
    w1, w4, w16 = jnp.exp(m1 - m), jnp.exp(m4 - m), jnp.exp(m16 - m)
    den = w1 * den1 + w4 * den4 + w16 * den16
    p16, p4, p1 = w16 * p16, w4 * p4, w1 * p1
    p_all = jnp.concatenate(
        [p16[:, :w_buf - tail],
         p16[:, w_buf - tail:w_buf - near] + p4[:, :tail - near],
         p16[:, w_buf - near:] + p4[:, tail - near:] + p1], axis=1).astype(BF16)
    pn_all = (w16 * pn16 + w4 * pn4 + w1 * pn1).astype(BF16)
    return p_all, pn_all, den


def _window_output(p_all, pn_all, den, vn, vt):
    t_new, d_att = vn.shape
    nh = N_ATT_HEADS
    head_mask = _window_head_mask(d_att)
    out = (_dot_nt(p_all, vt.astype(BF16)) + _dot(pn_all, vn.astype(BF16))) / den
    return jnp.concatenate(
        [jnp.sum(jnp.where(head_mask, out[t * nh:(t + 1) * nh], 0.0), axis=0, keepdims=True)
         for t in range(t_new)], axis=0)


def _attn_window_body(q_ref, kn_ref, vn_ref, kt_ref, vt_ref, o_ref):
    nb = kt_ref.shape[0]
    t_new = q_ref.shape[0] // nb
    tok = [slice(b * t_new, (b + 1) * t_new) for b in range(nb)]
    probs = [_window_probs(q_ref[tok[b], :], kn_ref[tok[b], :], kt_ref[b]) for b in range(nb)]
    for b in range(nb):
        o_ref[tok[b], :] = _window_output(*probs[b], vn_ref[tok[b], :], vt_ref[b])


def _attn_window(q, k_new, v_new, cache_k, cache_v, layer, nb):
    depth, bsz, w_buf, nh, hd = cache_k.shape
    t_new, d_att = q.shape[0] // bsz, q.shape[1]
    assert w_buf == DILATED_CFGS[-1][0] and t_new <= DILATED_CFGS[1][1] and nh * hd == d_att
    assert (nb * t_new) % 8 == 0
    rows_last = lambda c: jnp.transpose(c, (0, 1, 3, 4, 2)).reshape(depth, bsz, d_att, w_buf)
    new_spec = pl.BlockSpec((nb * t_new, d_att), lambda i: (i, 0))
    cache_spec = pl.BlockSpec((None, nb, d_att, w_buf), lambda i: (layer, i, 0, 0))
    vmem = 2 * 2 * nb * d_att * w_buf * 4 + 4 * nb * d_att * w_buf * 2 + 12 * 1024 * 1024
    return pl.pallas_call(
        _attn_window_body,
        grid=(bsz // nb,),
        in_specs=[new_spec, new_spec, new_spec, cache_spec, cache_spec],
        out_specs=new_spec,
        out_shape=jax.ShapeDtypeStruct(q.shape, F32),
        compiler_params=_params(("parallel",), vmem),
        name="attn_window",
    )(q, k_new, v_new, rows_last(cache_k), rows_last(cache_v))


def _ssm_prep_body(a_re_ref, a_im_ref, log_dt_ref, b_re_ref, b_im_ref, lam_re_ref, lam_im_ref,
                   bbar_re_ref, bbar_im_ref, step_re_ref, step_im_ref, *, chunk):
    ar = a_re_ref[...]
    ai = a_im_ref[...]
    dt = jnp.exp(log_dt_ref[...])
    mag = jnp.exp(dt * ar)
    lr = mag * jnp.cos(dt * ai)
    li = mag * jnp.sin(dt * ai)
    den = ar * ar + ai * ai
    nr, ni = lr - 1.0, li
    cr = (nr * ar + ni * ai) / den
    ci = (ni * ar - nr * ai) / den
    for h in range(b_re_ref.shape[0]):
        br, bi = b_re_ref[h], b_im_ref[h]
        bbar_re_ref[h] = cr * br - ci * bi
        bbar_im_ref[h] = cr * bi + ci * br
    pr, pi = lr, li
    for j in range(lam_re_ref.shape[0]):
        lam_re_ref[j] = pr
        lam_im_ref[j] = pi
        if j == chunk - 1:
            cr, ci = pr, pi
        pr, pi = pr * lr - pi * li, pr * li + pi * lr
    pr, pi = cr, ci
    for j in range(step_re_ref.shape[0]):
        step_re_ref[j] = pr
        step_im_ref[j] = pi
        pr, pi = pr * cr - pi * ci, pr * ci + pi * cr


def _ssm_prep(a_re, a_im, log_dt, b_re, b_im, n_pow, chunk):
    g, p = a_re.shape
    h = b_re.shape[-1]
    assert chunk <= n_pow
    b_re_t = jnp.transpose(b_re, (2, 0, 1))
    b_im_t = jnp.transpose(b_im, (2, 0, 1))
    gp = jax.ShapeDtypeStruct((n_pow, g, p), F32)
    hgp = jax.ShapeDtypeStruct((h, g, p), F32)
    sgp = jax.ShapeDtypeStruct((SSM_CHAIN_GROUP, g, p), F32)
    return pl.pallas_call(
        functools.partial(_ssm_prep_body, chunk=chunk),
        out_shape=[gp, gp, hgp, hgp, sgp, sgp],
        name="ssm_prep",
    )(a_re, a_im, log_dt.reshape(g, 1), b_re_t, b_im_t)


def _ssm_layouts(lam_re, lam_im, bbar_re_t, bbar_im_t, step_re, step_im, c_re, c_im, d_skip):
    n_pow, g, p = lam_re.shape
    h = bbar_re_t.shape[0]
    nq = g // SSM_GB
    eye = jnp.eye(SSM_GB, dtype=bool)
    bb = jnp.stack([bbar_re_t, bbar_im_t], 0).reshape(2, h, nq, SSM_GB, p)
    bb = jnp.transpose(bb, (2, 3, 1, 0, 4))
    bmat = jnp.where(eye[None, :, None, None, :, None], bb[:, :, :, :, None, :], 0.0)
    bmat = bmat.reshape(nq, SSM_GB * h, 2 * SSM_GB * p).astype(BF16)
    def c_layout(c):
        cc = jnp.transpose(c.reshape(nq, SSM_GB, h, p), (0, 1, 3, 2))
        m = jnp.where(eye[None, :, None, :, None], cc[:, :, :, None, :], 0.0)
        return m.reshape(nq, SSM_GB * p, SSM_GB * h).astype(BF16)
    per_block = lambda t: t.reshape(t.shape[0], nq, 1, SSM_GB * p)
    d_q = d_skip.reshape(nq, 1, SSM_GB * h)
    return (bmat, c_layout(c_re), c_layout(c_im), per_block(lam_re), per_block(lam_im),
            per_block(step_re), per_block(step_im), d_q)


def _ssm_body(u_ref, h0r_ref, h0i_ref, bmat_ref, cre_ref, cim_ref, lamr_ref, lami_ref, stepr_ref, stepi_ref, d_ref,
              y_ref, hfr_ref, hfi_ref, hloc_ref, hs_ref, ends_ref, grp_ref, *, t1, chained):
    rows = u_ref.shape[0]
    d_ssm = y_ref.shape[1] // t1
    nq = bmat_ref.shape[0]
    ns = lamr_ref.shape[-1]
    rt = SSM_ROW_TILE
    re, im = slice(0, ns), slice(ns, 2 * ns)

    def u_lanes(i, q):
        return slice(i * d_ssm + q * LANES, i * d_ssm + (q + 1) * LANES)

    def row_tiles(body):
        lax.fori_loop(0, rows // rt, lambda t, c: (body(pl.ds(pl.multiple_of(t * rt, rt), rt)), c)[1], 0,
                      unroll=True)

    def input_drive(q):
        for i in range(t1):
            hloc_ref[q % 2, i] = _dot(u_ref[:, u_lanes(i, q)].astype(BF16), bmat_ref[q])

    def local_scan(q):
        h = hloc_ref.at[q % 2]
        lr, li = lamr_ref[0, q], lami_ref[0, q]

        def tile(r):
            hr, hi = h[0, r, re], h[0, r, im]
            for i in range(1, t1):
                hr, hi = lr * hr - li * hi + h[i, r, re], lr * hi + li * hr + h[i, r, im]
                h[i, r, re] = hr
                h[i, r, im] = hi
        row_tiles(tile)

    nlb = ns // LANES

    def put_blocks(ref, rows_idx, vr, vi):
        for c in range(nlb):
            ref[c, rows_idx, :] = vr[:, c * LANES:(c + 1) * LANES]
            ref[nlb + c, rows_idx, :] = vi[:, c * LANES:(c + 1) * LANES]

    def get_blocks(ref, rows_idx):
        return (jnp.concatenate([ref[c, rows_idx, :] for c in range(nlb)], axis=1),
                jnp.concatenate([ref[nlb + c, rows_idx, :] for c in range(nlb)], axis=1))

    def chunk_starts(q):
        h = hloc_ref.at[q % 2]
        s_lanes = slice(q * ns, (q + 1) * ns)
        if not chained:
            put_blocks(hs_ref, slice(None), h0r_ref[:, s_lanes], h0i_ref[:, s_lanes])
            return
        grp = SSM_CHAIN_GROUP
        n_grp = rows // grp
        put_blocks(ends_ref, slice(None), h[t1 - 1, :, re], h[t1 - 1, :, im])
        member = lambda j: pl.ds(j, n_grp, stride=grp)
        cr, ci = stepr_ref[0, q], stepi_ref[0, q]
        gr, gi = get_blocks(ends_ref, member(0))
        inside = [(gr, gi)]
        for j in range(1, grp):
            er, ei = get_blocks(ends_ref, member(j))
            gr, gi = cr * gr - ci * gi + er, cr * gi + ci * gr + ei
            inside.append((gr, gi))
        grp_ref[0, :, re] = gr
        grp_ref[0, :, im] = gi
        wr, wi = stepr_ref[grp - 1, q], stepi_ref[grp - 1, q]

        def chain(c, g):
            sr, si = g
            row = pl.ds(c, 1)
            grp_ref[1, row, re] = sr
            grp_ref[1, row, im] = si
            return wr * sr - wi * si + grp_ref[0, row, re], wr * si + wi * sr + grp_ref[0, row, im]

        sr, si = lax.fori_loop(0, n_grp, chain, (h0r_ref[:, s_lanes], h0i_ref[:, s_lanes]))
        hfr_ref[:, s_lanes] = sr
        hfi_ref[:, s_lanes] = si
        sr, si = grp_ref[1, :, re], grp_ref[1, :, im]
        put_blocks(hs_ref, member(0), sr, si)
        for j in range(1, grp):
            pr, pi = stepr_ref[j - 1, q], stepi_ref[j - 1, q]
            gr, gi = inside[j - 1]
            put_blocks(hs_ref, member(j), gr + (pr * sr - pi * si), gi + (pr * si + pi * sr))

    def add_carry(q):
        h = hloc_ref.at[q % 2]

        def tile(r):
            hr0, hi0 = get_blocks(hs_ref, r)
            for i in range(t1):
                pr, pi = lamr_ref[i, q], lami_ref[i, q]
                h[i, r, re] = h[i, r, re] + (pr * hr0 - pi * hi0)
                h[i, r, im] = h[i, r, im] + (pr * hi0 + pi * hr0)
        row_tiles(tile)

    def readout(q):
        h = hloc_ref.at[q % 2]
        if not chained:
            s_lanes = slice(q * ns, (q + 1) * ns)
            hfr_ref[:, s_lanes] = h[t1 - 1, :, re]
            hfi_ref[:, s_lanes] = h[t1 - 1, :, im]
        for i in range(t1):
            y_ref[:, u_lanes(i, q)] = (_dot(h[i, :, re].astype(BF16), cre_ref[q])
                                       - _dot(h[i, :, im].astype(BF16), cim_ref[q])
                                       + d_ref[q] * u_ref[:, u_lanes(i, q)])

    input_drive(0)
    local_scan(0)
    chunk_starts(0)
    for q in range(nq):
        if q + 1 < nq:
            input_drive(q + 1)
        add_carry(q)
        readout(q)
        if q + 1 < nq:
            local_scan(q + 1)
            chunk_starts(q + 1)


def _ssm(u_chunks, h0_re, h0_im, layouts, t1, rows, chained):
    bmat, cre, cim, lam_re_q, lam_im_q, step_re_q, step_im_q, d_q = layouts
    lam_re_q, lam_im_q = lam_re_q[:t1], lam_im_q[:t1]
    n_blocks, rows_h, n_state = h0_re.shape
    width = u_chunks.shape[1]
    assert rows % SSM_ROW_TILE == 0 and u_chunks.shape[0] == n_blocks * rows
    assert not chained or rows % SSM_CHAIN_GROUP == 0
    ns2 = bmat.shape[2]
    u_spec = pl.BlockSpec((rows, width), lambda b: (b, 0))
    h_spec = pl.BlockSpec((None, rows_h, n_state), lambda b: (b, 0, 0))
    consts = [bmat, cre, cim, lam_re_q, lam_im_q, step_re_q, step_im_q, d_q]
    chain_rows = rows if chained else 8
    blocked = lambda r: pltpu.VMEM((ns2 // LANES, r, LANES), F32)
    vmem = (4 * rows * width * 4 + (2 * t1 + 1) * rows * ns2 * 4 + sum(c.size * c.dtype.itemsize for c in consts)
            + 8 * rows_h * n_state * 4 + 8 * 1024 * 1024)
    h_out = jax.ShapeDtypeStruct(h0_re.shape, F32)
    return pl.pallas_call(
        functools.partial(_ssm_body, t1=t1, chained=chained),
        grid=(n_blocks,),
        in_specs=[u_spec, h_spec, h_spec] + [_const_spec(c.shape) for c in consts],
        out_specs=[u_spec, h_spec, h_spec],
        out_shape=[jax.ShapeDtypeStruct(u_chunks.shape, F32), h_out, h_out],
        scratch_shapes=[pltpu.VMEM((2, t1, rows, ns2), F32), blocked(rows), blocked(chain_rows),
                        pltpu.VMEM((2, chain_rows // SSM_CHAIN_GROUP, ns2), F32)],
        compiler_params=_params(("parallel",), vmem),
        name="ssm_chained" if chained else "ssm_rows",
    )(u_chunks, h0_re, h0_im, *consts)


def _mix_body(x_ref, oatt_ref, yc_ref, wglu_ref, bglu_ref, gatt_ref, gssm_ref, wout_ref, lng_ref, lnb_ref,
              o_ref, y_scr, *, alpha, t1):
    d_att = oatt_ref.shape[1]
    n_lane_blocks, rows, _ = y_scr.shape
    d_ssm = n_lane_blocks * LANES
    chunks = rows // t1
    for c in range(n_lane_blocks):
        for i in range(t1):
            y_scr[c, pl.ds(i, chunks, stride=t1), :] = yc_ref[:, i * d_ssm + c * LANES:i * d_ssm + (c + 1) * LANES]
    g = jax.nn.gelu(jnp.concatenate([y_scr[c] for c in range(n_lane_blocks)], axis=1))
    z = g * jax.nn.sigmoid(_dot(g.astype(BF16), wglu_ref[...]) + bglu_ref[...])
    ra = _rms_norm(oatt_ref[...], gatt_ref[...]).astype(BF16)
    rz = _rms_norm(z, gssm_ref[...]).astype(BF16)
    mixed = _dot(ra, wout_ref[0:d_att, :]) + _dot(rz, wout_ref[d_att:, :])
    o_ref[...] = _layer_norm(alpha * x_ref[...] + mixed, lng_ref[...], lnb_ref[...])


def _mix(x2d, o_att, y_chunks, w_glu, b_glu, g_att, g_ssm, w_out, ln_g, ln_b, alpha, tm, t1):
    rows, dm = x2d.shape
    d_att, d_ssm = o_att.shape[1], y_chunks.shape[1] // t1
    consts = [w_glu, b_glu.reshape(1, -1), g_att.reshape(1, -1), g_ssm.reshape(1, -1), w_out,
              ln_g.reshape(1, -1), ln_b.reshape(1, -1)]
    row = lambda width: pl.BlockSpec((tm, width), lambda i: (i, 0))
    chunk_spec = pl.BlockSpec((tm // t1, t1 * d_ssm), lambda i: (i, 0))
    vmem = 2 * tm * (2 * dm + d_att + d_ssm) * 4 + 4 * (w_glu.size + w_out.size) + 12 * tm * dm * 4
    return pl.pallas_call(
        functools.partial(_mix_body, alpha=alpha, t1=t1),
        grid=(rows // tm,),
        in_specs=[row(dm), row(d_att), chunk_spec] + [_const_spec(c.shape) for c in consts],
        out_specs=row(dm),
        out_shape=jax.ShapeDtypeStruct(x2d.shape, F32),
        scratch_shapes=[pltpu.VMEM((d_ssm // LANES, tm, LANES), F32)],
        compiler_params=_params(("parallel",), vmem),
        name="mix_out_ln1",
    )(x2d, o_att, y_chunks, *consts)


def _memkv_body(m_ref, wk_ref, wv_ref, k_ref, v_ref, kh_ref, vh_ref):
    mb = m_ref[...].astype(BF16)
    hd = kh_ref.shape[2]
    for w_ref, o_ref, oh_ref in ((wk_ref, k_ref, kh_ref), (wv_ref, v_ref, vh_ref)):
        val = _dot(mb, w_ref[...])
        o_ref[...] = val
        for h in range(oh_ref.shape[1]):
            oh_ref[:, h, :] = val[:, h * hd:(h + 1) * hd]


def _memkv(mem2d, wk, wv, tm):
    rows, dm = mem2d.shape
    hd = dm // N_MEM_HEADS
    row = pl.BlockSpec((tm, dm), lambda i: (i, 0))
    row_h = pl.BlockSpec((tm, N_MEM_HEADS, hd), lambda i: (i, 0, 0))
    out = jax.ShapeDtypeStruct((rows, wk.shape[1]), F32)
    out_h = jax.ShapeDtypeStruct((rows, N_MEM_HEADS, hd), F32)
    vmem = 2 * 5 * tm * dm * 4 + 2 * (wk.size + wv.size) + 4 * tm * dm * 4
    return pl.pallas_call(
        _memkv_body,
        grid=(rows // tm,),
        in_specs=[row, _const_spec(wk.shape), _const_spec(wv.shape)],
        out_specs=[row, row, row_h, row_h],
        out_shape=[out, out, out_h, out_h],
        compiler_params=_params(("parallel",), vmem),
        name="mem_kv",
    )(mem2d, wk, wv)


def _memattn_body(x_ref, mk_ref, mv_ref, wq_ref, wo_ref, lng_ref, lnb_ref, o_ref, *, alpha):
    dm = x_ref.shape[1]
    hd = dm // N_MEM_HEADS
    heads = [slice(h * hd, (h + 1) * hd) for h in range(N_MEM_HEADS)]
    x = x_ref[...]
    q = (_dot(x.astype(BF16), wq_ref[...]) * (hd ** -0.5)).astype(BF16)
    scores = [_dot_nt(q[:, c], mk_ref[:, c].astype(BF16)) for c in heads]
    probs =[jnp.exp(s - jnp.max(s, axis=1, keepdims=True)) for s in scores]
    sums = [jnp.sum(p, axis=1, keepdims=True) for p in probs]
    outs = [_dot(p.astype(BF16), mv_ref[:, c].astype(BF16)) / l for p, l, c in zip(probs, sums, heads)]
    att = _dot(jnp.concatenate(outs, axis=1).astype(BF16), wo_ref[...])
    o_ref[...] = _layer_norm(alpha * x + att, lng_ref[...], lnb_ref[...])


def _memattn(x2d, mem_k, mem_v, wq, wo, ln_g, ln_b, alpha, tm):
    rows, dm = x2d.shape
    n_seq, n_mem, _ = mem_k.shape
    steps_per_mem = rows // n_seq // tm
    row = pl.BlockSpec((tm, dm), lambda i: (i, 0))
    mem_spec = pl.BlockSpec((None, n_mem, dm), lambda i: (i // steps_per_mem, 0, 0))
    consts = [wq, wo, ln_g.reshape(1, -1), ln_b.reshape(1, -1)]
    vmem = 2 * 2 * tm * dm * 4 + 2 * 2 * n_mem * dm * 4 + 2 * (wq.size + wo.size) + 12 * tm * dm * 4
    return pl.pallas_call(
        functools.partial(_memattn_body, alpha=alpha),
        grid=(rows // tm,),
        in_specs=[row, mem_spec, mem_spec] + [_const_spec(c.shape) for c in consts],
        out_specs=row,
        out_shape=jax.ShapeDtypeStruct(x2d.shape, F32),
        compiler_params=_params(("parallel",), vmem),
        name="mem_attn_ln2",
    )(x2d, mem_k, mem_v, *consts)


def _rows_matmul_body(x_ref, w_ref, o_ref, *, scale):
    o_ref[...] = _dot(x_ref[...].astype(BF16), w_ref[...]) * scale


def _rows_matmul(x2d, w, scale):
    rows, dm = x2d.shape
    vmem = 4 * rows * (dm + w.shape[1]) * 4 + 4 * w.size
    return pl.pallas_call(
        functools.partial(_rows_matmul_body, scale=scale),
        out_shape=jax.ShapeDtypeStruct((rows, w.shape[1]), F32),
        compiler_params=_params(None, vmem),
        name="rows_matmul",
    )(x2d, w)


def _memattn_cache_body(q_ref, mk_ref, mv_ref, o_ref):
    nb, rows, hd = q_ref.shape
    n_mem, nh = mk_ref.shape[1:3]
    cols = n_mem * nh
    head_ok = ((lax.broadcasted_iota(jnp.int32, (rows, cols), 0) & (nh - 1))
               == (lax.broadcasted_iota(jnp.int32, (rows, cols), 1) & (nh - 1)))
    scores = [jnp.where(head_ok, _dot_nt(q_ref[j].astype(BF16), mk_ref[j].reshape(cols, hd).astype(BF16)), NEG)
              for j in range(nb)]
    probs = [jnp.exp(s - jnp.max(s, axis=1, keepdims=True)) for s in scores]
    sums = [jnp.sum(p, axis=1, keepdims=True) for p in probs]
    for j in range(nb):
        o_ref[j] = _dot(probs[j].astype(BF16), mv_ref[j].reshape(cols, hd).astype(BF16)) / sums[j]


def _memattn_cache(q3, cache_k, cache_v, layer, nb):
    bsz, rows, hd = q3.shape
    _, _, n_mem, nh, _ = cache_k.shape
    q_spec = pl.BlockSpec((nb, rows, hd), lambda i: (i, 0, 0))
    c_spec = pl.BlockSpec((None, nb, n_mem, nh, hd), lambda i: (layer, i, 0, 0, 0))
    vmem = 2 * 2 * nb * n_mem * 8 * hd * 4 + 8 * nb * rows * n_mem * nh * 4 + 8 * 1024 * 1024
    return pl.pallas_call(
        _memattn_cache_body,
        grid=(bsz // nb,),
        in_specs=[q_spec, c_spec, c_spec],
        out_specs=q_spec,
        out_shape=jax.ShapeDtypeStruct(q3.shape, F32),
        compiler_params=_params(("parallel",), vmem),
        name="mem_attn_cache",
    )(q3, cache_k, cache_v)


def _proj_ln_body(x_ref, a_ref, w_ref, lng_ref, lnb_ref, o_ref, *, alpha):
    att = _dot(a_ref[...].astype(BF16), w_ref[...])
    o_ref[...] = _layer_norm(alpha * x_ref[...] + att, lng_ref[...], lnb_ref[...])


def _proj_ln(x2d, a2d, w, ln_g, ln_b, alpha):
    rows, dm = x2d.shape
    vmem = 8 * rows * dm * 4 + 4 * w.size
    return pl.pallas_call(
        functools.partial(_proj_ln_body, alpha=alpha),
        out_shape=jax.ShapeDtypeStruct(x2d.shape, F32),
        compiler_params=_params(None, vmem),
        name="proj_ln",
    )(x2d, a2d, w, ln_g.reshape(1, -1), ln_b.reshape(1, -1))


def _ffn_body(x_ref, wg_ref, wu_ref, wd_ref, lng_ref, lnb_ref, o_ref, acc_ref, *, alpha, tf):
    x = x_ref[...]
    xb = x.astype(BF16)
    for c in range(wg_ref.shape[1] // tf):
        cols = slice(c * tf, (c + 1) * tf)
        hid = (jax.nn.silu(_dot(xb, wg_ref[:, cols])) * _dot(xb, wu_ref[:, cols])).astype(BF16)
        part = _dot(hid, wd_ref[cols, :])
        if c == 0:
            acc_ref[...] = part
        else:
            acc_ref[...] += part
    o_ref[...] = _layer_norm(alpha * x + acc_ref[...], lng_ref[...], lnb_ref[...])


def _ffn(x2d, wg, wu, wd, ln_g, ln_b, alpha, tm, tf):
    rows, dm = x2d.shape
    d_ff = wg.shape[1]
    assert d_ff % tf == 0 and tf % LANES == 0 and rows % tm == 0
    row = pl.BlockSpec((tm, dm), lambda i: (i, 0))
    consts = [wg, wu, wd, ln_g.reshape(1, -1), ln_b.reshape(1, -1)]
    vmem = 2 * 2 * tm * dm * 4 + 2 * 3 * wg.size + tm * dm * 4 + 8 * tm * max(tf, dm) * 4
    return pl.pallas_call(
        functools.partial(_ffn_body, alpha=alpha, tf=tf),
        grid=(rows // tm,),
        in_specs=[row] + [_const_spec(c.shape) for c in consts],
        out_specs=row,
        out_shape=jax.ShapeDtypeStruct(x2d.shape, F32),
        scratch_shapes=[pltpu.VMEM((tm, dm), F32)],
        compiler_params=_params(("parallel",), vmem),
        name="swiglu_ln3",
    )(x2d, *consts)


PROMPT_CHUNK = 8
ROW_TILE = 512
FFN_COL_TILE = 256
SAMPLE_ATTN_SEQS = 2
SAMPLE_MEM_SEQS = 4


def kernel(x_prompt, x_sample, cache_win_k, cache_win_v, state_ssm_re, state_ssm_im, cache_mem_k, cache_mem_v, mem_prompt, w_in, g_att, g_ssm, ssm_a_re, ssm_a_im, ssm_log_dt, ssm_b_re, ssm_b_im, ssm_c_re, ssm_c_im, ssm_d, w_glu, b_glu, w_out, ln1_g, ln1_b, w_mem_q, w_mem_k, w_mem_v, w_mem_o, ln2_g, ln2_b, w_gate, w_up, w_down, ln3_g, ln3_b):
    depth = w_in.shape[0]
    bp, seq, dm = x_prompt.shape
    bs, t_new, _ = x_sample.shape
    n_groups, n_state = ssm_a_re.shape[1:]
    n_mem = mem_prompt.shape[1]
    alpha = (2 * depth) ** 0.25
    keep = min(DILATED_CFGS[-1][0], seq)
    assert keep == seq
    n_chunks = seq // PROMPT_CHUNK

    y_p = x_prompt.reshape(bp * seq, dm)
    y_s = x_sample.reshape(bs * t_new, dm)
    mem2d = mem_prompt.reshape(bp * n_mem, dm)
    outs = [[] for _ in range(10)]
    for l in range(depth):
        bf = lambda w: w[l].astype(BF16)
        w_in_l, w_glu_l, w_out_l = bf(w_in), bf(w_glu), bf(w_out)
        wq_l, wk_l, wv_l, wo_l = bf(w_mem_q), bf(w_mem_k), bf(w_mem_v), bf(w_mem_o)
        wg_l, wu_l, wd_l = bf(w_gate), bf(w_up), bf(w_down)
        prepped = _ssm_prep(ssm_a_re[l], ssm_a_im[l], ssm_log_dt[l], ssm_b_re[l], ssm_b_im[l],
                            max(PROMPT_CHUNK, t_new), PROMPT_CHUNK)
        layouts = _ssm_layouts(*prepped, ssm_c_re[l], ssm_c_im[l], ssm_d[l])
        mix_w = (w_glu_l, b_glu[l], g_att[l], g_ssm[l], w_out_l, ln1_g[l], ln1_b[l])

        q, k, v, u, k_t, v_t = _inproj(y_p, w_in_l, ROW_TILE, PROMPT_CHUNK, seq)
        d_att = q.shape[1]
        o_att = _attn_prompt(q, k, v, seq)
        zeros = jnp.zeros((bp, 1, n_groups * n_state), F32)
        y_ssm, hr_p, hi_p = _ssm(u, zeros, zeros, layouts, PROMPT_CHUNK, n_chunks, True)
        x1 = _mix(y_p, o_att, y_ssm, *mix_w, alpha, ROW_TILE, PROMPT_CHUNK)
        mk_p, mv_p, mk_heads, mv_heads = _memkv(mem2d, wk_l, wv_l, ROW_TILE)
        x2 = _memattn(x1, mk_p.reshape(bp, n_mem, dm), mv_p.reshape(bp, n_mem, dm), wq_l, wo_l,
                      ln2_g[l], ln2_b[l], alpha, ROW_TILE)
        y_p = _ffn(x2, wg_l, wu_l, wd_l, ln3_g[l], ln3_b[l], alpha, ROW_TILE, FFN_COL_TILE)

        qs, ks, vs, us = _inproj(y_s, w_in_l, bs * t_new, t_new)
        o_att_s = _attn_window(qs, ks, vs, cache_win_k, cache_win_v, l, SAMPLE_ATTN_SEQS)
        y_ssm_s, hr_s, hi_s = _ssm(us, state_ssm_re[l].reshape(1, bs, -1), state_ssm_im[l].reshape(1, bs, -1),
                                   layouts, t_new, bs, False)
        x1s = _mix(y_s, o_att_s, y_ssm_s, *mix_w, alpha, bs * t_new, t_new)
        mem_hd = dm // N_MEM_HEADS
        q_mem = _rows_matmul(x1s, wq_l, mem_hd ** -0.5).reshape(bs, t_new * N_MEM_HEADS, mem_hd)
        a_mem = _memattn_cache(q_mem, cache_mem_k, cache_mem_v, l, SAMPLE_MEM_SEQS)
        x2s = _proj_ln(x1s, a_mem.reshape(bs * t_new, dm), wo_l, ln2_g[l], ln2_b[l], alpha)
        y_s = _ffn(x2s, wg_l, wu_l, wd_l, ln3_g[l], ln3_b[l], alpha, bs * t_new, FFN_COL_TILE)

        head_shape = (N_ATT_HEADS, ATT_HEAD_DIM)
        state_shape = (n_groups, n_state)
        mem_shape = (bp, n_mem, N_MEM_HEADS, dm // N_MEM_HEADS)
        rows_first = lambda t: jnp.transpose(t.reshape(bp, *head_shape, seq), (0, 3, 1, 2))
        for lst, val in zip(outs, (
                rows_first(k_t)[:, seq - keep:], rows_first(v_t)[:, seq - keep:],
                ks.reshape(bs, t_new, *head_shape), vs.reshape(bs, t_new, *head_shape),
                hr_p.reshape(bp, *state_shape), hi_p.reshape(bp, *state_shape),
                hr_s.reshape(bs, *state_shape), hi_s.reshape(bs, *state_shape),
                mk_heads.reshape(mem_shape), mv_heads.reshape(mem_shape))):
            lst.append(val)
    return (y_p.reshape(bp, seq, dm), y_s.reshape(bs, t_new, dm)) + tuple(jnp.stack(o) for o in outs)
```

```python
import functools
import math

import jax
import jax.numpy as jnp
from jax import lax
from jax.experimental import pallas as pl
from jax.experimental.pallas import tpu as pltpu

F32 = jnp.float32
BF16 = jnp.bfloat16

N_ATT_HEADS = 8
ATT_HEAD_DIM = 64
D_ATT = N_ATT_HEADS * ATT_HEAD_DIM
DILATED_CFGS = ((128, 1), (512, 4), (2048, 16))
ATT_BLK = 128
ATT_GROUP = 4
SSM_CH = 16
SSM_STATE = 64
N_MEM_HEADS = 4
EPS = 1e-5
NEG = -1e30

LANES = 128
V7X_VMEM_CAP_BYTES = 56 * 1024 * 1024

SSM_GB = LANES // SSM_CH
SSM_ROW_TILE = 32
SSM_CHAIN_GROUP = 8


def _params(sem, vmem_bytes):
    return pltpu.CompilerParams(
        dimension_semantics=sem,
        vmem_limit_bytes=int(min(max(vmem_bytes, 16 * 1024 * 1024), V7X_VMEM_CAP_BYTES)),
    )


def _dot(a, b):
    return jnp.dot(a, b, preferred_element_type=F32)


def _dot_nt(a, b):
    return lax.dot_general(a, b, (((1,), (1,)), ((), ())), preferred_element_type=F32)


def _layer_norm(x, g, b):
    mu = jnp.mean(x, axis=-1, keepdims=True)
    xc = x - mu
    var = jnp.mean(xc * xc, axis=-1, keepdims=True)
    return xc * lax.rsqrt(var + EPS) * g + b


def _rms_norm(x, g):
    return x * lax.rsqrt(jnp.mean(x * x, axis=-1, keepdims=True) + EPS) * g


def _const_spec(shape):
    n = len(shape)
    return pl.BlockSpec(shape, lambda *_: (0,) * n, pipeline_mode=pl.Buffered(1))


def _inproj_body(x_ref, w_ref, q_ref, k_ref, v_ref, uc_ref, *rest, t1, transposed):
    u_scr = rest[-1]
    xb = x_ref[...].astype(BF16)
    d = q_ref.shape[1]
    q_ref[...] = _dot(xb, w_ref[:, 0 * d:1 * d])
    k = _dot(xb, w_ref[:, 1 * d:2 * d])
    v = _dot(xb, w_ref[:, 2 * d:3 * d])
    k_ref[...] = k
    v_ref[...] = v
    if transposed:
        kt_ref, vt_ref = rest[:2]
        kt_ref[...] = k.T
        vt_ref[...] = v.T
    u = _dot(xb, w_ref[:, 3 * d:4 * d])
    chunks = u_scr.shape[1] // t1
    for c in range(d // LANES):
        u_scr[c] = u[:, c * LANES:(c + 1) * LANES]
        for i in range(t1):
            uc_ref[:, i * d + c * LANES:i * d + (c + 1) * LANES] = u_scr[c, pl.ds(i, chunks, stride=t1), :]


def _inproj(x2d, w_in_bf16, tm, t1, seq=None):
    rows, dm = x2d.shape
    d = w_in_bf16.shape[1] // 4
    out = jax.ShapeDtypeStruct((rows, d), F32)
    row_spec = pl.BlockSpec((tm, d), lambda i: (i, 0))
    out_specs = [row_spec] * 3 + [pl.BlockSpec((tm // t1, t1 * d), lambda i: (i, 0))]
    out_shape = [out] * 3 + [jax.ShapeDtypeStruct((rows // t1, t1 * d), F32)]
    if seq is not None:
        steps = seq // tm
        out_specs += [pl.BlockSpec((None, d, tm), lambda i: (i // steps, 0, i % steps))] * 2
        out_shape += [jax.ShapeDtypeStruct((rows // seq, d, seq), F32)] * 2
    vmem = 2 * (tm * dm * 4 + 6 * tm * d * 4) + 2 * w_in_bf16.size * 2 + 10 * tm * d * 4
    return pl.pallas_call(
        functools.partial(_inproj_body, t1=t1, transposed=seq is not None),
        grid=(rows // tm,),
        in_specs=[pl.BlockSpec((tm, dm), lambda i: (i, 0)), _const_spec(w_in_bf16.shape)],
        out_specs=out_specs,
        out_shape=out_shape,
        scratch_shapes=[pltpu.VMEM((d // LANES, tm, LANES), F32)],
        compiler_params=_params(("parallel",), vmem),
        name="inproj",
    )(x2d, w_in_bf16)


def _attn_prompt_body(q_ref, k_ref, v_ref, o_ref, acc_ref, m_ref, l_ref):
    seq = q_ref.shape[0]
    blk = ATT_BLK
    scale = ATT_HEAD_DIM ** -0.5 * math.log2(math.e)
    lane = lax.broadcasted_iota(jnp.int32, (blk, LANES), 1)
    head0 = lane < ATT_HEAD_DIM
    qi = lax.broadcasted_iota(jnp.int32, (blk, blk), 0)
    kj = lax.broadcasted_iota(jnp.int32, (blk, blk), 1)
    tri = kj <= qi
    mask_rest = jnp.concatenate([kj >= qi, tri], axis=1)

    def rows_of(start, d):
        return pl.ds(start, blk) if d == 1 else pl.ds(start, blk, stride=d)

    def load_block(d, q_start, prev_start):
        rows = rows_of(q_start, d)
        q = q_ref[rows, :] * scale
        k2 = k_ref[rows, :].astype(BF16)
        v2 = v_ref[rows, :].astype(BF16)
        mask = tri
        if prev_start is not None:
            prow = rows_of(prev_start, d)
            k2 = jnp.concatenate([k_ref[prow, :].astype(BF16), k2], axis=0)
            v2 = jnp.concatenate([v_ref[prow, :].astype(BF16), v2], axis=0)
            mask = mask_rest
        qh = [jnp.where(head0 if h == 0 else jnp.logical_not(head0), q, 0.0).astype(BF16) for h in range(2)]
        return rows, qh, k2, v2, mask

    def attend_group(cfg, d, starts):
        blocks = [load_block(d, q_start, prev_start) for q_start, prev_start in starts]
        scores = [[jnp.where(mask, _dot_nt(qh[h], k2), NEG) for h in range(2)] for _, qh, k2, _, mask in blocks]
        maxes = [[jnp.max(s, axis=1, keepdims=True) for s in sb] for sb in scores]
        probs = [[jnp.exp2(s - m) for s, m in zip(sb, mb)] for sb, mb in zip(scores, maxes)]
        sums = [[jnp.sum(p, axis=1, keepdims=True) for p in pb] for pb in probs]
        pvs = [[_dot(p.astype(BF16), blk_[3]) for p in pb] for pb, blk_ in zip(probs, blocks)]
        new = [(blk_[0], jnp.where(head0, mb[0], mb[1]), jnp.where(head0, lb[0], lb[1]),
                jnp.where(head0, ob[0], ob[1])) for blk_, mb, lb, ob in zip(blocks, maxes, sums, pvs)]
        if cfg == 0:
            for rows, m, l, num in new:
                acc_ref[rows, :] = num
                m_ref[rows, :] = m
                l_ref[rows, :] = l
            return
        old = [(m_ref[rows, :], l_ref[rows, :], acc_ref[rows, :]) for rows, _, _, _ in new]
        for (rows, m, l, num), (m_old, l_old, acc_old) in zip(new, old):
            m_new = jnp.maximum(m_old, m)
            a = jnp.exp2(m_old - m_new)
            b = jnp.exp2(m - m_new)
            num = a * acc_old + b * num
            l = a * l_old + b * l
            if cfg == len(DILATED_CFGS) - 1:
                o_ref[rows, :] = num / l
            else:
                acc_ref[rows, :] = num
                m_ref[rows, :] = m_new
                l_ref[rows, :] = l

    for cfg, (window, d) in enumerate(DILATED_CFGS):
        span = d * blk
        blocks = [(r + n * span, r + (n - 1) * span if n else None) for n in range(seq // span) for r in range(d)]
        for g in range(0, len(blocks), ATT_GROUP):
            attend_group(cfg, d, blocks[g:g + ATT_GROUP])


def _attn_prompt(q, k, v, seq):
    rows, d_att = q.shape
    bsz = rows // seq
    assert d_att % LANES == 0 and LANES == 2 * ATT_HEAD_DIM
    for window, d in DILATED_CFGS:
        assert window // d == ATT_BLK and seq % (d * ATT_BLK) == 0
    spec = pl.BlockSpec((seq, LANES), lambda b, h: (b, h))
    blk_bytes = seq * LANES * 4
    return pl.pallas_call(
        _attn_prompt_body,
        grid=(bsz, d_att // LANES),
        in_specs=[spec, spec, spec],
        out_specs=spec,
        out_shape=jax.ShapeDtypeStruct(q.shape, F32),
        scratch_shapes=[pltpu.VMEM((seq, LANES), F32)] * 3,
        compiler_params=_params(("parallel", "parallel"), 11 * blk_bytes + 8 * 1024 * 1024),
        name="attn_prompt",
    )(q, k, v)


def _window_head_mask(d_att):
    nh = N_ATT_HEADS
    return (jnp.right_shift(lax.broadcasted_iota(jnp.int32, (nh, d_att), 1), ATT_HEAD_DIM.bit_length() - 1)
            == lax.broadcasted_iota(jnp.int32, (nh, d_att), 0))


def _window_probs(q, kn, kt):
    t_new, d_att = q.shape
    w_buf = kt.shape[1]
    nh = N_ATT_HEADS
    rows = t_new * nh
    (win1, _), (win4, dil4), (_, dil16) = DILATED_CFGS
    tail, near = win4, win1
    head_mask = _window_head_mask(d_att)
    trow = jnp.right_shift(lax.broadcasted_iota(jnp.int32, (rows, 1), 0), nh.bit_length() - 1)
    lane = lambda n: lax.broadcasted_iota(jnp.int32, (rows, n), 1)
    mask16 = (lane(w_buf) & (dil16 - 1)) == trow
    mask4 = (lane(tail) & (dil4 - 1)) == trow
    mask1 = lane(near) >= trow
    new_self = lane(t_new) == trow
    new_causal = lane(t_new) <= trow

    q = q * (ATT_HEAD_DIM ** -0.5)
    qbd = jnp.concatenate(
        [jnp.where(head_mask, jnp.broadcast_to(q[t:t + 1], (nh, d_att)), 0.0) for t in range(t_new)],
        axis=0).astype(BF16)
    s_all = _dot(qbd, kt.astype(BF16))
    s_new = _dot_nt(qbd, kn.astype(BF16))

    def softmax_parts(main, main_mask, new_mask):
        sm = jnp.where(main_mask, main, NEG)
        sn = jnp.where(new_mask, s_new, NEG)
        m = jnp.maximum(jnp.max(sm, axis=1, keepdims=True), jnp.max(sn, axis=1, keepdims=True))
        p = jnp.exp(sm - m)
        pn = jnp.exp(sn - m)
        den = jnp.sum(p, axis=1, keepdims=True) + jnp.sum(pn, axis=1, keepdims=True)
        return m, p, pn, den

    m16, p16, pn16, den16 = softmax_parts(s_all, mask16, new_self)
    m4, p4, pn4, den4 = softmax_parts(s_all[:, w_buf - tail:], mask4, new_self)
    m1, p1, pn1, den1 = softmax_parts(s_all[:, w_buf - near:], mask1, new_causal)
    m = jnp.maximum(jnp.maximum(m1, m4), m16)
    w1, w4, w16 = jnp.exp(m1 - m), jnp.exp(m4 - m), jnp.exp(m16 - m)
    den = w1 * den1 + w4 * den4 + w16 * den16
    p16, p4, p1 = w16 * p16, w4 * p4, w1 * p1
    p_all = jnp.concatenate(
        [p16[:, :w_buf - tail],
         p16[:, w_buf - tail:w_buf - near] + p4[:, :tail - near],
         p16[:, w_buf - near:] + p4[:, tail - near:] + p1], axis=1).astype(BF16)
    pn_all = (w16 * pn16 + w4 * pn4 + w1 * pn1).astype(BF16)
    return p_all, pn_all, den


def _window_output(p_all, pn_all, den, vn, vt):
    t_new, d_att = vn.shape
    nh = N_ATT_HEADS
    head_mask = _window_head_mask(d_att)
    out = (_dot_nt(p_all, vt.astype(BF16)) + _dot(pn_all, vn.astype(BF16))) / den
    return jnp.concatenate(
        [jnp.sum(jnp.where(head_mask, out[t * nh:(t + 1) * nh], 0.0), axis=0, keepdims=True)
         for t in range(t_new)], axis=0)


def _attn_window_body(q_ref, kn_ref, vn_ref, kt_ref, vt_ref, o_ref):
    nb = kt_ref.shape[0]
    t_new = q_ref.shape[0] // nb
    tok = [slice(b * t_new, (b + 1) * t_new) for b in range(nb)]
    probs = [_window_probs(q_ref[tok[b], :], kn_ref[tok[b], :], kt_ref[b]) for b in range(nb)]
    for b in range(nb):
        o_ref[tok[b], :] = _window_output(*probs[b], vn_ref[tok[b], :], vt_ref[b])


def _attn_window(q, k_new, v_new, cache_k, cache_v, layer, nb):
    depth, bsz, w_buf, nh, hd = cache_k.shape
    t_new, d_att = q.shape[0] // bsz, q.shape[1]
    assert w_buf == DILATED_CFGS[-1][0] and t_new <= DILATED_CFGS[1][1] and nh * hd == d_att
    assert (nb * t_new) % 8 == 0
    rows_last = lambda c: jnp.transpose(c, (0, 1, 3, 4, 2)).reshape(depth, bsz, d_att, w_buf)
    new_spec = pl.BlockSpec((nb * t_new, d_att), lambda i: (i, 0))
    cache_spec = pl.BlockSpec((None, nb, d_att, w_buf), lambda i: (layer, i, 0, 0))
    vmem = 2 * 2 * nb * d_att * w_buf * 4 + 4 * nb * d_att * w_buf * 2 + 12 * 1024 * 1024
    return pl.pallas_call(
        _attn_window_body,
        grid=(bsz // nb,),
        in_specs=[new_spec, new_spec, new_spec, cache_spec, cache_spec],
        out_specs=new_spec,
        out_shape=jax.ShapeDtypeStruct(q.shape, F32),
        compiler_params=_params(("parallel",), vmem),
        name="attn_window",
    )(q, k_new, v_new, rows_last(cache_k), rows_last(cache_v))


def _ssm_prep_body(a_re_ref, a_im_ref, log_dt_ref, b_re_ref, b_im_ref, lam_re_ref, lam_im_ref,
                   bbar_re_ref, bbar_im_ref, step_re_ref, step_im_ref, *, chunk):
    ar = a_re_ref[...]
    ai = a_im_ref[...]
    dt = jnp.exp(log_dt_ref[...])
    mag = jnp.exp(dt * ar)
    lr = mag * jnp.cos(dt * ai)
    li = mag * jnp.sin(dt * ai)
    den = ar * ar + ai * ai
    nr, ni = lr - 1.0, li
    cr = (nr * ar + ni * ai) / den
    ci = (ni * ar - nr * ai) / den
    for h in range(b_re_ref.shape[0]):
        br, bi = b_re_ref[h], b_im_ref[h]
        bbar_re_ref[h] = cr * br - ci * bi
        bbar_im_ref[h] = cr * bi + ci * br
    pr, pi = lr, li
    for j in range(lam_re_ref.shape[0]):
        lam_re_ref[j] = pr
        lam_im_ref[j] = pi
        if j == chunk - 1:
            cr, ci = pr, pi
        pr, pi = pr * lr - pi * li, pr * li + pi * lr
    pr, pi = cr, ci
    for j in range(step_re_ref.shape[0]):
        step_re_ref[j] = pr
        step_im_ref[j] = pi
        pr, pi = pr * cr - pi * ci, pr * ci + pi * cr


def _ssm_prep(a_re, a_im, log_dt, b_re, b_im, n_pow, chunk):
    g, p = a_re.shape
    h = b_re.shape[-1]
    assert chunk <= n_pow
    b_re_t = jnp.transpose(b_re, (2, 0, 1))
    b_im_t = jnp.transpose(b_im, (2, 0, 1))
    gp = jax.ShapeDtypeStruct((n_pow, g, p), F32)
    hgp = jax.ShapeDtypeStruct((h, g, p), F32)
    sgp = jax.ShapeDtypeStruct((SSM_CHAIN_GROUP, g, p), F32)
    return pl.pallas_call(
        functools.partial(_ssm_prep_body, chunk=chunk),
        out_shape=[gp, gp, hgp, hgp, sgp, sgp],
        name="ssm_prep",
    )(a_re, a_im, log_dt.reshape(g, 1), b_re_t, b_im_t)


def _ssm_layouts(lam_re, lam_im, bbar_re_t, bbar_im_t, step_re, step_im, c_re, c_im, d_skip):
    n_pow, g, p = lam_re.shape
    h = bbar_re_t.shape[0]
    nq = g // SSM_GB
    eye = jnp.eye(SSM_GB, dtype=bool)
    bb = jnp.stack([bbar_re_t, bbar_im_t], 0).reshape(2, h, nq, SSM_GB, p)
    bb = jnp.transpose(bb, (2, 3, 1, 0, 4))
    bmat = jnp.where(eye[None, :, None, None, :, None], bb[:, :, :, :, None, :], 0.0)
    bmat = bmat.reshape(nq, SSM_GB * h, 2 * SSM_GB * p).astype(BF16)
    def c_layout(c):
        cc = jnp.transpose(c.reshape(nq, SSM_GB, h, p), (0, 1, 3, 2))
        m = jnp.where(eye[None, :, None, :, None], cc[:, :, :, None, :], 0.0)
        return m.reshape(nq, SSM_GB * p, SSM_GB * h).astype(BF16)
    per_block = lambda t: t.reshape(t.shape[0], nq, 1, SSM_GB * p)
    d_q = d_skip.reshape(nq, 1, SSM_GB * h)
    return (bmat, c_layout(c_re), c_layout(c_im), per_block(lam_re), per_block(lam_im),
            per_block(step_re), per_block(step_im), d_q)


def _ssm_body(u_ref, h0r_ref, h0i_ref, bmat_ref, cre_ref, cim_ref, lamr_ref, lami_ref, stepr_ref, stepi_ref, d_ref,
              y_ref, hfr_ref, hfi_ref, hloc_ref, hs_ref, ends_ref, grp_ref, *, t1, chained):
    rows = u_ref.shape[0]
    d_ssm = y_ref.shape[1] // t1
    nq = bmat_ref.shape[0]
    ns = lamr_ref.shape[-1]
    rt = SSM_ROW_TILE
    re, im = slice(0, ns), slice(ns, 2 * ns)

    def u_lanes(i, q):
        return slice(i * d_ssm + q * LANES, i * d_ssm + (q + 1) * LANES)

    def row_tiles(body):
        lax.fori_loop(0, rows // rt, lambda t, c: (body(pl.ds(pl.multiple_of(t * rt, rt), rt)), c)[1], 0,
                      unroll=True)

    def input_drive(q):
        for i in range(t1):
            hloc_ref[q % 2, i] = _dot(u_ref[:, u_lanes(i, q)].astype(BF16), bmat_ref[q])

    def local_scan(q):
        h = hloc_ref.at[q % 2]
        lr, li = lamr_ref[0, q], lami_ref[0, q]

        def tile(r):
            hr, hi = h[0, r, re], h[0, r, im]
            for i in range(1, t1):
                hr, hi = lr * hr - li * hi + h[i, r, re], lr * hi + li * hr + h[i, r, im]
                h[i, r, re] = hr
                h[i, r, im] = hi
        row_tiles(tile)

    nlb = ns // LANES

    def put_blocks(ref, rows_idx, vr, vi):
        for c in range(nlb):
            ref[c, rows_idx, :] = vr[:, c * LANES:(c + 1) * LANES]
            ref[nlb + c, rows_idx, :] = vi[:, c * LANES:(c + 1) * LANES]

    def get_blocks(ref, rows_idx):
        return (jnp.concatenate([ref[c, rows_idx, :] for c in range(nlb)], axis=1),
                jnp.concatenate([ref[nlb + c, rows_idx, :] for c in range(nlb)], axis=1))

    def chunk_starts(q):
        h = hloc_ref.at[q % 2]
        s_lanes = slice(q * ns, (q + 1) * ns)
        if not chained:
            put_blocks(hs_ref, slice(None), h0r_ref[:, s_lanes], h0i_ref[:, s_lanes])
            return
        grp = SSM_CHAIN_GROUP
        n_grp = rows // grp
        put_blocks(ends_ref, slice(None), h[t1 - 1, :, re], h[t1 - 1, :, im])
        member = lambda j: pl.ds(j, n_grp, stride=grp)
        cr, ci = stepr_ref[0, q], stepi_ref[0, q]
        gr, gi = get_blocks(ends_ref, member(0))
        inside = [(gr, gi)]
        for j in range(1, grp):
            er, ei = get_blocks(ends_ref, member(j))
            gr, gi = cr * gr - ci * gi + er, cr * gi + ci * gr + ei
            inside.append((gr, gi))
        grp_ref[0, :, re] = gr
        grp_ref[0, :, im] = gi
        wr, wi = stepr_ref[grp - 1, q], stepi_ref[grp - 1, q]

        def chain(c, g):
            sr, si = g
            row = pl.ds(c, 1)
            grp_ref[1, row, re] = sr
            grp_ref[1, row, im] = si
            return wr * sr - wi * si + grp_ref[0, row, re], wr * si + wi * sr + grp_ref[0, row, im]

        sr, si = lax.fori_loop(0, n_grp, chain, (h0r_ref[:, s_lanes], h0i_ref[:, s_lanes]))
        hfr_ref[:, s_lanes] = sr
        hfi_ref[:, s_lanes] = si
        sr, si = grp_ref[1, :, re], grp_ref[1, :, im]
        put_blocks(hs_ref, member(0), sr, si)
        for j in range(1, grp):
            pr, pi = stepr_ref[j - 1, q], stepi_ref[j - 1, q]
            gr, gi = inside[j - 1]
            put_blocks(hs_ref, member(j), gr + (pr * sr - pi * si), gi + (pr * si + pi * sr))

    def add_carry(q):
        h = hloc_ref.at[q % 2]

        def tile(r):
            hr0, hi0 = get_blocks(hs_ref, r)
            for i in range(t1):
                pr, pi = lamr_ref[i, q], lami_ref[i, q]
                h[i, r, re] = h[i, r, re] + (pr * hr0 - pi * hi0)
                h[i, r, im] = h[i, r, im] + (pr * hi0 + pi * hr0)
        row_tiles(tile)

    def readout(q):
        h = hloc_ref.at[q % 2]
        if not chained:
            s_lanes = slice(q * ns, (q + 1) * ns)
            hfr_ref[:, s_lanes] = h[t1 - 1, :, re]
            hfi_ref[:, s_lanes] = h[t1 - 1, :, im]
        for i in range(t1):
            y_ref[:, u_lanes(i, q)] = (_dot(h[i, :, re].astype(BF16), cre_ref[q])
                                       - _dot(h[i, :, im].astype(BF16), cim_ref[q])
                                       + d_ref[q] * u_ref[:, u_lanes(i, q)])

    input_drive(0)
    local_scan(0)
    chunk_starts(0)
    for q in range(nq):
        if q + 1 < nq:
            input_drive(q + 1)
        add_carry(q)
        readout(q)
        if q + 1 < nq:
            local_scan(q + 1)
            chunk_starts(q + 1)


def _ssm(u_chunks, h0_re, h0_im, layouts, t1, rows, chained):
    bmat, cre, cim, lam_re_q, lam_im_q, step_re_q, step_im_q, d_q = layouts
    lam_re_q, lam_im_q = lam_re_q[:t1], lam_im_q[:t1]
    n_blocks, rows_h, n_state = h0_re.shape
    width = u_chunks.shape[1]
    assert rows % SSM_ROW_TILE == 0 and u_chunks.shape[0] == n_blocks * rows
    assert not chained or rows % SSM_CHAIN_GROUP == 0
    ns2 = bmat.shape[2]
    u_spec = pl.BlockSpec((rows, width), lambda b: (b, 0))
    h_spec = pl.BlockSpec((None, rows_h, n_state), lambda b: (b, 0, 0))
    consts = [bmat, cre, cim, lam_re_q, lam_im_q, step_re_q, step_im_q, d_q]
    chain_rows = rows if chained else 8
    blocked = lambda r: pltpu.VMEM((ns2 // LANES, r, LANES), F32)
    vmem = (4 * rows * width * 4 + (2 * t1 + 1) * rows * ns2 * 4 + sum(c.size * c.dtype.itemsize for c in consts)
            + 8 * rows_h * n_state * 4 + 8 * 1024 * 1024)
    h_out = jax.ShapeDtypeStruct(h0_re.shape, F32)
    return pl.pallas_call(
        functools.partial(_ssm_body, t1=t1, chained=chained),
        grid=(n_blocks,),
        in_specs=[u_spec, h_spec, h_spec] + [_const_spec(c.shape) for c in consts],
        out_specs=[u_spec, h_spec, h_spec],
        out_shape=[jax.ShapeDtypeStruct(u_chunks.shape, F32), h_out, h_out],
        scratch_shapes=[pltpu.VMEM((2, t1, rows, ns2), F32), blocked(rows), blocked(chain_rows),
                        pltpu.VMEM((2, chain_rows // SSM_CHAIN_GROUP, ns2), F32)],
        compiler_params=_params(("parallel",), vmem),
        name="ssm_chained" if chained else "ssm_rows",
    )(u_chunks, h0_re, h0_im, *consts)


def _mix_body(x_ref, oatt_ref, yc_ref, wglu_ref, bglu_ref, gatt_ref, gssm_ref, wout_ref, lng_ref, lnb_ref,
              o_ref, y_scr, *, alpha, t1):
    d_att = oatt_ref.shape[1]
    n_lane_blocks, rows, _ = y_scr.shape
    d_ssm = n_lane_blocks * LANES
    chunks = rows // t1
    for c in range(n_lane_blocks):
        for i in range(t1):
            y_scr[c, pl.ds(i, chunks, stride=t1), :] = yc_ref[:, i * d_ssm + c * LANES:i * d_ssm + (c + 1) * LANES]
    g = jax.nn.gelu(jnp.concatenate([y_scr[c] for c in range(n_lane_blocks)], axis=1))
    z = g * jax.nn.sigmoid(_dot(g.astype(BF16), wglu_ref[...]) + bglu_ref[...])
    ra = _rms_norm(oatt_ref[...], gatt_ref[...]).astype(BF16)
    rz = _rms_norm(z, gssm_ref[...]).astype(BF16)
    mixed = _dot(ra, wout_ref[0:d_att, :]) + _dot(rz, wout_ref[d_att:, :])
    o_ref[...] = _layer_norm(alpha * x_ref[...] + mixed, lng_ref[...], lnb_ref[...])


def _mix(x2d, o_att, y_chunks, w_glu, b_glu, g_att, g_ssm, w_out, ln_g, ln_b, alpha, tm, t1):
    rows, dm = x2d.shape
    d_att, d_ssm = o_att.shape[1], y_chunks.shape[1] // t1
    consts = [w_glu, b_glu.reshape(1, -1), g_att.reshape(1, -1), g_ssm.reshape(1, -1), w_out,
              ln_g.reshape(1, -1), ln_b.reshape(1, -1)]
    row = lambda width: pl.BlockSpec((tm, width), lambda i: (i, 0))
    chunk_spec = pl.BlockSpec((tm // t1, t1 * d_ssm), lambda i: (i, 0))
    vmem = 2 * tm * (2 * dm + d_att + d_ssm) * 4 + 4 * (w_glu.size + w_out.size) + 12 * tm * dm * 4
    return pl.pallas_call(
        functools.partial(_mix_body, alpha=alpha, t1=t1),
        grid=(rows // tm,),
        in_specs=[row(dm), row(d_att), chunk_spec] + [_const_spec(c.shape) for c in consts],
        out_specs=row(dm),
        out_shape=jax.ShapeDtypeStruct(x2d.shape, F32),
        scratch_shapes=[pltpu.VMEM((d_ssm // LANES, tm, LANES), F32)],
        compiler_params=_params(("parallel",), vmem),
        name="mix_out_ln1",
    )(x2d, o_att, y_chunks, *consts)


def _memkv_body(m_ref, wk_ref, wv_ref, k_ref, v_ref, kh_ref, vh_ref):
    mb = m_ref[...].astype(BF16)
    hd = kh_ref.shape[2]
    for w_ref, o_ref, oh_ref in ((wk_ref, k_ref, kh_ref), (wv_ref, v_ref, vh_ref)):
        val = _dot(mb, w_ref[...])
        o_ref[...] = val
        for h in range(oh_ref.shape[1]):
            oh_ref[:, h, :] = val[:, h * hd:(h + 1) * hd]


def _memkv(mem2d, wk, wv, tm):
    rows, dm = mem2d.shape
    hd = dm // N_MEM_HEADS
    row = pl.BlockSpec((tm, dm), lambda i: (i, 0))
    row_h = pl.BlockSpec((tm, N_MEM_HEADS, hd), lambda i: (i, 0, 0))
    out = jax.ShapeDtypeStruct((rows, wk.shape[1]), F32)
    out_h = jax.ShapeDtypeStruct((rows, N_MEM_HEADS, hd), F32)
    vmem = 2 * 5 * tm * dm * 4 + 2 * (wk.size + wv.size) + 4 * tm * dm * 4
    return pl.pallas_call(
        _memkv_body,
        grid=(rows // tm,),
        in_specs=[row, _const_spec(wk.shape), _const_spec(wv.shape)],
        out_specs=[row, row, row_h, row_h],
        out_shape=[out, out, out_h, out_h],
        compiler_params=_params(("parallel",), vmem),
        name="mem_kv",
    )(mem2d, wk, wv)


def _memattn_body(x_ref, mk_ref, mv_ref, wq_ref, wo_ref, lng_ref, lnb_ref, o_ref, *, alpha):
    dm = x_ref.shape[1]
    hd = dm // N_MEM_HEADS
    heads = [slice(h * hd, (h + 1) * hd) for h in range(N_MEM_HEADS)]
    x = x_ref[...]
    q = (_dot(x.astype(BF16), wq_ref[...]) * (hd ** -0.5)).astype(BF16)
    scores = [_dot_nt(q[:, c], mk_ref[:, c].astype(BF16)) for c in heads]
    probs =[jnp.exp(s - jnp.max(s, axis=1, keepdims=True)) for s in scores]
    sums = [jnp.sum(p, axis=1, keepdims=True) for p in probs]
    outs = [_dot(p.astype(BF16), mv_ref[:, c].astype(BF16)) / l for p, l, c in zip(probs, sums, heads)]
    att = _dot(jnp.concatenate(outs, axis=1).astype(BF16), wo_ref[...])
    o_ref[...] = _layer_norm(alpha * x + att, lng_ref[...], lnb_ref[...])


def _memattn(x2d, mem_k, mem_v, wq, wo, ln_g, ln_b, alpha, tm):
    rows, dm = x2d.shape
    n_seq, n_mem, _ = mem_k.shape
    steps_per_mem = rows // n_seq // tm
    row = pl.BlockSpec((tm, dm), lambda i: (i, 0))
    mem_spec = pl.BlockSpec((None, n_mem, dm), lambda i: (i // steps_per_mem, 0, 0))
    consts = [wq, wo, ln_g.reshape(1, -1), ln_b.reshape(1, -1)]
    vmem = 2 * 2 * tm * dm * 4 + 2 * 2 * n_mem * dm * 4 + 2 * (wq.size + wo.size) + 12 * tm * dm * 4
    return pl.pallas_call(
        functools.partial(_memattn_body, alpha=alpha),
        grid=(rows // tm,),
        in_specs=[row, mem_spec, mem_spec] + [_const_spec(c.shape) for c in consts],
        out_specs=row,
        out_shape=jax.ShapeDtypeStruct(x2d.shape, F32),
        compiler_params=_params(("parallel",), vmem),
        name="mem_attn_ln2",
    )(x2d, mem_k, mem_v, *consts)


def _rows_matmul_body(x_ref, w_ref, o_ref, *, scale):
    o_ref[...] = _dot(x_ref[...].astype(BF16), w_ref[...]) * scale


def _rows_matmul(x2d, w, scale):
    rows, dm = x2d.shape
    vmem = 4 * rows * (dm + w.shape[1]) * 4 + 4 * w.size
    return pl.pallas_call(
        functools.partial(_rows_matmul_body, scale=scale),
        out_shape=jax.ShapeDtypeStruct((rows, w.shape[1]), F32),
        compiler_params=_params(None, vmem),
        name="rows_matmul",
    )(x2d, w)


def _memattn_cache_body(q_ref, mk_ref, mv_ref, o_ref):
    nb, rows, hd = q_ref.shape
    n_mem, nh = mk_ref.shape[1:3]
    cols = n_mem * nh
    head_ok = ((lax.broadcasted_iota(jnp.int32, (rows, cols), 0) & (nh - 1))
               == (lax.broadcasted_iota(jnp.int32, (rows, cols), 1) & (nh - 1)))
    scores = [jnp.where(head_ok, _dot_nt(q_ref[j].astype(BF16), mk_ref[j].reshape(cols, hd).astype(BF16)), NEG)
              for j in range(nb)]
    probs = [jnp.exp(s - jnp.max(s, axis=1, keepdims=True)) for s in scores]
    sums = [jnp.sum(p, axis=1, keepdims=True) for p in probs]
    for j in range(nb):
        o_ref[j] = _dot(probs[j].astype(BF16), mv_ref[j].reshape(cols, hd).astype(BF16)) / sums[j]


def _memattn_cache(q3, cache_k, cache_v, layer, nb):
    bsz, rows, hd = q3.shape
    _, _, n_mem, nh, _ = cache_k.shape
    q_spec = pl.BlockSpec((nb, rows, hd), lambda i: (i, 0, 0))
    c_spec = pl.BlockSpec((None, nb, n_mem, nh, hd), lambda i: (layer, i, 0, 0, 0))
    vmem = 2 * 2 * nb * n_mem * 8 * hd * 4 + 8 * nb * rows * n_mem * nh * 4 + 8 * 1024 * 1024
    return pl.pallas_call(
        _memattn_cache_body,
        grid=(bsz // nb,),
        in_specs=[q_spec, c_spec, c_spec],
        out_specs=q_spec,
        out_shape=jax.ShapeDtypeStruct(q3.shape, F32),
        compiler_params=_params(("parallel",), vmem),
        name="mem_attn_cache",
    )(q3, cache_k, cache_v)


def _proj_ln_body(x_ref, a_ref, w_ref, lng_ref, lnb_ref, o_ref, *, alpha):
    att = _dot(a_ref[...].astype(BF16), w_ref[...])
    o_ref[...] = _layer_norm(alpha * x_ref[...] + att, lng_ref[...], lnb_ref[...])


def _proj_ln(x2d, a2d, w, ln_g, ln_b, alpha):
    rows, dm = x2d.shape
    vmem = 8 * rows * dm * 4 + 4 * w.size
    return pl.pallas_call(
        functools.partial(_proj_ln_body, alpha=alpha),
        out_shape=jax.ShapeDtypeStruct(x2d.shape, F32),
        compiler_params=_params(None, vmem),
        name="proj_ln",
    )(x2d, a2d, w, ln_g.reshape(1, -1), ln_b.reshape(1, -1))


def _ffn_body(x_ref, wg_ref, wu_ref, wd_ref, lng_ref, lnb_ref, o_ref, acc_ref, *, alpha, tf):
    x = x_ref[...]
    xb = x.astype(BF16)
    for c in range(wg_ref.shape[1] // tf):
        cols = slice(c * tf, (c + 1) * tf)
        hid = (jax.nn.silu(_dot(xb, wg_ref[:, cols])) * _dot(xb, wu_ref[:, cols])).astype(BF16)
        part = _dot(hid, wd_ref[cols, :])
        if c == 0:
            acc_ref[...] = part
        else:
            acc_ref[...] += part
    o_ref[...] = _layer_norm(alpha * x + acc_ref[...], lng_ref[...], lnb_ref[...])


def _ffn(x2d, wg, wu, wd, ln_g, ln_b, alpha, tm, tf):
    rows, dm = x2d.shape
    d_ff = wg.shape[1]
    assert d_ff % tf == 0 and tf % LANES == 0 and rows % tm == 0
    row = pl.BlockSpec((tm, dm), lambda i: (i, 0))
    consts = [wg, wu, wd, ln_g.reshape(1, -1), ln_b.reshape(1, -1)]
    vmem = 2 * 2 * tm * dm * 4 + 2 * 3 * wg.size + tm * dm * 4 + 8 * tm * max(tf, dm) * 4
    return pl.pallas_call(
        functools.partial(_ffn_body, alpha=alpha, tf=tf),
        grid=(rows // tm,),
        in_specs=[row] + [_const_spec(c.shape) for c in consts],
        out_specs=row,
        out_shape=jax.ShapeDtypeStruct(x2d.shape, F32),
        scratch_shapes=[pltpu.VMEM((tm, dm), F32)],
        compiler_params=_params(("parallel",), vmem),
        name="swiglu_ln3",
    )(x2d, *consts)


PROMPT_CHUNK = 8
ROW_TILE = 512
FFN_COL_TILE = 256
SAMPLE_ATTN_SEQS = 2
SAMPLE_MEM_SEQS = 4


def kernel(x_prompt, x_sample, cache_win_k, cache_win_v, state_ssm_re, state_ssm_im, cache_mem_k, cache_mem_v, mem_prompt, w_in, g_att, g_ssm, ssm_a_re, ssm_a_im, ssm_log_dt, ssm_b_re, ssm_b_im, ssm_c_re, ssm_c_im, ssm_d, w_glu, b_glu, w_out, ln1_g, ln1_b, w_mem_q, w_mem_k, w_mem_v, w_mem_o, ln2_g, ln2_b, w_gate, w_up, w_down, ln3_g, ln3_b):
    depth = w_in.shape[0]
    bp, seq, dm = x_prompt.shape
    bs, t_new, _ = x_sample.shape
    n_groups, n_state = ssm_a_re.shape[1:]
    n_mem = mem_prompt.shape[1]
    alpha = (2 * depth) ** 0.25
    keep = min(DILATED_CFGS[-1][0], seq)
    assert keep == seq
    n_chunks = seq // PROMPT_CHUNK

    y_p = x_prompt.reshape(bp * seq, dm)
    y_s = x_sample.reshape(bs * t_new, dm)
    mem2d = mem_prompt.reshape(bp * n_mem, dm)
    outs = [[] for _ in range(10)]
    for l in range(depth):
        bf = lambda w: w[l].astype(BF16)
        w_in_l, w_glu_l, w_out_l = bf(w_in), bf(w_glu), bf(w_out)
        wq_l, wk_l, wv_l, wo_l = bf(w_mem_q), bf(w_mem_k), bf(w_mem_v), bf(w_mem_o)
        wg_l, wu_l, wd_l = bf(w_gate), bf(w_up), bf(w_down)
        prepped = _ssm_prep(ssm_a_re[l], ssm_a_im[l], ssm_log_dt[l], ssm_b_re[l], ssm_b_im[l],
                            max(PROMPT_CHUNK, t_new), PROMPT_CHUNK)
        layouts = _ssm_layouts(*prepped, ssm_c_re[l], ssm_c_im[l], ssm_d[l])
        mix_w = (w_glu_l, b_glu[l], g_att[l], g_ssm[l], w_out_l, ln1_g[l], ln1_b[l])

        q, k, v, u, k_t, v_t = _inproj(y_p, w_in_l, ROW_TILE, PROMPT_CHUNK, seq)
        d_att = q.shape[1]
        o_att = _attn_prompt(q, k, v, seq)
        zeros = jnp.zeros((bp, 1, n_groups * n_state), F32)
        y_ssm, hr_p, hi_p = _ssm(u, zeros, zeros, layouts, PROMPT_CHUNK, n_chunks, True)
        x1 = _mix(y_p, o_att, y_ssm, *mix_w, alpha, ROW_TILE, PROMPT_CHUNK)
        mk_p, mv_p, mk_heads, mv_heads = _memkv(mem2d, wk_l, wv_l, ROW_TILE)
        x2 = _memattn(x1, mk_p.reshape(bp, n_mem, dm), mv_p.reshape(bp, n_mem, dm), wq_l, wo_l,
                      ln2_g[l], ln2_b[l], alpha, ROW_TILE)
        y_p = _ffn(x2, wg_l, wu_l, wd_l, ln3_g[l], ln3_b[l], alpha, ROW_TILE, FFN_COL_TILE)

        qs, ks, vs, us = _inproj(y_s, w_in_l, bs * t_new, t_new)
        o_att_s = _attn_window(qs, ks, vs, cache_win_k, cache_win_v, l, SAMPLE_ATTN_SEQS)
        y_ssm_s, hr_s, hi_s = _ssm(us, state_ssm_re[l].reshape(1, bs, -1), state_ssm_im[l].reshape(1, bs, -1),
                                   layouts, t_new, bs, False)
        x1s = _mix(y_s, o_att_s, y_ssm_s, *mix_w, alpha, bs * t_new, t_new)
        mem_hd = dm // N_MEM_HEADS
        q_mem = _rows_matmul(x1s, wq_l, mem_hd ** -0.5).reshape(bs, t_new * N_MEM_HEADS, mem_hd)
        a_mem = _memattn_cache(q_mem, cache_mem_k, cache_mem_v, l, SAMPLE_MEM_SEQS)
        x2s = _proj_ln(x1s, a_mem.reshape(bs * t_new, dm), wo_l, ln2_g[l], ln2_b[l], alpha)
        y_s = _ffn(x2s, wg_l, wu_l, wd_l, ln3_g[l], ln3_b[l], alpha, bs * t_new, FFN_COL_TILE)

        head_shape = (N_ATT_HEADS, ATT_HEAD_DIM)
        state_shape = (n_groups, n_state)
        mem_shape = (bp, n_mem, N_MEM_HEADS, dm // N_MEM_HEADS)
        rows_first = lambda t: jnp.transpose(t.reshape(bp, *head_shape, seq), (0, 3, 1, 2))
        for lst, val in zip(outs, (
                rows_first(k_t)[:, seq - keep:], rows_first(v_t)[:, seq - keep:],
                ks.reshape(bs, t_new, *head_shape), vs.reshape(bs, t_new, *head_shape),
                hr_p.reshape(bp, *state_shape), hi_p.reshape(bp, *state_shape),
                hr_s.reshape(bs, *state_shape), hi_s.reshape(bs, *state_shape),
                mk_heads.reshape(mem_shape), mv_heads.reshape(mem_shape))):
            lst.append(val)
    return (y_p.reshape(bp, seq, dm), y_s.reshape(bs, t_new, dm)) + tuple(jnp.stack(o) for o in outs)
```

```python
import functools
import math

import jax
import jax.numpy as jnp
from jax import lax
from jax.experimental import pallas as pl
from jax.experimental.pallas import tpu as pltpu

F32 = jnp.float32
BF16 = jnp.bfloat16

N_ATT_HEADS = 8
ATT_HEAD_DIM = 64
D_ATT = N_ATT_HEADS * ATT_HEAD_DIM
DILATED_CFGS = ((128, 1), (512, 4), (2048, 16))
ATT_BLK = 128
ATT_GROUP = 4
SSM_CH = 16
SSM_STATE = 64
N_MEM_HEADS = 4
MEM_ROW_SPLITS = 2
MIX_ROW_SPLITS = 2
EPS = 1e-5
NEG = -1e30

LANES = 128
V7X_VMEM_CAP_BYTES = 56 * 1024 * 1024

SSM_GB = LANES // SSM_CH
SSM_ROW_TILE = 32
SSM_CHAIN_GROUP = 8


def _params(sem, vmem_bytes):
    return pltpu.CompilerParams(
        dimension_semantics=sem,
        vmem_limit_bytes=int(min(max(vmem_bytes, 16 * 1024 * 1024), V7X_VMEM_CAP_BYTES)),
    )


def _dot(a, b):
    return jnp.dot(a, b, preferred_element_type=F32)


def _dot_nt(a, b):
    return lax.dot_general(a, b, (((1,), (1,)), ((), ())), preferred_element_type=F32)


def _layer_norm(x, g, b):
    mu = jnp.mean(x, axis=-1, keepdims=True)
    xc = x - mu
    var = jnp.mean(xc * xc, axis=-1, keepdims=True)
    return xc * lax.rsqrt(var + EPS) * g + b


def _rms_norm(x, g):
    return x * lax.rsqrt(jnp.mean(x * x, axis=-1, keepdims=True) + EPS) * g


def _const_spec(shape):
    n = len(shape)
    return pl.BlockSpec(shape, lambda *_: (0,) * n, pipeline_mode=pl.Buffered(1))


def _inproj_body(x_ref, w_ref, q_ref, k_ref, v_ref, uc_ref, *rest, t1, transposed):
    u_scr = rest[-1]
    xb = x_ref[...].astype(BF16)
    d = q_ref.shape[1]
    q_ref[...] = _dot(xb, w_ref[:, 0 * d:1 * d])
    k = _dot(xb, w_ref[:, 1 * d:2 * d])
    v = _dot(xb, w_ref[:, 2 * d:3 * d])
    k_ref[...] = k
    v_ref[...] = v
    if transposed:
        kt_ref, vt_ref = rest[:2]
        kt_ref[...] = k.T
        vt_ref[...] = v.T
    u = _dot(xb, w_ref[:, 3 * d:4 * d])
    chunks = u_scr.shape[1] // t1
    for c in range(d // LANES):
        u_scr[c] = u[:, c * LANES:(c + 1) * LANES]
        for i in range(t1):
            uc_ref[:, i * d + c * LANES:i * d + (c + 1) * LANES] = u_scr[c, pl.ds(i, chunks, stride=t1), :]


def _inproj(x2d, w_in_bf16, tm, t1, seq=None):
    rows, dm = x2d.shape
    d = w_in_bf16.shape[1] // 4
    out = jax.ShapeDtypeStruct((rows, d), F32)
    row_spec = pl.BlockSpec((tm, d), lambda i: (i, 0))
    out_specs = [row_spec] * 3 + [pl.BlockSpec((tm // t1, t1 * d), lambda i: (i, 0))]
    out_shape = [out] * 3 + [jax.ShapeDtypeStruct((rows // t1, t1 * d), F32)]
    if seq is not None:
        steps = seq // tm
        out_specs += [pl.BlockSpec((None, d, tm), lambda i: (i // steps, 0, i % steps))] * 2
        out_shape += [jax.ShapeDtypeStruct((rows // seq, d, seq), F32)] * 2
    vmem = 2 * (tm * dm * 4 + 6 * tm * d * 4) + 2 * w_in_bf16.size * 2 + 10 * tm * d * 4
    return pl.pallas_call(
        functools.partial(_inproj_body, t1=t1, transposed=seq is not None),
        grid=(rows // tm,),
        in_specs=[pl.BlockSpec((tm, dm), lambda i: (i, 0)), _const_spec(w_in_bf16.shape)],
        out_specs=out_specs,
        out_shape=out_shape,
        scratch_shapes=[pltpu.VMEM((d // LANES, tm, LANES), F32)],
        compiler_params=_params(("parallel",), vmem),
        name="inproj",
    )(x2d, w_in_bf16)


def _attn_prompt_body(q_ref, k_ref, v_ref, o_ref, acc_ref, m_ref, l_ref):
    seq = q_ref.shape[0]
    blk = ATT_BLK
    scale = ATT_HEAD_DIM ** -0.5 * math.log2(math.e)
    lane = lax.broadcasted_iota(jnp.int32, (blk, LANES), 1)
    head0 = lane < ATT_HEAD_DIM
    qi = lax.broadcasted_iota(jnp.int32, (blk, blk), 0)
    kj = lax.broadcasted_iota(jnp.int32, (blk, blk), 1)
    tri = kj <= qi
    mask_rest = jnp.concatenate([kj >= qi, tri], axis=1)

    def rows_of(start, d):
        return pl.ds(start, blk) if d == 1 else pl.ds(start, blk, stride=d)

    def load_block(d, q_start, prev_start):
        rows = rows_of(q_start, d)
        q = q_ref[rows, :] * scale
        k2 = k_ref[rows, :].astype(BF16)
        v2 = v_ref[rows, :].astype(BF16)
        mask = tri
        if prev_start is not None:
            prow = rows_of(prev_start, d)
            k2 = jnp.concatenate([k_ref[prow, :].astype(BF16), k2], axis=0)
            v2 = jnp.concatenate([v_ref[prow, :].astype(BF16), v2], axis=0)
            mask = mask_rest
        qh = [jnp.where(head0 if h == 0 else jnp.logical_not(head0), q, 0.0).astype(BF16) for h in range(2)]
        return rows, qh, k2, v2, mask

    def attend_group(cfg, d, starts):
        blocks = [load_block(d, q_start, prev_start) for q_start, prev_start in starts]
        scores = [[jnp.where(mask, _dot_nt(qh[h], k2), NEG) for h in range(2)] for _, qh, k2, _, mask in blocks]
        maxes = [[jnp.max(s, axis=1, keepdims=True) for s in sb] for sb in scores]
        probs = [[jnp.exp2(s - m) for s, m in zip(sb, mb)] for sb, mb in zip(scores, maxes)]
        sums = [[jnp.sum(p, axis=1, keepdims=True) for p in pb] for pb in probs]
        pvs = [[_dot(p.astype(BF16), blk_[3]) for p in pb] for pb, blk_ in zip(probs, blocks)]
        new = [(blk_[0], jnp.where(head0, mb[0], mb[1]), jnp.where(head0, lb[0], lb[1]),
                jnp.where(head0, ob[0], ob[1])) for blk_, mb, lb, ob in zip(blocks, maxes, sums, pvs)]
        if cfg == 0:
            for rows, m, l, num in new:
                acc_ref[rows, :] = num
                m_ref[rows, :] = m
                l_ref[rows, :] = l
            return
        old = [(m_ref[rows, :], l_ref[rows, :], acc_ref[rows, :]) for rows, _, _, _ in new]
        for (rows, m, l, num), (m_old, l_old, acc_old) in zip(new, old):
            m_new = jnp.maximum(m_old, m)
            a = jnp.exp2(m_old - m_new)
            b = jnp.exp2(m - m_new)
            num = a * acc_old + b * num
            l = a * l_old + b * l
            if cfg == len(DILATED_CFGS) - 1:
                o_ref[rows, :] = num / l
            else:
                acc_ref[rows, :] = num
                m_ref[rows, :] = m_new
                l_ref[rows, :] = l

    for cfg, (window, d) in enumerate(DILATED_CFGS):
        span = d * blk
        blocks = [(r + n * span, r + (n - 1) * span if n else None) for n in range(seq // span) for r in range(d)]
        for g in range(0, len(blocks), ATT_GROUP):
            attend_group(cfg, d, blocks[g:g + ATT_GROUP])


def _attn_prompt(q, k, v, seq):
    rows, d_att = q.shape
    bsz = rows // seq
    assert d_att % LANES == 0 and LANES == 2 * ATT_HEAD_DIM
    for window, d in DILATED_CFGS:
        assert window // d == ATT_BLK and seq % (d * ATT_BLK) == 0
    spec = pl.BlockSpec((seq, LANES), lambda b, h: (b, h))
    blk_bytes = seq * LANES * 4
    return pl.pallas_call(
        _attn_prompt_body,
        grid=(bsz, d_att // LANES),
        in_specs=[spec, spec, spec],
        out_specs=spec,
        out_shape=jax.ShapeDtypeStruct(q.shape, F32),
        scratch_shapes=[pltpu.VMEM((seq, LANES), F32)] * 3,
        compiler_params=_params(("parallel", "parallel"), 11 * blk_bytes + 8 * 1024 * 1024),
        name="attn_prompt",
    )(q, k, v)


def _window_head_mask(d_att):
    nh = N_ATT_HEADS
    return (jnp.right_shift(lax.broadcasted_iota(jnp.int32, (nh, d_att), 1), ATT_HEAD_DIM.bit_length() - 1)
            == lax.broadcasted_iota(jnp.int32, (nh, d_att), 0))


def _window_probs(q, kn, kt):
    t_new, d_att = q.shape
    w_buf = kt.shape[1]
    nh = N_ATT_HEADS
    rows = t_new * nh
    (win1, _), (win4, dil4), (_, dil16) = DILATED_CFGS
    tail, near = win4, win1
    head_mask = _window_head_mask(d_att)
    trow = jnp.right_shift(lax.broadcasted_iota(jnp.int32, (rows, 1), 0), nh.bit_length() - 1)
    lane = lambda n: lax.broadcasted_iota(jnp.int32, (rows, n), 1)
    mask16 = (lane(w_buf) & (dil16 - 1)) == trow
    mask4 = (lane(tail) & (dil4 - 1)) == trow
    mask1 = lane(near) >= trow
    new_self = lane(t_new) == trow
    new_causal = lane(t_new) <= trow

    q = q * (ATT_HEAD_DIM ** -0.5)
    qbd = jnp.concatenate(
        [jnp.where(head_mask, jnp.broadcast_to(q[t:t + 1], (nh, d_att)), 0.0) for t in range(t_new)],
        axis=0).astype(BF16)
    s_all = _dot(qbd, kt.astype(BF16))
    s_new = _dot_nt(qbd, kn.astype(BF16))

    def softmax_parts(main, main_mask, new_mask):
        sm = jnp.where(main_mask, main, NEG)
        sn = jnp.where(new_mask, s_new, NEG)
        m = jnp.maximum(jnp.max(sm, axis=1, keepdims=True), jnp.max(sn, axis=1, keepdims=True))
        p = jnp.exp(sm - m)
        pn = jnp.exp(sn - m)
        den = jnp.sum(p, axis=1, keepdims=True) + jnp.sum(pn, axis=1, keepdims=True)
        return m, p, pn, den

    m16, p16, pn16, den16 = softmax_parts(s_all, mask16, new_self)
    m4, p4, pn4, den4 = softmax_parts(s_all[:, w_buf - tail:], mask4, new_self)
    m1, p1, pn1, den1 = softmax_parts(s_all[:, w_buf - near:], mask1, new_causal)
    m = jnp.maximum(jnp.maximum(m1, m4), m16)
    w1, w4, w16 = jnp.exp(m1 - m), jnp.exp(m4 - m), jnp.exp(m16 - m)
    den = w1 * den1 + w4 * den4 + w16 * den16
    p16, p4, p1 = w16 * p16, w4 * p4, w1 * p1
    p_all = jnp.concatenate(
        [p16[:, :w_buf - tail],
         p16[:, w_buf - tail:w_buf - near] + p4[:, :tail - near],
         p16[:, w_buf - near:] + p4[:, tail - near:] + p1], axis=1).astype(BF16)
    pn_all = (w16 * pn16 + w4 * pn4 + w1 * pn1).astype(BF16)
    return p_all, pn_all, den


def _window_output(p_all, pn_all, den, vn, vt):
    t_new, d_att = vn.shape
    nh = N_ATT_HEADS
    head_mask = _window_head_mask(d_att)
    out = (_dot_nt(p_all, vt.astype(BF16)) + _dot(pn_all, vn.astype(BF16))) / den
    return jnp.concatenate(
        [jnp.sum(jnp.where(head_mask, out[t * nh:(t + 1) * nh], 0.0), axis=0, keepdims=True)
         for t in range(t_new)], axis=0)


def _attn_window_body(q_ref, kn_ref, vn_ref, kt_ref, vt_ref, o_ref):
    nb = kt_ref.shape[0]
    t_new = q_ref.shape[0] // nb
    tok = [slice(b * t_new, (b + 1) * t_new) for b in range(nb)]
    probs = [_window_probs(q_ref[tok[b], :], kn_ref[tok[b], :], kt_ref[b]) for b in range(nb)]
    for b in range(nb):
        o_ref[tok[b], :] = _window_output(*probs[b], vn_ref[tok[b], :], vt_ref[b])


def _attn_window(q, k_new, v_new, cache_k, cache_v, layer, nb):
    depth, bsz, w_buf, nh, hd = cache_k.shape
    t_new, d_att = q.shape[0] // bsz, q.shape[1]
    assert w_buf == DILATED_CFGS[-1][0] and t_new <= DILATED_CFGS[1][1] and nh * hd == d_att
    assert (nb * t_new) % 8 == 0
    rows_last = lambda c: jnp.transpose(c, (0, 1, 3, 4, 2)).reshape(depth, bsz, d_att, w_buf)
    new_spec = pl.BlockSpec((nb * t_new, d_att), lambda i: (i, 0))
    cache_spec = pl.BlockSpec((None, nb, d_att, w_buf), lambda i: (layer, i, 0, 0))
    vmem = 2 * 2 * nb * d_att * w_buf * 4 + 4 * nb * d_att * w_buf * 2 + 12 * 1024 * 1024
    return pl.pallas_call(
        _attn_window_body,
        grid=(bsz // nb,),
        in_specs=[new_spec, new_spec, new_spec, cache_spec, cache_spec],
        out_specs=new_spec,
        out_shape=jax.ShapeDtypeStruct(q.shape, F32),
        compiler_params=_params(("parallel",), vmem),
        name="attn_window",
    )(q, k_new, v_new, rows_last(cache_k), rows_last(cache_v))


def _ssm_prep_body(a_re_ref, a_im_ref, log_dt_ref, b_re_ref, b_im_ref, lam_re_ref, lam_im_ref,
                   bbar_re_ref, bbar_im_ref, step_re_ref, step_im_ref, *, chunk):
    ar = a_re_ref[...]
    ai = a_im_ref[...]
    dt = jnp.exp(log_dt_ref[...])
    mag = jnp.exp(dt * ar)
    lr = mag * jnp.cos(dt * ai)
    li = mag * jnp.sin(dt * ai)
    den = ar * ar + ai * ai
    nr, ni = lr - 1.0, li
    cr = (nr * ar + ni * ai) / den
    ci = (ni * ar - nr * ai) / den
    for h in range(b_re_ref.shape[0]):
        br, bi = b_re_ref[h], b_im_ref[h]
        bbar_re_ref[h] = cr * br - ci * bi
        bbar_im_ref[h] = cr * bi + ci * br
    pr, pi = lr, li
    for j in range(lam_re_ref.shape[0]):
        lam_re_ref[j] = pr
        lam_im_ref[j] = pi
        if j == chunk - 1:
            cr, ci = pr, pi
        pr, pi = pr * lr - pi * li, pr * li + pi * lr
    pr, pi = cr, ci
    for j in range(step_re_ref.shape[0]):
        step_re_ref[j] = pr
        step_im_ref[j] = pi
        pr, pi = pr * cr - pi * ci, pr * ci + pi * cr


def _ssm_prep(a_re, a_im, log_dt, b_re, b_im, n_pow, chunk):
    g, p = a_re.shape
    h = b_re.shape[-1]
    assert chunk <= n_pow
    b_re_t = jnp.transpose(b_re, (2, 0, 1))
    b_im_t = jnp.transpose(b_im, (2, 0, 1))
    gp = jax.ShapeDtypeStruct((n_pow, g, p), F32)
    hgp = jax.ShapeDtypeStruct((h, g, p), F32)
    sgp = jax.ShapeDtypeStruct((SSM_CHAIN_GROUP, g, p), F32)
    return pl.pallas_call(
        functools.partial(_ssm_prep_body, chunk=chunk),
        out_shape=[gp, gp, hgp, hgp, sgp, sgp],
        name="ssm_prep",
    )(a_re, a_im, log_dt.reshape(g, 1), b_re_t, b_im_t)


def _ssm_layouts(lam_re, lam_im, bbar_re_t, bbar_im_t, step_re, step_im, c_re, c_im, d_skip):
    n_pow, g, p = lam_re.shape
    h = bbar_re_t.shape[0]
    nq = g // SSM_GB
    eye = jnp.eye(SSM_GB, dtype=bool)
    bb = jnp.stack([bbar_re_t, bbar_im_t], 0).reshape(2, h, nq, SSM_GB, p)
    bb = jnp.transpose(bb, (2, 3, 1, 0, 4))
    bmat = jnp.where(eye[None, :, None, None, :, None], bb[:, :, :, :, None, :], 0.0)
    bmat = bmat.reshape(nq, SSM_GB * h, 2 * SSM_GB * p).astype(BF16)
    def c_layout(c):
        cc = jnp.transpose(c.reshape(nq, SSM_GB, h, p), (0, 1, 3, 2))
        m = jnp.where(eye[None, :, None, :, None], cc[:, :, :, None, :], 0.0)
        return m.reshape(nq, SSM_GB * p, SSM_GB * h).astype(BF16)
    per_block = lambda t: t.reshape(t.shape[0], nq, 1, SSM_GB * p)
    d_q = d_skip.reshape(nq, 1, SSM_GB * h)
    return (bmat, c_layout(c_re), c_layout(c_im), per_block(lam_re), per_block(lam_im),
            per_block(step_re), per_block(step_im), d_q)


def _ssm_body(u_ref, h0r_ref, h0i_ref, bmat_ref, cre_ref, cim_ref, lamr_ref, lami_ref, stepr_ref, stepi_ref, d_ref,
              y_ref, hfr_ref, hfi_ref, hloc_ref, hs_ref, ends_ref, grp_ref, *, t1, chained):
    rows = u_ref.shape[0]
    d_ssm = y_ref.shape[1] // t1
    nq = bmat_ref.shape[0]
    ns = lamr_ref.shape[-1]
    rt = SSM_ROW_TILE
    re, im = slice(0, ns), slice(ns, 2 * ns)

    def u_lanes(i, q):
        return slice(i * d_ssm + q * LANES, i * d_ssm + (q + 1) * LANES)

    def row_tiles(body):
        lax.fori_loop(0, rows // rt, lambda t, c: (body(pl.ds(pl.multiple_of(t * rt, rt), rt)), c)[1], 0,
                      unroll=True)

    def input_drive(q):
        for i in range(t1):
            hloc_ref[q % 2, i] = _dot(u_ref[:, u_lanes(i, q)].astype(BF16), bmat_ref[q])

    def local_scan(q):
        h = hloc_ref.at[q % 2]
        lr, li = lamr_ref[0, q], lami_ref[0, q]

        def tile(r):
            hr, hi = h[0, r, re], h[0, r, im]
            for i in range(1, t1):
                hr, hi = lr * hr - li * hi + h[i, r, re], lr * hi + li * hr + h[i, r, im]
                h[i, r, re] = hr
                h[i, r, im] = hi
        row_tiles(tile)

    nlb = ns // LANES

    def put_blocks(ref, rows_idx, vr, vi):
        for c in range(nlb):
            ref[c, rows_idx, :] = vr[:, c * LANES:(c + 1) * LANES]
            ref[nlb + c, rows_idx, :] = vi[:, c * LANES:(c + 1) * LANES]

    def get_blocks(ref, rows_idx):
        return (jnp.concatenate([ref[c, rows_idx, :] for c in range(nlb)], axis=1),
                jnp.concatenate([ref[nlb + c, rows_idx, :] for c in range(nlb)], axis=1))

    def chunk_starts(q):
        h = hloc_ref.at[q % 2]
        s_lanes = slice(q * ns, (q + 1) * ns)
        if not chained:
            put_blocks(hs_ref, slice(None), h0r_ref[:, s_lanes], h0i_ref[:, s_lanes])
            return
        grp = SSM_CHAIN_GROUP
        n_grp = rows // grp
        put_blocks(ends_ref, slice(None), h[t1 - 1, :, re], h[t1 - 1, :, im])
        member = lambda j: pl.ds(j, n_grp, stride=grp)
        cr, ci = stepr_ref[0, q], stepi_ref[0, q]
        gr, gi = get_blocks(ends_ref, member(0))
        inside = [(gr, gi)]
        for j in range(1, grp):
            er, ei = get_blocks(ends_ref, member(j))
            gr, gi = cr * gr - ci * gi + er, cr * gi + ci * gr + ei
            inside.append((gr, gi))
        grp_ref[0, :, re] = gr
        grp_ref[0, :, im] = gi
        wr, wi = stepr_ref[grp - 1, q], stepi_ref[grp - 1, q]

        def chain(c, g):
            sr, si = g
            row = pl.ds(c, 1)
            grp_ref[1, row, re] = sr
            grp_ref[1, row, im] = si
            return wr * sr - wi * si + grp_ref[0, row, re], wr * si + wi * sr + grp_ref[0, row, im]

        sr, si = lax.fori_loop(0, n_grp, chain, (h0r_ref[:, s_lanes], h0i_ref[:, s_lanes]))
        hfr_ref[:, s_lanes] = sr
        hfi_ref[:, s_lanes] = si
        sr, si = grp_ref[1, :, re], grp_ref[1, :, im]
        put_blocks(hs_ref, member(0), sr, si)
        for j in range(1, grp):
            pr, pi = stepr_ref[j - 1, q], stepi_ref[j - 1, q]
            gr, gi = inside[j - 1]
            put_blocks(hs_ref, member(j), gr + (pr * sr - pi * si), gi + (pr * si + pi * sr))

    def add_carry(q):
        h = hloc_ref.at[q % 2]

        def tile(r):
            hr0, hi0 = get_blocks(hs_ref, r)
            for i in range(t1):
                pr, pi = lamr_ref[i, q], lami_ref[i, q]
                h[i, r, re] = h[i, r, re] + (pr * hr0 - pi * hi0)
                h[i, r, im] = h[i, r, im] + (pr * hi0 + pi * hr0)
        row_tiles(tile)

    def readout(q):
        h = hloc_ref.at[q % 2]
        if not chained:
            s_lanes = slice(q * ns, (q + 1) * ns)
            hfr_ref[:, s_lanes] = h[t1 - 1, :, re]
            hfi_ref[:, s_lanes] = h[t1 - 1, :, im]
        for i in range(t1):
            y_ref[:, u_lanes(i, q)] = (_dot(h[i, :, re].astype(BF16), cre_ref[q])
                                       - _dot(h[i, :, im].astype(BF16), cim_ref[q])
                                       + d_ref[q] * u_ref[:, u_lanes(i, q)])

    input_drive(0)
    local_scan(0)
    chunk_starts(0)
    for q in range(nq):
        if q + 1 < nq:
            input_drive(q + 1)
        add_carry(q)
        readout(q)
        if q + 1 < nq:
            local_scan(q + 1)
            chunk_starts(q + 1)


def _ssm(u_chunks, h0_re, h0_im, layouts, t1, rows, chained):
    bmat, cre, cim, lam_re_q, lam_im_q, step_re_q, step_im_q, d_q = layouts
    lam_re_q, lam_im_q = lam_re_q[:t1], lam_im_q[:t1]
    n_blocks, rows_h, n_state = h0_re.shape
    width = u_chunks.shape[1]
    assert rows % SSM_ROW_TILE == 0 and u_chunks.shape[0] == n_blocks * rows
    assert not chained or rows % SSM_CHAIN_GROUP == 0
    ns2 = bmat.shape[2]
    u_spec = pl.BlockSpec((rows, width), lambda b: (b, 0))
    h_spec = pl.BlockSpec((None, rows_h, n_state), lambda b: (b, 0, 0))
    consts = [bmat, cre, cim, lam_re_q, lam_im_q, step_re_q, step_im_q, d_q]
    chain_rows = rows if chained else 8
    blocked = lambda r: pltpu.VMEM((ns2 // LANES, r, LANES), F32)
    vmem = (4 * rows * width * 4 + (2 * t1 + 1) * rows * ns2 * 4 + sum(c.size * c.dtype.itemsize for c in consts)
            + 8 * rows_h * n_state * 4 + 8 * 1024 * 1024)
    h_out = jax.ShapeDtypeStruct(h0_re.shape, F32)
    return pl.pallas_call(
        functools.partial(_ssm_body, t1=t1, chained=chained),
        grid=(n_blocks,),
        in_specs=[u_spec, h_spec, h_spec] + [_const_spec(c.shape) for c in consts],
        out_specs=[u_spec, h_spec, h_spec],
        out_shape=[jax.ShapeDtypeStruct(u_chunks.shape, F32), h_out, h_out],
        scratch_shapes=[pltpu.VMEM((2, t1, rows, ns2), F32), blocked(rows), blocked(chain_rows),
                        pltpu.VMEM((2, chain_rows // SSM_CHAIN_GROUP, ns2), F32)],
        compiler_params=_params(("parallel",), vmem),
        name="ssm_chained" if chained else "ssm_rows",
    )(u_chunks, h0_re, h0_im, *consts)


def _mix_body(x_ref, oatt_ref, yc_ref, wglu_ref, bglu_ref, gatt_ref, gssm_ref, wout_ref, lng_ref, lnb_ref,
              o_ref, y_scr, *, alpha, t1):
    d_att = oatt_ref.shape[1]
    n_lane_blocks, rows, _ = y_scr.shape
    d_ssm = n_lane_blocks * LANES
    chunks = rows // t1
    for c in range(n_lane_blocks):
        for i in range(t1):
            y_scr[c, pl.ds(i, chunks, stride=t1), :] = yc_ref[:, i * d_ssm + c * LANES:i * d_ssm + (c + 1) * LANES]
    parts = [pl.ds(i * (rows // MIX_ROW_SPLITS), rows // MIX_ROW_SPLITS) for i in range(MIX_ROW_SPLITS)]
    gs = [jax.nn.gelu(jnp.concatenate([y_scr[c, r, :] for c in range(n_lane_blocks)], axis=1)) for r in parts]
    gates = [_dot(g.astype(BF16), wglu_ref[...]) for g in gs]
    ras = [_rms_norm(oatt_ref[r, :], gatt_ref[...]).astype(BF16) for r in parts]
    att_parts = [_dot(ra, wout_ref[0:d_att, :]) for ra in ras]
    rzs = [_rms_norm(g * jax.nn.sigmoid(gate + bglu_ref[...]), gssm_ref[...]).astype(BF16)
           for g, gate in zip(gs, gates)]
    for r, att, rz in zip(parts, att_parts, rzs):
        mixed = att + _dot(rz, wout_ref[d_att:, :])
        o_ref[r, :] = _layer_norm(alpha * x_ref[r, :] + mixed, lng_ref[...], lnb_ref[...])


def _mix(x2d, o_att, y_chunks, w_glu, b_glu, g_att, g_ssm, w_out, ln_g, ln_b, alpha, tm, t1):
    rows, dm = x2d.shape
    d_att, d_ssm = o_att.shape[1], y_chunks.shape[1] // t1
    consts = [w_glu, b_glu.reshape(1, -1), g_att.reshape(1, -1), g_ssm.reshape(1, -1), w_out,
              ln_g.reshape(1, -1), ln_b.reshape(1, -1)]
    row = lambda width: pl.BlockSpec((tm, width), lambda i: (i, 0))
    chunk_spec = pl.BlockSpec((tm // t1, t1 * d_ssm), lambda i: (i, 0))
    vmem = 2 * tm * (2 * dm + d_att + d_ssm) * 4 + 4 * (w_glu.size + w_out.size) + 12 * tm * dm * 4
    return pl.pallas_call(
        functools.partial(_mix_body, alpha=alpha, t1=t1),
        grid=(rows // tm,),
        in_specs=[row(dm), row(d_att), chunk_spec] + [_const_spec(c.shape) for c in consts],
        out_specs=row(dm),
        out_shape=jax.ShapeDtypeStruct(x2d.shape, F32),
        scratch_shapes=[pltpu.VMEM((d_ssm // LANES, tm, LANES), F32)],
        compiler_params=_params(("parallel",), vmem),
        name="mix_out_ln1",
    )(x2d, o_att, y_chunks, *consts)


def _memkv_body(m_ref, wk_ref, wv_ref, k_ref, v_ref, kh_ref, vh_ref):
    mb = m_ref[...].astype(BF16)
    hd = kh_ref.shape[2]
    for w_ref, o_ref, oh_ref in ((wk_ref, k_ref, kh_ref), (wv_ref, v_ref, vh_ref)):
        val = _dot(mb, w_ref[...])
        o_ref[...] = val
        for h in range(oh_ref.shape[1]):
            oh_ref[:, h, :] = val[:, h * hd:(h + 1) * hd]


def _memkv(mem2d, wk, wv, tm):
    rows, dm = mem2d.shape
    hd = dm // N_MEM_HEADS
    row = pl.BlockSpec((tm, dm), lambda i: (i, 0))
    row_h = pl.BlockSpec((tm, N_MEM_HEADS, hd), lambda i: (i, 0, 0))
    out = jax.ShapeDtypeStruct((rows, wk.shape[1]), F32)
    out_h = jax.ShapeDtypeStruct((rows, N_MEM_HEADS, hd), F32)
    vmem = 2 * 5 * tm * dm * 4 + 2 * (wk.size + wv.size) + 4 * tm * dm * 4
    return pl.pallas_call(
        _memkv_body,
        grid=(rows // tm,),
        in_specs=[row, _const_spec(wk.shape), _const_spec(wv.shape)],
        out_specs=[row, row, row_h, row_h],
        out_shape=[out, out, out_h, out_h],
        compiler_params=_params(("parallel",), vmem),
        name="mem_kv",
    )(mem2d, wk, wv)


def _memattn_body(x_ref, mk_ref, mv_ref, wq_ref, wo_ref, lng_ref, lnb_ref, o_ref, *, alpha):
    tm, dm = x_ref.shape
    hd = dm // N_MEM_HEADS
    heads = [slice(h * hd, (h + 1) * hd) for h in range(N_MEM_HEADS)]
    parts = [pl.ds(i * (tm // MEM_ROW_SPLITS), tm // MEM_ROW_SPLITS) for i in range(MEM_ROW_SPLITS)]
    mk = [mk_ref[:, c].astype(BF16) for c in heads]
    mv = [mv_ref[:, c].astype(BF16) for c in heads]
    xs = [x_ref[r, :] for r in parts]
    qs = [(_dot(x.astype(BF16), wq_ref[...]) * (hd ** -0.5)).astype(BF16) for x in xs]
    scores = [[_dot_nt(q[:, c], k) for c, k in zip(heads, mk)] for q in qs]
    probs = [[jnp.exp(s - jnp.max(s, axis=1, keepdims=True)) for s in sp] for sp in scores]
    sums = [[jnp.sum(p, axis=1, keepdims=True) for p in pp] for pp in probs]
    outs = [[_dot(p.astype(BF16), v) / l for p, l, v in zip(pp, lp, mv)] for pp, lp in zip(probs, sums)]
    atts = [_dot(jnp.concatenate(op, axis=1).astype(BF16), wo_ref[...]) for op in outs]
    for r, x, att in zip(parts, xs, atts):
        o_ref[r, :] = _layer_norm(alpha * x + att, lng_ref[...], lnb_ref[...])


def _memattn(x2d, mem_k, mem_v, wq, wo, ln_g, ln_b, alpha, tm):
    rows, dm = x2d.shape
    n_seq, n_mem, _ = mem_k.shape
    steps_per_mem = rows // n_seq // tm
    row = pl.BlockSpec((tm, dm), lambda i: (i, 0))
    mem_spec = pl.BlockSpec((None, n_mem, dm), lambda i: (i // steps_per_mem, 0, 0))
    consts = [wq, wo, ln_g.reshape(1, -1), ln_b.reshape(1, -1)]
    vmem = 2 * 2 * tm * dm * 4 + 2 * 2 * n_mem * dm * 4 + 2 * (wq.size + wo.size) + 12 * tm * dm * 4
    return pl.pallas_call(
        functools.partial(_memattn_body, alpha=alpha),
        grid=(rows // tm,),
        in_specs=[row, mem_spec, mem_spec] + [_const_spec(c.shape) for c in consts],
        out_specs=row,
        out_shape=jax.ShapeDtypeStruct(x2d.shape, F32),
        compiler_params=_params(("parallel",), vmem),
        name="mem_attn_ln2",
    )(x2d, mem_k, mem_v, *consts)


def _rows_matmul_body(x_ref, w_ref, o_ref, *, scale):
    o_ref[...] = _dot(x_ref[...].astype(BF16), w_ref[...]) * scale


def _rows_matmul(x2d, w, scale):
    rows, dm = x2d.shape
    vmem = 4 * rows * (dm + w.shape[1]) * 4 + 4 * w.size
    return pl.pallas_call(
        functools.partial(_rows_matmul_body, scale=scale),
        out_shape=jax.ShapeDtypeStruct((rows, w.shape[1]), F32),
        compiler_params=_params(None, vmem),
        name="rows_matmul",
    )(x2d, w)


def _memattn_cache_probs(q_ref, mk_ref):
    nb, rows, hd = q_ref.shape
    n_mem, nh = mk_ref.shape[1:3]
    cols = n_mem * nh
    head_ok = ((lax.broadcasted_iota(jnp.int32, (rows, cols), 0) & (nh - 1))
               == (lax.broadcasted_iota(jnp.int32, (rows, cols), 1) & (nh - 1)))
    scores = [jnp.where(head_ok, _dot_nt(q_ref[j].astype(BF16), mk_ref[j].reshape(cols, hd).astype(BF16)), NEG)
              for j in range(nb)]
    probs = [jnp.exp(s - jnp.max(s, axis=1, keepdims=True)) for s in scores]
    return [(p.astype(BF16), jnp.sum(p, axis=1, keepdims=True)) for p in probs]


def _memattn_cache_output(probs, mv_ref, o_ref):
    n_mem, nh, hd = mv_ref.shape[1:]
    for j, (p, l) in enumerate(probs):
        o_ref[j] = _dot(p, mv_ref[j].reshape(n_mem * nh, hd).astype(BF16)) / l


def _memattn_cache_operands(q3, cache_k, cache_v, layer, nb):
    bsz, rows, hd = q3.shape
    _, _, n_mem, nh, _ = cache_k.shape
    q_spec = pl.BlockSpec((nb, rows, hd), lambda i: (i, 0, 0))
    c_spec = pl.BlockSpec((None, nb, n_mem, nh, hd), lambda i: (layer, i, 0, 0, 0))
    vmem = 2 * 2 * nb * n_mem * 8 * hd * 4 + 8 * nb * rows * n_mem * nh * 4
    return [q3, cache_k, cache_v], [q_spec, c_spec, c_spec], q_spec, jax.ShapeDtypeStruct(q3.shape, F32), vmem


def _proj_ln_body(x_ref, a_ref, w_ref, lng_ref, lnb_ref, o_ref, *, alpha):
    att = _dot(a_ref[...].astype(BF16), w_ref[...])
    o_ref[...] = _layer_norm(alpha * x_ref[...] + att, lng_ref[...], lnb_ref[...])


def _proj_ln(x2d, a2d, w, ln_g, ln_b, alpha):
    rows, dm = x2d.shape
    vmem = 8 * rows * dm * 4 + 4 * w.size
    return pl.pallas_call(
        functools.partial(_proj_ln_body, alpha=alpha),
        out_shape=jax.ShapeDtypeStruct(x2d.shape, F32),
        compiler_params=_params(None, vmem),
        name="proj_ln",
    )(x2d, a2d, w, ln_g.reshape(1, -1), ln_b.reshape(1, -1))


def _ffn_body(x_ref, wg_ref, wu_ref, wd_ref, lng_ref, lnb_ref, *rest, alpha, tf, hosts_guest):
    if hosts_guest:
        gq_ref, gk_ref, gv_ref, o_ref, go_ref, acc_ref = rest
        guest_probs = _memattn_cache_probs(gq_ref, gk_ref)
    else:
        o_ref, acc_ref = rest
    x = x_ref[...]
    xb = x.astype(BF16)
    n_chunks = wg_ref.shape[1] // tf
    cols = [slice(c * tf, (c + 1) * tf) for c in range(n_chunks)]

    def hidden(c):
        return (jax.nn.silu(_dot(xb, wg_ref[:, cols[c]])) * _dot(xb, wu_ref[:, cols[c]])).astype(BF16)

    hid = hidden(0)
    for c in range(n_chunks):
        nxt = hidden(c + 1) if c + 1 < n_chunks else None
        if hosts_guest and c == n_chunks // 2:
            _memattn_cache_output(guest_probs, gv_ref, go_ref)
        part = _dot(hid, wd_ref[cols[c], :])
        if c == 0:
            acc_ref[...] = part
        else:
            acc_ref[...] += part
        hid = nxt
    o_ref[...] = _layer_norm(alpha * x + acc_ref[...], lng_ref[...], lnb_ref[...])


def _ffn(x2d, wg, wu, wd, ln_g, ln_b, alpha, tm, tf, guest=None):
    rows, dm = x2d.shape
    d_ff = wg.shape[1]
    assert d_ff % tf == 0 and tf % LANES == 0 and rows % tm == 0
    row = pl.BlockSpec((tm, dm), lambda i: (i, 0))
    consts = [wg, wu, wd, ln_g.reshape(1, -1), ln_b.reshape(1, -1)]
    operands = [x2d, *consts]
    in_specs = [row] + [_const_spec(c.shape) for c in consts]
    out_specs, out_shape = row, jax.ShapeDtypeStruct(x2d.shape, F32)
    vmem = 2 * 2 * tm * dm * 4 + 2 * 3 * wg.size + tm * dm * 4 + 8 * tm * max(tf, dm) * 4
    if guest is not None:
        g_arrays, g_specs, g_out_spec, g_out_shape, g_vmem = guest
        assert g_arrays[0].shape[0] // g_specs[0].block_shape[0] == rows // tm
        operands, in_specs = operands + g_arrays, in_specs + g_specs
        out_specs, out_shape = [row, g_out_spec], [out_shape, g_out_shape]
        vmem += g_vmem
    return pl.pallas_call(
        functools.partial(_ffn_body, alpha=alpha, tf=tf, hosts_guest=guest is not None),
        grid=(rows // tm,),
        in_specs=in_specs,
        out_specs=out_specs,
        out_shape=out_shape,
        scratch_shapes=[pltpu.VMEM((tm, dm), F32)],
        compiler_params=_params(("parallel",), vmem),
        name="swiglu_ln3_hosting" if guest is not None else "swiglu_ln3",
    )(*operands)


PROMPT_CHUNK = 8
ROW_TILE = 512
FFN_COL_TILE = 256
SAMPLE_ATTN_SEQS = 2


def kernel(x_prompt, x_sample, cache_win_k, cache_win_v, state_ssm_re, state_ssm_im, cache_mem_k, cache_mem_v, mem_prompt, w_in, g_att, g_ssm, ssm_a_re, ssm_a_im, ssm_log_dt, ssm_b_re, ssm_b_im, ssm_c_re, ssm_c_im, ssm_d, w_glu, b_glu, w_out, ln1_g, ln1_b, w_mem_q, w_mem_k, w_mem_v, w_mem_o, ln2_g, ln2_b, w_gate, w_up, w_down, ln3_g, ln3_b):
    depth = w_in.shape[0]
    bp, seq, dm = x_prompt.shape
    bs, t_new, _ = x_sample.shape
    n_groups, n_state = ssm_a_re.shape[1:]
    n_mem = mem_prompt.shape[1]
    alpha = (2 * depth) ** 0.25
    keep = min(DILATED_CFGS[-1][0], seq)
    assert keep == seq
    n_chunks = seq // PROMPT_CHUNK

    y_p = x_prompt.reshape(bp * seq, dm)
    y_s = x_sample.reshape(bs * t_new, dm)
    mem2d = mem_prompt.reshape(bp * n_mem, dm)
    outs = [[] for _ in range(10)]
    for l in range(depth):
        bf = lambda w: w[l].astype(BF16)
        w_in_l, w_glu_l, w_out_l = bf(w_in), bf(w_glu), bf(w_out)
        wq_l, wk_l, wv_l, wo_l = bf(w_mem_q), bf(w_mem_k), bf(w_mem_v), bf(w_mem_o)
        wg_l, wu_l, wd_l = bf(w_gate), bf(w_up), bf(w_down)
        prepped = _ssm_prep(ssm_a_re[l], ssm_a_im[l], ssm_log_dt[l], ssm_b_re[l], ssm_b_im[l],
                            max(PROMPT_CHUNK, t_new), PROMPT_CHUNK)
        layouts = _ssm_layouts(*prepped, ssm_c_re[l], ssm_c_im[l], ssm_d[l])
        mix_w = (w_glu_l, b_glu[l], g_att[l], g_ssm[l], w_out_l, ln1_g[l], ln1_b[l])

        q, k, v, u, k_t, v_t = _inproj(y_p, w_in_l, ROW_TILE, PROMPT_CHUNK, seq)
        d_att = q.shape[1]
        o_att = _attn_prompt(q, k, v, seq)
        zeros = jnp.zeros((bp, 1, n_groups * n_state), F32)
        y_ssm, hr_p, hi_p = _ssm(u, zeros, zeros, layouts, PROMPT_CHUNK, n_chunks, True)
        x1 = _mix(y_p, o_att, y_ssm, *mix_w, alpha, ROW_TILE, PROMPT_CHUNK)
        mk_p, mv_p, mk_heads, mv_heads = _memkv(mem2d, wk_l, wv_l, ROW_TILE)
        x2 = _memattn(x1, mk_p.reshape(bp, n_mem, dm), mv_p.reshape(bp, n_mem, dm), wq_l, wo_l,
                      ln2_g[l], ln2_b[l], alpha, ROW_TILE)

        qs, ks, vs, us = _inproj(y_s, w_in_l, bs * t_new, t_new)
        o_att_s = _attn_window(qs, ks, vs, cache_win_k, cache_win_v, l, SAMPLE_ATTN_SEQS)
        y_ssm_s, hr_s, hi_s = _ssm(us, state_ssm_re[l].reshape(1, bs, -1), state_ssm_im[l].reshape(1, bs, -1),
                                   layouts, t_new, bs, False)
        x1s = _mix(y_s, o_att_s, y_ssm_s, *mix_w, alpha, bs * t_new, t_new)
        mem_hd = dm // N_MEM_HEADS
        q_mem = _rows_matmul(x1s, wq_l, mem_hd ** -0.5).reshape(bs, t_new * N_MEM_HEADS, mem_hd)
        ffn_steps = bp * seq // ROW_TILE
        guest = _memattn_cache_operands(q_mem, cache_mem_k, cache_mem_v, l, bs // ffn_steps)
        y_p, a_mem = _ffn(x2, wg_l, wu_l, wd_l, ln3_g[l], ln3_b[l], alpha, ROW_TILE, FFN_COL_TILE, guest)
        x2s = _proj_ln(x1s, a_mem.reshape(bs * t_new, dm), wo_l, ln2_g[l], ln2_b[l], alpha)
        y_s = _ffn(x2s, wg_l, wu_l, wd_l, ln3_g[l], ln3_b[l], alpha, bs * t_new, FFN_COL_TILE)

        head_shape = (N_ATT_HEADS, ATT_HEAD_DIM)
        state_shape = (n_groups, n_state)
        mem_shape = (bp, n_mem, N_MEM_HEADS, dm // N_MEM_HEADS)
        rows_first = lambda t: jnp.transpose(t.reshape(bp, *head_shape, seq), (0, 3, 1, 2))
        for lst, val in zip(outs, (
                rows_first(k_t)[:, seq - keep:], rows_first(v_t)[:, seq - keep:],
                ks.reshape(bs, t_new, *head_shape), vs.reshape(bs, t_new, *head_shape),
                hr_p.reshape(bp, *state_shape), hi_p.reshape(bp, *state_shape),
                hr_s.reshape(bs, *state_shape), hi_s.reshape(bs, *state_shape),
                mk_heads.reshape(mem_shape), mv_heads.reshape(mem_shape))):
            lst.append(val)
    return (y_p.reshape(bp, seq, dm), y_s.reshape(bs, t_new, dm)) + tuple(jnp.stack(o) for o in outs)
```

```python
import functools
import math

import jax
import jax.numpy as jnp
from jax import lax
from jax.experimental import pallas as pl
from jax.experimental.pallas import tpu as pltpu

F32 = jnp.float32
BF16 = jnp.bfloat16

N_ATT_HEADS = 8
ATT_HEAD_DIM = 64
D_ATT = N_ATT_HEADS * ATT_HEAD_DIM
DILATED_CFGS = ((128, 1), (512, 4), (2048, 16))
ATT_BLK = 128
ATT_GROUP = 4
SSM_CH = 16
SSM_STATE = 64
N_MEM_HEADS = 4
MEM_ROW_SPLITS = 2
MIX_ROW_SPLITS = 2
EPS = 1e-5
NEG = -1e30

LANES = 128
V7X_VMEM_CAP_BYTES = 56 * 1024 * 1024

SSM_GB = LANES // SSM_CH
SSM_ROW_TILE = 32
SSM_CHAIN_GROUP = 8


def _params(sem, vmem_bytes):
    return pltpu.CompilerParams(
        dimension_semantics=sem,
        vmem_limit_bytes=int(min(max(vmem_bytes, 16 * 1024 * 1024), V7X_VMEM_CAP_BYTES)),
    )


def _dot(a, b):
    return jnp.dot(a, b, preferred_element_type=F32)


def _dot_nt(a, b):
    return lax.dot_general(a, b, (((1,), (1,)), ((), ())), preferred_element_type=F32)


def _layer_norm(x, g, b):
    mu = jnp.mean(x, axis=-1, keepdims=True)
    xc = x - mu
    var = jnp.mean(xc * xc, axis=-1, keepdims=True)
    return xc * lax.rsqrt(var + EPS) * g + b


def _rms_norm(x, g):
    return x * lax.rsqrt(jnp.mean(x * x, axis=-1, keepdims=True) + EPS) * g


def _const_spec(shape):
    n = len(shape)
    return pl.BlockSpec(shape, lambda *_: (0,) * n, pipeline_mode=pl.Buffered(1))


def _inproj_body(x_ref, w_ref, q_ref, k_ref, v_ref, uc_ref, *rest, t1, transposed):
    u_scr = rest[-1]
    xb = x_ref[...].astype(BF16)
    d = q_ref.shape[1]
    q_ref[...] = _dot(xb, w_ref[:, 0 * d:1 * d])
    k = _dot(xb, w_ref[:, 1 * d:2 * d])
    v = _dot(xb, w_ref[:, 2 * d:3 * d])
    k_ref[...] = k
    v_ref[...] = v
    if transposed:
        kt_ref, vt_ref = rest[:2]
        kt_ref[...] = k.T
        vt_ref[...] = v.T
    u = _dot(xb, w_ref[:, 3 * d:4 * d])
    chunks = u_scr.shape[1] // t1
    for c in range(d // LANES):
        u_scr[c] = u[:, c * LANES:(c + 1) * LANES]
        for i in range(t1):
            uc_ref[:, i * d + c * LANES:i * d + (c + 1) * LANES] = u_scr[c, pl.ds(i, chunks, stride=t1), :]


def _inproj(x2d, w_in_bf16, tm, t1, seq=None):
    rows, dm = x2d.shape
    d = w_in_bf16.shape[1] // 4
    out = jax.ShapeDtypeStruct((rows, d), F32)
    row_spec = pl.BlockSpec((tm, d), lambda i: (i, 0))
    out_specs = [row_spec] * 3 + [pl.BlockSpec((tm // t1, t1 * d), lambda i: (i, 0))]
    out_shape = [out] * 3 + [jax.ShapeDtypeStruct((rows // t1, t1 * d), F32)]
    if seq is not None:
        steps = seq // tm
        out_specs += [pl.BlockSpec((None, d, tm), lambda i: (i // steps, 0, i % steps))] * 2
        out_shape += [jax.ShapeDtypeStruct((rows // seq, d, seq), F32)] * 2
    vmem = 2 * (tm * dm * 4 + 6 * tm * d * 4) + 2 * w_in_bf16.size * 2 + 10 * tm * d * 4
    return pl.pallas_call(
        functools.partial(_inproj_body, t1=t1, transposed=seq is not None),
        grid=(rows // tm,),
        in_specs=[pl.BlockSpec((tm, dm), lambda i: (i, 0)), _const_spec(w_in_bf16.shape)],
        out_specs=out_specs,
        out_shape=out_shape,
        scratch_shapes=[pltpu.VMEM((d // LANES, tm, LANES), F32)],
        compiler_params=_params(("parallel",), vmem),
        name="inproj",
    )(x2d, w_in_bf16)


def _attn_prompt_body(q_ref, k_ref, v_ref, o_ref, acc_ref, m_ref, l_ref):
    seq = q_ref.shape[0]
    blk = ATT_BLK
    scale = ATT_HEAD_DIM ** -0.5 * math.log2(math.e)
    lane = lax.broadcasted_iota(jnp.int32, (blk, LANES), 1)
    head0 = lane < ATT_HEAD_DIM
    qi = lax.broadcasted_iota(jnp.int32, (blk, blk), 0)
    kj = lax.broadcasted_iota(jnp.int32, (blk, blk), 1)
    tri = kj <= qi
    mask_rest = jnp.concatenate([kj >= qi, tri], axis=1)

    def rows_of(start, d):
        return pl.ds(start, blk) if d == 1 else pl.ds(start, blk, stride=d)

    def load_block(d, q_start, prev_start):
        rows = rows_of(q_start, d)
        q = q_ref[rows, :] * scale
        k2 = k_ref[rows, :].astype(BF16)
        v2 = v_ref[rows, :].astype(BF16)
        mask = tri
        if prev_start is not None:
            prow = rows_of(prev_start, d)
            k2 = jnp.concatenate([k_ref[prow, :].astype(BF16), k2], axis=0)
            v2 = jnp.concatenate([v_ref[prow, :].astype(BF16), v2], axis=0)
            mask = mask_rest
        qh = [jnp.where(head0 if h == 0 else jnp.logical_not(head0), q, 0.0).astype(BF16) for h in range(2)]
        return rows, qh, k2, v2, mask

    def attend_group(cfg, d, starts):
        blocks = [load_block(d, q_start, prev_start) for q_start, prev_start in starts]
        scores = [[jnp.where(mask, _dot_nt(qh[h], k2), NEG) for h in range(2)] for _, qh, k2, _, mask in blocks]
        maxes = [[jnp.max(s, axis=1, keepdims=True) for s in sb] for sb in scores]
        probs = [[jnp.exp2(s - m) for s, m in zip(sb, mb)] for sb, mb in zip(scores, maxes)]
        sums = [[jnp.sum(p, axis=1, keepdims=True) for p in pb] for pb in probs]
        pvs = [[_dot(p.astype(BF16), blk_[3]) for p in pb] for pb, blk_ in zip(probs, blocks)]
        new = [(blk_[0], jnp.where(head0, mb[0], mb[1]), jnp.where(head0, lb[0], lb[1]),
                jnp.where(head0, ob[0], ob[1])) for blk_, mb, lb, ob in zip(blocks, maxes, sums, pvs)]
        if cfg == 0:
            for rows, m, l, num in new:
                acc_ref[rows, :] = num
                m_ref[rows, :] = m
                l_ref[rows, :] = l
            return
        old = [(m_ref[rows, :], l_ref[rows, :], acc_ref[rows, :]) for rows, _, _, _ in new]
        for (rows, m, l, num), (m_old, l_old, acc_old) in zip(new, old):
            m_new = jnp.maximum(m_old, m)
            a = jnp.exp2(m_old - m_new)
            b = jnp.exp2(m - m_new)
            num = a * acc_old + b * num
            l = a * l_old + b * l
            if cfg == len(DILATED_CFGS) - 1:
                o_ref[rows, :] = num / l
            else:
                acc_ref[rows, :] = num
                m_ref[rows, :] = m_new
                l_ref[rows, :] = l

    for cfg, (window, d) in enumerate(DILATED_CFGS):
        span = d * blk
        blocks = [(r + n * span, r + (n - 1) * span if n else None) for n in range(seq // span) for r in range(d)]
        for g in range(0, len(blocks), ATT_GROUP):
            attend_group(cfg, d, blocks[g:g + ATT_GROUP])


def _attn_prompt(q, k, v, seq):
    rows, d_att = q.shape
    bsz = rows // seq
    assert d_att % LANES == 0 and LANES == 2 * ATT_HEAD_DIM
    for window, d in DILATED_CFGS:
        assert window // d == ATT_BLK and seq % (d * ATT_BLK) == 0
    spec = pl.BlockSpec((seq, LANES), lambda b, h: (b, h))
    blk_bytes = seq * LANES * 4
    return pl.pallas_call(
        _attn_prompt_body,
        grid=(bsz, d_att // LANES),
        in_specs=[spec, spec, spec],
        out_specs=spec,
        out_shape=jax.ShapeDtypeStruct(q.shape, F32),
        scratch_shapes=[pltpu.VMEM((seq, LANES), F32)] * 3,
        compiler_params=_params(("parallel", "parallel"), 11 * blk_bytes + 8 * 1024 * 1024),
        name="attn_prompt",
    )(q, k, v)


def _window_head_mask(d_att):
    nh = N_ATT_HEADS
    return (jnp.right_shift(lax.broadcasted_iota(jnp.int32, (nh, d_att), 1), ATT_HEAD_DIM.bit_length() - 1)
            == lax.broadcasted_iota(jnp.int32, (nh, d_att), 0))


def _window_probs(q, kn, kt):
    t_new, d_att = q.shape
    w_buf = kt.shape[1]
    nh = N_ATT_HEADS
    rows = t_new * nh
    (win1, _), (win4, dil4), (_, dil16) = DILATED_CFGS
    tail, near = win4, win1
    head_mask = _window_head_mask(d_att)
    trow = jnp.right_shift(lax.broadcasted_iota(jnp.int32, (rows, 1), 0), nh.bit_length() - 1)
    lane = lambda n: lax.broadcasted_iota(jnp.int32, (rows, n), 1)
    mask16 = (lane(w_buf) & (dil16 - 1)) == trow
    mask4 = (lane(tail) & (dil4 - 1)) == trow
    mask1 = lane(near) >= trow
    new_self = lane(t_new) == trow
    new_causal = lane(t_new) <= trow

    q = q * (ATT_HEAD_DIM ** -0.5)
    qbd = jnp.concatenate(
        [jnp.where(head_mask, jnp.broadcast_to(q[t:t + 1], (nh, d_att)), 0.0) for t in range(t_new)],
        axis=0).astype(BF16)
    s_all = _dot(qbd, kt.astype(BF16))
    s_new = _dot_nt(qbd, kn.astype(BF16))

    def softmax_parts(main, main_mask, new_mask):
        sm = jnp.where(main_mask, main, NEG)
        sn = jnp.where(new_mask, s_new, NEG)
        m = jnp.maximum(jnp.max(sm, axis=1, keepdims=True), jnp.max(sn, axis=1, keepdims=True))
        p = jnp.exp(sm - m)
        pn = jnp.exp(sn - m)
        den = jnp.sum(p, axis=1, keepdims=True) + jnp.sum(pn, axis=1, keepdims=True)
        return m, p, pn, den

    m16, p16, pn16, den16 = softmax_parts(s_all, mask16, new_self)
    m4, p4, pn4, den4 = softmax_parts(s_all[:, w_buf - tail:], mask4, new_self)
    m1, p1, pn1, den1 = softmax_parts(s_all[:, w_buf - near:], mask1, new_causal)
    m = jnp.maximum(jnp.maximum(m1, m4), m16)
    w1, w4, w16 = jnp.exp(m1 - m), jnp.exp(m4 - m), jnp.exp(m16 - m)
    den = w1 * den1 + w4 * den4 + w16 * den16
    p16, p4, p1 = w16 * p16, w4 * p4, w1 * p1
    p_all = jnp.concatenate(
        [p16[:, :w_buf - tail],
         p16[:, w_buf - tail:w_buf - near] + p4[:, :tail - near],
         p16[:, w_buf - near:] + p4[:, tail - near:] + p1], axis=1).astype(BF16)
    pn_all = (w16 * pn16 + w4 * pn4 + w1 * pn1).astype(BF16)
    return p_all, pn_all, den


def _window_output(p_all, pn_all, den, vn, vt):
    t_new, d_att = vn.shape
    nh = N_ATT_HEADS
    head_mask = _window_head_mask(d_att)
    out = (_dot_nt(p_all, vt.astype(BF16)) + _dot(pn_all, vn.astype(BF16))) / den
    return jnp.concatenate(
        [jnp.sum(jnp.where(head_mask, out[t * nh:(t + 1) * nh], 0.0), axis=0, keepdims=True)
         for t in range(t_new)], axis=0)


def _window_guest_probs(q_ref, kn_ref, kt_ref):
    nb = kt_ref.shape[0]
    t_new = q_ref.shape[0] // nb
    tok = [slice(b * t_new, (b + 1) * t_new) for b in range(nb)]
    return [_window_probs(q_ref[tok[b], :], kn_ref[tok[b], :], kt_ref[b]) for b in range(nb)]


def _window_guest_output(probs, vn_ref, vt_ref, o_ref):
    nb = vt_ref.shape[0]
    t_new = vn_ref.shape[0] // nb
    for b in range(nb):
        tok = slice(b * t_new, (b + 1) * t_new)
        o_ref[tok, :] = _window_output(*probs[b], vn_ref[tok, :], vt_ref[b])


def _window_guest_operands(q, k_new, v_new, cache_k, cache_v, layer, nb):
    depth, bsz, w_buf, nh, hd = cache_k.shape
    t_new, d_att = q.shape[0] // bsz, q.shape[1]
    assert w_buf == DILATED_CFGS[-1][0] and t_new <= DILATED_CFGS[1][1] and nh * hd == d_att
    assert (nb * t_new) % 8 == 0 and bsz % nb == 0
    rows_last = lambda c: jnp.transpose(c, (0, 1, 3, 4, 2)).reshape(depth, bsz, d_att, w_buf)
    new_spec = pl.BlockSpec((nb * t_new, d_att), lambda i: (i, 0))
    cache_spec = pl.BlockSpec((None, nb, d_att, w_buf), lambda i: (layer, i, 0, 0))
    vmem = 2 * 2 * nb * d_att * w_buf * 4 + 4 * nb * d_att * w_buf * 2 + 4 * 1024 * 1024
    return ([q, k_new, v_new, rows_last(cache_k), rows_last(cache_v)],
            [new_spec, new_spec, new_spec, cache_spec, cache_spec], new_spec,
            jax.ShapeDtypeStruct(q.shape, F32), vmem, bsz // nb)


def _ssm_prep_body(a_re_ref, a_im_ref, log_dt_ref, b_re_ref, b_im_ref, lam_re_ref, lam_im_ref,
                   bbar_re_ref, bbar_im_ref, step_re_ref, step_im_ref, *, chunk):
    ar = a_re_ref[...]
    ai = a_im_ref[...]
    dt = jnp.exp(log_dt_ref[...])
    mag = jnp.exp(dt * ar)
    lr = mag * jnp.cos(dt * ai)
    li = mag * jnp.sin(dt * ai)
    den = ar * ar + ai * ai
    nr, ni = lr - 1.0, li
    cr = (nr * ar + ni * ai) / den
    ci = (ni * ar - nr * ai) / den
    for h in range(b_re_ref.shape[0]):
        br, bi = b_re_ref[h], b_im_ref[h]
        bbar_re_ref[h] = cr * br - ci * bi
        bbar_im_ref[h] = cr * bi + ci * br
    pr, pi = lr, li
    for j in range(lam_re_ref.shape[0]):
        lam_re_ref[j] = pr
        lam_im_ref[j] = pi
        if j == chunk - 1:
            cr, ci = pr, pi
        pr, pi = pr * lr - pi * li, pr * li + pi * lr
    pr, pi = cr, ci
    for j in range(step_re_ref.shape[0]):
        step_re_ref[j] = pr
        step_im_ref[j] = pi
        pr, pi = pr * cr - pi * ci, pr * ci + pi * cr


def _ssm_prep(a_re, a_im, log_dt, b_re, b_im, n_pow, chunk):
    g, p = a_re.shape
    h = b_re.shape[-1]
    assert chunk <= n_pow
    b_re_t = jnp.transpose(b_re, (2, 0, 1))
    b_im_t = jnp.transpose(b_im, (2, 0, 1))
    gp = jax.ShapeDtypeStruct((n_pow, g, p), F32)
    hgp = jax.ShapeDtypeStruct((h, g, p), F32)
    sgp = jax.ShapeDtypeStruct((SSM_CHAIN_GROUP, g, p), F32)
    return pl.pallas_call(
        functools.partial(_ssm_prep_body, chunk=chunk),
        out_shape=[gp, gp, hgp, hgp, sgp, sgp],
        name="ssm_prep",
    )(a_re, a_im, log_dt.reshape(g, 1), b_re_t, b_im_t)


def _ssm_layouts(lam_re, lam_im, bbar_re_t, bbar_im_t, step_re, step_im, c_re, c_im, d_skip):
    n_pow, g, p = lam_re.shape
    h = bbar_re_t.shape[0]
    nq = g // SSM_GB
    eye = jnp.eye(SSM_GB, dtype=bool)
    bb = jnp.stack([bbar_re_t, bbar_im_t], 0).reshape(2, h, nq, SSM_GB, p)
    bb = jnp.transpose(bb, (2, 3, 1, 0, 4))
    bmat = jnp.where(eye[None, :, None, None, :, None], bb[:, :, :, :, None, :], 0.0)
    bmat = bmat.reshape(nq, SSM_GB * h, 2 * SSM_GB * p).astype(BF16)
    def c_layout(c):
        cc = jnp.transpose(c.reshape(nq, SSM_GB, h, p), (0, 1, 3, 2))
        m = jnp.where(eye[None, :, None, :, None], cc[:, :, :, None, :], 0.0)
        return m.reshape(nq, SSM_GB * p, SSM_GB * h).astype(BF16)
    per_block = lambda t: t.reshape(t.shape[0], nq, 1, SSM_GB * p)
    d_q = d_skip.reshape(nq, 1, SSM_GB * h)
    return (bmat, c_layout(c_re), c_layout(c_im), per_block(lam_re), per_block(lam_im),
            per_block(step_re), per_block(step_im), d_q)


def _ssm_body(u_ref, h0r_ref, h0i_ref, bmat_ref, cre_ref, cim_ref, lamr_ref, lami_ref, stepr_ref, stepi_ref, d_ref,
              y_ref, hfr_ref, hfi_ref, hloc_ref, hs_ref, ends_ref, grp_ref, *, t1, chained):
    rows = u_ref.shape[0]
    d_ssm = y_ref.shape[1] // t1
    nq = bmat_ref.shape[0]
    ns = lamr_ref.shape[-1]
    rt = SSM_ROW_TILE
    re, im = slice(0, ns), slice(ns, 2 * ns)

    def u_lanes(i, q):
        return slice(i * d_ssm + q * LANES, i * d_ssm + (q + 1) * LANES)

    def row_tiles(body):
        lax.fori_loop(0, rows // rt, lambda t, c: (body(pl.ds(pl.multiple_of(t * rt, rt), rt)), c)[1], 0,
                      unroll=True)

    def input_drive(q):
        for i in range(t1):
            hloc_ref[q % 2, i] = _dot(u_ref[:, u_lanes(i, q)].astype(BF16), bmat_ref[q])

    def local_scan(q):
        h = hloc_ref.at[q % 2]
        lr, li = lamr_ref[0, q], lami_ref[0, q]

        def tile(r):
            hr, hi = h[0, r, re], h[0, r, im]
            for i in range(1, t1):
                hr, hi = lr * hr - li * hi + h[i, r, re], lr * hi + li * hr + h[i, r, im]
                h[i, r, re] = hr
                h[i, r, im] = hi
        row_tiles(tile)

    nlb = ns // LANES

    def put_blocks(ref, rows_idx, vr, vi):
        for c in range(nlb):
            ref[c, rows_idx, :] = vr[:, c * LANES:(c + 1) * LANES]
            ref[nlb + c, rows_idx, :] = vi[:, c * LANES:(c + 1) * LANES]

    def get_blocks(ref, rows_idx):
        return (jnp.concatenate([ref[c, rows_idx, :] for c in range(nlb)], axis=1),
                jnp.concatenate([ref[nlb + c, rows_idx, :] for c in range(nlb)], axis=1))

    def chunk_starts(q):
        h = hloc_ref.at[q % 2]
        s_lanes = slice(q * ns, (q + 1) * ns)
        if not chained:
            put_blocks(hs_ref, slice(None), h0r_ref[:, s_lanes], h0i_ref[:, s_lanes])
            return
        grp = SSM_CHAIN_GROUP
        n_grp = rows // grp
        put_blocks(ends_ref, slice(None), h[t1 - 1, :, re], h[t1 - 1, :, im])
        member = lambda j: pl.ds(j, n_grp, stride=grp)
        cr, ci = stepr_ref[0, q], stepi_ref[0, q]
        gr, gi = get_blocks(ends_ref, member(0))
        inside = [(gr, gi)]
        for j in range(1, grp):
            er, ei = get_blocks(ends_ref, member(j))
            gr, gi = cr * gr - ci * gi + er, cr * gi + ci * gr + ei
            inside.append((gr, gi))
        grp_ref[0, :, re] = gr
        grp_ref[0, :, im] = gi
        wr, wi = stepr_ref[grp - 1, q], stepi_ref[grp - 1, q]

        def chain(c, g):
            sr, si = g
            row = pl.ds(c, 1)
            grp_ref[1, row, re] = sr
            grp_ref[1, row, im] = si
            return wr * sr - wi * si + grp_ref[0, row, re], wr * si + wi * sr + grp_ref[0, row, im]

        sr, si = lax.fori_loop(0, n_grp, chain, (h0r_ref[:, s_lanes], h0i_ref[:, s_lanes]))
        hfr_ref[:, s_lanes] = sr
        hfi_ref[:, s_lanes] = si
        sr, si = grp_ref[1, :, re], grp_ref[1, :, im]
        put_blocks(hs_ref, member(0), sr, si)
        for j in range(1, grp):
            pr, pi = stepr_ref[j - 1, q], stepi_ref[j - 1, q]
            gr, gi = inside[j - 1]
            put_blocks(hs_ref, member(j), gr + (pr * sr - pi * si), gi + (pr * si + pi * sr))

    def add_carry(q):
        h = hloc_ref.at[q % 2]

        def tile(r):
            hr0, hi0 = get_blocks(hs_ref, r)
            for i in range(t1):
                pr, pi = lamr_ref[i, q], lami_ref[i, q]
                h[i, r, re] = h[i, r, re] + (pr * hr0 - pi * hi0)
                h[i, r, im] = h[i, r, im] + (pr * hi0 + pi * hr0)
        row_tiles(tile)

    def readout(q):
        h = hloc_ref.at[q % 2]
        if not chained:
            s_lanes = slice(q * ns, (q + 1) * ns)
            hfr_ref[:, s_lanes] = h[t1 - 1, :, re]
            hfi_ref[:, s_lanes] = h[t1 - 1, :, im]
        for i in range(t1):
            y_ref[:, u_lanes(i, q)] = (_dot(h[i, :, re].astype(BF16), cre_ref[q])
                                       - _dot(h[i, :, im].astype(BF16), cim_ref[q])
                                       + d_ref[q] * u_ref[:, u_lanes(i, q)])

    input_drive(0)
    local_scan(0)
    chunk_starts(0)
    for q in range(nq):
        if q + 1 < nq:
            input_drive(q + 1)
        add_carry(q)
        readout(q)
        if q + 1 < nq:
            local_scan(q + 1)
            chunk_starts(q + 1)


def _ssm(u_chunks, h0_re, h0_im, layouts, t1, rows, chained):
    bmat, cre, cim, lam_re_q, lam_im_q, step_re_q, step_im_q, d_q = layouts
    lam_re_q, lam_im_q = lam_re_q[:t1], lam_im_q[:t1]
    n_blocks, rows_h, n_state = h0_re.shape
    width = u_chunks.shape[1]
    assert rows % SSM_ROW_TILE == 0 and u_chunks.shape[0] == n_blocks * rows
    assert not chained or rows % SSM_CHAIN_GROUP == 0
    ns2 = bmat.shape[2]
    u_spec = pl.BlockSpec((rows, width), lambda b: (b, 0))
    h_spec = pl.BlockSpec((None, rows_h, n_state), lambda b: (b, 0, 0))
    consts = [bmat, cre, cim, lam_re_q, lam_im_q, step_re_q, step_im_q, d_q]
    chain_rows = rows if chained else 8
    blocked = lambda r: pltpu.VMEM((ns2 // LANES, r, LANES), F32)
    vmem = (4 * rows * width * 4 + (2 * t1 + 1) * rows * ns2 * 4 + sum(c.size * c.dtype.itemsize for c in consts)
            + 8 * rows_h * n_state * 4 + 8 * 1024 * 1024)
    h_out = jax.ShapeDtypeStruct(h0_re.shape, F32)
    return pl.pallas_call(
        functools.partial(_ssm_body, t1=t1, chained=chained),
        grid=(n_blocks,),
        in_specs=[u_spec, h_spec, h_spec] + [_const_spec(c.shape) for c in consts],
        out_specs=[u_spec, h_spec, h_spec],
        out_shape=[jax.ShapeDtypeStruct(u_chunks.shape, F32), h_out, h_out],
        scratch_shapes=[pltpu.VMEM((2, t1, rows, ns2), F32), blocked(rows), blocked(chain_rows),
                        pltpu.VMEM((2, chain_rows // SSM_CHAIN_GROUP, ns2), F32)],
        compiler_params=_params(("parallel",), vmem),
        name="ssm_chained" if chained else "ssm_rows",
    )(u_chunks, h0_re, h0_im, *consts)


def _mix_body(x_ref, oatt_ref, yc_ref, wglu_ref, bglu_ref, gatt_ref, gssm_ref, wout_ref, lng_ref, lnb_ref,
              o_ref, y_scr, *, alpha, t1):
    d_att = oatt_ref.shape[1]
    n_lane_blocks, rows, _ = y_scr.shape
    d_ssm = n_lane_blocks * LANES
    chunks = rows // t1
    for c in range(n_lane_blocks):
        for i in range(t1):
            y_scr[c, pl.ds(i, chunks, stride=t1), :] = yc_ref[:, i * d_ssm + c * LANES:i * d_ssm + (c + 1) * LANES]
    parts = [pl.ds(i * (rows // MIX_ROW_SPLITS), rows // MIX_ROW_SPLITS) for i in range(MIX_ROW_SPLITS)]
    gs = [jax.nn.gelu(jnp.concatenate([y_scr[c, r, :] for c in range(n_lane_blocks)], axis=1)) for r in parts]
    gates = [_dot(g.astype(BF16), wglu_ref[...]) for g in gs]
    ras = [_rms_norm(oatt_ref[r, :], gatt_ref[...]).astype(BF16) for r in parts]
    att_parts = [_dot(ra, wout_ref[0:d_att, :]) for ra in ras]
    rzs = [_rms_norm(g * jax.nn.sigmoid(gate + bglu_ref[...]), gssm_ref[...]).astype(BF16)
           for g, gate in zip(gs, gates)]
    for r, att, rz in zip(parts, att_parts, rzs):
        mixed = att + _dot(rz, wout_ref[d_att:, :])
        o_ref[r, :] = _layer_norm(alpha * x_ref[r, :] + mixed, lng_ref[...], lnb_ref[...])


def _mix(x2d, o_att, y_chunks, w_glu, b_glu, g_att, g_ssm, w_out, ln_g, ln_b, alpha, tm, t1):
    rows, dm = x2d.shape
    d_att, d_ssm = o_att.shape[1], y_chunks.shape[1] // t1
    consts = [w_glu, b_glu.reshape(1, -1), g_att.reshape(1, -1), g_ssm.reshape(1, -1), w_out,
              ln_g.reshape(1, -1), ln_b.reshape(1, -1)]
    row = lambda width: pl.BlockSpec((tm, width), lambda i: (i, 0))
    chunk_spec = pl.BlockSpec((tm // t1, t1 * d_ssm), lambda i: (i, 0))
    vmem = 2 * tm * (2 * dm + d_att + d_ssm) * 4 + 4 * (w_glu.size + w_out.size) + 12 * tm * dm * 4
    return pl.pallas_call(
        functools.partial(_mix_body, alpha=alpha, t1=t1),
        grid=(rows // tm,),
        in_specs=[row(dm), row(d_att), chunk_spec] + [_const_spec(c.shape) for c in consts],
        out_specs=row(dm),
        out_shape=jax.ShapeDtypeStruct(x2d.shape, F32),
        scratch_shapes=[pltpu.VMEM((d_ssm // LANES, tm, LANES), F32)],
        compiler_params=_params(("parallel",), vmem),
        name="mix_out_ln1",
    )(x2d, o_att, y_chunks, *consts)


def _memkv_body(m_ref, wk_ref, wv_ref, k_ref, v_ref, kh_ref, vh_ref):
    mb = m_ref[...].astype(BF16)
    hd = kh_ref.shape[2]
    for w_ref, o_ref, oh_ref in ((wk_ref, k_ref, kh_ref), (wv_ref, v_ref, vh_ref)):
        val = _dot(mb, w_ref[...])
        o_ref[...] = val
        for h in range(oh_ref.shape[1]):
            oh_ref[:, h, :] = val[:, h * hd:(h + 1) * hd]


def _memkv(mem2d, wk, wv, tm):
    rows, dm = mem2d.shape
    hd = dm // N_MEM_HEADS
    row = pl.BlockSpec((tm, dm), lambda i: (i, 0))
    row_h = pl.BlockSpec((tm, N_MEM_HEADS, hd), lambda i: (i, 0, 0))
    out = jax.ShapeDtypeStruct((rows, wk.shape[1]), F32)
    out_h = jax.ShapeDtypeStruct((rows, N_MEM_HEADS, hd), F32)
    vmem = 2 * 5 * tm * dm * 4 + 2 * (wk.size + wv.size) + 4 * tm * dm * 4
    return pl.pallas_call(
        _memkv_body,
        grid=(rows // tm,),
        in_specs=[row, _const_spec(wk.shape), _const_spec(wv.shape)],
        out_specs=[row, row, row_h, row_h],
        out_shape=[out, out, out_h, out_h],
        compiler_params=_params(("parallel",), vmem),
        name="mem_kv",
    )(mem2d, wk, wv)


def _memattn_body(x_ref, mk_ref, mv_ref, wq_ref, wo_ref, lng_ref, lnb_ref,
                  gq_ref, gkn_ref, gvn_ref, gkt_ref, gvt_ref, o_ref, go_ref, *, alpha):
    guest_probs = _window_guest_probs(gq_ref, gkn_ref, gkt_ref)
    tm, dm = x_ref.shape
    hd = dm // N_MEM_HEADS
    heads = [slice(h * hd, (h + 1) * hd) for h in range(N_MEM_HEADS)]
    parts = [pl.ds(i * (tm // MEM_ROW_SPLITS), tm // MEM_ROW_SPLITS) for i in range(MEM_ROW_SPLITS)]
    mk = [mk_ref[:, c].astype(BF16) for c in heads]
    mv = [mv_ref[:, c].astype(BF16) for c in heads]
    xs = [x_ref[r, :] for r in parts]
    qs = [(_dot(x.astype(BF16), wq_ref[...]) * (hd ** -0.5)).astype(BF16) for x in xs]
    scores = [[_dot_nt(q[:, c], k) for c, k in zip(heads, mk)] for q in qs]
    _window_guest_output(guest_probs, gvn_ref, gvt_ref, go_ref)
    probs = [[jnp.exp(s - jnp.max(s, axis=1, keepdims=True)) for s in sp] for sp in scores]
    sums = [[jnp.sum(p, axis=1, keepdims=True) for p in pp] for pp in probs]
    outs = [[_dot(p.astype(BF16), v) / l for p, l, v in zip(pp, lp, mv)] for pp, lp in zip(probs, sums)]
    atts = [_dot(jnp.concatenate(op, axis=1).astype(BF16), wo_ref[...]) for op in outs]
    for r, x, att in zip(parts, xs, atts):
        o_ref[r, :] = _layer_norm(alpha * x + att, lng_ref[...], lnb_ref[...])


def _memattn(x2d, mem_k, mem_v, wq, wo, ln_g, ln_b, alpha, guest):
    rows, dm = x2d.shape
    n_seq, n_mem, _ = mem_k.shape
    g_arrays, g_specs, g_out_spec, g_out_shape, g_vmem, steps = guest
    tm = rows // steps
    steps_per_mem = rows // n_seq // tm
    assert tm % (8 * MEM_ROW_SPLITS) == 0 and steps_per_mem * n_seq * tm == rows
    row = pl.BlockSpec((tm, dm), lambda i: (i, 0))
    mem_spec = pl.BlockSpec((None, n_mem, dm), lambda i: (i // steps_per_mem, 0, 0))
    consts = [wq, wo, ln_g.reshape(1, -1), ln_b.reshape(1, -1)]
    vmem = (2 * 2 * tm * dm * 4 + 2 * 2 * n_mem * dm * 4 + 2 * (wq.size + wo.size) + 12 * tm * dm * 4
            + g_vmem)
    return pl.pallas_call(
        functools.partial(_memattn_body, alpha=alpha),
        grid=(steps,),
        in_specs=[row, mem_spec, mem_spec] + [_const_spec(c.shape) for c in consts] + g_specs,
        out_specs=[row, g_out_spec],
        out_shape=[jax.ShapeDtypeStruct(x2d.shape, F32), g_out_shape],
        compiler_params=_params(("parallel",), vmem),
        name="mem_attn_ln2_hosting",
    )(x2d, mem_k, mem_v, *consts, *g_arrays)


def _rows_matmul_body(x_ref, w_ref, o_ref, *, scale):
    o_ref[...] = _dot(x_ref[...].astype(BF16), w_ref[...]) * scale


def _rows_matmul(x2d, w, scale):
    rows, dm = x2d.shape
    vmem = 4 * rows * (dm + w.shape[1]) * 4 + 4 * w.size
    return pl.pallas_call(
        functools.partial(_rows_matmul_body, scale=scale),
        out_shape=jax.ShapeDtypeStruct((rows, w.shape[1]), F32),
        compiler_params=_params(None, vmem),
        name="rows_matmul",
    )(x2d, w)


def _memattn_cache_probs(q_ref, mk_ref):
    nb, rows, hd = q_ref.shape
    n_mem, nh = mk_ref.shape[1:3]
    cols = n_mem * nh
    head_ok = ((lax.broadcasted_iota(jnp.int32, (rows, cols), 0) & (nh - 1))
               == (lax.broadcasted_iota(jnp.int32, (rows, cols), 1) & (nh - 1)))
    scores = [jnp.where(head_ok, _dot_nt(q_ref[j].astype(BF16), mk_ref[j].reshape(cols, hd).astype(BF16)), NEG)
              for j in range(nb)]
    probs = [jnp.exp(s - jnp.max(s, axis=1, keepdims=True)) for s in scores]
    return [(p.astype(BF16), jnp.sum(p, axis=1, keepdims=True)) for p in probs]


def _memattn_cache_output(probs, mv_ref, o_ref):
    n_mem, nh, hd = mv_ref.shape[1:]
    for j, (p, l) in enumerate(probs):
        o_ref[j] = _dot(p, mv_ref[j].reshape(n_mem * nh, hd).astype(BF16)) / l


def _memattn_cache_operands(q3, cache_k, cache_v, layer, nb):
    bsz, rows, hd = q3.shape
    _, _, n_mem, nh, _ = cache_k.shape
    q_spec = pl.BlockSpec((nb, rows, hd), lambda i: (i, 0, 0))
    c_spec = pl.BlockSpec((None, nb, n_mem, nh, hd), lambda i: (layer, i, 0, 0, 0))
    vmem = 2 * 2 * nb * n_mem * 8 * hd * 4 + 8 * nb * rows * n_mem * nh * 4
    return [q3, cache_k, cache_v], [q_spec, c_spec, c_spec], q_spec, jax.ShapeDtypeStruct(q3.shape, F32), vmem


def _proj_ln_body(x_ref, a_ref, w_ref, lng_ref, lnb_ref, o_ref, *, alpha):
    att = _dot(a_ref[...].astype(BF16), w_ref[...])
    o_ref[...] = _layer_norm(alpha * x_ref[...] + att, lng_ref[...], lnb_ref[...])


def _proj_ln(x2d, a2d, w, ln_g, ln_b, alpha):
    rows, dm = x2d.shape
    vmem = 8 * rows * dm * 4 + 4 * w.size
    return pl.pallas_call(
        functools.partial(_proj_ln_body, alpha=alpha),
        out_shape=jax.ShapeDtypeStruct(x2d.shape, F32),
        compiler_params=_params(None, vmem),
        name="proj_ln",
    )(x2d, a2d, w, ln_g.reshape(1, -1), ln_b.reshape(1, -1))


def _ffn_body(x_ref, wg_ref, wu_ref, wd_ref, lng_ref, lnb_ref, *rest, alpha, tf, hosts_guest):
    if hosts_guest:
        gq_ref, gk_ref, gv_ref, o_ref, go_ref, acc_ref = rest
        guest_probs = _memattn_cache_probs(gq_ref, gk_ref)
    else:
        o_ref, acc_ref = rest
    x = x_ref[...]
    xb = x.astype(BF16)
    n_chunks = wg_ref.shape[1] // tf
    cols = [slice(c * tf, (c + 1) * tf) for c in range(n_chunks)]

    def hidden(c):
        return (jax.nn.silu(_dot(xb, wg_ref[:, cols[c]])) * _dot(xb, wu_ref[:, cols[c]])).astype(BF16)

    hid = hidden(0)
    for c in range(n_chunks):
        nxt = hidden(c + 1) if c + 1 < n_chunks else None
        if hosts_guest and c == n_chunks // 2:
            _memattn_cache_output(guest_probs, gv_ref, go_ref)
        part = _dot(hid, wd_ref[cols[c], :])
        if c == 0:
            acc_ref[...] = part
        else:
            acc_ref[...] += part
        hid = nxt
    o_ref[...] = _layer_norm(alpha * x + acc_ref[...], lng_ref[...], lnb_ref[...])


def _ffn(x2d, wg, wu, wd, ln_g, ln_b, alpha, tm, tf, guest=None):
    rows, dm = x2d.shape
    d_ff = wg.shape[1]
    assert d_ff % tf == 0 and tf % LANES == 0 and rows % tm == 0
    row = pl.BlockSpec((tm, dm), lambda i: (i, 0))
    consts = [wg, wu, wd, ln_g.reshape(1, -1), ln_b.reshape(1, -1)]
    operands = [x2d, *consts]
    in_specs = [row] + [_const_spec(c.shape) for c in consts]
    out_specs, out_shape = row, jax.ShapeDtypeStruct(x2d.shape, F32)
    vmem = 2 * 2 * tm * dm * 4 + 2 * 3 * wg.size + tm * dm * 4 + 8 * tm * max(tf, dm) * 4
    if guest is not None:
        g_arrays, g_specs, g_out_spec, g_out_shape, g_vmem = guest
        assert g_arrays[0].shape[0] // g_specs[0].block_shape[0] == rows // tm
        operands, in_specs = operands + g_arrays, in_specs + g_specs
        out_specs, out_shape = [row, g_out_spec], [out_shape, g_out_shape]
        vmem += g_vmem
    return pl.pallas_call(
        functools.partial(_ffn_body, alpha=alpha, tf=tf, hosts_guest=guest is not None),
        grid=(rows // tm,),
        in_specs=in_specs,
        out_specs=out_specs,
        out_shape=out_shape,
        scratch_shapes=[pltpu.VMEM((tm, dm), F32)],
        compiler_params=_params(("parallel",), vmem),
        name="swiglu_ln3_hosting" if guest is not None else "swiglu_ln3",
    )(*operands)


PROMPT_CHUNK = 8
ROW_TILE = 512
FFN_COL_TILE = 256
SAMPLE_ATTN_SEQS = 2


def kernel(x_prompt, x_sample, cache_win_k, cache_win_v, state_ssm_re, state_ssm_im, cache_mem_k, cache_mem_v, mem_prompt, w_in, g_att, g_ssm, ssm_a_re, ssm_a_im, ssm_log_dt, ssm_b_re, ssm_b_im, ssm_c_re, ssm_c_im, ssm_d, w_glu, b_glu, w_out, ln1_g, ln1_b, w_mem_q, w_mem_k, w_mem_v, w_mem_o, ln2_g, ln2_b, w_gate, w_up, w_down, ln3_g, ln3_b):
    depth = w_in.shape[0]
    bp, seq, dm = x_prompt.shape
    bs, t_new, _ = x_sample.shape
    n_groups, n_state = ssm_a_re.shape[1:]
    n_mem = mem_prompt.shape[1]
    alpha = (2 * depth) ** 0.25
    keep = min(DILATED_CFGS[-1][0], seq)
    assert keep == seq
    n_chunks = seq // PROMPT_CHUNK

    y_p = x_prompt.reshape(bp * seq, dm)
    y_s = x_sample.reshape(bs * t_new, dm)
    mem2d = mem_prompt.reshape(bp * n_mem, dm)
    outs = [[] for _ in range(10)]
    for l in range(depth):
        bf = lambda w: w[l].astype(BF16)
        w_in_l, w_glu_l, w_out_l = bf(w_in), bf(w_glu), bf(w_out)
        wq_l, wk_l, wv_l, wo_l = bf(w_mem_q), bf(w_mem_k), bf(w_mem_v), bf(w_mem_o)
        wg_l, wu_l, wd_l = bf(w_gate), bf(w_up), bf(w_down)
        prepped = _ssm_prep(ssm_a_re[l], ssm_a_im[l], ssm_log_dt[l], ssm_b_re[l], ssm_b_im[l],
                            max(PROMPT_CHUNK, t_new), PROMPT_CHUNK)
        layouts = _ssm_layouts(*prepped, ssm_c_re[l], ssm_c_im[l], ssm_d[l])
        mix_w = (w_glu_l, b_glu[l], g_att[l], g_ssm[l], w_out_l, ln1_g[l], ln1_b[l])

        q, k, v, u, k_t, v_t = _inproj(y_p, w_in_l, ROW_TILE, PROMPT_CHUNK, seq)
        d_att = q.shape[1]
        o_att = _attn_prompt(q, k, v, seq)
        zeros = jnp.zeros((bp, 1, n_groups * n_state), F32)
        y_ssm, hr_p, hi_p = _ssm(u, zeros, zeros, layouts, PROMPT_CHUNK, n_chunks, True)
        x1 = _mix(y_p, o_att, y_ssm, *mix_w, alpha, ROW_TILE, PROMPT_CHUNK)
        mk_p, mv_p, mk_heads, mv_heads = _memkv(mem2d, wk_l, wv_l, ROW_TILE)

        qs, ks, vs, us = _inproj(y_s, w_in_l, bs * t_new, t_new)
        window = _window_guest_operands(qs, ks, vs, cache_win_k, cache_win_v, l, SAMPLE_ATTN_SEQS)
        x2, o_att_s = _memattn(x1, mk_p.reshape(bp, n_mem, dm), mv_p.reshape(bp, n_mem, dm), wq_l, wo_l,
                               ln2_g[l], ln2_b[l], alpha, window)
        y_ssm_s, hr_s, hi_s = _ssm(us, state_ssm_re[l].reshape(1, bs, -1), state_ssm_im[l].reshape(1, bs, -1),
                                   layouts, t_new, bs, False)
        x1s = _mix(y_s, o_att_s, y_ssm_s, *mix_w, alpha, bs * t_new, t_new)
        mem_hd = dm // N_MEM_HEADS
        q_mem = _rows_matmul(x1s, wq_l, mem_hd ** -0.5).reshape(bs, t_new * N_MEM_HEADS, mem_hd)
        ffn_steps = bp * seq // ROW_TILE
        guest = _memattn_cache_operands(q_mem, cache_mem_k, cache_mem_v, l, bs // ffn_steps)
        y_p, a_mem = _ffn(x2, wg_l, wu_l, wd_l, ln3_g[l], ln3_b[l], alpha, ROW_TILE, FFN_COL_TILE, guest)
        x2s = _proj_ln(x1s, a_mem.reshape(bs * t_new, dm), wo_l, ln2_g[l], ln2_b[l], alpha)
        y_s = _ffn(x2s, wg_l, wu_l, wd_l, ln3_g[l], ln3_b[l], alpha, bs * t_new, FFN_COL_TILE)

        head_shape = (N_ATT_HEADS, ATT_HEAD_DIM)
        state_shape = (n_groups, n_state)
        mem_shape = (bp, n_mem, N_MEM_HEADS, dm // N_MEM_HEADS)
        rows_first = lambda t: jnp.transpose(t.reshape(bp, *head_shape, seq), (0, 3, 1, 2))
        for lst, val in zip(outs, (
                rows_first(k_t)[:, seq - keep:], rows_first(v_t)[:, seq - keep:],
                ks.reshape(bs, t_new, *head_shape), vs.reshape(bs, t_new, *head_shape),
                hr_p.reshape(bp, *state_shape), hi_p.reshape(bp, *state_shape),
                hr_s.reshape(bs, *state_shape), hi_s.reshape(bs, *state_shape),
                mk_heads.reshape(mem_shape), mv_heads.reshape(mem_shape))):
            lst.append(val)
    return (y_p.reshape(bp, seq, dm), y_s.reshape(bs, t_new, dm)) + tuple(jnp.stack(o) for o in outs)
```

```python
import functools
import math

import jax
import jax.numpy as jnp
from jax import lax
from jax.experimental import pallas as pl
from jax.experimental.pallas import tpu as pltpu

F32 = jnp.float32
BF16 = jnp.bfloat16

N_ATT_HEADS = 8
ATT_HEAD_DIM = 64
D_ATT = N_ATT_HEADS * ATT_HEAD_DIM
DILATED_CFGS = ((128, 1), (512, 4), (2048, 16))
ATT_BLK = 128
ATT_GROUP = 4
SSM_CH = 16
SSM_STATE = 64
N_MEM_HEADS = 4
MEM_ROW_SPLITS = 1
MIX_ROW_SPLITS = 2
EPS = 1e-5
NEG = -1e30

LANES = 128
V7X_VMEM_CAP_BYTES = 56 * 1024 * 1024

SSM_GB = LANES // SSM_CH
SSM_ROW_TILE = 32
SSM_CHAIN_GROUP = 8


def _params(sem, vmem_bytes):
    return pltpu.CompilerParams(
        dimension_semantics=sem,
        vmem_limit_bytes=int(min(max(vmem_bytes, 16 * 1024 * 1024), V7X_VMEM_CAP_BYTES)),
    )


def _dot(a, b):
    return jnp.dot(a, b, preferred_element_type=F32)


def _dot_nt(a, b):
    return lax.dot_general(a, b, (((1,), (1,)), ((), ())), preferred_element_type=F32)


def _layer_norm(x, g, b):
    mu = jnp.mean(x, axis=-1, keepdims=True)
    xc = x - mu
    var = jnp.mean(xc * xc, axis=-1, keepdims=True)
    return xc * lax.rsqrt(var + EPS) * g + b


def _rms_norm(x, g):
    return x * lax.rsqrt(jnp.mean(x * x, axis=-1, keepdims=True) + EPS) * g


def _const_spec(shape):
    n = len(shape)
    return pl.BlockSpec(shape, lambda *_: (0,) * n, pipeline_mode=pl.Buffered(1))


def _inproj_body(x_ref, w_ref, q_ref, k_ref, v_ref, uc_ref, *rest, t1, transposed):
    u_scr = rest[-1]
    xb = x_ref[...].astype(BF16)
    d = q_ref.shape[1]
    q_ref[...] = _dot(xb, w_ref[:, 0 * d:1 * d])
    k = _dot(xb, w_ref[:, 1 * d:2 * d])
    v = _dot(xb, w_ref[:, 2 * d:3 * d])
    k_ref[...] = k
    v_ref[...] = v
    if transposed:
        kt_ref, vt_ref = rest[:2]
        kt_ref[...] = k.T
        vt_ref[...] = v.T
    u = _dot(xb, w_ref[:, 3 * d:4 * d])
    chunks = u_scr.shape[1] // t1
    for c in range(d // LANES):
        u_scr[c] = u[:, c * LANES:(c + 1) * LANES]
        for i in range(t1):
            uc_ref[:, i * d + c * LANES:i * d + (c + 1) * LANES] = u_scr[c, pl.ds(i, chunks, stride=t1), :]


def _inproj(x2d, w_in_bf16, tm, t1, seq=None):
    rows, dm = x2d.shape
    d = w_in_bf16.shape[1] // 4
    out = jax.ShapeDtypeStruct((rows, d), F32)
    row_spec = pl.BlockSpec((tm, d), lambda i: (i, 0))
    out_specs = [row_spec] * 3 + [pl.BlockSpec((tm // t1, t1 * d), lambda i: (i, 0))]
    out_shape = [out] * 3 + [jax.ShapeDtypeStruct((rows // t1, t1 * d), F32)]
    if seq is not None:
        steps = seq // tm
        out_specs += [pl.BlockSpec((None, d, tm), lambda i: (i // steps, 0, i % steps))] * 2
        out_shape += [jax.ShapeDtypeStruct((rows // seq, d, seq), F32)] * 2
    vmem = 2 * (tm * dm * 4 + 6 * tm * d * 4) + 2 * w_in_bf16.size * 2 + 10 * tm * d * 4
    return pl.pallas_call(
        functools.partial(_inproj_body, t1=t1, transposed=seq is not None),
        grid=(rows // tm,),
        in_specs=[pl.BlockSpec((tm, dm), lambda i: (i, 0)), _const_spec(w_in_bf16.shape)],
        out_specs=out_specs,
        out_shape=out_shape,
        scratch_shapes=[pltpu.VMEM((d // LANES, tm, LANES), F32)],
        compiler_params=_params(("parallel",), vmem),
        name="inproj",
    )(x2d, w_in_bf16)


def _attn_prompt_body(q_ref, k_ref, v_ref, o_ref, acc_ref, m_ref, l_ref):
    seq = q_ref.shape[0]
    blk = ATT_BLK
    scale = ATT_HEAD_DIM ** -0.5 * math.log2(math.e)
    lane = lax.broadcasted_iota(jnp.int32, (blk, LANES), 1)
    head0 = lane < ATT_HEAD_DIM
    qi = lax.broadcasted_iota(jnp.int32, (blk, blk), 0)
    kj = lax.broadcasted_iota(jnp.int32, (blk, blk), 1)
    tri = kj <= qi
    mask_rest = jnp.concatenate([kj >= qi, tri], axis=1)

    def rows_of(start, d):
        return pl.ds(start, blk) if d == 1 else pl.ds(start, blk, stride=d)

    def load_block(d, q_start, prev_start):
        rows = rows_of(q_start, d)
        q = q_ref[rows, :] * scale
        k2 = k_ref[rows, :].astype(BF16)
        v2 = v_ref[rows, :].astype(BF16)
        mask = tri
        if prev_start is not None:
            prow = rows_of(prev_start, d)
            k2 = jnp.concatenate([k_ref[prow, :].astype(BF16), k2], axis=0)
            v2 = jnp.concatenate([v_ref[prow, :].astype(BF16), v2], axis=0)
            mask = mask_rest
        qh = [jnp.where(head0 if h == 0 else jnp.logical_not(head0), q, 0.0).astype(BF16) for h in range(2)]
        return rows, qh, k2, v2, mask

    def attend_group(cfg, d, starts):
        blocks = [load_block(d, q_start, prev_start) for q_start, prev_start in starts]
        scores = [[jnp.where(mask, _dot_nt(qh[h], k2), NEG) for h in range(2)] for _, qh, k2, _, mask in blocks]
        maxes = [[jnp.max(s, axis=1, keepdims=True) for s in sb] for sb in scores]
        probs = [[jnp.exp2(s - m) for s, m in zip(sb, mb)] for sb, mb in zip(scores, maxes)]
        sums = [[jnp.sum(p, axis=1, keepdims=True) for p in pb] for pb in probs]
        pvs = [[_dot(p.astype(BF16), blk_[3]) for p in pb] for pb, blk_ in zip(probs, blocks)]
        new = [(blk_[0], jnp.where(head0, mb[0], mb[1]), jnp.where(head0, lb[0], lb[1]),
                jnp.where(head0, ob[0], ob[1])) for blk_, mb, lb, ob in zip(blocks, maxes, sums, pvs)]
        if cfg == 0:
            for rows, m, l, num in new:
                acc_ref[rows, :] = num
                m_ref[rows, :] = m
                l_ref[rows, :] = l
            return
        old = [(m_ref[rows, :], l_ref[rows, :], acc_ref[rows, :]) for rows, _, _, _ in new]
        for (rows, m, l, num), (m_old, l_old, acc_old) in zip(new, old):
            m_new = jnp.maximum(m_old, m)
            a = jnp.exp2(m_old - m_new)
            b = jnp.exp2(m - m_new)
            num = a * acc_old + b * num
            l = a * l_old + b * l
            if cfg == len(DILATED_CFGS) - 1:
                o_ref[rows, :] = num / l
            else:
                acc_ref[rows, :] = num
                m_ref[rows, :] = m_new
                l_ref[rows, :] = l

    for cfg, (window, d) in enumerate(DILATED_CFGS):
        span = d * blk
        blocks = [(r + n * span, r + (n - 1) * span if n else None) for n in range(seq // span) for r in range(d)]
        for g in range(0, len(blocks), ATT_GROUP):
            attend_group(cfg, d, blocks[g:g + ATT_GROUP])


def _attn_prompt(q, k, v, seq):
    rows, d_att = q.shape
    bsz = rows // seq
    assert d_att % LANES == 0 and LANES == 2 * ATT_HEAD_DIM
    for window, d in DILATED_CFGS:
        assert window // d == ATT_BLK and seq % (d * ATT_BLK) == 0
    spec = pl.BlockSpec((seq, LANES), lambda b, h: (b, h))
    blk_bytes = seq * LANES * 4
    return pl.pallas_call(
        _attn_prompt_body,
        grid=(bsz, d_att // LANES),
        in_specs=[spec, spec, spec],
        out_specs=spec,
        out_shape=jax.ShapeDtypeStruct(q.shape, F32),
        scratch_shapes=[pltpu.VMEM((seq, LANES), F32)] * 3,
        compiler_params=_params(("parallel", "parallel"), 11 * blk_bytes + 8 * 1024 * 1024),
        name="attn_prompt",
    )(q, k, v)


def _window_head_mask(d_att):
    nh = N_ATT_HEADS
    return (jnp.right_shift(lax.broadcasted_iota(jnp.int32, (nh, d_att), 1), ATT_HEAD_DIM.bit_length() - 1)
            == lax.broadcasted_iota(jnp.int32, (nh, d_att), 0))


def _window_probs(q, kn, kt):
    t_new, d_att = q.shape
    w_buf = kt.shape[1]
    nh = N_ATT_HEADS
    rows = t_new * nh
    (win1, _), (win4, dil4), (_, dil16) = DILATED_CFGS
    tail, near = win4, win1
    head_mask = _window_head_mask(d_att)
    trow = jnp.right_shift(lax.broadcasted_iota(jnp.int32, (rows, 1), 0), nh.bit_length() - 1)
    lane = lambda n: lax.broadcasted_iota(jnp.int32, (rows, n), 1)
    mask16 = (lane(w_buf) & (dil16 - 1)) == trow
    mask4 = (lane(tail) & (dil4 - 1)) == trow
    mask1 = lane(near) >= trow
    new_self = lane(t_new) == trow
    new_causal = lane(t_new) <= trow

    q = q * (ATT_HEAD_DIM ** -0.5)
    qbd = jnp.concatenate(
        [jnp.where(head_mask, jnp.broadcast_to(q[t:t + 1], (nh, d_att)), 0.0) for t in range(t_new)],
        axis=0).astype(BF16)
    s_all = _dot(qbd, kt.astype(BF16))
    s_new = _dot_nt(qbd, kn.astype(BF16))

    def softmax_parts(main, main_mask, new_mask):
        sm = jnp.where(main_mask, main, NEG)
        sn = jnp.where(new_mask, s_new, NEG)
        m = jnp.maximum(jnp.max(sm, axis=1, keepdims=True), jnp.max(sn, axis=1, keepdims=True))
        p = jnp.exp(sm - m)
        pn = jnp.exp(sn - m)
        den = jnp.sum(p, axis=1, keepdims=True) + jnp.sum(pn, axis=1, keepdims=True)
        return m, p, pn, den

    m16, p16, pn16, den16 = softmax_parts(s_all, mask16, new_self)
    m4, p4, pn4, den4 = softmax_parts(s_all[:, w_buf - tail:], mask4, new_self)
    m1, p1, pn1, den1 = softmax_parts(s_all[:, w_buf - near:], mask1, new_causal)
    m = jnp.maximum(jnp.maximum(m1, m4), m16)
    w1, w4, w16 = jnp.exp(m1 - m), jnp.exp(m4 - m), jnp.exp(m16 - m)
    den = w1 * den1 + w4 * den4 + w16 * den16
    p16, p4, p1 = w16 * p16, w4 * p4, w1 * p1
    p_all = jnp.concatenate(
        [p16[:, :w_buf - tail],
         p16[:, w_buf - tail:w_buf - near] + p4[:, :tail - near],
         p16[:, w_buf - near:] + p4[:, tail - near:] + p1], axis=1).astype(BF16)
    pn_all = (w16 * pn16 + w4 * pn4 + w1 * pn1).astype(BF16)
    return p_all, pn_all, den


def _window_output(p_all, pn_all, den, vn, vt):
    t_new, d_att = vn.shape
    nh = N_ATT_HEADS
    head_mask = _window_head_mask(d_att)
    out = (_dot_nt(p_all, vt.astype(BF16)) + _dot(pn_all, vn.astype(BF16))) / den
    return jnp.concatenate(
        [jnp.sum(jnp.where(head_mask, out[t * nh:(t + 1) * nh], 0.0), axis=0, keepdims=True)
         for t in range(t_new)], axis=0)


def _window_guest_probs(q_ref, kn_ref, kt_ref):
    nb = kt_ref.shape[0]
    t_new = q_ref.shape[0] // nb
    tok = [slice(b * t_new, (b + 1) * t_new) for b in range(nb)]
    return [_window_probs(q_ref[tok[b], :], kn_ref[tok[b], :], kt_ref[b]) for b in range(nb)]


def _window_guest_output(probs, vn_ref, vt_ref, o_ref):
    nb = vt_ref.shape[0]
    t_new = vn_ref.shape[0] // nb
    for b in range(nb):
        tok = slice(b * t_new, (b + 1) * t_new)
        o_ref[tok, :] = _window_output(*probs[b], vn_ref[tok, :], vt_ref[b])


def _window_guest_operands(q, k_new, v_new, cache_k, cache_v, layer, nb):
    depth, bsz, w_buf, nh, hd = cache_k.shape
    t_new, d_att = q.shape[0] // bsz, q.shape[1]
    assert w_buf == DILATED_CFGS[-1][0] and t_new <= DILATED_CFGS[1][1] and nh * hd == d_att
    assert (nb * t_new) % 8 == 0 and bsz % nb == 0
    rows_last = lambda c: jnp.transpose(c, (0, 1, 3, 4, 2)).reshape(depth, bsz, d_att, w_buf)
    new_spec = pl.BlockSpec((nb * t_new, d_att), lambda i: (i, 0))
    cache_spec = pl.BlockSpec((None, nb, d_att, w_buf), lambda i: (layer, i, 0, 0))
    vmem = 2 * 2 * nb * d_att * w_buf * 4 + 4 * nb * d_att * w_buf * 2 + 4 * 1024 * 1024
    return ([q, k_new, v_new, rows_last(cache_k), rows_last(cache_v)],
            [new_spec, new_spec, new_spec, cache_spec, cache_spec], new_spec,
            jax.ShapeDtypeStruct(q.shape, F32), vmem, bsz // nb)


def _ssm_prep_body(a_re_ref, a_im_ref, log_dt_ref, b_re_ref, b_im_ref, lam_re_ref, lam_im_ref,
                   bbar_re_ref, bbar_im_ref, step_re_ref, step_im_ref, *, chunk):
    ar = a_re_ref[...]
    ai = a_im_ref[...]
    dt = jnp.exp(log_dt_ref[...])
    mag = jnp.exp(dt * ar)
    lr = mag * jnp.cos(dt * ai)
    li = mag * jnp.sin(dt * ai)
    den = ar * ar + ai * ai
    nr, ni = lr - 1.0, li
    cr = (nr * ar + ni * ai) / den
    ci = (ni * ar - nr * ai) / den
    for h in range(b_re_ref.shape[0]):
        br, bi = b_re_ref[h], b_im_ref[h]
        bbar_re_ref[h] = cr * br - ci * bi
        bbar_im_ref[h] = cr * bi + ci * br
    pr, pi = lr, li
    for j in range(lam_re_ref.shape[0]):
        lam_re_ref[j] = pr
        lam_im_ref[j] = pi
        if j == chunk - 1:
            cr, ci = pr, pi
        pr, pi = pr * lr - pi * li, pr * li + pi * lr
    pr, pi = cr, ci
    for j in range(step_re_ref.shape[0]):
        step_re_ref[j] = pr
        step_im_ref[j] = pi
        pr, pi = pr * cr - pi * ci, pr * ci + pi * cr


def _ssm_prep(a_re, a_im, log_dt, b_re, b_im, n_pow, chunk):
    g, p = a_re.shape
    h = b_re.shape[-1]
    assert chunk <= n_pow
    b_re_t = jnp.transpose(b_re, (2, 0, 1))
    b_im_t = jnp.transpose(b_im, (2, 0, 1))
    gp = jax.ShapeDtypeStruct((n_pow, g, p), F32)
    hgp = jax.ShapeDtypeStruct((h, g, p), F32)
    sgp = jax.ShapeDtypeStruct((SSM_CHAIN_GROUP, g, p), F32)
    return pl.pallas_call(
        functools.partial(_ssm_prep_body, chunk=chunk),
        out_shape=[gp, gp, hgp, hgp, sgp, sgp],
        name="ssm_prep",
    )(a_re, a_im, log_dt.reshape(g, 1), b_re_t, b_im_t)


def _ssm_layouts(lam_re, lam_im, bbar_re_t, bbar_im_t, step_re, step_im, c_re, c_im, d_skip):
    n_pow, g, p = lam_re.shape
    h = bbar_re_t.shape[0]
    nq = g // SSM_GB
    eye = jnp.eye(SSM_GB, dtype=bool)
    bb = jnp.stack([bbar_re_t, bbar_im_t], 0).reshape(2, h, nq, SSM_GB, p)
    bb = jnp.transpose(bb, (2, 3, 1, 0, 4))
    bmat = jnp.where(eye[None, :, None, None, :, None], bb[:, :, :, :, None, :], 0.0)
    bmat = bmat.reshape(nq, SSM_GB * h, 2 * SSM_GB * p).astype(BF16)
    def c_layout(c):
        cc = jnp.transpose(c.reshape(nq, SSM_GB, h, p), (0, 1, 3, 2))
        m = jnp.where(eye[None, :, None, :, None], cc[:, :, :, None, :], 0.0)
        return m.reshape(nq, SSM_GB * p, SSM_GB * h).astype(BF16)
    per_block = lambda t: t.reshape(t.shape[0], nq, 1, SSM_GB * p)
    d_q = d_skip.reshape(nq, 1, SSM_GB * h)
    return (bmat, c_layout(c_re), c_layout(c_im), per_block(lam_re), per_block(lam_im),
            per_block(step_re), per_block(step_im), d_q)


def _ssm_body(u_ref, h0r_ref, h0i_ref, bmat_ref, cre_ref, cim_ref, lamr_ref, lami_ref, stepr_ref, stepi_ref, d_ref,
              y_ref, hfr_ref, hfi_ref, hloc_ref, hs_ref, ends_ref, grp_ref, *, t1, chained):
    rows = u_ref.shape[0]
    d_ssm = y_ref.shape[1] // t1
    nq = bmat_ref.shape[0]
    ns = lamr_ref.shape[-1]
    rt = SSM_ROW_TILE
    re, im = slice(0, ns), slice(ns, 2 * ns)

    def u_lanes(i, q):
        return slice(i * d_ssm + q * LANES, i * d_ssm + (q + 1) * LANES)

    def row_tiles(body):
        lax.fori_loop(0, rows // rt, lambda t, c: (body(pl.ds(pl.multiple_of(t * rt, rt), rt)), c)[1], 0,
                      unroll=True)

    def input_drive(q):
        for i in range(t1):
            hloc_ref[q % 2, i] = _dot(u_ref[:, u_lanes(i, q)].astype(BF16), bmat_ref[q])

    def local_scan(q):
        h = hloc_ref.at[q % 2]
        lr, li = lamr_ref[0, q], lami_ref[0, q]

        def tile(r):
            hr, hi = h[0, r, re], h[0, r, im]
            for i in range(1, t1):
                hr, hi = lr * hr - li * hi + h[i, r, re], lr * hi + li * hr + h[i, r, im]
                h[i, r, re] = hr
                h[i, r, im] = hi
        row_tiles(tile)

    nlb = ns // LANES

    def put_blocks(ref, rows_idx, vr, vi):
        for c in range(nlb):
            ref[c, rows_idx, :] = vr[:, c * LANES:(c + 1) * LANES]
            ref[nlb + c, rows_idx, :] = vi[:, c * LANES:(c + 1) * LANES]

    def get_blocks(ref, rows_idx):
        return (jnp.concatenate([ref[c, rows_idx, :] for c in range(nlb)], axis=1),
                jnp.concatenate([ref[nlb + c, rows_idx, :] for c in range(nlb)], axis=1))

    def chunk_starts(q):
        h = hloc_ref.at[q % 2]
        s_lanes = slice(q * ns, (q + 1) * ns)
        if not chained:
            put_blocks(hs_ref, slice(None), h0r_ref[:, s_lanes], h0i_ref[:, s_lanes])
            return
        grp = SSM_CHAIN_GROUP
        n_grp = rows // grp
        put_blocks(ends_ref, slice(None), h[t1 - 1, :, re], h[t1 - 1, :, im])
        member = lambda j: pl.ds(j, n_grp, stride=grp)
        cr, ci = stepr_ref[0, q], stepi_ref[0, q]
        gr, gi = get_blocks(ends_ref, member(0))
        inside = [(gr, gi)]
        for j in range(1, grp):
            er, ei = get_blocks(ends_ref, member(j))
            gr, gi = cr * gr - ci * gi + er, cr * gi + ci * gr + ei
            inside.append((gr, gi))
        grp_ref[0, :, re] = gr
        grp_ref[0, :, im] = gi
        wr, wi = stepr_ref[grp - 1, q], stepi_ref[grp - 1, q]

        def chain(c, g):
            sr, si = g
            row = pl.ds(c, 1)
            grp_ref[1, row, re] = sr
            grp_ref[1, row, im] = si
            return wr * sr - wi * si + grp_ref[0, row, re], wr * si + wi * sr + grp_ref[0, row, im]

        sr, si = lax.fori_loop(0, n_grp, chain, (h0r_ref[:, s_lanes], h0i_ref[:, s_lanes]))
        hfr_ref[:, s_lanes] = sr
        hfi_ref[:, s_lanes] = si
        sr, si = grp_ref[1, :, re], grp_ref[1, :, im]
        put_blocks(hs_ref, member(0), sr, si)
        for j in range(1, grp):
            pr, pi = stepr_ref[j - 1, q], stepi_ref[j - 1, q]
            gr, gi = inside[j - 1]
            put_blocks(hs_ref, member(j), gr + (pr * sr - pi * si), gi + (pr * si + pi * sr))

    def add_carry(q):
        h = hloc_ref.at[q % 2]

        def tile(r):
            hr0, hi0 = get_blocks(hs_ref, r)
            for i in range(t1):
                pr, pi = lamr_ref[i, q], lami_ref[i, q]
                h[i, r, re] = h[i, r, re] + (pr * hr0 - pi * hi0)
                h[i, r, im] = h[i, r, im] + (pr * hi0 + pi * hr0)
        row_tiles(tile)

    def readout(q):
        h = hloc_ref.at[q % 2]
        if not chained:
            s_lanes = slice(q * ns, (q + 1) * ns)
            hfr_ref[:, s_lanes] = h[t1 - 1, :, re]
            hfi_ref[:, s_lanes] = h[t1 - 1, :, im]
        for i in range(t1):
            y_ref[:, u_lanes(i, q)] = (_dot(h[i, :, re].astype(BF16), cre_ref[q])
                                       - _dot(h[i, :, im].astype(BF16), cim_ref[q])
                                       + d_ref[q] * u_ref[:, u_lanes(i, q)])

    input_drive(0)
    local_scan(0)
    chunk_starts(0)
    for q in range(nq):
        if q + 1 < nq:
            input_drive(q + 1)
        add_carry(q)
        readout(q)
        if q + 1 < nq:
            local_scan(q + 1)
            chunk_starts(q + 1)


def _ssm(u_chunks, h0_re, h0_im, layouts, t1, rows, chained):
    bmat, cre, cim, lam_re_q, lam_im_q, step_re_q, step_im_q, d_q = layouts
    lam_re_q, lam_im_q = lam_re_q[:t1], lam_im_q[:t1]
    n_blocks, rows_h, n_state = h0_re.shape
    width = u_chunks.shape[1]
    assert rows % SSM_ROW_TILE == 0 and u_chunks.shape[0] == n_blocks * rows
    assert not chained or rows % SSM_CHAIN_GROUP == 0
    ns2 = bmat.shape[2]
    u_spec = pl.BlockSpec((rows, width), lambda b: (b, 0))
    h_spec = pl.BlockSpec((None, rows_h, n_state), lambda b: (b, 0, 0))
    consts = [bmat, cre, cim, lam_re_q, lam_im_q, step_re_q, step_im_q, d_q]
    chain_rows = rows if chained else 8
    blocked = lambda r: pltpu.VMEM((ns2 // LANES, r, LANES), F32)
    vmem = (4 * rows * width * 4 + (2 * t1 + 1) * rows * ns2 * 4 + sum(c.size * c.dtype.itemsize for c in consts)
            + 8 * rows_h * n_state * 4 + 8 * 1024 * 1024)
    h_out = jax.ShapeDtypeStruct(h0_re.shape, F32)
    return pl.pallas_call(
        functools.partial(_ssm_body, t1=t1, chained=chained),
        grid=(n_blocks,),
        in_specs=[u_spec, h_spec, h_spec] + [_const_spec(c.shape) for c in consts],
        out_specs=[u_spec, h_spec, h_spec],
        out_shape=[jax.ShapeDtypeStruct(u_chunks.shape, F32), h_out, h_out],
        scratch_shapes=[pltpu.VMEM((2, t1, rows, ns2), F32), blocked(rows), blocked(chain_rows),
                        pltpu.VMEM((2, chain_rows // SSM_CHAIN_GROUP, ns2), F32)],
        compiler_params=_params(("parallel",), vmem),
        name="ssm_chained" if chained else "ssm_rows",
    )(u_chunks, h0_re, h0_im, *consts)


def _mix_body(x_ref, oatt_ref, yc_ref, wglu_ref, bglu_ref, gatt_ref, gssm_ref, wout_ref, lng_ref, lnb_ref,
              o_ref, y_scr, *, alpha, t1):
    d_att = oatt_ref.shape[1]
    n_lane_blocks, rows, _ = y_scr.shape
    d_ssm = n_lane_blocks * LANES
    chunks = rows // t1
    for c in range(n_lane_blocks):
        for i in range(t1):
            y_scr[c, pl.ds(i, chunks, stride=t1), :] = yc_ref[:, i * d_ssm + c * LANES:i * d_ssm + (c + 1) * LANES]
    parts = [pl.ds(i * (rows // MIX_ROW_SPLITS), rows // MIX_ROW_SPLITS) for i in range(MIX_ROW_SPLITS)]
    gs = [jax.nn.gelu(jnp.concatenate([y_scr[c, r, :] for c in range(n_lane_blocks)], axis=1)) for r in parts]
    gates = [_dot(g.astype(BF16), wglu_ref[...]) for g in gs]
    ras = [_rms_norm(oatt_ref[r, :], gatt_ref[...]).astype(BF16) for r in parts]
    att_parts = [_dot(ra, wout_ref[0:d_att, :]) for ra in ras]
    rzs = [_rms_norm(g * jax.nn.sigmoid(gate + bglu_ref[...]), gssm_ref[...]).astype(BF16)
           for g, gate in zip(gs, gates)]
    for r, att, rz in zip(parts, att_parts, rzs):
        mixed = att + _dot(rz, wout_ref[d_att:, :])
        o_ref[r, :] = _layer_norm(alpha * x_ref[r, :] + mixed, lng_ref[...], lnb_ref[...])


def _mix(x2d, o_att, y_chunks, w_glu, b_glu, g_att, g_ssm, w_out, ln_g, ln_b, alpha, tm, t1):
    rows, dm = x2d.shape
    d_att, d_ssm = o_att.shape[1], y_chunks.shape[1] // t1
    consts = [w_glu, b_glu.reshape(1, -1), g_att.reshape(1, -1), g_ssm.reshape(1, -1), w_out,
              ln_g.reshape(1, -1), ln_b.reshape(1, -1)]
    row = lambda width: pl.BlockSpec((tm, width), lambda i: (i, 0))
    chunk_spec = pl.BlockSpec((tm // t1, t1 * d_ssm), lambda i: (i, 0))
    vmem = 2 * tm * (2 * dm + d_att + d_ssm) * 4 + 4 * (w_glu.size + w_out.size) + 12 * tm * dm * 4
    return pl.pallas_call(
        functools.partial(_mix_body, alpha=alpha, t1=t1),
        grid=(rows // tm,),
        in_specs=[row(dm), row(d_att), chunk_spec] + [_const_spec(c.shape) for c in consts],
        out_specs=row(dm),
        out_shape=jax.ShapeDtypeStruct(x2d.shape, F32),
        scratch_shapes=[pltpu.VMEM((d_ssm // LANES, tm, LANES), F32)],
        compiler_params=_params(("parallel",), vmem),
        name="mix_out_ln1",
    )(x2d, o_att, y_chunks, *consts)


def _memkv_body(m_ref, wk_ref, wv_ref, k_ref, v_ref, kh_ref, vh_ref):
    mb = m_ref[...].astype(BF16)
    hd = kh_ref.shape[2]
    for w_ref, o_ref, oh_ref in ((wk_ref, k_ref, kh_ref), (wv_ref, v_ref, vh_ref)):
        val = _dot(mb, w_ref[...])
        o_ref[...] = val
        for h in range(oh_ref.shape[1]):
            oh_ref[:, h, :] = val[:, h * hd:(h + 1) * hd]


def _memkv(mem2d, wk, wv, tm):
    rows, dm = mem2d.shape
    hd = dm // N_MEM_HEADS
    row = pl.BlockSpec((tm, dm), lambda i: (i, 0))
    row_h = pl.BlockSpec((tm, N_MEM_HEADS, hd), lambda i: (i, 0, 0))
    out = jax.ShapeDtypeStruct((rows, wk.shape[1]), F32)
    out_h = jax.ShapeDtypeStruct((rows, N_MEM_HEADS, hd), F32)
    vmem = 2 * 5 * tm * dm * 4 + 2 * (wk.size + wv.size) + 4 * tm * dm * 4
    return pl.pallas_call(
        _memkv_body,
        grid=(rows // tm,),
        in_specs=[row, _const_spec(wk.shape), _const_spec(wv.shape)],
        out_specs=[row, row, row_h, row_h],
        out_shape=[out, out, out_h, out_h],
        compiler_params=_params(("parallel",), vmem),
        name="mem_kv",
    )(mem2d, wk, wv)


def _memattn_body(x_ref, mk_ref, mv_ref, wq_ref, wo_ref, lng_ref, lnb_ref,
                  gq_ref, gkn_ref, gvn_ref, gkt_ref, gvt_ref, o_ref, go_ref, *, alpha):
    guest_probs = _window_guest_probs(gq_ref, gkn_ref, gkt_ref)
    tm, dm = x_ref.shape
    hd = dm // N_MEM_HEADS
    heads = [slice(h * hd, (h + 1) * hd) for h in range(N_MEM_HEADS)]
    parts = [pl.ds(i * (tm // MEM_ROW_SPLITS), tm // MEM_ROW_SPLITS) for i in range(MEM_ROW_SPLITS)]
    mk = [mk_ref[:, c].astype(BF16) for c in heads]
    mv = [mv_ref[:, c].astype(BF16) for c in heads]
    xs = [x_ref[r, :] for r in parts]
    qs = [(_dot(x.astype(BF16), wq_ref[...]) * (hd ** -0.5)).astype(BF16) for x in xs]
    scores = [[_dot_nt(q[:, c], k) for c, k in zip(heads, mk)] for q in qs]
    _window_guest_output(guest_probs, gvn_ref, gvt_ref, go_ref)
    probs = [[jnp.exp(s - jnp.max(s, axis=1, keepdims=True)) for s in sp] for sp in scores]
    sums = [[jnp.sum(p, axis=1, keepdims=True) for p in pp] for pp in probs]
    outs = [[_dot(p.astype(BF16), v) / l for p, l, v in zip(pp, lp, mv)] for pp, lp in zip(probs, sums)]
    atts = [_dot(jnp.concatenate(op, axis=1).astype(BF16), wo_ref[...]) for op in outs]
    for r, x, att in zip(parts, xs, atts):
        o_ref[r, :] = _layer_norm(alpha * x + att, lng_ref[...], lnb_ref[...])


def _memattn(x2d, mem_k, mem_v, wq, wo, ln_g, ln_b, alpha, guest):
    rows, dm = x2d.shape
    n_seq, n_mem, _ = mem_k.shape
    g_arrays, g_specs, g_out_spec, g_out_shape, g_vmem, steps = guest
    tm = rows // steps
    steps_per_mem = rows // n_seq // tm
    assert tm % (8 * MEM_ROW_SPLITS) == 0 and steps_per_mem * n_seq * tm == rows
    row = pl.BlockSpec((tm, dm), lambda i: (i, 0))
    mem_spec = pl.BlockSpec((None, n_mem, dm), lambda i: (i // steps_per_mem, 0, 0))
    consts = [wq, wo, ln_g.reshape(1, -1), ln_b.reshape(1, -1)]
    vmem = (2 * 2 * tm * dm * 4 + 2 * 2 * n_mem * dm * 4 + 2 * (wq.size + wo.size) + 12 * tm * dm * 4
            + g_vmem)
    return pl.pallas_call(
        functools.partial(_memattn_body, alpha=alpha),
        grid=(steps,),
        in_specs=[row, mem_spec, mem_spec] + [_const_spec(c.shape) for c in consts] + g_specs,
        out_specs=[row, g_out_spec],
        out_shape=[jax.ShapeDtypeStruct(x2d.shape, F32), g_out_shape],
        compiler_params=_params(("parallel",), vmem),
        name="mem_attn_ln2_hosting",
    )(x2d, mem_k, mem_v, *consts, *g_arrays)


def _rows_matmul_body(x_ref, w_ref, o_ref, *, scale):
    o_ref[...] = _dot(x_ref[...].astype(BF16), w_ref[...]) * scale


def _rows_matmul(x2d, w, scale):
    rows, dm = x2d.shape
    vmem = 4 * rows * (dm + w.shape[1]) * 4 + 4 * w.size
    return pl.pallas_call(
        functools.partial(_rows_matmul_body, scale=scale),
        out_shape=jax.ShapeDtypeStruct((rows, w.shape[1]), F32),
        compiler_params=_params(None, vmem),
        name="rows_matmul",
    )(x2d, w)


def _memattn_cache_probs(q_ref, mk_ref):
    nb, rows, hd = q_ref.shape
    n_mem, nh = mk_ref.shape[1:3]
    cols = n_mem * nh
    head_ok = ((lax.broadcasted_iota(jnp.int32, (rows, cols), 0) & (nh - 1))
               == (lax.broadcasted_iota(jnp.int32, (rows, cols), 1) & (nh - 1)))
    scores = [jnp.where(head_ok, _dot_nt(q_ref[j].astype(BF16), mk_ref[j].reshape(cols, hd).astype(BF16)), NEG)
              for j in range(nb)]
    probs = [jnp.exp(s - jnp.max(s, axis=1, keepdims=True)) for s in scores]
    return [(p.astype(BF16), jnp.sum(p, axis=1, keepdims=True)) for p in probs]


def _memattn_cache_output(probs, mv_ref, o_ref):
    n_mem, nh, hd = mv_ref.shape[1:]
    for j, (p, l) in enumerate(probs):
        o_ref[j] = _dot(p, mv_ref[j].reshape(n_mem * nh, hd).astype(BF16)) / l


def _memattn_cache_operands(q3, cache_k, cache_v, layer, nb):
    bsz, rows, hd = q3.shape
    _, _, n_mem, nh, _ = cache_k.shape
    q_spec = pl.BlockSpec((nb, rows, hd), lambda i: (i, 0, 0))
    c_spec = pl.BlockSpec((None, nb, n_mem, nh, hd), lambda i: (layer, i, 0, 0, 0))
    vmem = 2 * 2 * nb * n_mem * 8 * hd * 4 + 8 * nb * rows * n_mem * nh * 4
    return [q3, cache_k, cache_v], [q_spec, c_spec, c_spec], q_spec, jax.ShapeDtypeStruct(q3.shape, F32), vmem


def _proj_ln_body(x_ref, a_ref, w_ref, lng_ref, lnb_ref, o_ref, *, alpha):
    att = _dot(a_ref[...].astype(BF16), w_ref[...])
    o_ref[...] = _layer_norm(alpha * x_ref[...] + att, lng_ref[...], lnb_ref[...])


def _proj_ln(x2d, a2d, w, ln_g, ln_b, alpha):
    rows, dm = x2d.shape
    vmem = 8 * rows * dm * 4 + 4 * w.size
    return pl.pallas_call(
        functools.partial(_proj_ln_body, alpha=alpha),
        out_shape=jax.ShapeDtypeStruct(x2d.shape, F32),
        compiler_params=_params(None, vmem),
        name="proj_ln",
    )(x2d, a2d, w, ln_g.reshape(1, -1), ln_b.reshape(1, -1))


def _ffn_body(x_ref, wg_ref, wu_ref, wd_ref, lng_ref, lnb_ref, *rest, alpha, tf, hosts_guest):
    if hosts_guest:
        gq_ref, gk_ref, gv_ref, o_ref, go_ref, acc_ref = rest
        guest_probs = _memattn_cache_probs(gq_ref, gk_ref)
    else:
        o_ref, acc_ref = rest
    x = x_ref[...]
    xb = x.astype(BF16)
    n_chunks = wg_ref.shape[1] // tf
    cols = [slice(c * tf, (c + 1) * tf) for c in range(n_chunks)]

    def hidden(c):
        return (jax.nn.silu(_dot(xb, wg_ref[:, cols[c]])) * _dot(xb, wu_ref[:, cols[c]])).astype(BF16)

    hid = hidden(0)
    for c in range(n_chunks):
        nxt = hidden(c + 1) if c + 1 < n_chunks else None
        if hosts_guest and c == n_chunks // 2:
            _memattn_cache_output(guest_probs, gv_ref, go_ref)
        part = _dot(hid, wd_ref[cols[c], :])
        if c == 0:
            acc_ref[...] = part
        else:
            acc_ref[...] += part
        hid = nxt
    o_ref[...] = _layer_norm(alpha * x + acc_ref[...], lng_ref[...], lnb_ref[...])


def _ffn(x2d, wg, wu, wd, ln_g, ln_b, alpha, tm, tf, guest=None):
    rows, dm = x2d.shape
    d_ff = wg.shape[1]
    assert d_ff % tf == 0 and tf % LANES == 0 and rows % tm == 0
    row = pl.BlockSpec((tm, dm), lambda i: (i, 0))
    consts = [wg, wu, wd, ln_g.reshape(1, -1), ln_b.reshape(1, -1)]
    operands = [x2d, *consts]
    in_specs = [row] + [_const_spec(c.shape) for c in consts]
    out_specs, out_shape = row, jax.ShapeDtypeStruct(x2d.shape, F32)
    vmem = 2 * 2 * tm * dm * 4 + 2 * 3 * wg.size + tm * dm * 4 + 8 * tm * max(tf, dm) * 4
    if guest is not None:
        g_arrays, g_specs, g_out_spec, g_out_shape, g_vmem = guest
        assert g_arrays[0].shape[0] // g_specs[0].block_shape[0] == rows // tm
        operands, in_specs = operands + g_arrays, in_specs + g_specs
        out_specs, out_shape = [row, g_out_spec], [out_shape, g_out_shape]
        vmem += g_vmem
    return pl.pallas_call(
        functools.partial(_ffn_body, alpha=alpha, tf=tf, hosts_guest=guest is not None),
        grid=(rows // tm,),
        in_specs=in_specs,
        out_specs=out_specs,
        out_shape=out_shape,
        scratch_shapes=[pltpu.VMEM((tm, dm), F32)],
        compiler_params=_params(("parallel",), vmem),
        name="swiglu_ln3_hosting" if guest is not None else "swiglu_ln3",
    )(*operands)


PROMPT_CHUNK = 8
ROW_TILE = 512
FFN_COL_TILE = 256
SAMPLE_ATTN_SEQS = 2


def kernel(x_prompt, x_sample, cache_win_k, cache_win_v, state_ssm_re, state_ssm_im, cache_mem_k, cache_mem_v, mem_prompt, w_in, g_att, g_ssm, ssm_a_re, ssm_a_im, ssm_log_dt, ssm_b_re, ssm_b_im, ssm_c_re, ssm_c_im, ssm_d, w_glu, b_glu, w_out, ln1_g, ln1_b, w_mem_q, w_mem_k, w_mem_v, w_mem_o, ln2_g, ln2_b, w_gate, w_up, w_down, ln3_g, ln3_b):
    depth = w_in.shape[0]
    bp, seq, dm = x_prompt.shape
    bs, t_new, _ = x_sample.shape
    n_groups, n_state = ssm_a_re.shape[1:]
    n_mem = mem_prompt.shape[1]
    alpha = (2 * depth) ** 0.25
    keep = min(DILATED_CFGS[-1][0], seq)
    assert keep == seq
    n_chunks = seq // PROMPT_CHUNK

    y_p = x_prompt.reshape(bp * seq, dm)
    y_s = x_sample.reshape(bs * t_new, dm)
    mem2d = mem_prompt.reshape(bp * n_mem, dm)
    outs = [[] for _ in range(10)]
    for l in range(depth):
        bf = lambda w: w[l].astype(BF16)
        w_in_l, w_glu_l, w_out_l = bf(w_in), bf(w_glu), bf(w_out)
        wq_l, wk_l, wv_l, wo_l = bf(w_mem_q), bf(w_mem_k), bf(w_mem_v), bf(w_mem_o)
        wg_l, wu_l, wd_l = bf(w_gate), bf(w_up), bf(w_down)
        prepped = _ssm_prep(ssm_a_re[l], ssm_a_im[l], ssm_log_dt[l], ssm_b_re[l], ssm_b_im[l],
                            max(PROMPT_CHUNK, t_new), PROMPT_CHUNK)
        layouts = _ssm_layouts(*prepped, ssm_c_re[l], ssm_c_im[l], ssm_d[l])
        mix_w = (w_glu_l, b_glu[l], g_att[l], g_ssm[l], w_out_l, ln1_g[l], ln1_b[l])

        q, k, v, u, k_t, v_t = _inproj(y_p, w_in_l, ROW_TILE, PROMPT_CHUNK, seq)
        d_att = q.shape[1]
        o_att = _attn_prompt(q, k, v, seq)
        zeros = jnp.zeros((bp, 1, n_groups * n_state), F32)
        y_ssm, hr_p, hi_p = _ssm(u, zeros, zeros, layouts, PROMPT_CHUNK, n_chunks, True)
        x1 = _mix(y_p, o_att, y_ssm, *mix_w, alpha, ROW_TILE, PROMPT_CHUNK)
        mk_p, mv_p, mk_heads, mv_heads = _memkv(mem2d, wk_l, wv_l, ROW_TILE)

        qs, ks, vs, us = _inproj(y_s, w_in_l, bs * t_new, t_new)
        window = _window_guest_operands(qs, ks, vs, cache_win_k, cache_win_v, l, SAMPLE_ATTN_SEQS)
        x2, o_att_s = _memattn(x1, mk_p.reshape(bp, n_mem, dm), mv_p.reshape(bp, n_mem, dm), wq_l, wo_l,
                               ln2_g[l], ln2_b[l], alpha, window)
        y_ssm_s, hr_s, hi_s = _ssm(us, state_ssm_re[l].reshape(1, bs, -1), state_ssm_im[l].reshape(1, bs, -1),
                                   layouts, t_new, bs, False)
        x1s = _mix(y_s, o_att_s, y_ssm_s, *mix_w, alpha, bs * t_new, t_new)
        mem_hd = dm // N_MEM_HEADS
        q_mem = _rows_matmul(x1s, wq_l, mem_hd ** -0.5).reshape(bs, t_new * N_MEM_HEADS, mem_hd)
        ffn_steps = bp * seq // ROW_TILE
        guest = _memattn_cache_operands(q_mem, cache_mem_k, cache_mem_v, l, bs // ffn_steps)
        y_p, a_mem = _ffn(x2, wg_l, wu_l, wd_l, ln3_g[l], ln3_b[l], alpha, ROW_TILE, FFN_COL_TILE, guest)
        x2s = _proj_ln(x1s, a_mem.reshape(bs * t_new, dm), wo_l, ln2_g[l], ln2_b[l], alpha)
        y_s = _ffn(x2s, wg_l, wu_l, wd_l, ln3_g[l], ln3_b[l], alpha, bs * t_new, FFN_COL_TILE)

        head_shape = (N_ATT_HEADS, ATT_HEAD_DIM)
        state_shape = (n_groups, n_state)
        mem_shape = (bp, n_mem, N_MEM_HEADS, dm // N_MEM_HEADS)
        rows_first = lambda t: jnp.transpose(t.reshape(bp, *head_shape, seq), (0, 3, 1, 2))
        for lst, val in zip(outs, (
                rows_first(k_t)[:, seq - keep:], rows_first(v_t)[:, seq - keep:],
                ks.reshape(bs, t_new, *head_shape), vs.reshape(bs, t_new, *head_shape),
                hr_p.reshape(bp, *state_shape), hi_p.reshape(bp, *state_shape),
                hr_s.reshape(bs, *state_shape), hi_s.reshape(bs, *state_shape),
                mk_heads.reshape(mem_shape), mv_heads.reshape(mem_shape))):
            lst.append(val)
    return (y_p.reshape(bp, seq, dm), y_s.reshape(bs, t_new, dm)) + tuple(jnp.stack(o) for o in outs)
```

```python
import functools
import math

import jax
import jax.numpy as jnp
from jax import lax
from jax.experimental import pallas as pl
from jax.experimental.pallas import tpu as pltpu

F32 = jnp.float32
BF16 = jnp.bfloat16

N_ATT_HEADS = 8
ATT_HEAD_DIM = 64
D_ATT = N_ATT_HEADS * ATT_HEAD_DIM
DILATED_CFGS = ((128, 1), (512, 4), (2048, 16))
ATT_BLK = 128
ATT_GROUP = 4
SSM_CH = 16
SSM_STATE = 64
N_MEM_HEADS = 4
MIX_ROW_SPLITS = 2
EPS = 1e-5
NEG = -1e30

LANES = 128
V7X_VMEM_CAP_BYTES = 56 * 1024 * 1024

SSM_GB = LANES // SSM_CH
SSM_ROW_TILE = 32
SSM_CHAIN_GROUP = 8


def _params(sem, vmem_bytes):
    return pltpu.CompilerParams(
        dimension_semantics=sem,
        vmem_limit_bytes=int(min(max(vmem_bytes, 16 * 1024 * 1024), V7X_VMEM_CAP_BYTES)),
    )


def _dot(a, b):
    return jnp.dot(a, b, preferred_element_type=F32)


def _dot_nt(a, b):
    return lax.dot_general(a, b, (((1,), (1,)), ((), ())), preferred_element_type=F32)


def _layer_norm(x, g, b):
    mu = jnp.mean(x, axis=-1, keepdims=True)
    xc = x - mu
    var = jnp.mean(xc * xc, axis=-1, keepdims=True)
    return xc * lax.rsqrt(var + EPS) * g + b


def _rms_norm(x, g):
    return x * lax.rsqrt(jnp.mean(x * x, axis=-1, keepdims=True) + EPS) * g


def _const_spec(shape):
    n = len(shape)
    return pl.BlockSpec(shape, lambda *_: (0,) * n, pipeline_mode=pl.Buffered(1))


def _inproj_body(x_ref, w_ref, q_ref, k_ref, v_ref, uc_ref, *rest, t1, transposed):
    u_scr = rest[-1]
    xb = x_ref[...].astype(BF16)
    d = q_ref.shape[1]
    q_ref[...] = _dot(xb, w_ref[:, 0 * d:1 * d])
    k = _dot(xb, w_ref[:, 1 * d:2 * d])
    v = _dot(xb, w_ref[:, 2 * d:3 * d])
    k_ref[...] = k
    v_ref[...] = v
    if transposed:
        kt_ref, vt_ref = rest[:2]
        kt_ref[...] = k.T
        vt_ref[...] = v.T
    u = _dot(xb, w_ref[:, 3 * d:4 * d])
    chunks = u_scr.shape[1] // t1
    for c in range(d // LANES):
        u_scr[c] = u[:, c * LANES:(c + 1) * LANES]
        for i in range(t1):
            uc_ref[:, i * d + c * LANES:i * d + (c + 1) * LANES] = u_scr[c, pl.ds(i, chunks, stride=t1), :]


def _inproj(x2d, w_in_bf16, tm, t1, seq=None):
    rows, dm = x2d.shape
    d = w_in_bf16.shape[1] // 4
    out = jax.ShapeDtypeStruct((rows, d), F32)
    row_spec = pl.BlockSpec((tm, d), lambda i: (i, 0))
    out_specs = [row_spec] * 3 + [pl.BlockSpec((tm // t1, t1 * d), lambda i: (i, 0))]
    out_shape = [out] * 3 + [jax.ShapeDtypeStruct((rows // t1, t1 * d), F32)]
    if seq is not None:
        steps = seq // tm
        out_specs += [pl.BlockSpec((None, d, tm), lambda i: (i // steps, 0, i % steps))] * 2
        out_shape += [jax.ShapeDtypeStruct((rows // seq, d, seq), F32)] * 2
    vmem = 2 * (tm * dm * 4 + 6 * tm * d * 4) + 2 * w_in_bf16.size * 2 + 10 * tm * d * 4
    return pl.pallas_call(
        functools.partial(_inproj_body, t1=t1, transposed=seq is not None),
        grid=(rows // tm,),
        in_specs=[pl.BlockSpec((tm, dm), lambda i: (i, 0)), _const_spec(w_in_bf16.shape)],
        out_specs=out_specs,
        out_shape=out_shape,
        scratch_shapes=[pltpu.VMEM((d // LANES, tm, LANES), F32)],
        compiler_params=_params(("parallel",), vmem),
        name="inproj",
    )(x2d, w_in_bf16)


def _attn_prompt_body(q_ref, k_ref, v_ref, o_ref, acc_ref, m_ref, l_ref):
    seq = q_ref.shape[0]
    blk = ATT_BLK
    scale = ATT_HEAD_DIM ** -0.5 * math.log2(math.e)
    lane = lax.broadcasted_iota(jnp.int32, (blk, LANES), 1)
    head0 = lane < ATT_HEAD_DIM
    qi = lax.broadcasted_iota(jnp.int32, (blk, blk), 0)
    kj = lax.broadcasted_iota(jnp.int32, (blk, blk), 1)
    tri = kj <= qi
    mask_rest = jnp.concatenate([kj >= qi, tri], axis=1)

    def rows_of(start, d):
        return pl.ds(start, blk) if d == 1 else pl.ds(start, blk, stride=d)

    def load_block(d, q_start, prev_start):
        rows = rows_of(q_start, d)
        q = q_ref[rows, :] * scale
        k2 = k_ref[rows, :].astype(BF16)
        v2 = v_ref[rows, :].astype(BF16)
        mask = tri
        if prev_start is not None:
            prow = rows_of(prev_start, d)
            k2 = jnp.concatenate([k_ref[prow, :].astype(BF16), k2], axis=0)
            v2 = jnp.concatenate([v_ref[prow, :].astype(BF16), v2], axis=0)
            mask = mask_rest
        qh = [jnp.where(head0 if h == 0 else jnp.logical_not(head0), q, 0.0).astype(BF16) for h in range(2)]
        return rows, qh, k2, v2, mask

    def attend_group(cfg, d, starts):
        blocks = [load_block(d, q_start, prev_start) for q_start, prev_start in starts]
        scores = [[jnp.where(mask, _dot_nt(qh[h], k2), NEG) for h in range(2)] for _, qh, k2, _, mask in blocks]
        maxes = [[jnp.max(s, axis=1, keepdims=True) for s in sb] for sb in scores]
        probs = [[jnp.exp2(s - m) for s, m in zip(sb, mb)] for sb, mb in zip(scores, maxes)]
        sums = [[jnp.sum(p, axis=1, keepdims=True) for p in pb] for pb in probs]
        pvs = [[_dot(p.astype(BF16), blk_[3]) for p in pb] for pb, blk_ in zip(probs, blocks)]
        new = [(blk_[0], jnp.where(head0, mb[0], mb[1]), jnp.where(head0, lb[0], lb[1]),
                jnp.where(head0, ob[0], ob[1])) for blk_, mb, lb, ob in zip(blocks, maxes, sums, pvs)]
        if cfg == 0:
            for rows, m, l, num in new:
                acc_ref[rows, :] = num
                m_ref[rows, :] = m
                l_ref[rows, :] = l
            return
        old = [(m_ref[rows, :], l_ref[rows, :], acc_ref[rows, :]) for rows, _, _, _ in new]
        for (rows, m, l, num), (m_old, l_old, acc_old) in zip(new, old):
            m_new = jnp.maximum(m_old, m)
            a = jnp.exp2(m_old - m_new)
            b = jnp.exp2(m - m_new)
            num = a * acc_old + b * num
            l = a * l_old + b * l
            if cfg == len(DILATED_CFGS) - 1:
                o_ref[rows, :] = num / l
            else:
                acc_ref[rows, :] = num
                m_ref[rows, :] = m_new
                l_ref[rows, :] = l

    for cfg, (window, d) in enumerate(DILATED_CFGS):
        span = d * blk
        blocks = [(r + n * span, r + (n - 1) * span if n else None) for n in range(seq // span) for r in range(d)]
        for g in range(0, len(blocks), ATT_GROUP):
            attend_group(cfg, d, blocks[g:g + ATT_GROUP])


def _attn_prompt(q, k, v, seq):
    rows, d_att = q.shape
    bsz = rows // seq
    assert d_att % LANES == 0 and LANES == 2 * ATT_HEAD_DIM
    for window, d in DILATED_CFGS:
        assert window // d == ATT_BLK and seq % (d * ATT_BLK) == 0
    spec = pl.BlockSpec((seq, LANES), lambda b, h: (b, h))
    blk_bytes = seq * LANES * 4
    return pl.pallas_call(
        _attn_prompt_body,
        grid=(bsz, d_att // LANES),
        in_specs=[spec, spec, spec],
        out_specs=spec,
        out_shape=jax.ShapeDtypeStruct(q.shape, F32),
        scratch_shapes=[pltpu.VMEM((seq, LANES), F32)] * 3,
        compiler_params=_params(("parallel", "parallel"), 11 * blk_bytes + 8 * 1024 * 1024),
        name="attn_prompt",
    )(q, k, v)


def _window_head_mask(d_att):
    nh = N_ATT_HEADS
    return (jnp.right_shift(lax.broadcasted_iota(jnp.int32, (nh, d_att), 1), ATT_HEAD_DIM.bit_length() - 1)
            == lax.broadcasted_iota(jnp.int32, (nh, d_att), 0))


def _window_probs(q, kn, kt):
    t_new, d_att = q.shape
    w_buf = kt.shape[1]
    nh = N_ATT_HEADS
    rows = t_new * nh
    (win1, _), (win4, dil4), (_, dil16) = DILATED_CFGS
    tail, near = win4, win1
    head_mask = _window_head_mask(d_att)
    trow = jnp.right_shift(lax.broadcasted_iota(jnp.int32, (rows, 1), 0), nh.bit_length() - 1)
    lane = lambda n: lax.broadcasted_iota(jnp.int32, (rows, n), 1)
    mask16 = (lane(w_buf) & (dil16 - 1)) == trow
    mask4 = (lane(tail) & (dil4 - 1)) == trow
    mask1 = lane(near) >= trow
    new_self = lane(t_new) == trow
    new_causal = lane(t_new) <= trow

    q = q * (ATT_HEAD_DIM ** -0.5)
    qbd = jnp.concatenate(
        [jnp.where(head_mask, jnp.broadcast_to(q[t:t + 1], (nh, d_att)), 0.0) for t in range(t_new)],
        axis=0).astype(BF16)
    s_all = _dot(qbd, kt.astype(BF16))
    s_new = _dot_nt(qbd, kn.astype(BF16))

    def softmax_parts(main, main_mask, new_mask):
        sm = jnp.where(main_mask, main, NEG)
        sn = jnp.where(new_mask, s_new, NEG)
        m = jnp.maximum(jnp.max(sm, axis=1, keepdims=True), jnp.max(sn, axis=1, keepdims=True))
        p = jnp.exp(sm - m)
        pn = jnp.exp(sn - m)
        den = jnp.sum(p, axis=1, keepdims=True) + jnp.sum(pn, axis=1, keepdims=True)
        return m, p, pn, den

    m16, p16, pn16, den16 = softmax_parts(s_all, mask16, new_self)
    m4, p4, pn4, den4 = softmax_parts(s_all[:, w_buf - tail:], mask4, new_self)
    m1, p1, pn1, den1 = softmax_parts(s_all[:, w_buf - near:], mask1, new_causal)
    m = jnp.maximum(jnp.maximum(m1, m4), m16)
    w1, w4, w16 = jnp.exp(m1 - m), jnp.exp(m4 - m), jnp.exp(m16 - m)
    den = w1 * den1 + w4 * den4 + w16 * den16
    p16, p4, p1 = w16 * p16, w4 * p4, w1 * p1
    p_all = jnp.concatenate(
        [p16[:, :w_buf - tail],
         p16[:, w_buf - tail:w_buf - near] + p4[:, :tail - near],
         p16[:, w_buf - near:] + p4[:, tail - near:] + p1], axis=1).astype(BF16)
    pn_all = (w16 * pn16 + w4 * pn4 + w1 * pn1).astype(BF16)
    return p_all, pn_all, den


def _window_output(p_all, pn_all, den, vn, vt):
    t_new, d_att = vn.shape
    nh = N_ATT_HEADS
    head_mask = _window_head_mask(d_att)
    out = (_dot_nt(p_all, vt.astype(BF16)) + _dot(pn_all, vn.astype(BF16))) / den
    return jnp.concatenate(
        [jnp.sum(jnp.where(head_mask, out[t * nh:(t + 1) * nh], 0.0), axis=0, keepdims=True)
         for t in range(t_new)], axis=0)


def _window_guest_probs(q_ref, kn_ref, kt_ref):
    nb = kt_ref.shape[0]
    t_new = q_ref.shape[0] // nb
    tok = [slice(b * t_new, (b + 1) * t_new) for b in range(nb)]
    return [_window_probs(q_ref[tok[b], :], kn_ref[tok[b], :], kt_ref[b]) for b in range(nb)]


def _window_guest_output(probs, vn_ref, vt_ref, o_ref):
    nb = vt_ref.shape[0]
    t_new = vn_ref.shape[0] // nb
    for b in range(nb):
        tok = slice(b * t_new, (b + 1) * t_new)
        o_ref[tok, :] = _window_output(*probs[b], vn_ref[tok, :], vt_ref[b])


def _window_guest_operands(q, k_new, v_new, cache_k, cache_v, layer, nb):
    depth, bsz, w_buf, nh, hd = cache_k.shape
    t_new, d_att = q.shape[0] // bsz, q.shape[1]
    assert w_buf == DILATED_CFGS[-1][0] and t_new <= DILATED_CFGS[1][1] and nh * hd == d_att
    assert (nb * t_new) % 8 == 0 and bsz % nb == 0
    rows_last = lambda c: jnp.transpose(c, (0, 1, 3, 4, 2)).reshape(depth, bsz, d_att, w_buf)
    new_spec = pl.BlockSpec((nb * t_new, d_att), lambda i: (i, 0))
    cache_spec = pl.BlockSpec((None, nb, d_att, w_buf), lambda i: (layer, i, 0, 0))
    vmem = 2 * 2 * nb * d_att * w_buf * 4 + 4 * nb * d_att * w_buf * 2 + 4 * 1024 * 1024
    return ([q, k_new, v_new, rows_last(cache_k), rows_last(cache_v)],
            [new_spec, new_spec, new_spec, cache_spec, cache_spec], new_spec,
            jax.ShapeDtypeStruct(q.shape, F32), vmem, bsz // nb)


def _ssm_prep_body(a_re_ref, a_im_ref, log_dt_ref, b_re_ref, b_im_ref, lam_re_ref, lam_im_ref,
                   bbar_re_ref, bbar_im_ref, step_re_ref, step_im_ref, *, chunk):
    ar = a_re_ref[...]
    ai = a_im_ref[...]
    dt = jnp.exp(log_dt_ref[...])
    mag = jnp.exp(dt * ar)
    lr = mag * jnp.cos(dt * ai)
    li = mag * jnp.sin(dt * ai)
    den = ar * ar + ai * ai
    nr, ni = lr - 1.0, li
    cr = (nr * ar + ni * ai) / den
    ci = (ni * ar - nr * ai) / den
    for h in range(b_re_ref.shape[0]):
        br, bi = b_re_ref[h], b_im_ref[h]
        bbar_re_ref[h] = cr * br - ci * bi
        bbar_im_ref[h] = cr * bi + ci * br
    pr, pi = lr, li
    for j in range(lam_re_ref.shape[0]):
        lam_re_ref[j] = pr
        lam_im_ref[j] = pi
        if j == chunk - 1:
            cr, ci = pr, pi
        pr, pi = pr * lr - pi * li, pr * li + pi * lr
    pr, pi = cr, ci
    for j in range(step_re_ref.shape[0]):
        step_re_ref[j] = pr
        step_im_ref[j] = pi
        pr, pi = pr * cr - pi * ci, pr * ci + pi * cr


def _ssm_prep(a_re, a_im, log_dt, b_re, b_im, n_pow, chunk):
    g, p = a_re.shape
    h = b_re.shape[-1]
    assert chunk <= n_pow
    b_re_t = jnp.transpose(b_re, (2, 0, 1))
    b_im_t = jnp.transpose(b_im, (2, 0, 1))
    gp = jax.ShapeDtypeStruct((n_pow, g, p), F32)
    hgp = jax.ShapeDtypeStruct((h, g, p), F32)
    sgp = jax.ShapeDtypeStruct((SSM_CHAIN_GROUP, g, p), F32)
    return pl.pallas_call(
        functools.partial(_ssm_prep_body, chunk=chunk),
        out_shape=[gp, gp, hgp, hgp, sgp, sgp],
        name="ssm_prep",
    )(a_re, a_im, log_dt.reshape(g, 1), b_re_t, b_im_t)


def _ssm_layouts(lam_re, lam_im, bbar_re_t, bbar_im_t, step_re, step_im, c_re, c_im, d_skip):
    n_pow, g, p = lam_re.shape
    h = bbar_re_t.shape[0]
    nq = g // SSM_GB
    eye = jnp.eye(SSM_GB, dtype=bool)
    bb = jnp.stack([bbar_re_t, bbar_im_t], 0).reshape(2, h, nq, SSM_GB, p)
    bb = jnp.transpose(bb, (2, 3, 1, 0, 4))
    bmat = jnp.where(eye[None, :, None, None, :, None], bb[:, :, :, :, None, :], 0.0)
    bmat = bmat.reshape(nq, SSM_GB * h, 2 * SSM_GB * p).astype(BF16)
    def c_layout(c):
        cc = jnp.transpose(c.reshape(nq, SSM_GB, h, p), (0, 1, 3, 2))
        m = jnp.where(eye[None, :, None, :, None], cc[:, :, :, None, :], 0.0)
        return m.reshape(nq, SSM_GB * p, SSM_GB * h).astype(BF16)
    per_block = lambda t: t.reshape(t.shape[0], nq, 1, SSM_GB * p)
    d_q = d_skip.reshape(nq, 1, SSM_GB * h)
    return (bmat, c_layout(c_re), c_layout(c_im), per_block(lam_re), per_block(lam_im),
            per_block(step_re), per_block(step_im), d_q)


def _ssm_body(u_ref, h0r_ref, h0i_ref, bmat_ref, cre_ref, cim_ref, lamr_ref, lami_ref, stepr_ref, stepi_ref, d_ref,
              y_ref, hfr_ref, hfi_ref, hloc_ref, hs_ref, ends_ref, grp_ref, *, t1, chained):
    rows = u_ref.shape[0]
    d_ssm = y_ref.shape[1] // t1
    nq = bmat_ref.shape[0]
    ns = lamr_ref.shape[-1]
    rt = SSM_ROW_TILE
    re, im = slice(0, ns), slice(ns, 2 * ns)

    def u_lanes(i, q):
        return slice(i * d_ssm + q * LANES, i * d_ssm + (q + 1) * LANES)

    def row_tiles(body):
        lax.fori_loop(0, rows // rt, lambda t, c: (body(pl.ds(pl.multiple_of(t * rt, rt), rt)), c)[1], 0,
                      unroll=True)

    def input_drive(q):
        for i in range(t1):
            hloc_ref[q % 2, i] = _dot(u_ref[:, u_lanes(i, q)].astype(BF16), bmat_ref[q])

    def local_scan(q):
        h = hloc_ref.at[q % 2]
        lr, li = lamr_ref[0, q], lami_ref[0, q]

        def tile(r):
            hr, hi = h[0, r, re], h[0, r, im]
            for i in range(1, t1):
                hr, hi = lr * hr - li * hi + h[i, r, re], lr * hi + li * hr + h[i, r, im]
                h[i, r, re] = hr
                h[i, r, im] = hi
        row_tiles(tile)

    nlb = ns // LANES

    def put_blocks(ref, rows_idx, vr, vi):
        for c in range(nlb):
            ref[c, rows_idx, :] = vr[:, c * LANES:(c + 1) * LANES]
            ref[nlb + c, rows_idx, :] = vi[:, c * LANES:(c + 1) * LANES]

    def get_blocks(ref, rows_idx):
        return (jnp.concatenate([ref[c, rows_idx, :] for c in range(nlb)], axis=1),
                jnp.concatenate([ref[nlb + c, rows_idx, :] for c in range(nlb)], axis=1))

    def chunk_starts(q):
        h = hloc_ref.at[q % 2]
        s_lanes = slice(q * ns, (q + 1) * ns)
        if not chained:
            put_blocks(hs_ref, slice(None), h0r_ref[:, s_lanes], h0i_ref[:, s_lanes])
            return
        grp = SSM_CHAIN_GROUP
        n_grp = rows // grp
        put_blocks(ends_ref, slice(None), h[t1 - 1, :, re], h[t1 - 1, :, im])
        member = lambda j: pl.ds(j, n_grp, stride=grp)
        cr, ci = stepr_ref[0, q], stepi_ref[0, q]
        gr, gi = get_blocks(ends_ref, member(0))
        inside = [(gr, gi)]
        for j in range(1, grp):
            er, ei = get_blocks(ends_ref, member(j))
            gr, gi = cr * gr - ci * gi + er, cr * gi + ci * gr + ei
            inside.append((gr, gi))
        grp_ref[0, :, re] = gr
        grp_ref[0, :, im] = gi
        wr, wi = stepr_ref[grp - 1, q], stepi_ref[grp - 1, q]

        def chain(c, g):
            sr, si = g
            row = pl.ds(c, 1)
            grp_ref[1, row, re] = sr
            grp_ref[1, row, im] = si
            return wr * sr - wi * si + grp_ref[0, row, re], wr * si + wi * sr + grp_ref[0, row, im]

        sr, si = lax.fori_loop(0, n_grp, chain, (h0r_ref[:, s_lanes], h0i_ref[:, s_lanes]))
        hfr_ref[:, s_lanes] = sr
        hfi_ref[:, s_lanes] = si
        sr, si = grp_ref[1, :, re], grp_ref[1, :, im]
        put_blocks(hs_ref, member(0), sr, si)
        for j in range(1, grp):
            pr, pi = stepr_ref[j - 1, q], stepi_ref[j - 1, q]
            gr, gi = inside[j - 1]
            put_blocks(hs_ref, member(j), gr + (pr * sr - pi * si), gi + (pr * si + pi * sr))

    def add_carry(q):
        h = hloc_ref.at[q % 2]

        def tile(r):
            hr0, hi0 = get_blocks(hs_ref, r)
            for i in range(t1):
                pr, pi = lamr_ref[i, q], lami_ref[i, q]
                h[i, r, re] = h[i, r, re] + (pr * hr0 - pi * hi0)
                h[i, r, im] = h[i, r, im] + (pr * hi0 + pi * hr0)
        row_tiles(tile)

    def readout(q):
        h = hloc_ref.at[q % 2]
        if not chained:
            s_lanes = slice(q * ns, (q + 1) * ns)
            hfr_ref[:, s_lanes] = h[t1 - 1, :, re]
            hfi_ref[:, s_lanes] = h[t1 - 1, :, im]
        for i in range(t1):
            y_ref[:, u_lanes(i, q)] = (_dot(h[i, :, re].astype(BF16), cre_ref[q])
                                       - _dot(h[i, :, im].astype(BF16), cim_ref[q])
                                       + d_ref[q] * u_ref[:, u_lanes(i, q)])

    input_drive(0)
    local_scan(0)
    chunk_starts(0)
    for q in range(nq):
        if q + 1 < nq:
            input_drive(q + 1)
        add_carry(q)
        readout(q)
        if q + 1 < nq:
            local_scan(q + 1)
            chunk_starts(q + 1)


def _ssm(u_chunks, h0_re, h0_im, layouts, t1, rows, chained):
    bmat, cre, cim, lam_re_q, lam_im_q, step_re_q, step_im_q, d_q = layouts
    lam_re_q, lam_im_q = lam_re_q[:t1], lam_im_q[:t1]
    n_blocks, rows_h, n_state = h0_re.shape
    width = u_chunks.shape[1]
    assert rows % SSM_ROW_TILE == 0 and u_chunks.shape[0] == n_blocks * rows
    assert not chained or rows % SSM_CHAIN_GROUP == 0
    ns2 = bmat.shape[2]
    u_spec = pl.BlockSpec((rows, width), lambda b: (b, 0))
    h_spec = pl.BlockSpec((None, rows_h, n_state), lambda b: (b, 0, 0))
    consts = [bmat, cre, cim, lam_re_q, lam_im_q, step_re_q, step_im_q, d_q]
    chain_rows = rows if chained else 8
    blocked = lambda r: pltpu.VMEM((ns2 // LANES, r, LANES), F32)
    vmem = (4 * rows * width * 4 + (2 * t1 + 1) * rows * ns2 * 4 + sum(c.size * c.dtype.itemsize for c in consts)
            + 8 * rows_h * n_state * 4 + 8 * 1024 * 1024)
    h_out = jax.ShapeDtypeStruct(h0_re.shape, F32)
    return pl.pallas_call(
        functools.partial(_ssm_body, t1=t1, chained=chained),
        grid=(n_blocks,),
        in_specs=[u_spec, h_spec, h_spec] + [_const_spec(c.shape) for c in consts],
        out_specs=[u_spec, h_spec, h_spec],
        out_shape=[jax.ShapeDtypeStruct(u_chunks.shape, F32), h_out, h_out],
        scratch_shapes=[pltpu.VMEM((2, t1, rows, ns2), F32), blocked(rows), blocked(chain_rows),
                        pltpu.VMEM((2, chain_rows // SSM_CHAIN_GROUP, ns2), F32)],
        compiler_params=_params(("parallel",), vmem),
        name="ssm_chained" if chained else "ssm_rows",
    )(u_chunks, h0_re, h0_im, *consts)


def _mix_body(x_ref, oatt_ref, yc_ref, wglu_ref, bglu_ref, gatt_ref, gssm_ref, wout_ref, lng_ref, lnb_ref,
              o_ref, y_scr, *, alpha, t1):
    d_att = oatt_ref.shape[1]
    n_lane_blocks, rows, _ = y_scr.shape
    d_ssm = n_lane_blocks * LANES
    chunks = rows // t1
    for c in range(n_lane_blocks):
        for i in range(t1):
            y_scr[c, pl.ds(i, chunks, stride=t1), :] = yc_ref[:, i * d_ssm + c * LANES:i * d_ssm + (c + 1) * LANES]
    parts = [pl.ds(i * (rows // MIX_ROW_SPLITS), rows // MIX_ROW_SPLITS) for i in range(MIX_ROW_SPLITS)]
    gs = [jax.nn.gelu(jnp.concatenate([y_scr[c, r, :] for c in range(n_lane_blocks)], axis=1)) for r in parts]
    gates = [_dot(g.astype(BF16), wglu_ref[...]) for g in gs]
    ras = [_rms_norm(oatt_ref[r, :], gatt_ref[...]).astype(BF16) for r in parts]
    att_parts = [_dot(ra, wout_ref[0:d_att, :]) for ra in ras]
    rzs = [_rms_norm(g * jax.nn.sigmoid(gate + bglu_ref[...]), gssm_ref[...]).astype(BF16)
           for g, gate in zip(gs, gates)]
    for r, att, rz in zip(parts, att_parts, rzs):
        mixed = att + _dot(rz, wout_ref[d_att:, :])
        o_ref[r, :] = _layer_norm(alpha * x_ref[r, :] + mixed, lng_ref[...], lnb_ref[...])


def _mix(x2d, o_att, y_chunks, w_glu, b_glu, g_att, g_ssm, w_out, ln_g, ln_b, alpha, tm, t1):
    rows, dm = x2d.shape
    d_att, d_ssm = o_att.shape[1], y_chunks.shape[1] // t1
    consts = [w_glu, b_glu.reshape(1, -1), g_att.reshape(1, -1), g_ssm.reshape(1, -1), w_out,
              ln_g.reshape(1, -1), ln_b.reshape(1, -1)]
    row = lambda width: pl.BlockSpec((tm, width), lambda i: (i, 0))
    chunk_spec = pl.BlockSpec((tm // t1, t1 * d_ssm), lambda i: (i, 0))
    vmem = 2 * tm * (2 * dm + d_att + d_ssm) * 4 + 4 * (w_glu.size + w_out.size) + 12 * tm * dm * 4
    return pl.pallas_call(
        functools.partial(_mix_body, alpha=alpha, t1=t1),
        grid=(rows // tm,),
        in_specs=[row(dm), row(d_att), chunk_spec] + [_const_spec(c.shape) for c in consts],
        out_specs=row(dm),
        out_shape=jax.ShapeDtypeStruct(x2d.shape, F32),
        scratch_shapes=[pltpu.VMEM((d_ssm // LANES, tm, LANES), F32)],
        compiler_params=_params(("parallel",), vmem),
        name="mix_out_ln1",
    )(x2d, o_att, y_chunks, *consts)


def _memkv_body(m_ref, wk_ref, wv_ref, k_ref, v_ref, kh_ref, vh_ref):
    mb = m_ref[...].astype(BF16)
    hd = kh_ref.shape[2]
    for w_ref, o_ref, oh_ref in ((wk_ref, k_ref, kh_ref), (wv_ref, v_ref, vh_ref)):
        val = _dot(mb, w_ref[...])
        o_ref[...] = val
        for h in range(oh_ref.shape[1]):
            oh_ref[:, h, :] = val[:, h * hd:(h + 1) * hd]


def _memkv(mem2d, wk, wv, tm):
    rows, dm = mem2d.shape
    hd = dm // N_MEM_HEADS
    row = pl.BlockSpec((tm, dm), lambda i: (i, 0))
    row_h = pl.BlockSpec((tm, N_MEM_HEADS, hd), lambda i: (i, 0, 0))
    out = jax.ShapeDtypeStruct((rows, wk.shape[1]), F32)
    out_h = jax.ShapeDtypeStruct((rows, N_MEM_HEADS, hd), F32)
    vmem = 2 * 5 * tm * dm * 4 + 2 * (wk.size + wv.size) + 4 * tm * dm * 4
    return pl.pallas_call(
        _memkv_body,
        grid=(rows // tm,),
        in_specs=[row, _const_spec(wk.shape), _const_spec(wv.shape)],
        out_specs=[row, row, row_h, row_h],
        out_shape=[out, out, out_h, out_h],
        compiler_params=_params(("parallel",), vmem),
        name="mem_kv",
    )(mem2d, wk, wv)


def _memattn_body(x_ref, mk_ref, mv_ref, wq_ref, wo_ref, lng_ref, lnb_ref,
                  gq_ref, gkn_ref, gvn_ref, gkt_ref, gvt_ref, o_ref, go_ref, *, alpha):
    guest_probs = _window_guest_probs(gq_ref, gkn_ref, gkt_ref)
    dm = x_ref.shape[1]
    hd = dm // N_MEM_HEADS
    heads = [slice(h * hd, (h + 1) * hd) for h in range(N_MEM_HEADS)]
    q = (_dot(x_ref[...].astype(BF16), wq_ref[...]) * (hd ** -0.5)).astype(BF16)
    scores = [_dot_nt(q[:, c], mk_ref[:, c].astype(BF16)) for c in heads]
    _window_guest_output(guest_probs, gvn_ref, gvt_ref, go_ref)
    probs = [jnp.exp(s - jnp.max(s, axis=1, keepdims=True)) for s in scores]
    sums = [jnp.sum(p, axis=1, keepdims=True) for p in probs]
    outs = [_dot(p.astype(BF16), mv_ref[:, c].astype(BF16)) / l for p, l, c in zip(probs, sums, heads)]
    att = _dot(jnp.concatenate(outs, axis=1).astype(BF16), wo_ref[...])
    o_ref[...] = _layer_norm(alpha * x_ref[...] + att, lng_ref[...], lnb_ref[...])


def _memattn(x2d, mem_k, mem_v, wq, wo, ln_g, ln_b, alpha, guest):
    rows, dm = x2d.shape
    n_seq, n_mem, _ = mem_k.shape
    g_arrays, g_specs, g_out_spec, g_out_shape, g_vmem, steps = guest
    tm = rows // steps
    steps_per_mem = rows // n_seq // tm
    assert tm % 8 == 0 and steps_per_mem * n_seq * tm == rows
    row = pl.BlockSpec((tm, dm), lambda i: (i, 0))
    mem_spec = pl.BlockSpec((None, n_mem, dm), lambda i: (i // steps_per_mem, 0, 0))
    consts = [wq, wo, ln_g.reshape(1, -1), ln_b.reshape(1, -1)]
    vmem = (2 * 2 * tm * dm * 4 + 2 * 2 * n_mem * dm * 4 + 2 * (wq.size + wo.size) + 12 * tm * dm * 4
            + g_vmem)
    return pl.pallas_call(
        functools.partial(_memattn_body, alpha=alpha),
        grid=(steps,),
        in_specs=[row, mem_spec, mem_spec] + [_const_spec(c.shape) for c in consts] + g_specs,
        out_specs=[row, g_out_spec],
        out_shape=[jax.ShapeDtypeStruct(x2d.shape, F32), g_out_shape],
        compiler_params=_params(("parallel",), vmem),
        name="mem_attn_ln2_hosting",
    )(x2d, mem_k, mem_v, *consts, *g_arrays)


def _rows_matmul_body(x_ref, w_ref, o_ref, *, scale):
    o_ref[...] = _dot(x_ref[...].astype(BF16), w_ref[...]) * scale


def _rows_matmul(x2d, w, scale):
    rows, dm = x2d.shape
    vmem = 4 * rows * (dm + w.shape[1]) * 4 + 4 * w.size
    return pl.pallas_call(
        functools.partial(_rows_matmul_body, scale=scale),
        out_shape=jax.ShapeDtypeStruct((rows, w.shape[1]), F32),
        compiler_params=_params(None, vmem),
        name="rows_matmul",
    )(x2d, w)


def _memattn_cache_probs(q_ref, mk_ref):
    nb, rows, hd = q_ref.shape
    n_mem, nh = mk_ref.shape[1:3]
    cols = n_mem * nh
    head_ok = ((lax.broadcasted_iota(jnp.int32, (rows, cols), 0) & (nh - 1))
               == (lax.broadcasted_iota(jnp.int32, (rows, cols), 1) & (nh - 1)))
    scores = [jnp.where(head_ok, _dot_nt(q_ref[j].astype(BF16), mk_ref[j].reshape(cols, hd).astype(BF16)), NEG)
              for j in range(nb)]
    probs = [jnp.exp(s - jnp.max(s, axis=1, keepdims=True)) for s in scores]
    return [(p.astype(BF16), jnp.sum(p, axis=1, keepdims=True)) for p in probs]


def _memattn_cache_output(probs, mv_ref, o_ref):
    n_mem, nh, hd = mv_ref.shape[1:]
    for j, (p, l) in enumerate(probs):
        o_ref[j] = _dot(p, mv_ref[j].reshape(n_mem * nh, hd).astype(BF16)) / l


def _memattn_cache_operands(q3, cache_k, cache_v, layer, nb):
    bsz, rows, hd = q3.shape
    _, _, n_mem, nh, _ = cache_k.shape
    q_spec = pl.BlockSpec((nb, rows, hd), lambda i: (i, 0, 0))
    c_spec = pl.BlockSpec((None, nb, n_mem, nh, hd), lambda i: (layer, i, 0, 0, 0))
    vmem = 2 * 2 * nb * n_mem * 8 * hd * 4 + 8 * nb * rows * n_mem * nh * 4
    return [q3, cache_k, cache_v], [q_spec, c_spec, c_spec], q_spec, jax.ShapeDtypeStruct(q3.shape, F32), vmem


def _proj_ln_body(x_ref, a_ref, w_ref, lng_ref, lnb_ref, o_ref, *, alpha):
    att = _dot(a_ref[...].astype(BF16), w_ref[...])
    o_ref[...] = _layer_norm(alpha * x_ref[...] + att, lng_ref[...], lnb_ref[...])


def _proj_ln(x2d, a2d, w, ln_g, ln_b, alpha):
    rows, dm = x2d.shape
    vmem = 8 * rows * dm * 4 + 4 * w.size
    return pl.pallas_call(
        functools.partial(_proj_ln_body, alpha=alpha),
        out_shape=jax.ShapeDtypeStruct(x2d.shape, F32),
        compiler_params=_params(None, vmem),
        name="proj_ln",
    )(x2d, a2d, w, ln_g.reshape(1, -1), ln_b.reshape(1, -1))


def _ffn_body(x_ref, wg_ref, wu_ref, wd_ref, lng_ref, lnb_ref, *rest, alpha, tf, hosts_guest):
    if hosts_guest:
        gq_ref, gk_ref, gv_ref, o_ref, go_ref, acc_ref = rest
        guest_probs = _memattn_cache_probs(gq_ref, gk_ref)
    else:
        o_ref, acc_ref = rest
    x = x_ref[...]
    xb = x.astype(BF16)
    n_chunks = wg_ref.shape[1] // tf
    cols = [slice(c * tf, (c + 1) * tf) for c in range(n_chunks)]

    def hidden(c):
        return (jax.nn.silu(_dot(xb, wg_ref[:, cols[c]])) * _dot(xb, wu_ref[:, cols[c]])).astype(BF16)

    hid = hidden(0)
    for c in range(n_chunks):
        nxt = hidden(c + 1) if c + 1 < n_chunks else None
        if hosts_guest and c == n_chunks // 2:
            _memattn_cache_output(guest_probs, gv_ref, go_ref)
        part = _dot(hid, wd_ref[cols[c], :])
        if c == 0:
            acc_ref[...] = part
        else:
            acc_ref[...] += part
        hid = nxt
    o_ref[...] = _layer_norm(alpha * x + acc_ref[...], lng_ref[...], lnb_ref[...])


def _ffn(x2d, wg, wu, wd, ln_g, ln_b, alpha, tm, tf, guest=None):
    rows, dm = x2d.shape
    d_ff = wg.shape[1]
    assert d_ff % tf == 0 and tf % LANES == 0 and rows % tm == 0
    row = pl.BlockSpec((tm, dm), lambda i: (i, 0))
    consts = [wg, wu, wd, ln_g.reshape(1, -1), ln_b.reshape(1, -1)]
    operands = [x2d, *consts]
    in_specs = [row] + [_const_spec(c.shape) for c in consts]
    out_specs, out_shape = row, jax.ShapeDtypeStruct(x2d.shape, F32)
    vmem = 2 * 2 * tm * dm * 4 + 2 * 3 * wg.size + tm * dm * 4 + 8 * tm * max(tf, dm) * 4
    if guest is not None:
        g_arrays, g_specs, g_out_spec, g_out_shape, g_vmem = guest
        assert g_arrays[0].shape[0] // g_specs[0].block_shape[0] == rows // tm
        operands, in_specs = operands + g_arrays, in_specs + g_specs
        out_specs, out_shape = [row, g_out_spec], [out_shape, g_out_shape]
        vmem += g_vmem
    return pl.pallas_call(
        functools.partial(_ffn_body, alpha=alpha, tf=tf, hosts_guest=guest is not None),
        grid=(rows // tm,),
        in_specs=in_specs,
        out_specs=out_specs,
        out_shape=out_shape,
        scratch_shapes=[pltpu.VMEM((tm, dm), F32)],
        compiler_params=_params(("parallel",), vmem),
        name="swiglu_ln3_hosting" if guest is not None else "swiglu_ln3",
    )(*operands)


PROMPT_CHUNK = 8
ROW_TILE = 512
FFN_COL_TILE = 256
SAMPLE_ATTN_SEQS = 2


def kernel(x_prompt, x_sample, cache_win_k, cache_win_v, state_ssm_re, state_ssm_im, cache_mem_k, cache_mem_v, mem_prompt, w_in, g_att, g_ssm, ssm_a_re, ssm_a_im, ssm_log_dt, ssm_b_re, ssm_b_im, ssm_c_re, ssm_c_im, ssm_d, w_glu, b_glu, w_out, ln1_g, ln1_b, w_mem_q, w_mem_k, w_mem_v, w_mem_o, ln2_g, ln2_b, w_gate, w_up, w_down, ln3_g, ln3_b):
    depth = w_in.shape[0]
    bp, seq, dm = x_prompt.shape
    bs, t_new, _ = x_sample.shape
    n_groups, n_state = ssm_a_re.shape[1:]
    n_mem = mem_prompt.shape[1]
    alpha = (2 * depth) ** 0.25
    keep = min(DILATED_CFGS[-1][0], seq)
    assert keep == seq
    n_chunks = seq // PROMPT_CHUNK

    y_p = x_prompt.reshape(bp * seq, dm)
    y_s = x_sample.reshape(bs * t_new, dm)
    mem2d = mem_prompt.reshape(bp * n_mem, dm)
    outs = [[] for _ in range(10)]
    for l in range(depth):
        bf = lambda w: w[l].astype(BF16)
        w_in_l, w_glu_l, w_out_l = bf(w_in), bf(w_glu), bf(w_out)
        wq_l, wk_l, wv_l, wo_l = bf(w_mem_q), bf(w_mem_k), bf(w_mem_v), bf(w_mem_o)
        wg_l, wu_l, wd_l = bf(w_gate), bf(w_up), bf(w_down)
        prepped = _ssm_prep(ssm_a_re[l], ssm_a_im[l], ssm_log_dt[l], ssm_b_re[l], ssm_b_im[l],
                            max(PROMPT_CHUNK, t_new), PROMPT_CHUNK)
        layouts = _ssm_layouts(*prepped, ssm_c_re[l], ssm_c_im[l], ssm_d[l])
        mix_w = (w_glu_l, b_glu[l], g_att[l], g_ssm[l], w_out_l, ln1_g[l], ln1_b[l])

        q, k, v, u, k_t, v_t = _inproj(y_p, w_in_l, ROW_TILE, PROMPT_CHUNK, seq)
        d_att = q.shape[1]
        o_att = _attn_prompt(q, k, v, seq)
        zeros = jnp.zeros((bp, 1, n_groups * n_state), F32)
        y_ssm, hr_p, hi_p = _ssm(u, zeros, zeros, layouts, PROMPT_CHUNK, n_chunks, True)
        x1 = _mix(y_p, o_att, y_ssm, *mix_w, alpha, ROW_TILE, PROMPT_CHUNK)
        mk_p, mv_p, mk_heads, mv_heads = _memkv(mem2d, wk_l, wv_l, ROW_TILE)

        qs, ks, vs, us = _inproj(y_s, w_in_l, bs * t_new, t_new)
        window = _window_guest_operands(qs, ks, vs, cache_win_k, cache_win_v, l, SAMPLE_ATTN_SEQS)
        x2, o_att_s = _memattn(x1, mk_p.reshape(bp, n_mem, dm), mv_p.reshape(bp, n_mem, dm), wq_l, wo_l,
                               ln2_g[l], ln2_b[l], alpha, window)
        y_ssm_s, hr_s, hi_s = _ssm(us, state_ssm_re[l].reshape(1, bs, -1), state_ssm_im[l].reshape(1, bs, -1),
                                   layouts, t_new, bs, False)
        x1s = _mix(y_s, o_att_s, y_ssm_s, *mix_w, alpha, bs * t_new, t_new)
        mem_hd = dm // N_MEM_HEADS
        q_mem = _rows_matmul(x1s, wq_l, mem_hd ** -0.5).reshape(bs, t_new * N_MEM_HEADS, mem_hd)
        ffn_steps = bp * seq // ROW_TILE
        guest = _memattn_cache_operands(q_mem, cache_mem_k, cache_mem_v, l, bs // ffn_steps)
        y_p, a_mem = _ffn(x2, wg_l, wu_l, wd_l, ln3_g[l], ln3_b[l], alpha, ROW_TILE, FFN_COL_TILE, guest)
        x2s = _proj_ln(x1s, a_mem.reshape(bs * t_new, dm), wo_l, ln2_g[l], ln2_b[l], alpha)
        y_s = _ffn(x2s, wg_l, wu_l, wd_l, ln3_g[l], ln3_b[l], alpha, bs * t_new, FFN_COL_TILE)

        head_shape = (N_ATT_HEADS, ATT_HEAD_DIM)
        state_shape = (n_groups, n_state)
        mem_shape = (bp, n_mem, N_MEM_HEADS, dm // N_MEM_HEADS)
        rows_first = lambda t: jnp.transpose(t.reshape(bp, *head_shape, seq), (0, 3, 1, 2))
        for lst, val in zip(outs, (
                rows_first(k_t)[:, seq - keep:], rows_first(v_t)[:, seq - keep:],
                ks.reshape(bs, t_new, *head_shape), vs.reshape(bs, t_new, *head_shape),
                hr_p.reshape(bp, *state_shape), hi_p.reshape(bp, *state_shape),
                hr_s.reshape(bs, *state_shape), hi_s.reshape(bs, *state_shape),
                mk_heads.reshape(mem_shape), mv_heads.reshape(mem_shape))):
            lst.append(val)
    return (y_p.reshape(bp, seq, dm), y_s.reshape(bs, t_new, dm)) + tuple(jnp.stack(o) for o in outs)
```

```python
import functools
import math

import jax
import jax.numpy as jnp
from jax import lax
from jax.experimental import pallas as pl
from jax.experimental.pallas import tpu as pltpu

F32 = jnp.float32
BF16 = jnp.bfloat16

N_ATT_HEADS = 8
ATT_HEAD_DIM = 64
D_ATT = N_ATT_HEADS * ATT_HEAD_DIM
DILATED_CFGS = ((128, 1), (512, 4), (2048, 16))
ATT_BLK = 128
ATT_GROUP = 4
SSM_CH = 16
SSM_STATE = 64
N_MEM_HEADS = 4
MIX_ROW_SPLITS = 4
EPS = 1e-5
NEG = -1e30

LANES = 128
V7X_VMEM_CAP_BYTES = 56 * 1024 * 1024

SSM_GB = LANES // SSM_CH
SSM_ROW_TILE = 32
SSM_CHAIN_GROUP = 8


def _params(sem, vmem_bytes):
    return pltpu.CompilerParams(
        dimension_semantics=sem,
        vmem_limit_bytes=int(min(max(vmem_bytes, 16 * 1024 * 1024), V7X_VMEM_CAP_BYTES)),
    )


def _dot(a, b):
    return jnp.dot(a, b, preferred_element_type=F32)


def _dot_nt(a, b):
    return lax.dot_general(a, b, (((1,), (1,)), ((), ())), preferred_element_type=F32)


def _layer_norm(x, g, b):
    mu = jnp.mean(x, axis=-1, keepdims=True)
    xc = x - mu
    var = jnp.mean(xc * xc, axis=-1, keepdims=True)
    return xc * lax.rsqrt(var + EPS) * g + b


def _rms_norm(x, g):
    return x * lax.rsqrt(jnp.mean(x * x, axis=-1, keepdims=True) + EPS) * g


def _const_spec(shape):
    n = len(shape)
    return pl.BlockSpec(shape, lambda *_: (0,) * n, pipeline_mode=pl.Buffered(1))


def _inproj_body(x_ref, w_ref, q_ref, k_ref, v_ref, uc_ref, *rest, t1, transposed):
    u_scr = rest[-1]
    xb = x_ref[...].astype(BF16)
    d = q_ref.shape[1]
    q_ref[...] = _dot(xb, w_ref[:, 0 * d:1 * d])
    k = _dot(xb, w_ref[:, 1 * d:2 * d])
    v = _dot(xb, w_ref[:, 2 * d:3 * d])
    k_ref[...] = k
    v_ref[...] = v
    if transposed:
        kt_ref, vt_ref = rest[:2]
        kt_ref[...] = k.T
        vt_ref[...] = v.T
    u = _dot(xb, w_ref[:, 3 * d:4 * d])
    chunks = u_scr.shape[1] // t1
    for c in range(d // LANES):
        u_scr[c] = u[:, c * LANES:(c + 1) * LANES]
        for i in range(t1):
            uc_ref[:, i * d + c * LANES:i * d + (c + 1) * LANES] = u_scr[c, pl.ds(i, chunks, stride=t1), :]


def _inproj(x2d, w_in_bf16, tm, t1, seq=None):
    rows, dm = x2d.shape
    d = w_in_bf16.shape[1] // 4
    out = jax.ShapeDtypeStruct((rows, d), F32)
    row_spec = pl.BlockSpec((tm, d), lambda i: (i, 0))
    out_specs = [row_spec] * 3 + [pl.BlockSpec((tm // t1, t1 * d), lambda i: (i, 0))]
    out_shape = [out] * 3 + [jax.ShapeDtypeStruct((rows // t1, t1 * d), F32)]
    if seq is not None:
        steps = seq // tm
        out_specs += [pl.BlockSpec((None, d, tm), lambda i: (i // steps, 0, i % steps))] * 2
        out_shape += [jax.ShapeDtypeStruct((rows // seq, d, seq), F32)] * 2
    vmem = 2 * (tm * dm * 4 + 6 * tm * d * 4) + 2 * w_in_bf16.size * 2 + 10 * tm * d * 4
    return pl.pallas_call(
        functools.partial(_inproj_body, t1=t1, transposed=seq is not None),
        grid=(rows // tm,),
        in_specs=[pl.BlockSpec((tm, dm), lambda i: (i, 0)), _const_spec(w_in_bf16.shape)],
        out_specs=out_specs,
        out_shape=out_shape,
        scratch_shapes=[pltpu.VMEM((d // LANES, tm, LANES), F32)],
        compiler_params=_params(("parallel",), vmem),
        name="inproj",
    )(x2d, w_in_bf16)


def _attn_prompt_body(q_ref, k_ref, v_ref, o_ref, acc_ref, m_ref, l_ref):
    seq = q_ref.shape[0]
    blk = ATT_BLK
    scale = ATT_HEAD_DIM ** -0.5 * math.log2(math.e)
    lane = lax.broadcasted_iota(jnp.int32, (blk, LANES), 1)
    head0 = lane < ATT_HEAD_DIM
    qi = lax.broadcasted_iota(jnp.int32, (blk, blk), 0)
    kj = lax.broadcasted_iota(jnp.int32, (blk, blk), 1)
    tri = kj <= qi
    mask_rest = jnp.concatenate([kj >= qi, tri], axis=1)

    def rows_of(start, d):
        return pl.ds(start, blk) if d == 1 else pl.ds(start, blk, stride=d)

    def load_block(d, q_start, prev_start):
        rows = rows_of(q_start, d)
        q = q_ref[rows, :] * scale
        k2 = k_ref[rows, :].astype(BF16)
        v2 = v_ref[rows, :].astype(BF16)
        mask = tri
        if prev_start is not None:
            prow = rows_of(prev_start, d)
            k2 = jnp.concatenate([k_ref[prow, :].astype(BF16), k2], axis=0)
            v2 = jnp.concatenate([v_ref[prow, :].astype(BF16), v2], axis=0)
            mask = mask_rest
        qh = [jnp.where(head0 if h == 0 else jnp.logical_not(head0), q, 0.0).astype(BF16) for h in range(2)]
        return rows, qh, k2, v2, mask

    def attend_group(cfg, d, starts):
        blocks = [load_block(d, q_start, prev_start) for q_start, prev_start in starts]
        scores = [[jnp.where(mask, _dot_nt(qh[h], k2), NEG) for h in range(2)] for _, qh, k2, _, mask in blocks]
        maxes = [[jnp.max(s, axis=1, keepdims=True) for s in sb] for sb in scores]
        probs = [[jnp.exp2(s - m) for s, m in zip(sb, mb)] for sb, mb in zip(scores, maxes)]
        sums = [[jnp.sum(p, axis=1, keepdims=True) for p in pb] for pb in probs]
        pvs = [[_dot(p.astype(BF16), blk_[3]) for p in pb] for pb, blk_ in zip(probs, blocks)]
        new = [(blk_[0], jnp.where(head0, mb[0], mb[1]), jnp.where(head0, lb[0], lb[1]),
                jnp.where(head0, ob[0], ob[1])) for blk_, mb, lb, ob in zip(blocks, maxes, sums, pvs)]
        if cfg == 0:
            for rows, m, l, num in new:
                acc_ref[rows, :] = num
                m_ref[rows, :] = m
                l_ref[rows, :] = l
            return
        old = [(m_ref[rows, :], l_ref[rows, :], acc_ref[rows, :]) for rows, _, _, _ in new]
        for (rows, m, l, num), (m_old, l_old, acc_old) in zip(new, old):
            m_new = jnp.maximum(m_old, m)
            a = jnp.exp2(m_old - m_new)
            b = jnp.exp2(m - m_new)
            num = a * acc_old + b * num
            l = a * l_old + b * l
            if cfg == len(DILATED_CFGS) - 1:
                o_ref[rows, :] = num / l
            else:
                acc_ref[rows, :] = num
                m_ref[rows, :] = m_new
                l_ref[rows, :] = l

    for cfg, (window, d) in enumerate(DILATED_CFGS):
        span = d * blk
        blocks = [(r + n * span, r + (n - 1) * span if n else None) for n in range(seq // span) for r in range(d)]
        for g in range(0, len(blocks), ATT_GROUP):
            attend_group(cfg, d, blocks[g:g + ATT_GROUP])


def _attn_prompt(q, k, v, seq):
    rows, d_att = q.shape
    bsz = rows // seq
    assert d_att % LANES == 0 and LANES == 2 * ATT_HEAD_DIM
    for window, d in DILATED_CFGS:
        assert window // d == ATT_BLK and seq % (d * ATT_BLK) == 0
    spec = pl.BlockSpec((seq, LANES), lambda b, h: (b, h))
    blk_bytes = seq * LANES * 4
    return pl.pallas_call(
        _attn_prompt_body,
        grid=(bsz, d_att // LANES),
        in_specs=[spec, spec, spec],
        out_specs=spec,
        out_shape=jax.ShapeDtypeStruct(q.shape, F32),
        scratch_shapes=[pltpu.VMEM((seq, LANES), F32)] * 3,
        compiler_params=_params(("parallel", "parallel"), 11 * blk_bytes + 8 * 1024 * 1024),
        name="attn_prompt",
    )(q, k, v)


def _window_head_mask(d_att):
    nh = N_ATT_HEADS
    return (jnp.right_shift(lax.broadcasted_iota(jnp.int32, (nh, d_att), 1), ATT_HEAD_DIM.bit_length() - 1)
            == lax.broadcasted_iota(jnp.int32, (nh, d_att), 0))


def _window_probs(q, kn, kt):
    t_new, d_att = q.shape
    w_buf = kt.shape[1]
    nh = N_ATT_HEADS
    rows = t_new * nh
    (win1, _), (win4, dil4), (_, dil16) = DILATED_CFGS
    tail, near = win4, win1
    head_mask = _window_head_mask(d_att)
    trow = jnp.right_shift(lax.broadcasted_iota(jnp.int32, (rows, 1), 0), nh.bit_length() - 1)
    lane = lambda n: lax.broadcasted_iota(jnp.int32, (rows, n), 1)
    mask16 = (lane(w_buf) & (dil16 - 1)) == trow
    mask4 = (lane(tail) & (dil4 - 1)) == trow
    mask1 = lane(near) >= trow
    new_self = lane(t_new) == trow
    new_causal = lane(t_new) <= trow

    q = q * (ATT_HEAD_DIM ** -0.5)
    qbd = jnp.concatenate(
        [jnp.where(head_mask, jnp.broadcast_to(q[t:t + 1], (nh, d_att)), 0.0) for t in range(t_new)],
        axis=0).astype(BF16)
    s_all = _dot(qbd, kt.astype(BF16))
    s_new = _dot_nt(qbd, kn.astype(BF16))

    def softmax_parts(main, main_mask, new_mask):
        sm = jnp.where(main_mask, main, NEG)
        sn = jnp.where(new_mask, s_new, NEG)
        m = jnp.maximum(jnp.max(sm, axis=1, keepdims=True), jnp.max(sn, axis=1, keepdims=True))
        p = jnp.exp(sm - m)
        pn = jnp.exp(sn - m)
        den = jnp.sum(p, axis=1, keepdims=True) + jnp.sum(pn, axis=1, keepdims=True)
        return m, p, pn, den

    m16, p16, pn16, den16 = softmax_parts(s_all, mask16, new_self)
    m4, p4, pn4, den4 = softmax_parts(s_all[:, w_buf - tail:], mask4, new_self)
    m1, p1, pn1, den1 = softmax_parts(s_all[:, w_buf - near:], mask1, new_causal)
    m = jnp.maximum(jnp.maximum(m1, m4), m16)
    w1, w4, w16 = jnp.exp(m1 - m), jnp.exp(m4 - m), jnp.exp(m16 - m)
    den = w1 * den1 + w4 * den4 + w16 * den16
    p16, p4, p1 = w16 * p16, w4 * p4, w1 * p1
    p_all = jnp.concatenate(
        [p16[:, :w_buf - tail],
         p16[:, w_buf - tail:w_buf - near] + p4[:, :tail - near],
         p16[:, w_buf - near:] + p4[:, tail - near:] + p1], axis=1).astype(BF16)
    pn_all = (w16 * pn16 + w4 * pn4 + w1 * pn1).astype(BF16)
    return p_all, pn_all, den


def _window_output(p_all, pn_all, den, vn, vt):
    t_new, d_att = vn.shape
    nh = N_ATT_HEADS
    head_mask = _window_head_mask(d_att)
    out = (_dot_nt(p_all, vt.astype(BF16)) + _dot(pn_all, vn.astype(BF16))) / den
    return jnp.concatenate(
        [jnp.sum(jnp.where(head_mask, out[t * nh:(t + 1) * nh], 0.0), axis=0, keepdims=True)
         for t in range(t_new)], axis=0)


def _window_guest_probs(q_ref, kn_ref, kt_ref):
    nb = kt_ref.shape[0]
    t_new = q_ref.shape[0] // nb
    tok = [slice(b * t_new, (b + 1) * t_new) for b in range(nb)]
    return [_window_probs(q_ref[tok[b], :], kn_ref[tok[b], :], kt_ref[b]) for b in range(nb)]


def _window_guest_output(probs, vn_ref, vt_ref, o_ref):
    nb = vt_ref.shape[0]
    t_new = vn_ref.shape[0] // nb
    for b in range(nb):
        tok = slice(b * t_new, (b + 1) * t_new)
        o_ref[tok, :] = _window_output(*probs[b], vn_ref[tok, :], vt_ref[b])


def _window_guest_operands(q, k_new, v_new, cache_k, cache_v, layer, nb):
    depth, bsz, w_buf, nh, hd = cache_k.shape
    t_new, d_att = q.shape[0] // bsz, q.shape[1]
    assert w_buf == DILATED_CFGS[-1][0] and t_new <= DILATED_CFGS[1][1] and nh * hd == d_att
    assert (nb * t_new) % 8 == 0 and bsz % nb == 0
    rows_last = lambda c: jnp.transpose(c, (0, 1, 3, 4, 2)).reshape(depth, bsz, d_att, w_buf)
    new_spec = pl.BlockSpec((nb * t_new, d_att), lambda i: (i, 0))
    cache_spec = pl.BlockSpec((None, nb, d_att, w_buf), lambda i: (layer, i, 0, 0))
    vmem = 2 * 2 * nb * d_att * w_buf * 4 + 4 * nb * d_att * w_buf * 2 + 4 * 1024 * 1024
    return ([q, k_new, v_new, rows_last(cache_k), rows_last(cache_v)],
            [new_spec, new_spec, new_spec, cache_spec, cache_spec], new_spec,
            jax.ShapeDtypeStruct(q.shape, F32), vmem, bsz // nb)


def _ssm_prep_body(a_re_ref, a_im_ref, log_dt_ref, b_re_ref, b_im_ref, lam_re_ref, lam_im_ref,
                   bbar_re_ref, bbar_im_ref, step_re_ref, step_im_ref, *, chunk):
    ar = a_re_ref[...]
    ai = a_im_ref[...]
    dt = jnp.exp(log_dt_ref[...])
    mag = jnp.exp(dt * ar)
    lr = mag * jnp.cos(dt * ai)
    li = mag * jnp.sin(dt * ai)
    den = ar * ar + ai * ai
    nr, ni = lr - 1.0, li
    cr = (nr * ar + ni * ai) / den
    ci = (ni * ar - nr * ai) / den
    for h in range(b_re_ref.shape[0]):
        br, bi = b_re_ref[h], b_im_ref[h]
        bbar_re_ref[h] = cr * br - ci * bi
        bbar_im_ref[h] = cr * bi + ci * br
    pr, pi = lr, li
    for j in range(lam_re_ref.shape[0]):
        lam_re_ref[j] = pr
        lam_im_ref[j] = pi
        if j == chunk - 1:
            cr, ci = pr, pi
        pr, pi = pr * lr - pi * li, pr * li + pi * lr
    pr, pi = cr, ci
    for j in range(step_re_ref.shape[0]):
        step_re_ref[j] = pr
        step_im_ref[j] = pi
        pr, pi = pr * cr - pi * ci, pr * ci + pi * cr


def _ssm_prep(a_re, a_im, log_dt, b_re, b_im, n_pow, chunk):
    g, p = a_re.shape
    h = b_re.shape[-1]
    assert chunk <= n_pow
    b_re_t = jnp.transpose(b_re, (2, 0, 1))
    b_im_t = jnp.transpose(b_im, (2, 0, 1))
    gp = jax.ShapeDtypeStruct((n_pow, g, p), F32)
    hgp = jax.ShapeDtypeStruct((h, g, p), F32)
    sgp = jax.ShapeDtypeStruct((SSM_CHAIN_GROUP, g, p), F32)
    return pl.pallas_call(
        functools.partial(_ssm_prep_body, chunk=chunk),
        out_shape=[gp, gp, hgp, hgp, sgp, sgp],
        name="ssm_prep",
    )(a_re, a_im, log_dt.reshape(g, 1), b_re_t, b_im_t)


def _ssm_layouts(lam_re, lam_im, bbar_re_t, bbar_im_t, step_re, step_im, c_re, c_im, d_skip):
    n_pow, g, p = lam_re.shape
    h = bbar_re_t.shape[0]
    nq = g // SSM_GB
    eye = jnp.eye(SSM_GB, dtype=bool)
    bb = jnp.stack([bbar_re_t, bbar_im_t], 0).reshape(2, h, nq, SSM_GB, p)
    bb = jnp.transpose(bb, (2, 3, 1, 0, 4))
    bmat = jnp.where(eye[None, :, None, None, :, None], bb[:, :, :, :, None, :], 0.0)
    bmat = bmat.reshape(nq, SSM_GB * h, 2 * SSM_GB * p).astype(BF16)
    def c_layout(c):
        cc = jnp.transpose(c.reshape(nq, SSM_GB, h, p), (0, 1, 3, 2))
        m = jnp.where(eye[None, :, None, :, None], cc[:, :, :, None, :], 0.0)
        return m.reshape(nq, SSM_GB * p, SSM_GB * h).astype(BF16)
    per_block = lambda t: t.reshape(t.shape[0], nq, 1, SSM_GB * p)
    d_q = d_skip.reshape(nq, 1, SSM_GB * h)
    return (bmat, c_layout(c_re), c_layout(c_im), per_block(lam_re), per_block(lam_im),
            per_block(step_re), per_block(step_im), d_q)


def _ssm_body(u_ref, h0r_ref, h0i_ref, bmat_ref, cre_ref, cim_ref, lamr_ref, lami_ref, stepr_ref, stepi_ref, d_ref,
              y_ref, hfr_ref, hfi_ref, hloc_ref, hs_ref, ends_ref, grp_ref, *, t1, chained):
    rows = u_ref.shape[0]
    d_ssm = y_ref.shape[1] // t1
    nq = bmat_ref.shape[0]
    ns = lamr_ref.shape[-1]
    rt = SSM_ROW_TILE
    re, im = slice(0, ns), slice(ns, 2 * ns)

    def u_lanes(i, q):
        return slice(i * d_ssm + q * LANES, i * d_ssm + (q + 1) * LANES)

    def row_tiles(body):
        lax.fori_loop(0, rows // rt, lambda t, c: (body(pl.ds(pl.multiple_of(t * rt, rt), rt)), c)[1], 0,
                      unroll=True)

    def input_drive(q):
        for i in range(t1):
            hloc_ref[q % 2, i] = _dot(u_ref[:, u_lanes(i, q)].astype(BF16), bmat_ref[q])

    def local_scan(q):
        h = hloc_ref.at[q % 2]
        lr, li = lamr_ref[0, q], lami_ref[0, q]

        def tile(r):
            hr, hi = h[0, r, re], h[0, r, im]
            for i in range(1, t1):
                hr, hi = lr * hr - li * hi + h[i, r, re], lr * hi + li * hr + h[i, r, im]
                h[i, r, re] = hr
                h[i, r, im] = hi
        row_tiles(tile)

    nlb = ns // LANES

    def put_blocks(ref, rows_idx, vr, vi):
        for c in range(nlb):
            ref[c, rows_idx, :] = vr[:, c * LANES:(c + 1) * LANES]
            ref[nlb + c, rows_idx, :] = vi[:, c * LANES:(c + 1) * LANES]

    def get_blocks(ref, rows_idx):
        return (jnp.concatenate([ref[c, rows_idx, :] for c in range(nlb)], axis=1),
                jnp.concatenate([ref[nlb + c, rows_idx, :] for c in range(nlb)], axis=1))

    def chunk_starts(q):
        h = hloc_ref.at[q % 2]
        s_lanes = slice(q * ns, (q + 1) * ns)
        if not chained:
            put_blocks(hs_ref, slice(None), h0r_ref[:, s_lanes], h0i_ref[:, s_lanes])
            return
        grp = SSM_CHAIN_GROUP
        n_grp = rows // grp
        put_blocks(ends_ref, slice(None), h[t1 - 1, :, re], h[t1 - 1, :, im])
        member = lambda j: pl.ds(j, n_grp, stride=grp)
        cr, ci = stepr_ref[0, q], stepi_ref[0, q]
        gr, gi = get_blocks(ends_ref, member(0))
        inside = [(gr, gi)]
        for j in range(1, grp):
            er, ei = get_blocks(ends_ref, member(j))
            gr, gi = cr * gr - ci * gi + er, cr * gi + ci * gr + ei
            inside.append((gr, gi))
        grp_ref[0, :, re] = gr
        grp_ref[0, :, im] = gi
        wr, wi = stepr_ref[grp - 1, q], stepi_ref[grp - 1, q]

        def chain(c, g):
            sr, si = g
            row = pl.ds(c, 1)
            grp_ref[1, row, re] = sr
            grp_ref[1, row, im] = si
            return wr * sr - wi * si + grp_ref[0, row, re], wr * si + wi * sr + grp_ref[0, row, im]

        sr, si = lax.fori_loop(0, n_grp, chain, (h0r_ref[:, s_lanes], h0i_ref[:, s_lanes]))
        hfr_ref[:, s_lanes] = sr
        hfi_ref[:, s_lanes] = si
        sr, si = grp_ref[1, :, re], grp_ref[1, :, im]
        put_blocks(hs_ref, member(0), sr, si)
        for j in range(1, grp):
            pr, pi = stepr_ref[j - 1, q], stepi_ref[j - 1, q]
            gr, gi = inside[j - 1]
            put_blocks(hs_ref, member(j), gr + (pr * sr - pi * si), gi + (pr * si + pi * sr))

    def add_carry(q):
        h = hloc_ref.at[q % 2]

        def tile(r):
            hr0, hi0 = get_blocks(hs_ref, r)
            for i in range(t1):
                pr, pi = lamr_ref[i, q], lami_ref[i, q]
                h[i, r, re] = h[i, r, re] + (pr * hr0 - pi * hi0)
                h[i, r, im] = h[i, r, im] + (pr * hi0 + pi * hr0)
        row_tiles(tile)

    def readout(q):
        h = hloc_ref.at[q % 2]
        if not chained:
            s_lanes = slice(q * ns, (q + 1) * ns)
            hfr_ref[:, s_lanes] = h[t1 - 1, :, re]
            hfi_ref[:, s_lanes] = h[t1 - 1, :, im]
        for i in range(t1):
            y_ref[:, u_lanes(i, q)] = (_dot(h[i, :, re].astype(BF16), cre_ref[q])
                                       - _dot(h[i, :, im].astype(BF16), cim_ref[q])
                                       + d_ref[q] * u_ref[:, u_lanes(i, q)])

    input_drive(0)
    local_scan(0)
    chunk_starts(0)
    for q in range(nq):
        if q + 1 < nq:
            input_drive(q + 1)
        add_carry(q)
        readout(q)
        if q + 1 < nq:
            local_scan(q + 1)
            chunk_starts(q + 1)


def _ssm(u_chunks, h0_re, h0_im, layouts, t1, rows, chained):
    bmat, cre, cim, lam_re_q, lam_im_q, step_re_q, step_im_q, d_q = layouts
    lam_re_q, lam_im_q = lam_re_q[:t1], lam_im_q[:t1]
    n_blocks, rows_h, n_state = h0_re.shape
    width = u_chunks.shape[1]
    assert rows % SSM_ROW_TILE == 0 and u_chunks.shape[0] == n_blocks * rows
    assert not chained or rows % SSM_CHAIN_GROUP == 0
    ns2 = bmat.shape[2]
    u_spec = pl.BlockSpec((rows, width), lambda b: (b, 0))
    h_spec = pl.BlockSpec((None, rows_h, n_state), lambda b: (b, 0, 0))
    consts = [bmat, cre, cim, lam_re_q, lam_im_q, step_re_q, step_im_q, d_q]
    chain_rows = rows if chained else 8
    blocked = lambda r: pltpu.VMEM((ns2 // LANES, r, LANES), F32)
    vmem = (4 * rows * width * 4 + (2 * t1 + 1) * rows * ns2 * 4 + sum(c.size * c.dtype.itemsize for c in consts)
            + 8 * rows_h * n_state * 4 + 8 * 1024 * 1024)
    h_out = jax.ShapeDtypeStruct(h0_re.shape, F32)
    return pl.pallas_call(
        functools.partial(_ssm_body, t1=t1, chained=chained),
        grid=(n_blocks,),
        in_specs=[u_spec, h_spec, h_spec] + [_const_spec(c.shape) for c in consts],
        out_specs=[u_spec, h_spec, h_spec],
        out_shape=[jax.ShapeDtypeStruct(u_chunks.shape, F32), h_out, h_out],
        scratch_shapes=[pltpu.VMEM((2, t1, rows, ns2), F32), blocked(rows), blocked(chain_rows),
                        pltpu.VMEM((2, chain_rows // SSM_CHAIN_GROUP, ns2), F32)],
        compiler_params=_params(("parallel",), vmem),
        name="ssm_chained" if chained else "ssm_rows",
    )(u_chunks, h0_re, h0_im, *consts)


def _mix_body(x_ref, oatt_ref, yc_ref, wglu_ref, bglu_ref, gatt_ref, gssm_ref, wout_ref, lng_ref, lnb_ref,
              o_ref, y_scr, *, alpha, t1):
    d_att = oatt_ref.shape[1]
    n_lane_blocks, rows, _ = y_scr.shape
    d_ssm = n_lane_blocks * LANES
    chunks = rows // t1
    for c in range(n_lane_blocks):
        for i in range(t1):
            y_scr[c, pl.ds(i, chunks, stride=t1), :] = yc_ref[:, i * d_ssm + c * LANES:i * d_ssm + (c + 1) * LANES]
    parts = [pl.ds(i * (rows // MIX_ROW_SPLITS), rows // MIX_ROW_SPLITS) for i in range(MIX_ROW_SPLITS)]
    gs = [jax.nn.gelu(jnp.concatenate([y_scr[c, r, :] for c in range(n_lane_blocks)], axis=1)) for r in parts]
    gates = [_dot(g.astype(BF16), wglu_ref[...]) for g in gs]
    ras = [_rms_norm(oatt_ref[r, :], gatt_ref[...]).astype(BF16) for r in parts]
    att_parts = [_dot(ra, wout_ref[0:d_att, :]) for ra in ras]
    rzs = [_rms_norm(g * jax.nn.sigmoid(gate + bglu_ref[...]), gssm_ref[...]).astype(BF16)
           for g, gate in zip(gs, gates)]
    for r, att, rz in zip(parts, att_parts, rzs):
        mixed = att + _dot(rz, wout_ref[d_att:, :])
        o_ref[r, :] = _layer_norm(alpha * x_ref[r, :] + mixed, lng_ref[...], lnb_ref[...])


def _mix(x2d, o_att, y_chunks, w_glu, b_glu, g_att, g_ssm, w_out, ln_g, ln_b, alpha, tm, t1):
    rows, dm = x2d.shape
    d_att, d_ssm = o_att.shape[1], y_chunks.shape[1] // t1
    consts = [w_glu, b_glu.reshape(1, -1), g_att.reshape(1, -1), g_ssm.reshape(1, -1), w_out,
              ln_g.reshape(1, -1), ln_b.reshape(1, -1)]
    row = lambda width: pl.BlockSpec((tm, width), lambda i: (i, 0))
    chunk_spec = pl.BlockSpec((tm // t1, t1 * d_ssm), lambda i: (i, 0))
    vmem = 2 * tm * (2 * dm + d_att + d_ssm) * 4 + 4 * (w_glu.size + w_out.size) + 12 * tm * dm * 4
    return pl.pallas_call(
        functools.partial(_mix_body, alpha=alpha, t1=t1),
        grid=(rows // tm,),
        in_specs=[row(dm), row(d_att), chunk_spec] + [_const_spec(c.shape) for c in consts],
        out_specs=row(dm),
        out_shape=jax.ShapeDtypeStruct(x2d.shape, F32),
        scratch_shapes=[pltpu.VMEM((d_ssm // LANES, tm, LANES), F32)],
        compiler_params=_params(("parallel",), vmem),
        name="mix_out_ln1",
    )(x2d, o_att, y_chunks, *consts)


def _memkv_body(m_ref, wk_ref, wv_ref, k_ref, v_ref, kh_ref, vh_ref):
    mb = m_ref[...].astype(BF16)
    hd = kh_ref.shape[2]
    for w_ref, o_ref, oh_ref in ((wk_ref, k_ref, kh_ref), (wv_ref, v_ref, vh_ref)):
        val = _dot(mb, w_ref[...])
        o_ref[...] = val
        for h in range(oh_ref.shape[1]):
            oh_ref[:, h, :] = val[:, h * hd:(h + 1) * hd]


def _memkv(mem2d, wk, wv, tm):
    rows, dm = mem2d.shape
    hd = dm // N_MEM_HEADS
    row = pl.BlockSpec((tm, dm), lambda i: (i, 0))
    row_h = pl.BlockSpec((tm, N_MEM_HEADS, hd), lambda i: (i, 0, 0))
    out = jax.ShapeDtypeStruct((rows, wk.shape[1]), F32)
    out_h = jax.ShapeDtypeStruct((rows, N_MEM_HEADS, hd), F32)
    vmem = 2 * 5 * tm * dm * 4 + 2 * (wk.size + wv.size) + 4 * tm * dm * 4
    return pl.pallas_call(
        _memkv_body,
        grid=(rows // tm,),
        in_specs=[row, _const_spec(wk.shape), _const_spec(wv.shape)],
        out_specs=[row, row, row_h, row_h],
        out_shape=[out, out, out_h, out_h],
        compiler_params=_params(("parallel",), vmem),
        name="mem_kv",
    )(mem2d, wk, wv)


def _memattn_body(x_ref, mk_ref, mv_ref, wq_ref, wo_ref, lng_ref, lnb_ref,
                  gq_ref, gkn_ref, gvn_ref, gkt_ref, gvt_ref, o_ref, go_ref, *, alpha):
    guest_probs = _window_guest_probs(gq_ref, gkn_ref, gkt_ref)
    dm = x_ref.shape[1]
    hd = dm // N_MEM_HEADS
    heads = [slice(h * hd, (h + 1) * hd) for h in range(N_MEM_HEADS)]
    q = (_dot(x_ref[...].astype(BF16), wq_ref[...]) * (hd ** -0.5)).astype(BF16)
    scores = [_dot_nt(q[:, c], mk_ref[:, c].astype(BF16)) for c in heads]
    _window_guest_output(guest_probs, gvn_ref, gvt_ref, go_ref)
    probs = [jnp.exp(s - jnp.max(s, axis=1, keepdims=True)) for s in scores]
    sums = [jnp.sum(p, axis=1, keepdims=True) for p in probs]
    outs = [_dot(p.astype(BF16), mv_ref[:, c].astype(BF16)) / l for p, l, c in zip(probs, sums, heads)]
    att = _dot(jnp.concatenate(outs, axis=1).astype(BF16), wo_ref[...])
    o_ref[...] = _layer_norm(alpha * x_ref[...] + att, lng_ref[...], lnb_ref[...])


def _memattn(x2d, mem_k, mem_v, wq, wo, ln_g, ln_b, alpha, guest):
    rows, dm = x2d.shape
    n_seq, n_mem, _ = mem_k.shape
    g_arrays, g_specs, g_out_spec, g_out_shape, g_vmem, steps = guest
    tm = rows // steps
    steps_per_mem = rows // n_seq // tm
    assert tm % 8 == 0 and steps_per_mem * n_seq * tm == rows
    row = pl.BlockSpec((tm, dm), lambda i: (i, 0))
    mem_spec = pl.BlockSpec((None, n_mem, dm), lambda i: (i // steps_per_mem, 0, 0))
    consts = [wq, wo, ln_g.reshape(1, -1), ln_b.reshape(1, -1)]
    vmem = (2 * 2 * tm * dm * 4 + 2 * 2 * n_mem * dm * 4 + 2 * (wq.size + wo.size) + 12 * tm * dm * 4
            + g_vmem)
    return pl.pallas_call(
        functools.partial(_memattn_body, alpha=alpha),
        grid=(steps,),
        in_specs=[row, mem_spec, mem_spec] + [_const_spec(c.shape) for c in consts] + g_specs,
        out_specs=[row, g_out_spec],
        out_shape=[jax.ShapeDtypeStruct(x2d.shape, F32), g_out_shape],
        compiler_params=_params(("parallel",), vmem),
        name="mem_attn_ln2_hosting",
    )(x2d, mem_k, mem_v, *consts, *g_arrays)


def _rows_matmul_body(x_ref, w_ref, o_ref, *, scale):
    o_ref[...] = _dot(x_ref[...].astype(BF16), w_ref[...]) * scale


def _rows_matmul(x2d, w, scale):
    rows, dm = x2d.shape
    vmem = 4 * rows * (dm + w.shape[1]) * 4 + 4 * w.size
    return pl.pallas_call(
        functools.partial(_rows_matmul_body, scale=scale),
        out_shape=jax.ShapeDtypeStruct((rows, w.shape[1]), F32),
        compiler_params=_params(None, vmem),
        name="rows_matmul",
    )(x2d, w)


def _memattn_cache_probs(q_ref, mk_ref):
    nb, rows, hd = q_ref.shape
    n_mem, nh = mk_ref.shape[1:3]
    cols = n_mem * nh
    head_ok = ((lax.broadcasted_iota(jnp.int32, (rows, cols), 0) & (nh - 1))
               == (lax.broadcasted_iota(jnp.int32, (rows, cols), 1) & (nh - 1)))
    scores = [jnp.where(head_ok, _dot_nt(q_ref[j].astype(BF16), mk_ref[j].reshape(cols, hd).astype(BF16)), NEG)
              for j in range(nb)]
    probs = [jnp.exp(s - jnp.max(s, axis=1, keepdims=True)) for s in scores]
    return [(p.astype(BF16), jnp.sum(p, axis=1, keepdims=True)) for p in probs]


def _memattn_cache_output(probs, mv_ref, o_ref):
    n_mem, nh, hd = mv_ref.shape[1:]
    for j, (p, l) in enumerate(probs):
        o_ref[j] = _dot(p, mv_ref[j].reshape(n_mem * nh, hd).astype(BF16)) / l


def _memattn_cache_operands(q3, cache_k, cache_v, layer, nb):
    bsz, rows, hd = q3.shape
    _, _, n_mem, nh, _ = cache_k.shape
    q_spec = pl.BlockSpec((nb, rows, hd), lambda i: (i, 0, 0))
    c_spec = pl.BlockSpec((None, nb, n_mem, nh, hd), lambda i: (layer, i, 0, 0, 0))
    vmem = 2 * 2 * nb * n_mem * 8 * hd * 4 + 8 * nb * rows * n_mem * nh * 4
    return [q3, cache_k, cache_v], [q_spec, c_spec, c_spec], q_spec, jax.ShapeDtypeStruct(q3.shape, F32), vmem


def _proj_ln_body(x_ref, a_ref, w_ref, lng_ref, lnb_ref, o_ref, *, alpha):
    att = _dot(a_ref[...].astype(BF16), w_ref[...])
    o_ref[...] = _layer_norm(alpha * x_ref[...] + att, lng_ref[...], lnb_ref[...])


def _proj_ln(x2d, a2d, w, ln_g, ln_b, alpha):
    rows, dm = x2d.shape
    vmem = 8 * rows * dm * 4 + 4 * w.size
    return pl.pallas_call(
        functools.partial(_proj_ln_body, alpha=alpha),
        out_shape=jax.ShapeDtypeStruct(x2d.shape, F32),
        compiler_params=_params(None, vmem),
        name="proj_ln",
    )(x2d, a2d, w, ln_g.reshape(1, -1), ln_b.reshape(1, -1))


def _ffn_body(x_ref, wg_ref, wu_ref, wd_ref, lng_ref, lnb_ref, *rest, alpha, tf, hosts_guest):
    if hosts_guest:
        gq_ref, gk_ref, gv_ref, o_ref, go_ref, acc_ref = rest
        guest_probs = _memattn_cache_probs(gq_ref, gk_ref)
    else:
        o_ref, acc_ref = rest
    x = x_ref[...]
    xb = x.astype(BF16)
    n_chunks = wg_ref.shape[1] // tf
    cols = [slice(c * tf, (c + 1) * tf) for c in range(n_chunks)]

    def hidden(c):
        return (jax.nn.silu(_dot(xb, wg_ref[:, cols[c]])) * _dot(xb, wu_ref[:, cols[c]])).astype(BF16)

    hid = hidden(0)
    for c in range(n_chunks):
        nxt = hidden(c + 1) if c + 1 < n_chunks else None
        if hosts_guest and c == n_chunks // 2:
            _memattn_cache_output(guest_probs, gv_ref, go_ref)
        part = _dot(hid, wd_ref[cols[c], :])
        if c == 0:
            acc_ref[...] = part
        else:
            acc_ref[...] += part
        hid = nxt
    o_ref[...] = _layer_norm(alpha * x + acc_ref[...], lng_ref[...], lnb_ref[...])


def _ffn(x2d, wg, wu, wd, ln_g, ln_b, alpha, tm, tf, guest=None):
    rows, dm = x2d.shape
    d_ff = wg.shape[1]
    assert d_ff % tf == 0 and tf % LANES == 0 and rows % tm == 0
    row = pl.BlockSpec((tm, dm), lambda i: (i, 0))
    consts = [wg, wu, wd, ln_g.reshape(1, -1), ln_b.reshape(1, -1)]
    operands = [x2d, *consts]
    in_specs = [row] + [_const_spec(c.shape) for c in consts]
    out_specs, out_shape = row, jax.ShapeDtypeStruct(x2d.shape, F32)
    vmem = 2 * 2 * tm * dm * 4 + 2 * 3 * wg.size + tm * dm * 4 + 8 * tm * max(tf, dm) * 4
    if guest is not None:
        g_arrays, g_specs, g_out_spec, g_out_shape, g_vmem = guest
        assert g_arrays[0].shape[0] // g_specs[0].block_shape[0] == rows // tm
        operands, in_specs = operands + g_arrays, in_specs + g_specs
        out_specs, out_shape = [row, g_out_spec], [out_shape, g_out_shape]
        vmem += g_vmem
    return pl.pallas_call(
        functools.partial(_ffn_body, alpha=alpha, tf=tf, hosts_guest=guest is not None),
        grid=(rows // tm,),
        in_specs=in_specs,
        out_specs=out_specs,
        out_shape=out_shape,
        scratch_shapes=[pltpu.VMEM((tm, dm), F32)],
        compiler_params=_params(("parallel",), vmem),
        name="swiglu_ln3_hosting" if guest is not None else "swiglu_ln3",
    )(*operands)


PROMPT_CHUNK = 8
ROW_TILE = 512
FFN_COL_TILE = 256
SAMPLE_ATTN_SEQS = 2


def kernel(x_prompt, x_sample, cache_win_k, cache_win_v, state_ssm_re, state_ssm_im, cache_mem_k, cache_mem_v, mem_prompt, w_in, g_att, g_ssm, ssm_a_re, ssm_a_im, ssm_log_dt, ssm_b_re, ssm_b_im, ssm_c_re, ssm_c_im, ssm_d, w_glu, b_glu, w_out, ln1_g, ln1_b, w_mem_q, w_mem_k, w_mem_v, w_mem_o, ln2_g, ln2_b, w_gate, w_up, w_down, ln3_g, ln3_b):
    depth = w_in.shape[0]
    bp, seq, dm = x_prompt.shape
    bs, t_new, _ = x_sample.shape
    n_groups, n_state = ssm_a_re.shape[1:]
    n_mem = mem_prompt.shape[1]
    alpha = (2 * depth) ** 0.25
    keep = min(DILATED_CFGS[-1][0], seq)
    assert keep == seq
    n_chunks = seq // PROMPT_CHUNK

    y_p = x_prompt.reshape(bp * seq, dm)
    y_s = x_sample.reshape(bs * t_new, dm)
    mem2d = mem_prompt.reshape(bp * n_mem, dm)
    outs = [[] for _ in range(10)]
    for l in range(depth):
        bf = lambda w: w[l].astype(BF16)
        w_in_l, w_glu_l, w_out_l = bf(w_in), bf(w_glu), bf(w_out)
        wq_l, wk_l, wv_l, wo_l = bf(w_mem_q), bf(w_mem_k), bf(w_mem_v), bf(w_mem_o)
        wg_l, wu_l, wd_l = bf(w_gate), bf(w_up), bf(w_down)
        prepped = _ssm_prep(ssm_a_re[l], ssm_a_im[l], ssm_log_dt[l], ssm_b_re[l], ssm_b_im[l],
                            max(PROMPT_CHUNK, t_new), PROMPT_CHUNK)
        layouts = _ssm_layouts(*prepped, ssm_c_re[l], ssm_c_im[l], ssm_d[l])
        mix_w = (w_glu_l, b_glu[l], g_att[l], g_ssm[l], w_out_l, ln1_g[l], ln1_b[l])

        q, k, v, u, k_t, v_t = _inproj(y_p, w_in_l, ROW_TILE, PROMPT_CHUNK, seq)
        d_att = q.shape[1]
        o_att = _attn_prompt(q, k, v, seq)
        zeros = jnp.zeros((bp, 1, n_groups * n_state), F32)
        y_ssm, hr_p, hi_p = _ssm(u, zeros, zeros, layouts, PROMPT_CHUNK, n_chunks, True)
        x1 = _mix(y_p, o_att, y_ssm, *mix_w, alpha, ROW_TILE, PROMPT_CHUNK)
        mk_p, mv_p, mk_heads, mv_heads = _memkv(mem2d, wk_l, wv_l, ROW_TILE)

        qs, ks, vs, us = _inproj(y_s, w_in_l, bs * t_new, t_new)
        window = _window_guest_operands(qs, ks, vs, cache_win_k, cache_win_v, l, SAMPLE_ATTN_SEQS)
        x2, o_att_s = _memattn(x1, mk_p.reshape(bp, n_mem, dm), mv_p.reshape(bp, n_mem, dm), wq_l, wo_l,
                               ln2_g[l], ln2_b[l], alpha, window)
        y_ssm_s, hr_s, hi_s = _ssm(us, state_ssm_re[l].reshape(1, bs, -1), state_ssm_im[l].reshape(1, bs, -1),
                                   layouts, t_new, bs, False)
        x1s = _mix(y_s, o_att_s, y_ssm_s, *mix_w, alpha, bs * t_new, t_new)
        mem_hd = dm // N_MEM_HEADS
        q_mem = _rows_matmul(x1s, wq_l, mem_hd ** -0.5).reshape(bs, t_new * N_MEM_HEADS, mem_hd)
        ffn_steps = bp * seq // ROW_TILE
        guest = _memattn_cache_operands(q_mem, cache_mem_k, cache_mem_v, l, bs // ffn_steps)
        y_p, a_mem = _ffn(x2, wg_l, wu_l, wd_l, ln3_g[l], ln3_b[l], alpha, ROW_TILE, FFN_COL_TILE, guest)
        x2s = _proj_ln(x1s, a_mem.reshape(bs * t_new, dm), wo_l, ln2_g[l], ln2_b[l], alpha)
        y_s = _ffn(x2s, wg_l, wu_l, wd_l, ln3_g[l], ln3_b[l], alpha, bs * t_new, FFN_COL_TILE)

        head_shape = (N_ATT_HEADS, ATT_HEAD_DIM)
        state_shape = (n_groups, n_state)
        mem_shape = (bp, n_mem, N_MEM_HEADS, dm // N_MEM_HEADS)
        rows_first = lambda t: jnp.transpose(t.reshape(bp, *head_shape, seq), (0, 3, 1, 2))
        for lst, val in zip(outs, (
                rows_first(k_t)[:, seq - keep:], rows_first(v_t)[:, seq - keep:],
                ks.reshape(bs, t_new, *head_shape), vs.reshape(bs, t_new, *head_shape),
                hr_p.reshape(bp, *state_shape), hi_p.reshape(bp, *state_shape),
                hr_s.reshape(bs, *state_shape), hi_s.reshape(bs, *state_shape),
                mk_heads.reshape(mem_shape), mv_heads.reshape(mem_shape))):
            lst.append(val)
    return (y_p.reshape(bp, seq, dm), y_s.reshape(bs, t_new, dm)) + tuple(jnp.stack(o) for o in outs)
```
